```python
import jax, jax.numpy as jnp
from jax import lax
import numpy as np

D_MODEL = 1024
BATCH = 8
SEQ = 16384
DEPTH = 4

N_MIXERS = 2
N_HEADS = 16
HEAD_DIM = D_MODEL // N_HEADS
Q_BLOCK = 128
CONV_WIDTH = 3
D_FF = 4 * D_MODEL
PLE_DIM = 256
N_ATTN = (DEPTH + 1) // 2
N_CONV = DEPTH // 2
RMS_EPS = 1e-6
NEG_INF = -1e30

kernel_name = "fox_shortconv_hybrid_trunk"


def rmsnorm(x, g):
    xf = x.astype(jnp.float32)
    y = xf * lax.rsqrt(jnp.mean(xf * xf, axis=-1, keepdims=True) + RMS_EPS)
    return (y * g.astype(jnp.float32)).astype(x.dtype)


def fox_attention(h, w_in, b_f, w_out):
    B, S, D = h.shape
    proj = h @ w_in
    q = proj[..., :D].reshape(B, S, N_HEADS, HEAD_DIM).transpose(0, 2, 1, 3)
    k = proj[..., D:2 * D].reshape(B, S, N_HEADS, HEAD_DIM).transpose(0, 2, 1, 3)
    v = proj[..., 2 * D:3 * D].reshape(B, S, N_HEADS, HEAD_DIM).transpose(0, 2, 1, 3)
    q = q * jnp.asarray(HEAD_DIM ** -0.5, q.dtype)
    f_logit = (proj[..., 3 * D:] + b_f).astype(jnp.float32)
    log_f = jax.nn.log_sigmoid(f_logit)
    c = lax.cumsum(log_f, axis=1).transpose(0, 2, 1)
    k_pos = jnp.arange(S)

    def q_block(i):
        start = i * Q_BLOCK
        qb = lax.dynamic_slice_in_dim(q, start, Q_BLOCK, axis=2)
        cb = lax.dynamic_slice_in_dim(c, start, Q_BLOCK, axis=2)
        s = jnp.einsum('bhqd,bhkd->bhqk', qb, k, preferred_element_type=jnp.float32)
        s = s + cb[..., :, None] - c[..., None, :]
        q_pos = start + jnp.arange(Q_BLOCK)
        s = jnp.where(k_pos[None, :] <= q_pos[:, None], s, NEG_INF)
        pr = jax.nn.softmax(s, axis=-1)
        return jnp.einsum('bhqk,bhkd->bhqd', pr.astype(v.dtype), v)

    o = lax.map(q_block, jnp.arange(S // Q_BLOCK))
    o = o.transpose(1, 0, 3, 2, 4).reshape(B, S, D)
    return o @ w_out


def short_conv(h, w_in, conv_w, w_out):
    D = h.shape[-1]
    proj = h @ w_in
    b_gate = proj[..., :D]
    c_gate = proj[..., D:2 * D]
    u = proj[..., 2 * D:]
    z = c_gate * u
    zc = lax.conv_general_dilated(
        z, conv_w[:, None, :].astype(z.dtype), window_strides=(1,),
        padding=[(CONV_WIDTH - 1, 0)],
        dimension_numbers=('NWC', 'WIO', 'NWC'),
        feature_group_count=D)
    return (b_gate * zc) @ w_out


def sq_relu_mlp(h, w_up, w_down):
    return jnp.square(jax.nn.relu(h @ w_up)) @ w_down


def _fwd_setup_inputs(seed: int = 0) -> dict:
    key = jax.random.key(seed)
    ks = jax.random.split(key, 14)
    f32 = jnp.float32
    nrm = lambda k, shape, scale: jax.random.normal(k, shape, f32) * scale
    x = jax.random.normal(ks[0], (BATCH, SEQ, D_MODEL), f32)
    p = jax.random.normal(ks[1], (DEPTH, BATCH, SEQ, PLE_DIM), f32)
    norm_g = 1.0 + nrm(ks[2], (DEPTH, 6, D_MODEL), 0.05)
    w_attn_in = nrm(ks[3], (N_ATTN, D_MODEL, 3 * D_MODEL + N_HEADS), D_MODEL ** -0.5)
    b_forget = 2.0 + nrm(ks[4], (N_ATTN, N_HEADS), 0.5)
    w_attn_out = nrm(ks[5], (N_ATTN, D_MODEL, D_MODEL), D_MODEL ** -0.5)
    w_conv_in = nrm(ks[6], (N_CONV, D_MODEL, 3 * D_MODEL), D_MODEL ** -0.5)
    conv_w = nrm(ks[7], (N_CONV, CONV_WIDTH, D_MODEL), CONV_WIDTH ** -0.5)
    w_conv_out = nrm(ks[8], (N_CONV, D_MODEL, D_MODEL), D_MODEL ** -0.5)
    w_mlp_up = nrm(ks[9], (DEPTH, D_MODEL, D_FF), D_MODEL ** -0.5)
    w_mlp_down = nrm(ks[10], (DEPTH, D_FF, D_MODEL), D_FF ** -0.5)
    w_ple_proj = nrm(ks[11], (DEPTH, PLE_DIM, D_MODEL), PLE_DIM ** -0.5)
    w_ple_gate = nrm(ks[12], (DEPTH, D_MODEL, D_MODEL), D_MODEL ** -0.5)
    return {"x": x, "p": p, "norm_g": norm_g, "w_attn_in": w_attn_in,
            "b_forget": b_forget, "w_attn_out": w_attn_out, "w_conv_in": w_conv_in,
            "conv_w": conv_w, "w_conv_out": w_conv_out, "w_mlp_up": w_mlp_up,
            "w_mlp_down": w_mlp_down, "w_ple_proj": w_ple_proj, "w_ple_gate": w_ple_gate}


def _fwd_reference(x, p, norm_g, w_attn_in, b_forget, w_attn_out, w_conv_in, conv_w,
              w_conv_out, w_mlp_up, w_mlp_down, w_ple_proj, w_ple_gate):
    for i in range(DEPTH):
        g = norm_g[i]
        hn = rmsnorm(x, g[0])
        j = i // N_MIXERS
        if i % N_MIXERS == 0:
            m = fox_attention(hn, w_attn_in[j], b_forget[j], w_attn_out[j])
        else:
            m = short_conv(hn, w_conv_in[j], conv_w[j], w_conv_out[j])
        x = x + rmsnorm(m, g[1])
        f = sq_relu_mlp(rmsnorm(x, g[2]), w_mlp_up[i], w_mlp_down[i])
        x = x + rmsnorm(f, g[3])
        gate = jax.nn.sigmoid(rmsnorm(x, g[4]) @ w_ple_gate[i])
        e = (p[i] @ w_ple_proj[i]) * gate
        x = x + rmsnorm(e, g[5])
    return x


import jax as _jax
import jax.numpy as _jnp

TWIN_FORMAT = 'train_step'
FWD_PARAMS = ['x', 'p', 'norm_g', 'w_attn_in', 'b_forget', 'w_attn_out', 'w_conv_in', 'conv_w', 'w_conv_out', 'w_mlp_up', 'w_mlp_down', 'w_ple_proj', 'w_ple_gate']
TWIN_WEIGHTS = ['norm_g', 'w_attn_in', 'b_forget', 'w_attn_out', 'w_conv_in', 'conv_w', 'w_conv_out', 'w_mlp_up', 'w_mlp_down', 'w_ple_proj', 'w_ple_gate']
TWIN_DIFF_INPUT = 'x'
TWIN_INPUTS = ['x', 'p', 'norm_g', 'w_attn_in', 'b_forget', 'w_attn_out', 'w_conv_in', 'conv_w', 'w_conv_out', 'w_mlp_up', 'w_mlp_down', 'w_ple_proj', 'w_ple_gate', 'loss_target', 'm_norm_g', 'm_w_attn_in', 'm_b_forget', 'm_w_attn_out', 'm_w_conv_in', 'm_conv_w', 'm_w_conv_out', 'm_w_mlp_up', 'm_w_mlp_down', 'm_w_ple_proj', 'm_w_ple_gate', 'v_norm_g', 'v_w_attn_in', 'v_b_forget', 'v_w_attn_out', 'v_w_conv_in', 'v_conv_w', 'v_w_conv_out', 'v_w_mlp_up', 'v_w_mlp_down', 'v_w_ple_proj', 'v_w_ple_gate']
TWIN_OUTPUTS = ['loss', 'grad_x', 'grad_norm_g', 'grad_w_attn_in', 'grad_b_forget', 'grad_w_attn_out', 'grad_w_conv_in', 'grad_conv_w', 'grad_w_conv_out', 'grad_w_mlp_up', 'grad_w_mlp_down', 'grad_w_ple_proj', 'grad_w_ple_gate', 'delta_norm_g', 'delta_w_attn_in', 'delta_b_forget', 'delta_w_attn_out', 'delta_w_conv_in', 'delta_conv_w', 'delta_w_conv_out', 'delta_w_mlp_up', 'delta_w_mlp_down', 'delta_w_ple_proj', 'delta_w_ple_gate', 'new_m_norm_g', 'new_m_w_attn_in', 'new_m_b_forget', 'new_m_w_attn_out', 'new_m_w_conv_in', 'new_m_conv_w', 'new_m_w_conv_out', 'new_m_w_mlp_up', 'new_m_w_mlp_down', 'new_m_w_ple_proj', 'new_m_w_ple_gate', 'new_v_norm_g', 'new_v_w_attn_in', 'new_v_b_forget', 'new_v_w_attn_out', 'new_v_w_conv_in', 'new_v_conv_w', 'new_v_w_conv_out', 'new_v_w_mlp_up', 'new_v_w_mlp_down', 'new_v_w_ple_proj', 'new_v_w_ple_gate']
TWIN_LEAF_KINDS = {'loss': 'loss', 'grad_x': 'grad_x', 'grad_norm_g': 'grad_w', 'grad_w_attn_in': 'grad_w', 'grad_b_forget': 'grad_w', 'grad_w_attn_out': 'grad_w', 'grad_w_conv_in': 'grad_w', 'grad_conv_w': 'grad_w', 'grad_w_conv_out': 'grad_w', 'grad_w_mlp_up': 'grad_w', 'grad_w_mlp_down': 'grad_w', 'grad_w_ple_proj': 'grad_w', 'grad_w_ple_gate': 'grad_w', 'delta_norm_g': 'delta_w', 'delta_w_attn_in': 'delta_w', 'delta_b_forget': 'delta_w', 'delta_w_attn_out': 'delta_w', 'delta_w_conv_in': 'delta_w', 'delta_conv_w': 'delta_w', 'delta_w_conv_out': 'delta_w', 'delta_w_mlp_up': 'delta_w', 'delta_w_mlp_down': 'delta_w', 'delta_w_ple_proj': 'delta_w', 'delta_w_ple_gate': 'delta_w', 'new_m_norm_g': 'new_m', 'new_m_w_attn_in': 'new_m', 'new_m_b_forget': 'new_m', 'new_m_w_attn_out': 'new_m', 'new_m_w_conv_in': 'new_m', 'new_m_conv_w': 'new_m', 'new_m_w_conv_out': 'new_m', 'new_m_w_mlp_up': 'new_m', 'new_m_w_mlp_down': 'new_m', 'new_m_w_ple_proj': 'new_m', 'new_m_w_ple_gate': 'new_m', 'new_v_norm_g': 'new_v', 'new_v_w_attn_in': 'new_v', 'new_v_b_forget': 'new_v', 'new_v_w_attn_out': 'new_v', 'new_v_w_conv_in': 'new_v', 'new_v_conv_w': 'new_v', 'new_v_w_conv_out': 'new_v', 'new_v_w_mlp_up': 'new_v', 'new_v_w_mlp_down': 'new_v', 'new_v_w_ple_proj': 'new_v', 'new_v_w_ple_gate': 'new_v'}


def _forward(args):
    return _fwd_reference(*[args[k] for k in FWD_PARAMS])


def _output_shape():
    def fwd():
        inp = _fwd_setup_inputs(0)
        return _fwd_reference(*[inp[k] for k in FWD_PARAMS])
    out = _jax.eval_shape(fwd)
    return out.shape, out.dtype

N_MICROBATCH = 1
ADAM_LR = 0.001
ADAM_B1 = 0.9
ADAM_B2 = 0.999
ADAM_EPS = 1e-08
ADAM_WD = 0.01
ADAM_STEP = 10
PER_EXAMPLE_BATCH_AXIS = {'x': 0, 'p': 1, 'loss_target': 0}
SHARED_INPUTS = []
_WEIGHT_DTYPES = {'norm_g': _jnp.float32, 'w_attn_in': _jnp.float32, 'b_forget': _jnp.float32, 'w_attn_out': _jnp.float32, 'w_conv_in': _jnp.float32, 'conv_w': _jnp.float32, 'w_conv_out': _jnp.float32, 'w_mlp_up': _jnp.float32, 'w_mlp_down': _jnp.float32, 'w_ple_proj': _jnp.float32, 'w_ple_gate': _jnp.float32}
MOMENT_SCALE = {'norm_g': 9.670063e+01, 'w_attn_in': 1.690099e+01, 'b_forget': 4.493799e+01, 'w_attn_out': 3.300941e+01, 'w_conv_in': 2.334884e+00, 'conv_w': 2.592977e+00, 'w_conv_out': 2.622318e+00, 'w_mlp_up': 7.449323e+00, 'w_mlp_down': 6.074366e+01, 'w_ple_proj': 2.469473e+00, 'w_ple_gate': 9.559665e-01}


def _to_microbatches(a, axis):
    t = _jnp.moveaxis(a, axis, 0)
    t = t.reshape((N_MICROBATCH, t.shape[0] // N_MICROBATCH) + t.shape[1:])
    return _jnp.moveaxis(t, 1, axis + 1)


def setup_inputs(seed: int = 0) -> dict:
    inp = _fwd_setup_inputs(seed)
    key = _jax.random.fold_in(_jax.random.key(seed), 7919)
    shape, _ = _output_shape()
    out = dict(inp)
    out["loss_target"] = _jax.random.normal(_jax.random.fold_in(key, 0), shape, _jnp.float32)
    for i, name in enumerate(TWIN_WEIGHTS):
        w = inp[name].astype(_jnp.float32)
        if MOMENT_SCALE is None:
            s = _jnp.sqrt(_jnp.mean(_jnp.square(w)) + 1e-30)
        else:
            s = MOMENT_SCALE[name]
        km, kv = _jax.random.split(_jax.random.fold_in(key, i + 1))
        out[name] = w
        out["m_" + name] = s * _jax.random.normal(km, w.shape, _jnp.float32)
        out["v_" + name] = (s * s) * _jax.random.uniform(kv, w.shape, _jnp.float32, 0.5, 1.5)
    if N_MICROBATCH > 1:
        for name, axis in PER_EXAMPLE_BATCH_AXIS.items():
            out[name] = _to_microbatches(out[name], axis)
    return {'x': out['x'], 'p': out['p'], 'norm_g': out['norm_g'], 'w_attn_in': out['w_attn_in'], 'b_forget': out['b_forget'], 'w_attn_out': out['w_attn_out'], 'w_conv_in': out['w_conv_in'], 'conv_w': out['conv_w'], 'w_conv_out': out['w_conv_out'], 'w_mlp_up': out['w_mlp_up'], 'w_mlp_down': out['w_mlp_down'], 'w_ple_proj': out['w_ple_proj'], 'w_ple_gate': out['w_ple_gate'], 'loss_target': out['loss_target'], 'm_norm_g': out['m_norm_g'], 'm_w_attn_in': out['m_w_attn_in'], 'm_b_forget': out['m_b_forget'], 'm_w_attn_out': out['m_w_attn_out'], 'm_w_conv_in': out['m_w_conv_in'], 'm_conv_w': out['m_conv_w'], 'm_w_conv_out': out['m_w_conv_out'], 'm_w_mlp_up': out['m_w_mlp_up'], 'm_w_mlp_down': out['m_w_mlp_down'], 'm_w_ple_proj': out['m_w_ple_proj'], 'm_w_ple_gate': out['m_w_ple_gate'], 'v_norm_g': out['v_norm_g'], 'v_w_attn_in': out['v_w_attn_in'], 'v_b_forget': out['v_b_forget'], 'v_w_attn_out': out['v_w_attn_out'], 'v_w_conv_in': out['v_w_conv_in'], 'v_conv_w': out['v_conv_w'], 'v_w_conv_out': out['v_w_conv_out'], 'v_w_mlp_up': out['v_w_mlp_up'], 'v_w_mlp_down': out['v_w_mlp_down'], 'v_w_ple_proj': out['v_w_ple_proj'], 'v_w_ple_gate': out['v_w_ple_gate']}


def _loss(weights, diff, rest, loss_target):
    with _jax.named_scope("forward"):
        args = {**rest, TWIN_DIFF_INPUT: diff, **{k: w.astype(_WEIGHT_DTYPES[k]) for k, w in weights.items()}}
        y = _forward(args)
    with _jax.named_scope("loss_head"):
        err = _jnp.square(y.astype(_jnp.float32) - loss_target)
        return 0.5 * _jnp.sum(_jnp.mean(err, axis=-1)) if err.ndim else 0.5 * err


def _adamw(w, g, m, v):
    m = ADAM_B1 * m + (1.0 - ADAM_B1) * g
    v = ADAM_B2 * v + (1.0 - ADAM_B2) * _jnp.square(g)
    m_hat = m / (1.0 - ADAM_B1 ** ADAM_STEP)
    v_hat = v / (1.0 - ADAM_B2 ** ADAM_STEP)
    delta = -ADAM_LR * (m_hat / (_jnp.sqrt(v_hat) + ADAM_EPS) + ADAM_WD * w)
    return delta, m, v


def reference(x, p, norm_g, w_attn_in, b_forget, w_attn_out, w_conv_in, conv_w, w_conv_out, w_mlp_up, w_mlp_down, w_ple_proj, w_ple_gate, loss_target, m_norm_g, m_w_attn_in, m_b_forget, m_w_attn_out, m_w_conv_in, m_conv_w, m_w_conv_out, m_w_mlp_up, m_w_mlp_down, m_w_ple_proj, m_w_ple_gate, v_norm_g, v_w_attn_in, v_b_forget, v_w_attn_out, v_w_conv_in, v_conv_w, v_w_conv_out, v_w_mlp_up, v_w_mlp_down, v_w_ple_proj, v_w_ple_gate):
    given = dict(x=x, p=p, norm_g=norm_g, w_attn_in=w_attn_in, b_forget=b_forget, w_attn_out=w_attn_out, w_conv_in=w_conv_in, conv_w=conv_w, w_conv_out=w_conv_out, w_mlp_up=w_mlp_up, w_mlp_down=w_mlp_down, w_ple_proj=w_ple_proj, w_ple_gate=w_ple_gate, loss_target=loss_target, m_norm_g=m_norm_g, m_w_attn_in=m_w_attn_in, m_b_forget=m_b_forget, m_w_attn_out=m_w_attn_out, m_w_conv_in=m_w_conv_in, m_conv_w=m_conv_w, m_w_conv_out=m_w_conv_out, m_w_mlp_up=m_w_mlp_up, m_w_mlp_down=m_w_mlp_down, m_w_ple_proj=m_w_ple_proj, m_w_ple_gate=m_w_ple_gate, v_norm_g=v_norm_g, v_w_attn_in=v_w_attn_in, v_b_forget=v_b_forget, v_w_attn_out=v_w_attn_out, v_w_conv_in=v_w_conv_in, v_conv_w=v_conv_w, v_w_conv_out=v_w_conv_out, v_w_mlp_up=v_w_mlp_up, v_w_mlp_down=v_w_mlp_down, v_w_ple_proj=v_w_ple_proj, v_w_ple_gate=v_w_ple_gate)
    weights = {n: given[n] for n in TWIN_WEIGHTS}
    shared = {n: given[n] for n in SHARED_INPUTS}
    per_example = {n: given[n] for n in ['x', 'p']}
    grad_fn = _jax.value_and_grad(_loss, argnums=(0, 1))

    def one_microbatch(ex, loss_target):
        ex = dict(ex)
        diff = ex.pop(TWIN_DIFF_INPUT)
        return grad_fn(weights, diff, {**shared, **ex}, loss_target)

    if N_MICROBATCH == 1:
        loss, (grad_w, grad_x) = one_microbatch(per_example, given["loss_target"])
    else:
        def body(carry, xs):
            loss_sum, grad_sum = carry
            l_k, (gw_k, gx_k) = one_microbatch(xs[0], xs[1])
            with _jax.named_scope("update"):
                return (loss_sum + l_k, _jax.tree.map(_jnp.add, grad_sum, gw_k)), gx_k

        init = (_jnp.zeros((), _jnp.float32), _jax.tree.map(_jnp.zeros_like, weights))
        (loss, grad_w), grad_x = _jax.lax.scan(body, init, (per_example, given["loss_target"]))
    with _jax.named_scope("update"):
        delta_w, new_m, new_v = {}, {}, {}
        for n in TWIN_WEIGHTS:
            delta_w[n], new_m[n], new_v[n] = _adamw(weights[n], grad_w[n], given["m_" + n], given["v_" + n])
    return (loss, grad_x, *[grad_w[n] for n in TWIN_WEIGHTS], *[delta_w[n] for n in TWIN_WEIGHTS],
            *[new_m[n] for n in TWIN_WEIGHTS], *[new_v[n] for n in TWIN_WEIGHTS])
```

```python
import functools

import numpy as np
import jax
import jax.numpy as jnp
from jax import lax
from jax.experimental import pallas as pl
from jax.experimental.pallas import tpu as pltpu

F32 = jnp.float32
BF16 = jnp.bfloat16
MESH = pl.DeviceIdType.MESH

RMS_EPS = 1e-6
NEG_INF = -1e30
ADAM_LR = 0.001
ADAM_B1 = 0.9
ADAM_B2 = 0.999
ADAM_EPS = 1e-08
ADAM_WD = 0.01
ADAM_STEP = 10

N_CHIPS = 4
LANES = 128
ROW_TILE_FWD = 512
ROW_TILE_BWD = 256
ATTN_TILE = 512
COL_CHUNK = 512
PACK_COLS = 1024
PACK_ROW_TILE = 256
PACK_ROW_ALIGN = 2 * PACK_ROW_TILE
VMEM_LIMIT = 56 * 1024 * 1024


def _tile(n, pref):
    return pref if n % pref == 0 else n


def _dot(a, b):
    return jnp.dot(a, b, preferred_element_type=F32)


def _dot_tn(a, b):
    return lax.dot_general(a, b, (((0,), (0,)), ((), ())), preferred_element_type=F32)


def _dot_nt(a, b):
    return lax.dot_general(a, b, (((1,), (1,)), ((), ())), preferred_element_type=F32)


def _rms_fwd(x, g):
    r = lax.rsqrt(jnp.mean(x * x, axis=-1, keepdims=True) + RMS_EPS)
    return (x * r) * g


def _rms_bwd(x, g, dy):
    r = lax.rsqrt(jnp.mean(x * x, axis=-1, keepdims=True) + RMS_EPS)
    xh = x * r
    dyg = dy * g
    dx = r * (dyg - xh * jnp.mean(dyg * xh, axis=-1, keepdims=True))
    return dx, jnp.sum(dy * xh, axis=0, keepdims=True)


def _params(n_axes=1):
    return pltpu.CompilerParams(dimension_semantics=("arbitrary",) * n_axes, vmem_limit_bytes=VMEM_LIMIT)


def _row_call(name, body, n_rows, tm, row_ins, const_ins, row_outs, acc_outs=(), scratch=(), reverse=False):
    nb = n_rows // tm
    rmap = (lambda i: (nb - 1 - i, 0)) if reverse else (lambda i: (i, 0))

    def whole(shape):
        nd = len(shape)
        return pl.BlockSpec(tuple(shape), lambda i: (0,) * nd)

    in_specs = [pl.BlockSpec((tm, a.shape[1]), rmap) for a in row_ins] + [whole(a.shape) for a in const_ins]
    out_shape = [jax.ShapeDtypeStruct((n_rows, w), dt) for (w, dt) in row_outs]
    out_shape += [jax.ShapeDtypeStruct(tuple(s), dt) for (s, dt) in acc_outs]
    out_specs = [pl.BlockSpec((tm, w), rmap) for (w, _) in row_outs] + [whole(s) for (s, _) in acc_outs]
    return pl.pallas_call(
        body, name=name, grid=(nb,), in_specs=in_specs, out_specs=out_specs, out_shape=out_shape,
        scratch_shapes=list(scratch), compiler_params=_params(),
    )(*row_ins, *const_ins)


def _attn_in_fwd(x, g, wqkv, wf, bf, qscale):
    T, D = x.shape
    tm, ch = _tile(T, ROW_TILE_FWD), _tile(D, COL_CHUNK)

    def body(x_ref, g_ref, w_ref, wf_ref, bf_ref, q_ref, k_ref, v_ref, fl_ref, lf_ref):
        h = _rms_fwd(x_ref[...], g_ref[...]).astype(BF16)
        for part, o_ref in enumerate((q_ref, k_ref, v_ref)):
            for n0 in range(0, D, ch):
                r = _dot(h, w_ref[:, part * D + n0:part * D + n0 + ch])
                if part == 0:
                    r = r * qscale
                o_ref[:, n0:n0 + ch] = r.astype(BF16)
        fl = _dot(h, wf_ref[...]) + bf_ref[...]
        fl_ref[...] = fl
        lf_ref[...] = jnp.minimum(fl, 0.0) - jnp.log1p(jnp.exp(-jnp.abs(fl)))

    return _row_call("attn_in_fwd", body, T, tm, [x], [g, wqkv, wf, bf],
                     [(D, BF16), (D, BF16), (D, BF16), (LANES, F32), (LANES, F32)])


def _conv_in_fwd(x, g, w, cw):
    T, D = x.shape
    tm, ch = _tile(T, ROW_TILE_FWD), _tile(D, COL_CHUNK)

    def body(x_ref, g_ref, w_ref, cw_ref, b_ref, c_ref, u_ref, zc_ref, y_ref, tail_ref):
        i = pl.program_id(0)

        @pl.when(i == 0)
        def _():
            tail_ref[...] = jnp.zeros_like(tail_ref)

        h = _rms_fwd(x_ref[...], g_ref[...]).astype(BF16)
        for part, o_ref in enumerate((b_ref, c_ref, u_ref)):
            for n0 in range(0, D, ch):
                o_ref[:, n0:n0 + ch] = _dot(h, w_ref[:, part * D + n0:part * D + n0 + ch])
        z = c_ref[...] * u_ref[...]
        row = lax.broadcasted_iota(jnp.int32, (tm, 1), 0)
        t6, t7 = tail_ref[6:7, :], tail_ref[7:8, :]
        z1 = jnp.where(row == 0, t7, pltpu.roll(z, 1, axis=0))
        z2 = jnp.where(row == 0, t6, jnp.where(row == 1, t7, pltpu.roll(z, 2, axis=0)))
        zc = cw_ref[0:1, :] * z2 + cw_ref[1:2, :] * z1 + cw_ref[2:3, :] * z
        zc_ref[...] = zc
        y_ref[...] = (b_ref[...] * zc).astype(BF16)
        tail_ref[...] = z[tm - 8:tm, :]

    return _row_call("conv_in_fwd", body, T, tm, [x], [g, w, cw],
                     [(D, F32), (D, F32), (D, F32), (D, F32), (D, BF16)], scratch=[pltpu.VMEM((8, D), F32)])


def _mlp_up_fwd(x, g, w):
    T, D = x.shape
    F = w.shape[1]
    tm, ch = _tile(T, ROW_TILE_FWD), _tile(F, COL_CHUNK)

    def body(x_ref, g_ref, w_ref, up_ref, a_ref):
        h = _rms_fwd(x_ref[...], g_ref[...]).astype(BF16)
        for n0 in range(0, F, ch):
            r = _dot(h, w_ref[:, n0:n0 + ch])
            up_ref[:, n0:n0 + ch] = r.astype(BF16)
            rl = jnp.maximum(r, 0.0)
            a_ref[:, n0:n0 + ch] = (rl * rl).astype(BF16)

    return _row_call("mlp_up_fwd", body, T, tm, [x], [g, w], [(F, BF16), (F, BF16)])


def _out_proj_fwd(name, a, w, g, x):
    T, D = x.shape
    tm = _tile(T, ROW_TILE_FWD)

    def body(a_ref, x_ref, w_ref, g_ref, m_ref, xn_ref):
        m = _dot(a_ref[...], w_ref[...])
        m_ref[...] = m
        xn_ref[...] = x_ref[...] + _rms_fwd(m, g_ref[...])

    return _row_call(name, body, T, tm, [a, x], [w, g], [(D, F32), (D, F32)])


def _ple_fwd(x, p, g4, g5, wg, wp):
    T, D = x.shape
    tm = _tile(T, ROW_TILE_FWD)

    def body(x_ref, p_ref, g4_ref, g5_ref, wg_ref, wp_ref, gl_ref, pe_ref, xn_ref):
        xv = x_ref[...]
        gl = _dot(_rms_fwd(xv, g4_ref[...]).astype(BF16), wg_ref[...])
        pe = _dot(p_ref[...].astype(BF16), wp_ref[...])
        gl_ref[...] = gl
        pe_ref[...] = pe
        e = pe * (1.0 / (1.0 + jnp.exp(-gl)))
        xn_ref[...] = xv + _rms_fwd(e, g5_ref[...])

    return _row_call("ple_fwd", body, T, tm, [x, p], [g4, g5, wg, wp], [(D, F32), (D, F32), (D, F32)])


def _loss_fwd_bwd(y, target):
    T, D = y.shape
    tm = _tile(T, ROW_TILE_FWD)

    def body(y_ref, t_ref, dy_ref, loss_ref):
        @pl.when(pl.program_id(0) == 0)
        def _():
            loss_ref[...] = jnp.zeros_like(loss_ref)

        err = y_ref[...] - t_ref[...]
        dy_ref[...] = err * (1.0 / D)
        part = 0.5 * jnp.sum(jnp.mean(err * err, axis=-1, keepdims=True), axis=0, keepdims=True)
        loss_ref[...] += jnp.broadcast_to(part, loss_ref.shape)

    return _row_call("loss", body, T, tm, [y, target], [], [(D, F32)], acc_outs=[((8, LANES), F32)])


def _out_proj_bwd(name, dres, m, g, a_ins, wt, mode):
    T, D = dres.shape
    Ka = wt.shape[1]
    tm, ch = _tile(T, ROW_TILE_BWD), _tile(Ka, COL_CHUNK)
    out_dt = F32 if mode == "plain_f32" else BF16

    def body(dres_ref, m_ref, *rest):
        a_ref = rest[0]
        up_ref = rest[1] if mode == "relu2" else None
        k = len(a_ins)
        g_ref, wt_ref, da_ref, dw_ref, dg_ref = rest[k:k + 5]

        @pl.when(pl.program_id(0) == 0)
        def _():
            dw_ref[...] = jnp.zeros_like(dw_ref)
            dg_ref[...] = jnp.zeros_like(dg_ref)

        dm, dgp = _rms_bwd(m_ref[...], g_ref[...], dres_ref[...])
        dg_ref[...] += dgp
        dmb = dm.astype(BF16)
        for n0 in range(0, Ka, ch):
            da = _dot(dmb, wt_ref[:, n0:n0 + ch])
            if mode == "relu2":
                da = da * (2.0 * jnp.maximum(up_ref[:, n0:n0 + ch].astype(F32), 0.0))
            da_ref[:, n0:n0 + ch] = da.astype(out_dt)
            dw_ref[n0:n0 + ch, :] += _dot_tn(a_ref[:, n0:n0 + ch], dmb)

    return _row_call(name, body, T, tm, [dres, m] + list(a_ins), [g, wt],
                     [(Ka, out_dt)], acc_outs=[((Ka, D), F32), ((1, D), F32)])


def _in_proj_bwd(name, pieces, wts, x, g, dres):
    T, D = x.shape
    tm = _tile(T, ROW_TILE_BWD)
    k = len(pieces)
    widths = [pc.shape[1] for pc in pieces]

    def body(*refs):
        pc_refs = refs[:k]
        x_ref, dres_ref, g_ref = refs[k:k + 3]
        wt_refs = refs[k + 3:2 * k + 3]
        dx_ref = refs[2 * k + 3]
        dw_refs = refs[2 * k + 4:3 * k + 4]
        dg_ref = refs[3 * k + 4]

        @pl.when(pl.program_id(0) == 0)
        def _():
            for r in dw_refs:
                r[...] = jnp.zeros_like(r)
            dg_ref[...] = jnp.zeros_like(dg_ref)

        xv, gv = x_ref[...], g_ref[...]
        hb = _rms_fwd(xv, gv).astype(BF16)
        dh = None
        for pc_ref, wt_ref, dw_ref, n in zip(pc_refs, wt_refs, dw_refs, widths):
            d = _dot(pc_ref[...], wt_ref[...])
            dh = d if dh is None else dh + d
            ch = _tile(n, COL_CHUNK)
            for n0 in range(0, n, ch):
                dw_ref[:, n0:n0 + ch] += _dot_tn(hb, pc_ref[:, n0:n0 + ch])
        dxn, dgp = _rms_bwd(xv, gv, dh)
        dg_ref[...] += dgp
        dx_ref[...] = dres_ref[...] + dxn

    return _row_call(name, body, T, tm, list(pieces) + [x, dres], [g] + list(wts), [(D, F32)],
                     acc_outs=[((D, n), F32) for n in widths] + [((1, D), F32)])


def _conv_bwd(dy, b, c, u, zc, cw):
    T, D = dy.shape
    tm = _tile(T, ROW_TILE_BWD)

    def body(dy_ref, b_ref, c_ref, u_ref, zc_ref, cw_ref, db_ref, dc_ref, du_ref, dcw_ref, head_ref):
        @pl.when(pl.program_id(0) == 0)
        def _():
            head_ref[...] = jnp.zeros_like(head_ref)
            dcw_ref[...] = jnp.zeros_like(dcw_ref)

        dyv, cv, uv = dy_ref[...], c_ref[...], u_ref[...]
        db_ref[...] = (dyv * zc_ref[...]).astype(BF16)
        dzc = dyv * b_ref[...]
        row = lax.broadcasted_iota(jnp.int32, (tm, 1), 0)
        h0, h1 = head_ref[0:1, :], head_ref[1:2, :]
        d1 = jnp.where(row == tm - 1, h0, pltpu.roll(dzc, tm - 1, axis=0))
        d2 = jnp.where(row == tm - 1, h1, jnp.where(row == tm - 2, h0, pltpu.roll(dzc, tm - 2, axis=0)))
        dz = cw_ref[2:3, :] * dzc + cw_ref[1:2, :] * d1 + cw_ref[0:1, :] * d2
        dc_ref[...] = (dz * uv).astype(BF16)
        du_ref[...] = (dz * cv).astype(BF16)
        z = cv * uv
        dcw_ref[0:1, :] += jnp.sum(d2 * z, axis=0, keepdims=True)
        dcw_ref[1:2, :] += jnp.sum(d1 * z, axis=0, keepdims=True)
        dcw_ref[2:3, :] += jnp.sum(dzc * z, axis=0, keepdims=True)
        head_ref[...] = dzc[0:8, :]

    return _row_call("conv_bwd", body, T, tm, [dy, b, c, u, zc], [cw], [(D, BF16), (D, BF16), (D, BF16)],
                     acc_outs=[((8, D), F32)], scratch=[pltpu.VMEM((8, D), F32)], reverse=True)


def _ple_bwd(dres, x, p, gl, pe, g4, g5, wgt):
    T, D = x.shape
    P = p.shape[1]
    tm = _tile(T, ROW_TILE_BWD)

    def body(dres_ref, x_ref, p_ref, gl_ref, pe_ref, g4_ref, g5_ref, wgt_ref,
             dx_ref, dwp_ref, dwg_ref, dg4_ref, dg5_ref):
        @pl.when(pl.program_id(0) == 0)
        def _():
            for r in (dwp_ref, dwg_ref, dg4_ref, dg5_ref):
                r[...] = jnp.zeros_like(r)

        dr, xv, pe_v = dres_ref[...], x_ref[...], pe_ref[...]
        gate = 1.0 / (1.0 + jnp.exp(-gl_ref[...]))
        de, dg5p = _rms_bwd(pe_v * gate, g5_ref[...], dr)
        dg5_ref[...] += dg5p
        dpe = (de * gate).astype(BF16)
        dgl = (de * pe_v * gate * (1.0 - gate)).astype(BF16)
        dwp_ref[...] += _dot_tn(p_ref[...].astype(BF16), dpe)
        g4v = g4_ref[...]
        dwg_ref[...] += _dot_tn(_rms_fwd(xv, g4v).astype(BF16), dgl)
        dxn, dg4p = _rms_bwd(xv, g4v, _dot(dgl, wgt_ref[...]))
        dg4_ref[...] += dg4p
        dx_ref[...] = dr + dxn

    return _row_call("ple_bwd", body, T, tm, [dres, x, p, gl, pe], [g4, g5, wgt], [(D, F32)],
                     acc_outs=[((P, D), F32), ((D, D), F32), ((1, D), F32), ((1, D), F32)])


def _scan_lanes(v, reverse):
    n = v.shape[1]
    lane = lax.broadcasted_iota(jnp.int32, v.shape, 1)
    s = 1
    while s < n:
        if reverse:
            v = v + jnp.where(lane < n - s, pltpu.roll(v, n - s, axis=1), 0.0)
        else:
            v = v + jnp.where(lane >= s, pltpu.roll(v, s, axis=1), 0.0)
        s *= 2
    return v


def _gate_cumsum(lf_t):
    def body(lf_ref, c_ref):
        c_ref[...] = _scan_lanes(lf_ref[...], reverse=False)

    return pl.pallas_call(body, name="gate_cumsum", out_shape=jax.ShapeDtypeStruct(lf_t.shape, F32),
                          compiler_params=pltpu.CompilerParams(vmem_limit_bytes=VMEM_LIMIT))(lf_t)


def _gate_bwd(drow_t, dcol_t, fl_t):
    H = fl_t.shape[0]

    def body(dr_ref, dc_ref, fl_ref, dfl_ref, dbf_ref):
        dlf = _scan_lanes(dr_ref[...] - dc_ref[...], reverse=True)
        dfl = dlf * (1.0 / (1.0 + jnp.exp(fl_ref[...])))
        dfl_ref[...] = dfl
        dbf_ref[...] = jnp.sum(dfl, axis=1, keepdims=True)

    return pl.pallas_call(
        body, name="gate_bwd",
        out_shape=(jax.ShapeDtypeStruct(fl_t.shape, F32), jax.ShapeDtypeStruct((H, 1), F32)),
        compiler_params=pltpu.CompilerParams(vmem_limit_bytes=VMEM_LIMIT))(drow_t, dcol_t, fl_t)


def _causal_tables(nb, q_outer):
    a, b = [], []
    for o in range(nb):
        inner = range(o + 1) if q_outer else range(o, nb)
        for n in inner:
            a.append(o)
            b.append(n)
    return jnp.asarray(np.array(a, np.int32)), jnp.asarray(np.array(b, np.int32))


def _flash_fwd(q, k, v, c_tok, c_rows):
    T, D = q.shape
    H = c_tok.shape[1]
    Hp = H // 2
    W = D // Hp
    dh = W // 2
    t = _tile(T, ATTN_TILE)
    nb = T // t
    ii, jj = _causal_tables(nb, q_outer=True)
    n_steps = int(ii.shape[0])

    def body(ii_ref, jj_ref, q_ref, k_ref, v_ref, c_ref, cka_ref, ckb_ref, o_ref, lse_ref,
             qm_s, cq_s, m_s, l_s, acc_s):
        hp, n = pl.program_id(0), pl.program_id(1)
        i, j = ii_ref[n], jj_ref[n]
        lane = lax.broadcasted_iota(jnp.int32, (1, W), 1)
        in_a = lane < dh

        @pl.when(j == 0)
        def _():
            qv = q_ref[...]
            qm_s[0] = jnp.where(in_a, qv, jnp.zeros_like(qv))
            qm_s[1] = jnp.where(in_a, jnp.zeros_like(qv), qv)
            cb = c_ref[...]
            hl = lax.broadcasted_iota(jnp.int32, (1, H), 1)
            cq_s[0] = jnp.sum(jnp.where(hl == 2 * hp, cb, 0.0), axis=1, keepdims=True)
            cq_s[1] = jnp.sum(jnp.where(hl == 2 * hp + 1, cb, 0.0), axis=1, keepdims=True)
            m_s[...] = jnp.full(m_s.shape, NEG_INF, F32)
            l_s[...] = jnp.zeros_like(l_s)
            acc_s[...] = jnp.zeros_like(acc_s)

        kv, vv = k_ref[...], v_ref[...]
        qpos = i * t + lax.broadcasted_iota(jnp.int32, (t, 1), 0)
        kpos = j * t + lax.broadcasted_iota(jnp.int32, (1, t), 1)
        mask = kpos <= qpos
        alphas, pv = [], None
        for hh, ck_ref in enumerate((cka_ref, ckb_ref)):
            s = _dot_nt(qm_s[hh], kv) + (cq_s[hh] - ck_ref[...])
            s = jnp.where(mask, s, NEG_INF)
            m_prev = m_s[hh]
            m_new = jnp.maximum(m_prev, jnp.max(s, axis=1, keepdims=True))
            alpha = jnp.exp(m_prev - m_new)
            pr = jnp.exp(s - m_new)
            l_s[hh] = alpha * l_s[hh] + jnp.sum(pr, axis=1, keepdims=True)
            m_s[hh] = m_new
            vm = jnp.where(in_a, vv, jnp.zeros_like(vv)) if hh == 0 else jnp.where(in_a, jnp.zeros_like(vv), vv)
            d = _dot(pr.astype(BF16), vm)
            pv = d if pv is None else pv + d
            alphas.append(alpha)
        acc_s[...] = acc_s[...] * jnp.where(in_a, alphas[0], alphas[1]) + pv

        @pl.when(j == i)
        def _():
            o_ref[...] = (acc_s[...] * jnp.where(in_a, 1.0 / l_s[0], 1.0 / l_s[1])).astype(BF16)
            lse_a = m_s[0] + jnp.log(l_s[0])
            lse_b = m_s[1] + jnp.log(l_s[1])
            lse_ref[...] = jnp.where(lane == 0, lse_a, jnp.where(lane == 1, lse_b, 0.0))

    grid_spec = pltpu.PrefetchScalarGridSpec(
        num_scalar_prefetch=2, grid=(Hp, n_steps),
        in_specs=[
            pl.BlockSpec((t, W), lambda hp, n, ii, jj: (ii[n], hp)),
            pl.BlockSpec((t, W), lambda hp, n, ii, jj: (jj[n], hp)),
            pl.BlockSpec((t, W), lambda hp, n, ii, jj: (jj[n], hp)),
            pl.BlockSpec((t, H), lambda hp, n, ii, jj: (ii[n], 0)),
            pl.BlockSpec((None, 1, t), lambda hp, n, ii, jj: (2 * hp, 0, jj[n])),
            pl.BlockSpec((None, 1, t), lambda hp, n, ii, jj: (2 * hp + 1, 0, jj[n])),
        ],
        out_specs=[
            pl.BlockSpec((t, W), lambda hp, n, ii, jj: (ii[n], hp)),
            pl.BlockSpec((None, t, W), lambda hp, n, ii, jj: (hp, ii[n], 0)),
        ],
        scratch_shapes=[
            pltpu.VMEM((2, t, W), BF16), pltpu.VMEM((2, t, 1), F32), pltpu.VMEM((2, t, 1), F32),
            pltpu.VMEM((2, t, 1), F32), pltpu.VMEM((t, W), F32),
        ],
    )
    return pl.pallas_call(
        body, name="flash_fwd", grid_spec=grid_spec,
        out_shape=(jax.ShapeDtypeStruct((T, D), BF16), jax.ShapeDtypeStruct((Hp, T, W), F32)),
        compiler_params=_params(2),
    )(ii, jj, q, k, v, c_tok, c_rows, c_rows)


def _flash_bwd(q, k, v, do, o, lse, c_tok, c_rows, qscale):
    T, D = q.shape
    H = c_tok.shape[1]
    Hp = H // 2
    W = D // Hp
    dh = W // 2
    t = _tile(T, ATTN_TILE)
    nb = T // t
    jj, ii = _causal_tables(nb, q_outer=False)
    n_steps = int(ii.shape[0])

    def body(jj_ref, ii_ref, q_ref, k_ref, v_ref, do_ref, o_ref, lse_ref, c_ref, cka_ref, ckb_ref,
             dq_ref, dk_ref, dv_ref, dca_ref, dcb_ref, drow_ref, dq_acc, dk_acc, dv_acc):
        hp, n = pl.program_id(0), pl.program_id(1)
        i, j = ii_ref[n], jj_ref[n]
        lane = lax.broadcasted_iota(jnp.int32, (1, W), 1)
        in_a = lane < dh

        @pl.when(i == j)
        def _():
            dk_acc[...] = jnp.zeros_like(dk_acc)
            dv_acc[...] = jnp.zeros_like(dv_acc)
            dca_ref[...] = jnp.zeros_like(dca_ref)
            dcb_ref[...] = jnp.zeros_like(dcb_ref)

        qv, kv, vv, dov = q_ref[...], k_ref[...], v_ref[...], do_ref[...]
        prod = dov.astype(F32) * o_ref[...].astype(F32)
        lse = lse_ref[...]
        cb = c_ref[...]
        hl = lax.broadcasted_iota(jnp.int32, (1, H), 1)
        qpos = i * t + lax.broadcasted_iota(jnp.int32, (t, 1), 0)
        kpos = j * t + lax.broadcasted_iota(jnp.int32, (1, t), 1)
        mask = kpos <= qpos
        zb = jnp.zeros_like(qv)
        dq_blk, drows = None, []
        for hh, (ck_ref, dc_ref) in enumerate(((cka_ref, dca_ref), (ckb_ref, dcb_ref))):
            sel = in_a if hh == 0 else jnp.logical_not(in_a)
            qm, dom, km = jnp.where(sel, qv, zb), jnp.where(sel, dov, zb), jnp.where(sel, kv, zb)
            delta = jnp.sum(jnp.where(sel, prod, 0.0), axis=1, keepdims=True)
            cq = jnp.sum(jnp.where(hl == 2 * hp + hh, cb, 0.0), axis=1, keepdims=True)
            s = _dot_nt(qm, kv) + (cq - ck_ref[...])
            s = jnp.where(mask, s, NEG_INF)
            pr = jnp.exp(s - lse[:, hh:hh + 1])
            ds = pr * (_dot_nt(dom, vv) - delta)
            prb, dsb = pr.astype(BF16), ds.astype(BF16)
            dv_acc[...] += _dot_tn(prb, dom)
            dk_acc[...] += _dot_tn(dsb, qm)
            d = _dot(dsb, km)
            dq_blk = d if dq_blk is None else dq_blk + d
            dc_ref[...] += jnp.sum(ds, axis=0, keepdims=True)
            drows.append(jnp.sum(ds, axis=1, keepdims=True))
        drow_blk = jnp.where(lane == 0, drows[0], jnp.where(lane == 1, drows[1], 0.0))
        rows = pl.ds(pl.multiple_of(i * t, t), t)

        @pl.when(j == 0)
        def _():
            dq_acc[rows, :] = dq_blk
            drow_ref[rows, :] = drow_blk

        @pl.when(j > 0)
        def _():
            dq_acc[rows, :] += dq_blk
            drow_ref[rows, :] += drow_blk

        @pl.when(i == nb - 1)
        def _():
            dk_ref[...] = dk_acc[...].astype(BF16)
            dv_ref[...] = dv_acc[...].astype(BF16)

        @pl.when(n == n_steps - 1)
        def _():
            dq_ref[...] = (dq_acc[...] * qscale).astype(BF16)

    grid_spec = pltpu.PrefetchScalarGridSpec(
        num_scalar_prefetch=2, grid=(Hp, n_steps),
        in_specs=[
            pl.BlockSpec((t, W), lambda hp, n, jj, ii: (ii[n], hp)),
            pl.BlockSpec((t, W), lambda hp, n, jj, ii: (jj[n], hp)),
            pl.BlockSpec((t, W), lambda hp, n, jj, ii: (jj[n], hp)),
            pl.BlockSpec((t, W), lambda hp, n, jj, ii: (ii[n], hp)),
            pl.BlockSpec((t, W), lambda hp, n, jj, ii: (ii[n], hp)),
            pl.BlockSpec((None, t, W), lambda hp, n, jj, ii: (hp, ii[n], 0)),
            pl.BlockSpec((t, H), lambda hp, n, jj, ii: (ii[n], 0)),
            pl.BlockSpec((None, 1, t), lambda hp, n, jj, ii: (2 * hp, 0, jj[n])),
            pl.BlockSpec((None, 1, t), lambda hp, n, jj, ii: (2 * hp + 1, 0, jj[n])),
        ],
        out_specs=[
            pl.BlockSpec((T, W), lambda hp, n, jj, ii: (0, hp)),
            pl.BlockSpec((t, W), lambda hp, n, jj, ii: (jj[n], hp)),
            pl.BlockSpec((t, W), lambda hp, n, jj, ii: (jj[n], hp)),
            pl.BlockSpec((None, 1, t), lambda hp, n, jj, ii: (hp, 0, jj[n])),
            pl.BlockSpec((None, 1, t), lambda hp, n, jj, ii: (hp, 0, jj[n])),
            pl.BlockSpec((None, T, W), lambda hp, n, jj, ii: (hp, 0, 0)),
        ],
        scratch_shapes=[pltpu.VMEM((T, W), F32), pltpu.VMEM((t, W), F32), pltpu.VMEM((t, W), F32)],
    )
    return pl.pallas_call(
        body, name="flash_bwd", grid_spec=grid_spec,
        out_shape=(jax.ShapeDtypeStruct((T, D), BF16), jax.ShapeDtypeStruct((T, D), BF16),
                   jax.ShapeDtypeStruct((T, D), BF16), jax.ShapeDtypeStruct((Hp, 1, T), F32),
                   jax.ShapeDtypeStruct((Hp, 1, T), F32), jax.ShapeDtypeStruct((Hp, T, W), F32)),
        compiler_params=_params(2),
    )(jj, ii, q, k, v, do, o, lse, c_tok, c_rows, c_rows)


HBM_SPEC = pl.BlockSpec(memory_space=pltpu.HBM)


def _mesh_pos():
    return lax.axis_index("x"), lax.axis_index("y"), lax.axis_index("c")


def _all_gather_shards(shards):
    n = len(shards)

    def body(*refs):
        in_refs, out_refs = refs[:n], refs[n:2 * n]
        send1, recv1, send2, recv2, lsem = refs[2 * n:]
        x, y, c = _mesh_pos()
        me = 2 * x + y
        chips = [(1 - x, y), (x, 1 - y), (1 - x, 1 - y)]
        local = [pltpu.make_async_copy(in_refs[t], out_refs[t].at[me], lsem.at[t]) for t in range(n)]
        for cp in local:
            cp.start()

        def half(t, chip_idx, pc):
            hr = shards[t].shape[0] // 2
            return out_refs[t].at[chip_idx, pl.ds(pc * hr, hr), :]

        first = []
        for t in range(n):
            hr = shards[t].shape[0] // 2
            for kk, (cx, cy) in enumerate(chips):
                first.append(pltpu.make_async_remote_copy(
                    src_ref=in_refs[t].at[pl.ds(c * hr, hr), :], dst_ref=half(t, me, c),
                    send_sem=send1.at[3 * t + kk], recv_sem=recv1.at[3 * t + kk],
                    device_id=(cx, cy, c), device_id_type=MESH))
        for cp in first:
            cp.start()
        passed = []
        for t in range(n):
            for kk, (cx, cy) in enumerate(chips):
                src_chip = 2 * cx + cy
                landed = half(t, src_chip, c)
                pltpu.make_async_remote_copy(
                    src_ref=landed, dst_ref=landed, send_sem=send1.at[3 * t + kk], recv_sem=recv1.at[3 * t + kk],
                    device_id=(cx, cy, c), device_id_type=MESH).wait_recv()
                fwd = pltpu.make_async_remote_copy(
                    src_ref=landed, dst_ref=landed, send_sem=send2.at[3 * t + kk], recv_sem=recv2.at[3 * t + kk],
                    device_id=(x, y, 1 - c), device_id_type=MESH)
                fwd.start()
                passed.append(fwd)
        for t in range(n):
            for kk, (cx, cy) in enumerate(chips):
                other = half(t, 2 * cx + cy, 1 - c)
                pltpu.make_async_remote_copy(
                    src_ref=other, dst_ref=other, send_sem=send2.at[3 * t + kk], recv_sem=recv2.at[3 * t + kk],
                    device_id=(x, y, 1 - c), device_id_type=MESH).wait_recv()
        for cp in first + passed:
            cp.wait_send()
        for cp in local:
            cp.wait()

    return pl.pallas_call(
        body, name="weights_all_gather",
        out_shape=[jax.ShapeDtypeStruct((N_CHIPS,) + s.shape, s.dtype) for s in shards],
        in_specs=[HBM_SPEC] * n, out_specs=[HBM_SPEC] * n,
        scratch_shapes=[pltpu.SemaphoreType.DMA((3 * n,)), pltpu.SemaphoreType.DMA((3 * n,)),
                        pltpu.SemaphoreType.DMA((3 * n,)), pltpu.SemaphoreType.DMA((3 * n,)),
                        pltpu.SemaphoreType.DMA((n,))],
    )(*shards)


def _sibling_swap_halves(g):
    _, M, C = g.shape
    hr = M // 2

    def body(g_ref, r_ref, ssem, rsem):
        x, y, c = _mesh_pos()
        cps = [pltpu.make_async_remote_copy(
            src_ref=g_ref.at[s, pl.ds((1 - c) * hr, hr), :], dst_ref=r_ref.at[s],
            send_sem=ssem.at[s], recv_sem=rsem.at[s], device_id=(x, y, 1 - c), device_id_type=MESH)
            for s in range(N_CHIPS)]
        for cp in cps:
            cp.start()
        for cp in cps:
            cp.wait()

    return pl.pallas_call(
        body, name="grads_sibling_swap", out_shape=jax.ShapeDtypeStruct((N_CHIPS, hr, C), F32),
        in_specs=[HBM_SPEC], out_specs=HBM_SPEC,
        scratch_shapes=[pltpu.SemaphoreType.DMA((N_CHIPS,)), pltpu.SemaphoreType.DMA((N_CHIPS,))],
    )(g)


def _pair_add(g, r, c_idx):
    _, M, C = g.shape
    hr = M // 2
    tr = PACK_ROW_TILE
    nbk = hr // tr

    def body(c_ref, g_ref, r_ref, o_ref):
        o_ref[...] = g_ref[...] + r_ref[...]

    grid_spec = pltpu.PrefetchScalarGridSpec(
        num_scalar_prefetch=1, grid=(N_CHIPS, nbk),
        in_specs=[pl.BlockSpec((None, tr, C), lambda s, i, c: (s, c[0] * nbk + i, 0)),
                  pl.BlockSpec((None, tr, C), lambda s, i, c: (s, i, 0))],
        out_specs=pl.BlockSpec((None, tr, C), lambda s, i, c: (s, i, 0)),
    )
    return pl.pallas_call(body, name="grads_pair_add", grid_spec=grid_spec,
                          out_shape=jax.ShapeDtypeStruct((N_CHIPS, hr, C), F32),
                          compiler_params=_params(2))(c_idx, g, r)


def _chip_scatter(pp):
    def body(p_ref, r_ref, ssem, rsem, lsem):
        x, y, c = _mesh_pos()
        me = 2 * x + y
        chips = [(1 - x, y), (x, 1 - y), (1 - x, 1 - y)]
        own = pltpu.make_async_copy(p_ref.at[me], r_ref.at[me], lsem)
        own.start()
        cps = [pltpu.make_async_remote_copy(
            src_ref=p_ref.at[2 * cx + cy], dst_ref=r_ref.at[me], send_sem=ssem.at[kk], recv_sem=rsem.at[kk],
            device_id=(cx, cy, c), device_id_type=MESH) for kk, (cx, cy) in enumerate(chips)]
        for cp in cps:
            cp.start()
        for kk, (cx, cy) in enumerate(chips):
            got = r_ref.at[2 * cx + cy]
            pltpu.make_async_remote_copy(
                src_ref=got, dst_ref=got, send_sem=ssem.at[kk], recv_sem=rsem.at[kk],
                device_id=(cx, cy, c), device_id_type=MESH).wait_recv()
        for cp in cps:
            cp.wait_send()
        own.wait()

    return pl.pallas_call(
        body, name="grads_chip_scatter", out_shape=jax.ShapeDtypeStruct(pp.shape, F32),
        in_specs=[HBM_SPEC], out_specs=HBM_SPEC,
        scratch_shapes=[pltpu.SemaphoreType.DMA((3,)), pltpu.SemaphoreType.DMA((3,)), pltpu.SemaphoreType.DMA],
    )(pp)


def _chip_sum(r):
    _, hr, C = r.shape
    tr = PACK_ROW_TILE

    def body(r_ref, o_ref):
        o_ref[...] = ((r_ref[0] + r_ref[1]) + r_ref[2]) + r_ref[3]

    return pl.pallas_call(
        body, name="grads_chip_sum", grid=(hr // tr,),
        in_specs=[pl.BlockSpec((N_CHIPS, tr, C), lambda i: (0, i, 0))],
        out_specs=pl.BlockSpec((tr, C), lambda i: (i, 0)),
        out_shape=jax.ShapeDtypeStruct((hr, C), F32), compiler_params=_params())(r)


def _sibling_join(qh):
    hr, C = qh.shape

    def body(q_ref, o_ref, ssem, rsem, lsem):
        x, y, c = _mesh_pos()
        mine = o_ref.at[pl.ds(c * hr, hr), :]
        own = pltpu.make_async_copy(q_ref, mine, lsem)
        own.start()
        cp = pltpu.make_async_remote_copy(src_ref=q_ref, dst_ref=mine, send_sem=ssem, recv_sem=rsem,
                                          device_id=(x, y, 1 - c), device_id_type=MESH)
        cp.start()
        theirs = o_ref.at[pl.ds((1 - c) * hr, hr), :]
        pltpu.make_async_remote_copy(src_ref=theirs, dst_ref=theirs, send_sem=ssem, recv_sem=rsem,
                                     device_id=(x, y, 1 - c), device_id_type=MESH).wait_recv()
        cp.wait_send()
        own.wait()

    return pl.pallas_call(
        body, name="grads_sibling_join", out_shape=jax.ShapeDtypeStruct((2 * hr, C), F32),
        in_specs=[HBM_SPEC], out_specs=HBM_SPEC,
        scratch_shapes=[pltpu.SemaphoreType.DMA, pltpu.SemaphoreType.DMA, pltpu.SemaphoreType.DMA],
    )(qh)


def _adamw(w, g, m, v):
    M, C = w.shape
    tr = PACK_ROW_TILE

    def body(w_ref, g_ref, m_ref, v_ref, d_ref, mo_ref, vo_ref):
        gv = g_ref[...]
        mn = ADAM_B1 * m_ref[...] + (1.0 - ADAM_B1) * gv
        vn = ADAM_B2 * v_ref[...] + (1.0 - ADAM_B2) * (gv * gv)
        m_hat = mn / (1.0 - ADAM_B1 ** ADAM_STEP)
        v_hat = vn / (1.0 - ADAM_B2 ** ADAM_STEP)
        d_ref[...] = -ADAM_LR * (m_hat / (jnp.sqrt(v_hat) + ADAM_EPS) + ADAM_WD * w_ref[...])
        mo_ref[...] = mn
        vo_ref[...] = vn

    spec = pl.BlockSpec((tr, C), lambda i: (i, 0))
    return pl.pallas_call(
        body, name="adamw", grid=(M // tr,), in_specs=[spec] * 4, out_specs=[spec] * 3,
        out_shape=[jax.ShapeDtypeStruct((M, C), F32)] * 3, compiler_params=_params())(w, g, m, v)


WEIGHT_NAMES = ("norm_g", "w_attn_in", "b_forget", "w_attn_out", "w_conv_in", "conv_w", "w_conv_out",
                "w_mlp_up", "w_mlp_down", "w_ple_proj", "w_ple_gate")
COL_SHARDED = ("norm_g", "w_attn_in", "w_conv_in", "conv_w", "w_mlp_up", "w_ple_proj")
ROW_SHARDED = ("w_attn_out", "w_conv_out", "w_mlp_down", "w_ple_gate")


def _unshard(name, gathered, shard_shape):
    a = gathered.reshape((N_CHIPS,) + tuple(shard_shape))
    if name in COL_SHARDED:
        a = jnp.moveaxis(a, 0, -2)
        return a.reshape(a.shape[:-2] + (N_CHIPS * shard_shape[-1],))
    a = jnp.moveaxis(a, 0, 1)
    return a.reshape((shard_shape[0], N_CHIPS * shard_shape[1], shard_shape[2]))


def _to_shard_major(name, full):
    if name == "b_forget":
        return jnp.broadcast_to(full.reshape(1, -1), (N_CHIPS, full.size))
    if name in COL_SHARDED:
        a = full.reshape(full.shape[:-1] + (N_CHIPS, full.shape[-1] // N_CHIPS))
        a = jnp.moveaxis(a, -2, 0)
    else:
        a = full.reshape((full.shape[0], N_CHIPS, full.shape[1] // N_CHIPS, full.shape[2]))
        a = jnp.moveaxis(a, 1, 0)
    return a.reshape(N_CHIPS, -1)


def _pack_layout(shard_shapes):
    offs, row = {}, 0
    for name in WEIGHT_NAMES:
        n = int(np.prod(shard_shapes[name]))
        rows = -(-n // PACK_COLS)
        offs[name] = (row, rows, n)
        row += rows
    total = -(-row // PACK_ROW_ALIGN) * PACK_ROW_ALIGN
    return offs, total


def _pack(flat_by_name, offs, total_rows, lead=()):
    parts, row = [], 0
    for name in WEIGHT_NAMES:
        _, rows, n = offs[name]
        a = flat_by_name[name].astype(F32)
        pad = rows * PACK_COLS - n
        if pad:
            a = jnp.pad(a, [(0, 0)] * len(lead) + [(0, pad)])
        parts.append(a.reshape(tuple(lead) + (rows, PACK_COLS)))
        row += rows
    if total_rows > row:
        parts.append(jnp.zeros(tuple(lead) + (total_rows - row, PACK_COLS), F32))
    return jnp.concatenate(parts, axis=len(lead))


def _unpack(packed, offs, shard_shapes):
    out = {}
    for name in WEIGHT_NAMES:
        row, rows, n = offs[name]
        out[name] = packed[row:row + rows].reshape(-1)[:n].reshape(shard_shapes[name])
    return out


def kernel(x, p, norm_g, w_attn_in, b_forget, w_attn_out, w_conv_in, conv_w, w_conv_out, w_mlp_up, w_mlp_down, w_ple_proj, w_ple_gate, loss_target, m_norm_g, m_w_attn_in, m_b_forget, m_w_attn_out, m_w_conv_in, m_conv_w, m_w_conv_out, m_w_mlp_up, m_w_mlp_down, m_w_ple_proj, m_w_ple_gate, v_norm_g, v_w_attn_in, v_b_forget, v_w_attn_out, v_w_conv_in, v_conv_w, v_w_conv_out, v_w_mlp_up, v_w_mlp_down, v_w_ple_proj, v_w_ple_gate):
    w_local = dict(norm_g=norm_g, w_attn_in=w_attn_in, b_forget=b_forget, w_attn_out=w_attn_out,
                   w_conv_in=w_conv_in, conv_w=conv_w, w_conv_out=w_conv_out, w_mlp_up=w_mlp_up,
                   w_mlp_down=w_mlp_down, w_ple_proj=w_ple_proj, w_ple_gate=w_ple_gate)
    m_local = dict(norm_g=m_norm_g, w_attn_in=m_w_attn_in, b_forget=m_b_forget, w_attn_out=m_w_attn_out,
                   w_conv_in=m_w_conv_in, conv_w=m_conv_w, w_conv_out=m_w_conv_out, w_mlp_up=m_w_mlp_up,
                   w_mlp_down=m_w_mlp_down, w_ple_proj=m_w_ple_proj, w_ple_gate=m_w_ple_gate)
    v_local = dict(norm_g=v_norm_g, w_attn_in=v_w_attn_in, b_forget=v_b_forget, w_attn_out=v_w_attn_out,
                   w_conv_in=v_w_conv_in, conv_w=v_conv_w, w_conv_out=v_w_conv_out, w_mlp_up=v_w_mlp_up,
                   w_mlp_down=v_w_mlp_down, w_ple_proj=v_w_ple_proj, w_ple_gate=v_w_ple_gate)
    shard_shapes = {k: tuple(a.shape) for k, a in w_local.items()}

    xs = x[0]
    target = loss_target[0]
    T, D = xs.shape
    depth = p.shape[0]
    H = b_forget.shape[1]
    qscale = float(D // H) ** -0.5

    big = [n for n in WEIGHT_NAMES if n not in ("norm_g", "conv_w", "b_forget")]
    small = jnp.concatenate([norm_g.reshape(-1), conv_w.reshape(-1)])
    n_small = small.shape[0]
    small_rows = -(-n_small // (LANES * 16)) * 16
    small = jnp.pad(small, (0, small_rows * LANES - n_small)).reshape(small_rows, LANES)
    shards = [w_local[n].astype(BF16).reshape(-1, shard_shapes[n][-1]) for n in big] + [small]
    gathered = _all_gather_shards(shards)
    full = {n: _unshard(n, g, shard_shapes[n]) for n, g in zip(big, gathered[:-1])}
    gs = gathered[-1].reshape(N_CHIPS, -1)
    full["norm_g"] = _unshard("norm_g", gs[:, :norm_g.size], shard_shapes["norm_g"])
    full["conv_w"] = _unshard("conv_w", gs[:, norm_g.size:n_small], shard_shapes["conv_w"])
    gains = full["norm_g"]

    def gain(i, k):
        return gains[i, k].reshape(1, D)

    def taps(j):
        return jnp.pad(full["conv_w"][j], ((0, 5), (0, 0)))

    saved = []
    h = xs
    for i in range(depth):
        j = i // 2
        s = {"x0": h}
        if i % 2 == 0:
            w_in = full["w_attn_in"][j]
            wqkv = w_in[:, :3 * D]
            wf = jnp.pad(w_in[:, 3 * D:], ((0, 0), (0, LANES - H)))
            bf = jnp.pad(b_forget[j].reshape(1, H), ((0, 0), (0, LANES - H)))
            q, k, v, fl, lf = _attn_in_fwd(h, gain(i, 0), wqkv, wf, bf, qscale)
            c_t = _gate_cumsum(lf[:, :H].T)
            c_tok, c_rows = c_t.T, c_t.reshape(H, 1, T)
            o, lse = _flash_fwd(q, k, v, c_tok, c_rows)
            s.update(q=q, k=k, v=v, fl=fl, c_tok=c_tok, c_rows=c_rows, o=o, lse=lse, wqkv=wqkv, wf=wf)
            mix_in, w_out = o, full["w_attn_out"][j]
        else:
            b, c, u, zc, y = _conv_in_fwd(h, gain(i, 0), full["w_conv_in"][j], taps(j))
            s.update(b=b, c=c, u=u, zc=zc, y=y)
            mix_in, w_out = y, full["w_conv_out"][j]
        m1, x1 = _out_proj_fwd("mixer_out_fwd", mix_in, w_out, gain(i, 1), h)
        up, a = _mlp_up_fwd(x1, gain(i, 2), full["w_mlp_up"][i])
        m3, x2 = _out_proj_fwd("mlp_down_fwd", a, full["w_mlp_down"][i], gain(i, 3), x1)
        gl, pe, x3 = _ple_fwd(x2, p[i, 0], gain(i, 4), gain(i, 5), full["w_ple_gate"][i], full["w_ple_proj"][i])
        s.update(m1=m1, x1=x1, up=up, a=a, m3=m3, x2=x2, gl=gl, pe=pe, w_out=w_out)
        saved.append(s)
        h = x3

    dh, loss_blk = _loss_fwd_bwd(h, target)
    loss = lax.psum(loss_blk[0, 0], ("x", "y", "c"))

    g_gain = [[None] * 6 for _ in range(depth)]
    grads = {n: [None] * w_local[n].shape[0] for n in WEIGHT_NAMES if n != "norm_g"}
    for i in reversed(range(depth)):
        j = i // 2
        s = saved[i]
        dx2, dwp, dwg, g_gain[i][4], g_gain[i][5] = _ple_bwd(
            dh, s["x2"], p[i, 0], s["gl"], s["pe"], gain(i, 4), gain(i, 5), full["w_ple_gate"][i].T)
        grads["w_ple_proj"][i], grads["w_ple_gate"][i] = dwp, dwg
        dup, dwd, g_gain[i][3] = _out_proj_bwd(
            "mlp_down_bwd", dx2, s["m3"], gain(i, 3), [s["a"], s["up"]], full["w_mlp_down"][i].T, "relu2")
        grads["w_mlp_down"][i] = dwd
        dx1, dwu, g_gain[i][2] = _in_proj_bwd(
            "mlp_up_bwd", [dup], [full["w_mlp_up"][i].T], s["x1"], gain(i, 2), dx2)
        grads["w_mlp_up"][i] = dwu
        if i % 2 == 0:
            do, dwo, g_gain[i][1] = _out_proj_bwd(
                "attn_out_bwd", dx1, s["m1"], gain(i, 1), [s["o"]], s["w_out"].T, "plain")
            grads["w_attn_out"][j] = dwo
            dq, dk, dv, dca, dcb, drw = _flash_bwd(s["q"], s["k"], s["v"], do, s["o"], s["lse"],
                                                   s["c_tok"], s["c_rows"], qscale)
            dcol = jnp.concatenate([dca, dcb], axis=1).reshape(H, T)
            drow = jnp.moveaxis(drw[:, :, :2], 2, 1).reshape(H, T)
            dfl_t, dbf = _gate_bwd(drow, dcol, s["fl"][:, :H].T)
            grads["b_forget"][j] = dbf.reshape(H)
            dfl = jnp.pad(dfl_t.T, ((0, 0), (0, LANES - H))).astype(BF16)
            wqkv_t = s["wqkv"].T
            dh, dwq, dwk, dwv, dwf, g_gain[i][0] = _in_proj_bwd(
                "attn_in_bwd", [dq, dk, dv, dfl], [wqkv_t[:D], wqkv_t[D:2 * D], wqkv_t[2 * D:], s["wf"].T],
                s["x0"], gain(i, 0), dx1)
            grads["w_attn_in"][j] = jnp.concatenate([dwq, dwk, dwv, dwf[:, :H]], axis=1)
        else:
            dy, dwo, g_gain[i][1] = _out_proj_bwd(
                "conv_out_bwd", dx1, s["m1"], gain(i, 1), [s["y"]], s["w_out"].T, "plain_f32")
            grads["w_conv_out"][j] = dwo
            db, dc, du, dcw = _conv_bwd(dy, s["b"], s["c"], s["u"], s["zc"], taps(j))
            grads["conv_w"][j] = dcw[:conv_w.shape[1]]
            w_t = full["w_conv_in"][j].T
            dh, dwb, dwc, dwu2, g_gain[i][0] = _in_proj_bwd(
                "conv_in_bwd", [db, dc, du], [w_t[:D], w_t[D:2 * D], w_t[2 * D:]], s["x0"], gain(i, 0), dx1)
            grads["w_conv_in"][j] = jnp.concatenate([dwb, dwc, dwu2], axis=1)
    grad_x = dh.reshape(x.shape)

    grad_full = {n: jnp.stack(grads[n]) for n in grads}
    grad_full["norm_g"] = jnp.stack([jnp.concatenate(row, axis=0) for row in g_gain])

    offs, total_rows = _pack_layout(shard_shapes)
    g_packed = _pack({n: _to_shard_major(n, grad_full[n]) for n in WEIGHT_NAMES}, offs, total_rows, lead=(N_CHIPS,))
    c_idx = lax.axis_index("c").astype(jnp.int32).reshape(1)
    pair = _pair_add(g_packed, _sibling_swap_halves(g_packed), c_idx)
    g_red = _sibling_join(_chip_sum(_chip_scatter(pair)))
    w_p = _pack({n: w_local[n].reshape(-1) for n in WEIGHT_NAMES}, offs, total_rows)
    m_p = _pack({n: m_local[n].reshape(-1) for n in WEIGHT_NAMES}, offs, total_rows)
    v_p = _pack({n: v_local[n].reshape(-1) for n in WEIGHT_NAMES}, offs, total_rows)
    d_p, mn_p, vn_p = _adamw(w_p, g_red, m_p, v_p)
    g_out = _unpack(g_red, offs, shard_shapes)
    d_out = _unpack(d_p, offs, shard_shapes)
    m_out = _unpack(mn_p, offs, shard_shapes)
    v_out = _unpack(vn_p, offs, shard_shapes)
    return (loss, grad_x, *[g_out[n] for n in WEIGHT_NAMES], *[d_out[n] for n in WEIGHT_NAMES],
            *[m_out[n] for n in WEIGHT_NAMES], *[v_out[n] for n in WEIGHT_NAMES])
```

```python
import numpy as np
import jax
import jax.numpy as jnp
from jax import lax
from jax.experimental import pallas as pl
from jax.experimental.pallas import tpu as pltpu

F32 = jnp.float32
BF16 = jnp.bfloat16
MESH = pl.DeviceIdType.MESH

RMS_EPS = 1e-6
NEG_INF = -1e30
ADAM_LR = 0.001
ADAM_B1 = 0.9
ADAM_B2 = 0.999
ADAM_EPS = 1e-08
ADAM_WD = 0.01
ADAM_STEP = 10

N_CHIPS = 4
LANES = 128
ROW_TILE_FWD = 512
ROW_TILE_BWD = 256
ATTN_TILE = 512
COL_CHUNK = 512
PACK_COLS = 1024
PACK_ROW_TILE = 256
PACK_ROW_ALIGN = 2 * PACK_ROW_TILE
N_AUG = 3
VMEM_LIMIT = 56 * 1024 * 1024


def _tile(n, pref):
    return pref if n % pref == 0 else n


def _dot(a, b):
    return jnp.dot(a, b, preferred_element_type=F32)


def _dot_tn(a, b):
    return lax.dot_general(a, b, (((0,), (0,)), ((), ())), preferred_element_type=F32)


def _dot_nt(a, b):
    return lax.dot_general(a, b, (((1,), (1,)), ((), ())), preferred_element_type=F32)


def _rms_fwd(x, g):
    r = lax.rsqrt(jnp.mean(x * x, axis=-1, keepdims=True) + RMS_EPS)
    return (x * r) * g


def _rms_bwd(x, g, dy):
    r = lax.rsqrt(jnp.mean(x * x, axis=-1, keepdims=True) + RMS_EPS)
    xh = x * r
    dyg = dy * g
    dx = r * (dyg - xh * jnp.mean(dyg * xh, axis=-1, keepdims=True))
    return dx, jnp.sum(dy * xh, axis=0, keepdims=True)


def _params(n_axes=1):
    return pltpu.CompilerParams(dimension_semantics=("arbitrary",) * n_axes, vmem_limit_bytes=VMEM_LIMIT)


def _row_call(name, body, n_rows, tm, row_ins, const_ins, row_outs, acc_outs=(), scratch=(), reverse=False):
    nb = n_rows // tm
    rmap = (lambda i: (nb - 1 - i, 0)) if reverse else (lambda i: (i, 0))

    def whole(shape):
        nd = len(shape)
        return pl.BlockSpec(tuple(shape), lambda i: (0,) * nd)

    in_specs = [pl.BlockSpec((tm, a.shape[1]), rmap) for a in row_ins] + [whole(a.shape) for a in const_ins]
    out_shape = [jax.ShapeDtypeStruct((n_rows, w), dt) for (w, dt) in row_outs]
    out_shape += [jax.ShapeDtypeStruct(tuple(s), dt) for (s, dt) in acc_outs]
    out_specs = [pl.BlockSpec((tm, w), rmap) for (w, _) in row_outs] + [whole(s) for (s, _) in acc_outs]
    return pl.pallas_call(
        body, name=name, grid=(nb,), in_specs=in_specs, out_specs=out_specs, out_shape=out_shape,
        scratch_shapes=list(scratch), compiler_params=_params(),
    )(*row_ins, *const_ins)


def _attn_in_fwd(x, g, wqkv, wf, bf, qscale):
    T, D = x.shape
    tm, ch = _tile(T, ROW_TILE_FWD), _tile(D, COL_CHUNK)

    def body(x_ref, g_ref, w_ref, wf_ref, bf_ref, q_ref, k_ref, v_ref, fl_ref, lf_ref):
        h = _rms_fwd(x_ref[...], g_ref[...]).astype(BF16)
        for part, o_ref in enumerate((q_ref, k_ref, v_ref)):
            for n0 in range(0, D, ch):
                r = _dot(h, w_ref[:, part * D + n0:part * D + n0 + ch])
                if part == 0:
                    r = r * qscale
                o_ref[:, n0:n0 + ch] = r.astype(BF16)
        fl = _dot(h, wf_ref[...]) + bf_ref[...]
        fl_ref[...] = fl
        lf_ref[...] = jnp.minimum(fl, 0.0) - jnp.log1p(jnp.exp(-jnp.abs(fl)))

    return _row_call("attn_in_fwd", body, T, tm, [x], [g, wqkv, wf, bf],
                     [(D, BF16), (D, BF16), (D, BF16), (LANES, F32), (LANES, F32)])


def _conv_in_fwd(x, g, w, cw):
    T, D = x.shape
    tm, ch = _tile(T, ROW_TILE_FWD), _tile(D, COL_CHUNK)

    def body(x_ref, g_ref, w_ref, cw_ref, b_ref, c_ref, u_ref, zc_ref, y_ref, tail_ref):
        i = pl.program_id(0)

        @pl.when(i == 0)
        def _():
            tail_ref[...] = jnp.zeros_like(tail_ref)

        h = _rms_fwd(x_ref[...], g_ref[...]).astype(BF16)
        for part, o_ref in enumerate((b_ref, c_ref, u_ref)):
            for n0 in range(0, D, ch):
                o_ref[:, n0:n0 + ch] = _dot(h, w_ref[:, part * D + n0:part * D + n0 + ch])
        z = c_ref[...] * u_ref[...]
        row = lax.broadcasted_iota(jnp.int32, (tm, 1), 0)
        t6, t7 = tail_ref[6:7, :], tail_ref[7:8, :]
        z1 = jnp.where(row == 0, t7, pltpu.roll(z, 1, axis=0))
        z2 = jnp.where(row == 0, t6, jnp.where(row == 1, t7, pltpu.roll(z, 2, axis=0)))
        zc = cw_ref[0:1, :] * z2 + cw_ref[1:2, :] * z1 + cw_ref[2:3, :] * z
        zc_ref[...] = zc
        y_ref[...] = (b_ref[...] * zc).astype(BF16)
        tail_ref[...] = z[tm - 8:tm, :]

    return _row_call("conv_in_fwd", body, T, tm, [x], [g, w, cw],
                     [(D, F32), (D, F32), (D, F32), (D, F32), (D, BF16)], scratch=[pltpu.VMEM((8, D), F32)])


def _mlp_up_fwd(x, g, w):
    T, D = x.shape
    F = w.shape[1]
    tm, ch = _tile(T, ROW_TILE_FWD), _tile(F, COL_CHUNK)

    def body(x_ref, g_ref, w_ref, up_ref, a_ref):
        h = _rms_fwd(x_ref[...], g_ref[...]).astype(BF16)
        for n0 in range(0, F, ch):
            r = _dot(h, w_ref[:, n0:n0 + ch])
            up_ref[:, n0:n0 + ch] = r.astype(BF16)
            rl = jnp.maximum(r, 0.0)
            a_ref[:, n0:n0 + ch] = (rl * rl).astype(BF16)

    return _row_call("mlp_up_fwd", body, T, tm, [x], [g, w], [(F, BF16), (F, BF16)])


def _out_proj_fwd(name, a, w, g, x):
    T, D = x.shape
    tm = _tile(T, ROW_TILE_FWD)

    def body(a_ref, x_ref, w_ref, g_ref, m_ref, xn_ref):
        m = _dot(a_ref[...], w_ref[...])
        m_ref[...] = m
        xn_ref[...] = x_ref[...] + _rms_fwd(m, g_ref[...])

    return _row_call(name, body, T, tm, [a, x], [w, g], [(D, F32), (D, F32)])


def _ple_fwd(x, p, g4, g5, wg, wp):
    T, D = x.shape
    tm = _tile(T, ROW_TILE_FWD)

    def body(x_ref, p_ref, g4_ref, g5_ref, wg_ref, wp_ref, gl_ref, pe_ref, xn_ref):
        xv = x_ref[...]
        gl = _dot(_rms_fwd(xv, g4_ref[...]).astype(BF16), wg_ref[...])
        pe = _dot(p_ref[...].astype(BF16), wp_ref[...])
        gl_ref[...] = gl
        pe_ref[...] = pe
        e = pe * (1.0 / (1.0 + jnp.exp(-gl)))
        xn_ref[...] = xv + _rms_fwd(e, g5_ref[...])

    return _row_call("ple_fwd", body, T, tm, [x, p], [g4, g5, wg, wp], [(D, F32), (D, F32), (D, F32)])


def _loss_fwd_bwd(y, target):
    T, D = y.shape
    tm = _tile(T, ROW_TILE_FWD)

    def body(y_ref, t_ref, dy_ref, loss_ref):
        @pl.when(pl.program_id(0) == 0)
        def _():
            loss_ref[...] = jnp.zeros_like(loss_ref)

        err = y_ref[...] - t_ref[...]
        dy_ref[...] = err * (1.0 / D)
        part = 0.5 * jnp.sum(jnp.mean(err * err, axis=-1, keepdims=True), axis=0, keepdims=True)
        loss_ref[...] += jnp.broadcast_to(part, loss_ref.shape)

    return _row_call("loss", body, T, tm, [y, target], [], [(D, F32)], acc_outs=[((8, LANES), F32)])


def _out_proj_bwd(name, dres, m, g, a_ins, wt, mode):
    T, D = dres.shape
    Ka = wt.shape[1]
    tm, ch = _tile(T, ROW_TILE_BWD), _tile(Ka, COL_CHUNK)
    out_dt = F32 if mode == "plain_f32" else BF16

    def body(dres_ref, m_ref, *rest):
        a_ref = rest[0]
        up_ref = rest[1] if mode == "relu2" else None
        k = len(a_ins)
        g_ref, wt_ref, da_ref, dw_ref, dg_ref = rest[k:k + 5]

        @pl.when(pl.program_id(0) == 0)
        def _():
            dw_ref[...] = jnp.zeros_like(dw_ref)
            dg_ref[...] = jnp.zeros_like(dg_ref)

        dm, dgp = _rms_bwd(m_ref[...], g_ref[...], dres_ref[...])
        dg_ref[...] += dgp
        dmb = dm.astype(BF16)
        for n0 in range(0, Ka, ch):
            da = _dot(dmb, wt_ref[:, n0:n0 + ch])
            if mode == "relu2":
                da = da * (2.0 * jnp.maximum(up_ref[:, n0:n0 + ch].astype(F32), 0.0))
            da_ref[:, n0:n0 + ch] = da.astype(out_dt)
            dw_ref[n0:n0 + ch, :] += _dot_tn(a_ref[:, n0:n0 + ch], dmb)

    return _row_call(name, body, T, tm, [dres, m] + list(a_ins), [g, wt],
                     [(Ka, out_dt)], acc_outs=[((Ka, D), F32), ((1, D), F32)])


def _in_proj_bwd(name, pieces, wts, x, g, dres):
    T, D = x.shape
    tm = _tile(T, ROW_TILE_BWD)
    k = len(pieces)
    widths = [pc.shape[1] for pc in pieces]

    def body(*refs):
        pc_refs = refs[:k]
        x_ref, dres_ref, g_ref = refs[k:k + 3]
        wt_refs = refs[k + 3:2 * k + 3]
        dx_ref = refs[2 * k + 3]
        dw_refs = refs[2 * k + 4:3 * k + 4]
        dg_ref = refs[3 * k + 4]

        @pl.when(pl.program_id(0) == 0)
        def _():
            for r in dw_refs:
                r[...] = jnp.zeros_like(r)
            dg_ref[...] = jnp.zeros_like(dg_ref)

        xv, gv = x_ref[...], g_ref[...]
        hb = _rms_fwd(xv, gv).astype(BF16)
        dh = None
        for pc_ref, wt_ref, dw_ref, n in zip(pc_refs, wt_refs, dw_refs, widths):
            d = _dot(pc_ref[...], wt_ref[...])
            dh = d if dh is None else dh + d
            ch = _tile(n, COL_CHUNK)
            for n0 in range(0, n, ch):
                dw_ref[:, n0:n0 + ch] += _dot_tn(hb, pc_ref[:, n0:n0 + ch])
        dxn, dgp = _rms_bwd(xv, gv, dh)
        dg_ref[...] += dgp
        dx_ref[...] = dres_ref[...] + dxn

    return _row_call(name, body, T, tm, list(pieces) + [x, dres], [g] + list(wts), [(D, F32)],
                     acc_outs=[((D, n), F32) for n in widths] + [((1, D), F32)])


def _conv_bwd(dy, b, c, u, zc, cw):
    T, D = dy.shape
    tm = _tile(T, ROW_TILE_BWD)

    def body(dy_ref, b_ref, c_ref, u_ref, zc_ref, cw_ref, db_ref, dc_ref, du_ref, dcw_ref, head_ref):
        @pl.when(pl.program_id(0) == 0)
        def _():
            head_ref[...] = jnp.zeros_like(head_ref)
            dcw_ref[...] = jnp.zeros_like(dcw_ref)

        dyv, cv, uv = dy_ref[...], c_ref[...], u_ref[...]
        db_ref[...] = (dyv * zc_ref[...]).astype(BF16)
        dzc = dyv * b_ref[...]
        row = lax.broadcasted_iota(jnp.int32, (tm, 1), 0)
        h0, h1 = head_ref[0:1, :], head_ref[1:2, :]
        d1 = jnp.where(row == tm - 1, h0, pltpu.roll(dzc, tm - 1, axis=0))
        d2 = jnp.where(row == tm - 1, h1, jnp.where(row == tm - 2, h0, pltpu.roll(dzc, tm - 2, axis=0)))
        dz = cw_ref[2:3, :] * dzc + cw_ref[1:2, :] * d1 + cw_ref[0:1, :] * d2
        dc_ref[...] = (dz * uv).astype(BF16)
        du_ref[...] = (dz * cv).astype(BF16)
        z = cv * uv
        dcw_ref[0:1, :] += jnp.sum(d2 * z, axis=0, keepdims=True)
        dcw_ref[1:2, :] += jnp.sum(d1 * z, axis=0, keepdims=True)
        dcw_ref[2:3, :] += jnp.sum(dzc * z, axis=0, keepdims=True)
        head_ref[...] = dzc[0:8, :]

    return _row_call("conv_bwd", body, T, tm, [dy, b, c, u, zc], [cw], [(D, BF16), (D, BF16), (D, BF16)],
                     acc_outs=[((8, D), F32)], scratch=[pltpu.VMEM((8, D), F32)], reverse=True)


def _ple_bwd(dres, x, p, gl, pe, g4, g5, wgt):
    T, D = x.shape
    P = p.shape[1]
    tm = _tile(T, ROW_TILE_BWD)

    def body(dres_ref, x_ref, p_ref, gl_ref, pe_ref, g4_ref, g5_ref, wgt_ref,
             dx_ref, dwp_ref, dwg_ref, dg4_ref, dg5_ref):
        @pl.when(pl.program_id(0) == 0)
        def _():
            for r in (dwp_ref, dwg_ref, dg4_ref, dg5_ref):
                r[...] = jnp.zeros_like(r)

        dr, xv, pe_v = dres_ref[...], x_ref[...], pe_ref[...]
        gate = 1.0 / (1.0 + jnp.exp(-gl_ref[...]))
        de, dg5p = _rms_bwd(pe_v * gate, g5_ref[...], dr)
        dg5_ref[...] += dg5p
        dpe = (de * gate).astype(BF16)
        dgl = (de * pe_v * gate * (1.0 - gate)).astype(BF16)
        dwp_ref[...] += _dot_tn(p_ref[...].astype(BF16), dpe)
        g4v = g4_ref[...]
        dwg_ref[...] += _dot_tn(_rms_fwd(xv, g4v).astype(BF16), dgl)
        dxn, dg4p = _rms_bwd(xv, g4v, _dot(dgl, wgt_ref[...]))
        dg4_ref[...] += dg4p
        dx_ref[...] = dr + dxn

    return _row_call("ple_bwd", body, T, tm, [dres, x, p, gl, pe], [g4, g5, wgt], [(D, F32)],
                     acc_outs=[((P, D), F32), ((D, D), F32), ((1, D), F32), ((1, D), F32)])


def _scan_lanes(v, reverse):
    n = v.shape[1]
    lane = lax.broadcasted_iota(jnp.int32, v.shape, 1)
    s = 1
    while s < n:
        if reverse:
            v = v + jnp.where(lane < n - s, pltpu.roll(v, n - s, axis=1), 0.0)
        else:
            v = v + jnp.where(lane >= s, pltpu.roll(v, s, axis=1), 0.0)
        s *= 2
    return v


def _gate_cumsum(lf_t):
    def body(lf_ref, *piece_refs):
        rest = _scan_lanes(lf_ref[...], reverse=False)
        for r in piece_refs:
            piece = rest.astype(BF16)
            r[...] = piece
            rest = rest - piece.astype(F32)

    return pl.pallas_call(body, name="gate_cumsum", out_shape=[jax.ShapeDtypeStruct(lf_t.shape, BF16)] * N_AUG,
                          compiler_params=pltpu.CompilerParams(vmem_limit_bytes=VMEM_LIMIT))(lf_t)


def _gate_bwd(drow_t, dcol_t, fl_t):
    H = fl_t.shape[0]

    def body(dr_ref, dc_ref, fl_ref, dfl_ref, dbf_ref):
        dlf = _scan_lanes(dr_ref[...] - dc_ref[...], reverse=True)
        dfl = dlf * (1.0 / (1.0 + jnp.exp(fl_ref[...])))
        dfl_ref[...] = dfl
        dbf_ref[...] = jnp.sum(dfl, axis=1, keepdims=True)

    return pl.pallas_call(
        body, name="gate_bwd",
        out_shape=(jax.ShapeDtypeStruct(fl_t.shape, F32), jax.ShapeDtypeStruct((H, 1), F32)),
        compiler_params=pltpu.CompilerParams(vmem_limit_bytes=VMEM_LIMIT))(drow_t, dcol_t, fl_t)


def _aug_operands(pieces, H, D):
    T = pieces[0].shape[1]
    dh = D // H
    one = jnp.ones((H, T), BF16)
    qa = jnp.stack(list(pieces) + [one] * N_AUG, axis=-1)
    ka = jnp.stack([one] * N_AUG + [-pc for pc in pieces], axis=-1)

    def place(a):
        a = jnp.pad(a, ((0, 0), (0, 0), (0, dh - 2 * N_AUG))).reshape(H // 2, 2, T, dh)
        return jnp.transpose(a[:, ::-1], (2, 0, 1, 3)).reshape(T, D)

    return place(qa), place(ka)


def _causal_tables(nb, q_outer):
    a, b = [], []
    for o in range(nb):
        inner = range(o + 1) if q_outer else range(o, nb)
        for n in inner:
            a.append(o)
            b.append(n)
    return jnp.asarray(np.array(a, np.int32)), jnp.asarray(np.array(b, np.int32))


def _flash_fwd(q, k, v, qaug, kaug, H):
    T, D = q.shape
    Hp = H // 2
    W = D // Hp
    dh = W // 2
    t = _tile(T, ATTN_TILE)
    nb = T // t
    ii, jj = _causal_tables(nb, q_outer=True)
    n_steps = int(ii.shape[0])

    def body(ii_ref, jj_ref, q_ref, k_ref, v_ref, qa_ref, ka_ref, o_ref, lsea_ref, lseb_ref,
             qm_s, m_s, l_s, acc_s):
        n = pl.program_id(1)
        i, j = ii_ref[n], jj_ref[n]
        in_a = lax.broadcasted_iota(jnp.int32, (1, W), 1) < dh
        top = lax.broadcasted_iota(jnp.int32, (W, 1), 0) < dh

        @pl.when(j == 0)
        def _():
            qv, qa = q_ref[...], qa_ref[...]
            qm_s[0] = jnp.where(in_a, qv, qa)
            qm_s[1] = jnp.where(in_a, qa, qv)
            m_s[...] = jnp.full(m_s.shape, NEG_INF, F32)
            l_s[...] = jnp.zeros_like(l_s)
            acc_s[...] = jnp.zeros_like(acc_s)

        def step(diag):
            kv, ka, vv = k_ref[...], ka_ref[...], v_ref[...]
            zb = jnp.zeros_like(vv)
            if diag:
                keep = lax.broadcasted_iota(jnp.int32, (t, t), 0) <= lax.broadcasted_iota(jnp.int32, (t, t), 1)
            alphas, pv = [], None
            for hh in range(2):
                km = jnp.where(in_a, kv, ka) if hh == 0 else jnp.where(in_a, ka, kv)
                vm = jnp.where(in_a, vv, zb) if hh == 0 else jnp.where(in_a, zb, vv)
                st = _dot_nt(km, qm_s[hh])
                if diag:
                    st = jnp.where(keep, st, NEG_INF)
                m_prev = m_s[hh]
                m_new = jnp.maximum(m_prev, jnp.max(st, axis=0, keepdims=True))
                alpha = jnp.exp(m_prev - m_new)
                pt = jnp.exp(st - m_new)
                l_s[hh] = alpha * l_s[hh] + jnp.sum(pt, axis=0, keepdims=True)
                m_s[hh] = m_new
                d = _dot_tn(vm, pt.astype(BF16))
                pv = d if pv is None else pv + d
                alphas.append(alpha)
            acc_s[...] = acc_s[...] * jnp.where(top, alphas[0], alphas[1]) + pv

        @pl.when(i != j)
        def _():
            step(False)

        @pl.when(i == j)
        def _():
            step(True)
            inv = jnp.where(top, 1.0 / l_s[0], 1.0 / l_s[1])
            o_ref[...] = (acc_s[...] * inv).T.astype(BF16)
            lsea_ref[...] = m_s[0] + jnp.log(l_s[0])
            lseb_ref[...] = m_s[1] + jnp.log(l_s[1])

    qspec = pl.BlockSpec((t, W), lambda hp, n, ii, jj: (ii[n], hp))
    kspec = pl.BlockSpec((t, W), lambda hp, n, ii, jj: (jj[n], hp))
    rspec = pl.BlockSpec((None, 1, t), lambda hp, n, ii, jj: (hp, 0, ii[n]))
    grid_spec = pltpu.PrefetchScalarGridSpec(
        num_scalar_prefetch=2, grid=(Hp, n_steps),
        in_specs=[qspec, kspec, kspec, qspec, kspec],
        out_specs=[qspec, rspec, rspec],
        scratch_shapes=[pltpu.VMEM((2, t, W), BF16), pltpu.VMEM((2, 1, t), F32), pltpu.VMEM((2, 1, t), F32),
                        pltpu.VMEM((W, t), F32)],
    )
    return pl.pallas_call(
        body, name="flash_fwd", grid_spec=grid_spec,
        out_shape=(jax.ShapeDtypeStruct((T, D), BF16), jax.ShapeDtypeStruct((Hp, 1, T), F32),
                   jax.ShapeDtypeStruct((Hp, 1, T), F32)),
        compiler_params=_params(2),
    )(ii, jj, q, k, v, qaug, kaug)


def _attn_delta(do, o, head_sel):
    T, D = do.shape
    H = head_sel.shape[0]
    tm = _tile(T, ROW_TILE_FWD)

    def body(do_ref, o_ref, sel_ref, out_ref):
        prod = do_ref[...].astype(F32) * o_ref[...].astype(F32)
        out_ref[...] = lax.dot_general(sel_ref[...], prod, (((1,), (1,)), ((), ())),
                                       precision=lax.Precision.HIGHEST, preferred_element_type=F32)

    return pl.pallas_call(
        body, name="attn_delta", grid=(T // tm,),
        in_specs=[pl.BlockSpec((tm, D), lambda i: (i, 0)), pl.BlockSpec((tm, D), lambda i: (i, 0)),
                  pl.BlockSpec((H, D), lambda i: (0, 0))],
        out_specs=pl.BlockSpec((H, tm), lambda i: (0, i)),
        out_shape=jax.ShapeDtypeStruct((H, T), F32), compiler_params=_params())(do, o, head_sel)


def _flash_bwd(q, k, v, qaug, kaug, do, lse_a, lse_b, delta, qscale):
    T, D = q.shape
    Hp = lse_a.shape[0]
    W = D // Hp
    dh = W // 2
    t = _tile(T, ATTN_TILE)
    nb = T // t
    jj, ii = _causal_tables(nb, q_outer=False)
    n_steps = int(ii.shape[0])
    row_lane = (dh, 0)
    col_lane = (dh + N_AUG, N_AUG)

    def body(jj_ref, ii_ref, q_ref, k_ref, v_ref, qa_ref, ka_ref, do_ref, lsea_ref, lseb_ref, dla_ref, dlb_ref,
             dqt_ref, dk_ref, dv_ref, dcol_ref, drowa_ref, drowb_ref, dqt_acc, dk_acc, dv_acc):
        n = pl.program_id(1)
        i, j = ii_ref[n], jj_ref[n]
        lane = lax.broadcasted_iota(jnp.int32, (1, W), 1)
        in_a = lane < dh
        top = lax.broadcasted_iota(jnp.int32, (W, 1), 0) < dh

        @pl.when(i == j)
        def _():
            dk_acc[...] = jnp.zeros_like(dk_acc)
            dv_acc[...] = jnp.zeros_like(dv_acc)

        def step(diag):
            qv, qa, kv, ka = q_ref[...], qa_ref[...], k_ref[...], ka_ref[...]
            vv, dov = v_ref[...], do_ref[...]
            zb = jnp.zeros_like(dov)
            if diag:
                keep = lax.broadcasted_iota(jnp.int32, (t, t), 0) <= lax.broadcasted_iota(jnp.int32, (t, t), 1)
            dv_blk = None
            for hh, (lse_ref, dl_ref) in enumerate(((lsea_ref, dla_ref), (lseb_ref, dlb_ref))):
                km = jnp.where(in_a, kv, ka) if hh == 0 else jnp.where(in_a, ka, kv)
                qm = jnp.where(in_a, qv, qa) if hh == 0 else jnp.where(in_a, qa, qv)
                dom = jnp.where(in_a, dov, zb) if hh == 0 else jnp.where(in_a, zb, dov)
                st = _dot_nt(km, qm)
                if diag:
                    st = jnp.where(keep, st, NEG_INF)
                pt = jnp.exp(st - lse_ref[...])
                dst = pt * (_dot_nt(vv, dom) - dl_ref[...])
                ptb, dsb = pt.astype(BF16), dst.astype(BF16)
                d = _dot(ptb, dom)
                dv_blk = d if dv_blk is None else dv_blk + d
                dk_acc[hh] += _dot(dsb, qm)
                dq_blk = _dot_tn(km, dsb)

                @pl.when(j == 0)
                def _():
                    dqt_acc[i, hh] = dq_blk

                @pl.when(j > 0)
                def _():
                    dqt_acc[i, hh] += dq_blk
            dv_acc[...] += dv_blk

        @pl.when(i != j)
        def _():
            step(False)

        @pl.when(i == j)
        def _():
            step(True)

        @pl.when(i == nb - 1)
        def _():
            dka, dkb = dk_acc[0], dk_acc[1]
            dk_ref[...] = jnp.where(in_a, dka, dkb).astype(BF16)
            dv_ref[...] = dv_acc[...].astype(BF16)
            dcol_ref[...] = jnp.where(lane == 0, dka[:, col_lane[0]:col_lane[0] + 1],
                                      jnp.where(lane == 1, dkb[:, col_lane[1]:col_lane[1] + 1], 0.0))

        @pl.when(n == n_steps - 1)
        def _():
            for b in range(nb):
                blk_a, blk_b = dqt_acc[b, 0], dqt_acc[b, 1]
                dqt_ref[:, b * t:(b + 1) * t] = (jnp.where(top, blk_a, blk_b) * qscale).astype(BF16)
                drowa_ref[:, b * t:(b + 1) * t] = blk_a[row_lane[0]:row_lane[0] + 1, :]
                drowb_ref[:, b * t:(b + 1) * t] = blk_b[row_lane[1]:row_lane[1] + 1, :]

    qspec = pl.BlockSpec((t, W), lambda hp, n, jj, ii: (ii[n], hp))
    kspec = pl.BlockSpec((t, W), lambda hp, n, jj, ii: (jj[n], hp))
    pair_row = pl.BlockSpec((None, 1, t), lambda hp, n, jj, ii: (hp, 0, ii[n]))
    whole_row = pl.BlockSpec((None, 1, T), lambda hp, n, jj, ii: (hp, 0, 0))
    grid_spec = pltpu.PrefetchScalarGridSpec(
        num_scalar_prefetch=2, grid=(Hp, n_steps),
        in_specs=[
            qspec, kspec, kspec, qspec, kspec, qspec, pair_row, pair_row,
            pl.BlockSpec((None, 1, t), lambda hp, n, jj, ii: (2 * hp, 0, ii[n])),
            pl.BlockSpec((None, 1, t), lambda hp, n, jj, ii: (2 * hp + 1, 0, ii[n])),
        ],
        out_specs=[
            pl.BlockSpec((W, T), lambda hp, n, jj, ii: (hp, 0)),
            kspec, kspec,
            pl.BlockSpec((None, t, W), lambda hp, n, jj, ii: (hp, jj[n], 0)),
            whole_row, whole_row,
        ],
        scratch_shapes=[pltpu.VMEM((nb, 2, W, t), F32), pltpu.VMEM((2, t, W), F32), pltpu.VMEM((t, W), F32)],
    )
    return pl.pallas_call(
        body, name="flash_bwd", grid_spec=grid_spec,
        out_shape=(jax.ShapeDtypeStruct((D, T), BF16), jax.ShapeDtypeStruct((T, D), BF16),
                   jax.ShapeDtypeStruct((T, D), BF16), jax.ShapeDtypeStruct((Hp, T, W), F32),
                   jax.ShapeDtypeStruct((Hp, 1, T), F32), jax.ShapeDtypeStruct((Hp, 1, T), F32)),
        compiler_params=_params(2),
    )(jj, ii, q, k, v, qaug, kaug, do, lse_a, lse_b, delta, delta)


HBM_SPEC = pl.BlockSpec(memory_space=pltpu.HBM)


def _mesh_pos():
    return lax.axis_index("x"), lax.axis_index("y"), lax.axis_index("c")


def _all_gather_shards(shards):
    n = len(shards)

    def body(*refs):
        in_refs, out_refs = refs[:n], refs[n:2 * n]
        send1, recv1, send2, recv2, lsem = refs[2 * n:]
        x, y, c = _mesh_pos()
        me = 2 * x + y
        chips = [(1 - x, y), (x, 1 - y), (1 - x, 1 - y)]
        local = [pltpu.make_async_copy(in_refs[t], out_refs[t].at[me], lsem.at[t]) for t in range(n)]
        for cp in local:
            cp.start()

        def half(t, chip_idx, pc):
            hr = shards[t].shape[0] // 2
            return out_refs[t].at[chip_idx, pl.ds(pc * hr, hr), :]

        first = []
        for t in range(n):
            hr = shards[t].shape[0] // 2
            for kk, (cx, cy) in enumerate(chips):
                first.append(pltpu.make_async_remote_copy(
                    src_ref=in_refs[t].at[pl.ds(c * hr, hr), :], dst_ref=half(t, me, c),
                    send_sem=send1.at[3 * t + kk], recv_sem=recv1.at[3 * t + kk],
                    device_id=(cx, cy, c), device_id_type=MESH))
        for cp in first:
            cp.start()
        passed = []
        for t in range(n):
            for kk, (cx, cy) in enumerate(chips):
                src_chip = 2 * cx + cy
                landed = half(t, src_chip, c)
                pltpu.make_async_remote_copy(
                    src_ref=landed, dst_ref=landed, send_sem=send1.at[3 * t + kk], recv_sem=recv1.at[3 * t + kk],
                    device_id=(cx, cy, c), device_id_type=MESH).wait_recv()
                fwd = pltpu.make_async_remote_copy(
                    src_ref=landed, dst_ref=landed, send_sem=send2.at[3 * t + kk], recv_sem=recv2.at[3 * t + kk],
                    device_id=(x, y, 1 - c), device_id_type=MESH)
                fwd.start()
                passed.append(fwd)
        for t in range(n):
            for kk, (cx, cy) in enumerate(chips):
                other = half(t, 2 * cx + cy, 1 - c)
                pltpu.make_async_remote_copy(
                    src_ref=other, dst_ref=other, send_sem=send2.at[3 * t + kk], recv_sem=recv2.at[3 * t + kk],
                    device_id=(x, y, 1 - c), device_id_type=MESH).wait_recv()
        for cp in first + passed:
            cp.wait_send()
        for cp in local:
            cp.wait()

    return pl.pallas_call(
        body, name="weights_all_gather",
        out_shape=[jax.ShapeDtypeStruct((N_CHIPS,) + s.shape, s.dtype) for s in shards],
        in_specs=[HBM_SPEC] * n, out_specs=[HBM_SPEC] * n,
        scratch_shapes=[pltpu.SemaphoreType.DMA((3 * n,)), pltpu.SemaphoreType.DMA((3 * n,)),
                        pltpu.SemaphoreType.DMA((3 * n,)), pltpu.SemaphoreType.DMA((3 * n,)),
                        pltpu.SemaphoreType.DMA((n,))],
    )(*shards)


def _sibling_swap_halves(g):
    _, M, C = g.shape
    hr = M // 2

    def body(g_ref, r_ref, ssem, rsem):
        x, y, c = _mesh_pos()
        cps = [pltpu.make_async_remote_copy(
            src_ref=g_ref.at[s, pl.ds((1 - c) * hr, hr), :], dst_ref=r_ref.at[s],
            send_sem=ssem.at[s], recv_sem=rsem.at[s], device_id=(x, y, 1 - c), device_id_type=MESH)
            for s in range(N_CHIPS)]
        for cp in cps:
            cp.start()
        for cp in cps:
            cp.wait()

    return pl.pallas_call(
        body, name="grads_sibling_swap", out_shape=jax.ShapeDtypeStruct((N_CHIPS, hr, C), F32),
        in_specs=[HBM_SPEC], out_specs=HBM_SPEC,
        scratch_shapes=[pltpu.SemaphoreType.DMA((N_CHIPS,)), pltpu.SemaphoreType.DMA((N_CHIPS,))],
    )(g)


def _pair_add(g, r, c_idx):
    _, M, C = g.shape
    hr = M // 2
    tr = PACK_ROW_TILE
    nbk = hr // tr

    def body(c_ref, g_ref, r_ref, o_ref):
        o_ref[...] = (g_ref[...] + r_ref[...]).astype(BF16)

    grid_spec = pltpu.PrefetchScalarGridSpec(
        num_scalar_prefetch=1, grid=(N_CHIPS, nbk),
        in_specs=[pl.BlockSpec((None, tr, C), lambda s, i, c: (s, c[0] * nbk + i, 0)),
                  pl.BlockSpec((None, tr, C), lambda s, i, c: (s, i, 0))],
        out_specs=pl.BlockSpec((None, tr, C), lambda s, i, c: (s, i, 0)),
    )
    return pl.pallas_call(body, name="grads_pair_add", grid_spec=grid_spec,
                          out_shape=jax.ShapeDtypeStruct((N_CHIPS, hr, C), BF16),
                          compiler_params=_params(2))(c_idx, g, r)


def _chip_scatter(pp):
    def body(p_ref, r_ref, ssem, rsem, lsem):
        x, y, c = _mesh_pos()
        me = 2 * x + y
        chips = [(1 - x, y), (x, 1 - y), (1 - x, 1 - y)]
        own = pltpu.make_async_copy(p_ref.at[me], r_ref.at[me], lsem)
        own.start()
        cps = [pltpu.make_async_remote_copy(
            src_ref=p_ref.at[2 * cx + cy], dst_ref=r_ref.at[me], send_sem=ssem.at[kk], recv_sem=rsem.at[kk],
            device_id=(cx, cy, c), device_id_type=MESH) for kk, (cx, cy) in enumerate(chips)]
        for cp in cps:
            cp.start()
        for kk, (cx, cy) in enumerate(chips):
            got = r_ref.at[2 * cx + cy]
            pltpu.make_async_remote_copy(
                src_ref=got, dst_ref=got, send_sem=ssem.at[kk], recv_sem=rsem.at[kk],
                device_id=(cx, cy, c), device_id_type=MESH).wait_recv()
        for cp in cps:
            cp.wait_send()
        own.wait()

    return pl.pallas_call(
        body, name="grads_chip_scatter", out_shape=jax.ShapeDtypeStruct(pp.shape, pp.dtype),
        in_specs=[HBM_SPEC], out_specs=HBM_SPEC,
        scratch_shapes=[pltpu.SemaphoreType.DMA((3,)), pltpu.SemaphoreType.DMA((3,)), pltpu.SemaphoreType.DMA],
    )(pp)


def _chip_sum(r):
    _, hr, C = r.shape
    tr = PACK_ROW_TILE

    def body(r_ref, o_ref):
        r0, r1, r2, r3 = (r_ref[s].astype(F32) for s in range(N_CHIPS))
        o_ref[...] = ((r0 + r1) + r2) + r3

    return pl.pallas_call(
        body, name="grads_chip_sum", grid=(hr // tr,),
        in_specs=[pl.BlockSpec((N_CHIPS, tr, C), lambda i: (0, i, 0))],
        out_specs=pl.BlockSpec((tr, C), lambda i: (i, 0)),
        out_shape=jax.ShapeDtypeStruct((hr, C), F32), compiler_params=_params())(r)


def _sibling_join(qh):
    hr, C = qh.shape

    def body(q_ref, o_ref, ssem, rsem, lsem):
        x, y, c = _mesh_pos()
        mine = o_ref.at[pl.ds(c * hr, hr), :]
        own = pltpu.make_async_copy(q_ref, mine, lsem)
        own.start()
        cp = pltpu.make_async_remote_copy(src_ref=q_ref, dst_ref=mine, send_sem=ssem, recv_sem=rsem,
                                          device_id=(x, y, 1 - c), device_id_type=MESH)
        cp.start()
        theirs = o_ref.at[pl.ds((1 - c) * hr, hr), :]
        pltpu.make_async_remote_copy(src_ref=theirs, dst_ref=theirs, send_sem=ssem, recv_sem=rsem,
                                     device_id=(x, y, 1 - c), device_id_type=MESH).wait_recv()
        cp.wait_send()
        own.wait()

    return pl.pallas_call(
        body, name="grads_sibling_join", out_shape=jax.ShapeDtypeStruct((2 * hr, C), F32),
        in_specs=[HBM_SPEC], out_specs=HBM_SPEC,
        scratch_shapes=[pltpu.SemaphoreType.DMA, pltpu.SemaphoreType.DMA, pltpu.SemaphoreType.DMA],
    )(qh)


def _adamw(w, g, m, v):
    M, C = w.shape
    tr = PACK_ROW_TILE

    def body(w_ref, g_ref, m_ref, v_ref, d_ref, mo_ref, vo_ref):
        gv = g_ref[...]
        mn = ADAM_B1 * m_ref[...] + (1.0 - ADAM_B1) * gv
        vn = ADAM_B2 * v_ref[...] + (1.0 - ADAM_B2) * (gv * gv)
        m_hat = mn / (1.0 - ADAM_B1 ** ADAM_STEP)
        v_hat = vn / (1.0 - ADAM_B2 ** ADAM_STEP)
        d_ref[...] = -ADAM_LR * (m_hat / (jnp.sqrt(v_hat) + ADAM_EPS) + ADAM_WD * w_ref[...])
        mo_ref[...] = mn
        vo_ref[...] = vn

    spec = pl.BlockSpec((tr, C), lambda i: (i, 0))
    return pl.pallas_call(
        body, name="adamw", grid=(M // tr,), in_specs=[spec] * 4, out_specs=[spec] * 3,
        out_shape=[jax.ShapeDtypeStruct((M, C), F32)] * 3, compiler_params=_params())(w, g, m, v)


WEIGHT_NAMES = ("norm_g", "w_attn_in", "b_forget", "w_attn_out", "w_conv_in", "conv_w", "w_conv_out",
                "w_mlp_up", "w_mlp_down", "w_ple_proj", "w_ple_gate")
COL_SHARDED = ("norm_g", "w_attn_in", "w_conv_in", "conv_w", "w_mlp_up", "w_ple_proj")
ROW_SHARDED = ("w_attn_out", "w_conv_out", "w_mlp_down", "w_ple_gate")


def _unshard(name, gathered, shard_shape):
    a = gathered.reshape((N_CHIPS,) + tuple(shard_shape))
    if name in COL_SHARDED:
        a = jnp.moveaxis(a, 0, -2)
        return a.reshape(a.shape[:-2] + (N_CHIPS * shard_shape[-1],))
    a = jnp.moveaxis(a, 0, 1)
    return a.reshape((shard_shape[0], N_CHIPS * shard_shape[1], shard_shape[2]))


def _to_shard_major(name, full):
    if name == "b_forget":
        return jnp.broadcast_to(full.reshape(1, -1), (N_CHIPS, full.size))
    if name in COL_SHARDED:
        a = full.reshape(full.shape[:-1] + (N_CHIPS, full.shape[-1] // N_CHIPS))
        a = jnp.moveaxis(a, -2, 0)
    else:
        a = full.reshape((full.shape[0], N_CHIPS, full.shape[1] // N_CHIPS, full.shape[2]))
        a = jnp.moveaxis(a, 1, 0)
    return a.reshape(N_CHIPS, -1)


def _pack_layout(shard_shapes):
    offs, row = {}, 0
    for name in WEIGHT_NAMES:
        n = int(np.prod(shard_shapes[name]))
        rows = -(-n // PACK_COLS)
        offs[name] = (row, rows, n)
        row += rows
    total = -(-row // PACK_ROW_ALIGN) * PACK_ROW_ALIGN
    return offs, total


def _pack(flat_by_name, offs, total_rows, lead=()):
    parts, row = [], 0
    for name in WEIGHT_NAMES:
        _, rows, n = offs[name]
        a = flat_by_name[name].astype(F32)
        pad = rows * PACK_COLS - n
        if pad:
            a = jnp.pad(a, [(0, 0)] * len(lead) + [(0, pad)])
        parts.append(a.reshape(tuple(lead) + (rows, PACK_COLS)))
        row += rows
    if total_rows > row:
        parts.append(jnp.zeros(tuple(lead) + (total_rows - row, PACK_COLS), F32))
    return jnp.concatenate(parts, axis=len(lead))


def _unpack(packed, offs, shard_shapes):
    out = {}
    for name in WEIGHT_NAMES:
        row, rows, n = offs[name]
        out[name] = packed[row:row + rows].reshape(-1)[:n].reshape(shard_shapes[name])
    return out


def kernel(x, p, norm_g, w_attn_in, b_forget, w_attn_out, w_conv_in, conv_w, w_conv_out, w_mlp_up, w_mlp_down, w_ple_proj, w_ple_gate, loss_target, m_norm_g, m_w_attn_in, m_b_forget, m_w_attn_out, m_w_conv_in, m_conv_w, m_w_conv_out, m_w_mlp_up, m_w_mlp_down, m_w_ple_proj, m_w_ple_gate, v_norm_g, v_w_attn_in, v_b_forget, v_w_attn_out, v_w_conv_in, v_conv_w, v_w_conv_out, v_w_mlp_up, v_w_mlp_down, v_w_ple_proj, v_w_ple_gate):
    w_local = dict(norm_g=norm_g, w_attn_in=w_attn_in, b_forget=b_forget, w_attn_out=w_attn_out,
                   w_conv_in=w_conv_in, conv_w=conv_w, w_conv_out=w_conv_out, w_mlp_up=w_mlp_up,
                   w_mlp_down=w_mlp_down, w_ple_proj=w_ple_proj, w_ple_gate=w_ple_gate)
    m_local = dict(norm_g=m_norm_g, w_attn_in=m_w_attn_in, b_forget=m_b_forget, w_attn_out=m_w_attn_out,
                   w_conv_in=m_w_conv_in, conv_w=m_conv_w, w_conv_out=m_w_conv_out, w_mlp_up=m_w_mlp_up,
                   w_mlp_down=m_w_mlp_down, w_ple_proj=m_w_ple_proj, w_ple_gate=m_w_ple_gate)
    v_local = dict(norm_g=v_norm_g, w_attn_in=v_w_attn_in, b_forget=v_b_forget, w_attn_out=v_w_attn_out,
                   w_conv_in=v_w_conv_in, conv_w=v_conv_w, w_conv_out=v_w_conv_out, w_mlp_up=v_w_mlp_up,
                   w_mlp_down=v_w_mlp_down, w_ple_proj=v_w_ple_proj, w_ple_gate=v_w_ple_gate)
    shard_shapes = {k: tuple(a.shape) for k, a in w_local.items()}

    xs = x[0]
    target = loss_target[0]
    T, D = xs.shape
    depth = p.shape[0]
    H = b_forget.shape[1]
    qscale = float(D // H) ** -0.5
    head_sel = (jnp.arange(D)[None, :] // (D // H) == jnp.arange(H)[:, None]).astype(F32)

    big = [n for n in WEIGHT_NAMES if n not in ("norm_g", "conv_w", "b_forget")]
    small = jnp.concatenate([norm_g.reshape(-1), conv_w.reshape(-1)])
    n_small = small.shape[0]
    small_rows = -(-n_small // (LANES * 16)) * 16
    small = jnp.pad(small, (0, small_rows * LANES - n_small)).reshape(small_rows, LANES)
    shards = [w_local[n].astype(BF16).reshape(-1, shard_shapes[n][-1]) for n in big] + [small]
    gathered = _all_gather_shards(shards)
    full = {n: _unshard(n, g, shard_shapes[n]) for n, g in zip(big, gathered[:-1])}
    gs = gathered[-1].reshape(N_CHIPS, -1)
    full["norm_g"] = _unshard("norm_g", gs[:, :norm_g.size], shard_shapes["norm_g"])
    full["conv_w"] = _unshard("conv_w", gs[:, norm_g.size:n_small], shard_shapes["conv_w"])
    gains = full["norm_g"]

    def gain(i, k):
        return gains[i, k].reshape(1, D)

    def taps(j):
        return jnp.pad(full["conv_w"][j], ((0, 5), (0, 0)))

    saved = []
    h = xs
    for i in range(depth):
        j = i // 2
        s = {"x0": h}
        if i % 2 == 0:
            w_in = full["w_attn_in"][j]
            wqkv = w_in[:, :3 * D]
            wf = jnp.pad(w_in[:, 3 * D:], ((0, 0), (0, LANES - H)))
            bf = jnp.pad(b_forget[j].reshape(1, H), ((0, 0), (0, LANES - H)))
            q, k, v, fl, lf = _attn_in_fwd(h, gain(i, 0), wqkv, wf, bf, qscale)
            qaug, kaug = _aug_operands(_gate_cumsum(lf[:, :H].T), H, D)
            o, lse_a, lse_b = _flash_fwd(q, k, v, qaug, kaug, H)
            s.update(q=q, k=k, v=v, fl=fl, qaug=qaug, kaug=kaug, o=o, lse_a=lse_a, lse_b=lse_b, wqkv=wqkv, wf=wf)
            mix_in, w_out = o, full["w_attn_out"][j]
        else:
            b, c, u, zc, y = _conv_in_fwd(h, gain(i, 0), full["w_conv_in"][j], taps(j))
            s.update(b=b, c=c, u=u, zc=zc, y=y)
            mix_in, w_out = y, full["w_conv_out"][j]
        m1, x1 = _out_proj_fwd("mixer_out_fwd", mix_in, w_out, gain(i, 1), h)
        up, a = _mlp_up_fwd(x1, gain(i, 2), full["w_mlp_up"][i])
        m3, x2 = _out_proj_fwd("mlp_down_fwd", a, full["w_mlp_down"][i], gain(i, 3), x1)
        gl, pe, x3 = _ple_fwd(x2, p[i, 0], gain(i, 4), gain(i, 5), full["w_ple_gate"][i], full["w_ple_proj"][i])
        s.update(m1=m1, x1=x1, up=up, a=a, m3=m3, x2=x2, gl=gl, pe=pe, w_out=w_out)
        saved.append(s)
        h = x3

    dh, loss_blk = _loss_fwd_bwd(h, target)
    loss = lax.psum(loss_blk[0, 0], ("x", "y", "c"))

    g_gain = [[None] * 6 for _ in range(depth)]
    grads = {n: [None] * w_local[n].shape[0] for n in WEIGHT_NAMES if n != "norm_g"}
    for i in reversed(range(depth)):
        j = i // 2
        s = saved[i]
        dx2, dwp, dwg, g_gain[i][4], g_gain[i][5] = _ple_bwd(
            dh, s["x2"], p[i, 0], s["gl"], s["pe"], gain(i, 4), gain(i, 5), full["w_ple_gate"][i].T)
        grads["w_ple_proj"][i], grads["w_ple_gate"][i] = dwp, dwg
        dup, dwd, g_gain[i][3] = _out_proj_bwd(
            "mlp_down_bwd", dx2, s["m3"], gain(i, 3), [s["a"], s["up"]], full["w_mlp_down"][i].T, "relu2")
        grads["w_mlp_down"][i] = dwd
        dx1, dwu, g_gain[i][2] = _in_proj_bwd(
            "mlp_up_bwd", [dup], [full["w_mlp_up"][i].T], s["x1"], gain(i, 2), dx2)
        grads["w_mlp_up"][i] = dwu
        if i % 2 == 0:
            do, dwo, g_gain[i][1] = _out_proj_bwd(
                "attn_out_bwd", dx1, s["m1"], gain(i, 1), [s["o"]], s["w_out"].T, "plain")
            grads["w_attn_out"][j] = dwo
            delta = _attn_delta(do, s["o"], head_sel).reshape(H, 1, T)
            dqt, dk, dv, dcx, dra, drb = _flash_bwd(s["q"], s["k"], s["v"], s["qaug"], s["kaug"], do,
                                                    s["lse_a"], s["lse_b"], delta, qscale)
            dcol = jnp.moveaxis(dcx[:, :, :2], 2, 1).reshape(H, T)
            drow = jnp.concatenate([dra, drb], axis=1).reshape(H, T)
            dfl_t, dbf = _gate_bwd(drow, dcol, s["fl"][:, :H].T)
            grads["b_forget"][j] = dbf.reshape(H)
            dfl = jnp.pad(dfl_t.T, ((0, 0), (0, LANES - H))).astype(BF16)
            wqkv_t = s["wqkv"].T
            dh, dwq, dwk, dwv, dwf, g_gain[i][0] = _in_proj_bwd(
                "attn_in_bwd", [dqt.T, dk, dv, dfl], [wqkv_t[:D], wqkv_t[D:2 * D], wqkv_t[2 * D:], s["wf"].T],
                s["x0"], gain(i, 0), dx1)
            grads["w_attn_in"][j] = jnp.concatenate([dwq, dwk, dwv, dwf[:, :H]], axis=1)
        else:
            dy, dwo, g_gain[i][1] = _out_proj_bwd(
                "conv_out_bwd", dx1, s["m1"], gain(i, 1), [s["y"]], s["w_out"].T, "plain_f32")
            grads["w_conv_out"][j] = dwo
            db, dc, du, dcw = _conv_bwd(dy, s["b"], s["c"], s["u"], s["zc"], taps(j))
            grads["conv_w"][j] = dcw[:conv_w.shape[1]]
            w_t = full["w_conv_in"][j].T
            dh, dwb, dwc, dwu2, g_gain[i][0] = _in_proj_bwd(
                "conv_in_bwd", [db, dc, du], [w_t[:D], w_t[D:2 * D], w_t[2 * D:]], s["x0"], gain(i, 0), dx1)
            grads["w_conv_in"][j] = jnp.concatenate([dwb, dwc, dwu2], axis=1)
    grad_x = dh.reshape(x.shape)

    grad_full = {n: jnp.stack(grads[n]) for n in grads}
    grad_full["norm_g"] = jnp.stack([jnp.concatenate(row, axis=0) for row in g_gain])

    offs, total_rows = _pack_layout(shard_shapes)
    g_packed = _pack({n: _to_shard_major(n, grad_full[n]) for n in WEIGHT_NAMES}, offs, total_rows, lead=(N_CHIPS,))
    c_idx = lax.axis_index("c").astype(jnp.int32).reshape(1)
    pair = _pair_add(g_packed, _sibling_swap_halves(g_packed), c_idx)
    g_red = _sibling_join(_chip_sum(_chip_scatter(pair)))
    w_p = _pack({n: w_local[n].reshape(-1) for n in WEIGHT_NAMES}, offs, total_rows)
    m_p = _pack({n: m_local[n].reshape(-1) for n in WEIGHT_NAMES}, offs, total_rows)
    v_p = _pack({n: v_local[n].reshape(-1) for n in WEIGHT_NAMES}, offs, total_rows)
    d_p, mn_p, vn_p = _adamw(w_p, g_red, m_p, v_p)
    g_out = _unpack(g_red, offs, shard_shapes)
    d_out = _unpack(d_p, offs, shard_shapes)
    m_out = _unpack(mn_p, offs, shard_shapes)
    v_out = _unpack(vn_p, offs, shard_shapes)
    return (loss, grad_x, *[g_out[n] for n in WEIGHT_NAMES], *[d_out[n] for n in WEIGHT_NAMES],
            *[m_out[n] for n in WEIGHT_NAMES], *[v_out[n] for n in WEIGHT_NAMES])
```

```python
import numpy as np
import jax
import jax.numpy as jnp
from jax import lax
from jax.experimental import pallas as pl
from jax.experimental.pallas import tpu as pltpu

F32 = jnp.float32
BF16 = jnp.bfloat16
MESH = pl.DeviceIdType.MESH

RMS_EPS = 1e-6
NEG_INF = -1e30
ADAM_LR = 0.001
ADAM_B1 = 0.9
ADAM_B2 = 0.999
ADAM_EPS = 1e-08
ADAM_WD = 0.01
ADAM_STEP = 10

N_CHIPS = 4
LANES = 128
ROW_TILE_FWD = 512
ROW_TILE_BWD = 256
ATTN_TILE = 512
ATTN_KEY_STRIP = 128
ATTN_QUERY_STRIP = 256
COL_CHUNK = 512
PACK_COLS = 1024
PACK_ROW_TILE = 256
PACK_ROW_ALIGN = 2 * PACK_ROW_TILE
LOCAL_COPY_PIECES = 16
N_AUG = 3
VMEM_LIMIT = 56 * 1024 * 1024


def _tile(n, pref):
    return pref if n % pref == 0 else n


def _dot(a, b):
    return jnp.dot(a, b, preferred_element_type=F32)


def _dot_tn(a, b):
    return lax.dot_general(a, b, (((0,), (0,)), ((), ())), preferred_element_type=F32)


def _dot_nt(a, b):
    return lax.dot_general(a, b, (((1,), (1,)), ((), ())), preferred_element_type=F32)


def _rms_fwd(x, g):
    r = lax.rsqrt(jnp.mean(x * x, axis=-1, keepdims=True) + RMS_EPS)
    return (x * r) * g


def _rms_bwd(x, g, dy):
    r = lax.rsqrt(jnp.mean(x * x, axis=-1, keepdims=True) + RMS_EPS)
    xh = x * r
    dyg = dy * g
    dx = r * (dyg - xh * jnp.mean(dyg * xh, axis=-1, keepdims=True))
    return dx, jnp.sum(dy * xh, axis=0, keepdims=True)


def _params(n_axes=1):
    return pltpu.CompilerParams(dimension_semantics=("arbitrary",) * n_axes, vmem_limit_bytes=VMEM_LIMIT)


def _row_call(name, body, n_rows, tm, row_ins, const_ins, row_outs, acc_outs=(), scratch=(), reverse=False):
    nb = n_rows // tm
    rmap = (lambda i: (nb - 1 - i, 0)) if reverse else (lambda i: (i, 0))

    def whole(shape):
        nd = len(shape)
        return pl.BlockSpec(tuple(shape), lambda i: (0,) * nd)

    in_specs = [pl.BlockSpec((tm, a.shape[1]), rmap) for a in row_ins] + [whole(a.shape) for a in const_ins]
    out_shape = [jax.ShapeDtypeStruct((n_rows, w), dt) for (w, dt) in row_outs]
    out_shape += [jax.ShapeDtypeStruct(tuple(s), dt) for (s, dt) in acc_outs]
    out_specs = [pl.BlockSpec((tm, w), rmap) for (w, _) in row_outs] + [whole(s) for (s, _) in acc_outs]
    return pl.pallas_call(
        body, name=name, grid=(nb,), in_specs=in_specs, out_specs=out_specs, out_shape=out_shape,
        scratch_shapes=list(scratch), compiler_params=_params(),
    )(*row_ins, *const_ins)


def _attn_in_fwd(x, g, wqkv, wf, bf, qscale):
    T, D = x.shape
    tm, ch = _tile(T, ROW_TILE_FWD), _tile(D, COL_CHUNK)

    def body(x_ref, g_ref, w_ref, wf_ref, bf_ref, q_ref, k_ref, v_ref, fl_ref, lf_ref):
        h = _rms_fwd(x_ref[...], g_ref[...]).astype(BF16)
        for part, o_ref in enumerate((q_ref, k_ref, v_ref)):
            for n0 in range(0, D, ch):
                r = _dot(h, w_ref[:, part * D + n0:part * D + n0 + ch])
                if part == 0:
                    r = r * qscale
                o_ref[:, n0:n0 + ch] = r.astype(BF16)
        fl = _dot(h, wf_ref[...]) + bf_ref[...]
        fl_ref[...] = fl
        lf_ref[...] = jnp.minimum(fl, 0.0) - jnp.log1p(jnp.exp(-jnp.abs(fl)))

    return _row_call("attn_in_fwd", body, T, tm, [x], [g, wqkv, wf, bf],
                     [(D, BF16), (D, BF16), (D, BF16), (LANES, F32), (LANES, F32)])


def _conv_in_fwd(x, g, w, cw):
    T, D = x.shape
    tm, ch = _tile(T, ROW_TILE_FWD), _tile(D, COL_CHUNK)

    def body(x_ref, g_ref, w_ref, cw_ref, b_ref, c_ref, u_ref, zc_ref, y_ref, tail_ref):
        i = pl.program_id(0)

        @pl.when(i == 0)
        def _():
            tail_ref[...] = jnp.zeros_like(tail_ref)

        h = _rms_fwd(x_ref[...], g_ref[...]).astype(BF16)
        for part, o_ref in enumerate((b_ref, c_ref, u_ref)):
            for n0 in range(0, D, ch):
                o_ref[:, n0:n0 + ch] = _dot(h, w_ref[:, part * D + n0:part * D + n0 + ch])
        z = c_ref[...] * u_ref[...]
        row = lax.broadcasted_iota(jnp.int32, (tm, 1), 0)
        t6, t7 = tail_ref[6:7, :], tail_ref[7:8, :]
        z1 = jnp.where(row == 0, t7, pltpu.roll(z, 1, axis=0))
        z2 = jnp.where(row == 0, t6, jnp.where(row == 1, t7, pltpu.roll(z, 2, axis=0)))
        zc = cw_ref[0:1, :] * z2 + cw_ref[1:2, :] * z1 + cw_ref[2:3, :] * z
        zc_ref[...] = zc
        y_ref[...] = (b_ref[...] * zc).astype(BF16)
        tail_ref[...] = z[tm - 8:tm, :]

    return _row_call("conv_in_fwd", body, T, tm, [x], [g, w, cw],
                     [(D, F32), (D, F32), (D, F32), (D, F32), (D, BF16)], scratch=[pltpu.VMEM((8, D), F32)])


def _mlp_up_fwd(x, g, w):
    T, D = x.shape
    F = w.shape[1]
    tm, ch = _tile(T, ROW_TILE_FWD), _tile(F, COL_CHUNK)

    def body(x_ref, g_ref, w_ref, up_ref, a_ref):
        h = _rms_fwd(x_ref[...], g_ref[...]).astype(BF16)
        for n0 in range(0, F, ch):
            r = _dot(h, w_ref[:, n0:n0 + ch])
            up_ref[:, n0:n0 + ch] = r.astype(BF16)
            rl = jnp.maximum(r, 0.0)
            a_ref[:, n0:n0 + ch] = (rl * rl).astype(BF16)

    return _row_call("mlp_up_fwd", body, T, tm, [x], [g, w], [(F, BF16), (F, BF16)])


def _out_proj_fwd(name, a, w, g, x):
    T, D = x.shape
    tm = _tile(T, ROW_TILE_FWD)

    def body(a_ref, x_ref, w_ref, g_ref, m_ref, xn_ref):
        m = _dot(a_ref[...], w_ref[...])
        m_ref[...] = m
        xn_ref[...] = x_ref[...] + _rms_fwd(m, g_ref[...])

    return _row_call(name, body, T, tm, [a, x], [w, g], [(D, F32), (D, F32)])


def _ple_fwd(x, p, g4, g5, wg, wp):
    T, D = x.shape
    tm = _tile(T, ROW_TILE_FWD)

    def body(x_ref, p_ref, g4_ref, g5_ref, wg_ref, wp_ref, gl_ref, pe_ref, xn_ref):
        xv = x_ref[...]
        gl = _dot(_rms_fwd(xv, g4_ref[...]).astype(BF16), wg_ref[...])
        pe = _dot(p_ref[...].astype(BF16), wp_ref[...])
        gl_ref[...] = gl
        pe_ref[...] = pe
        e = pe * (1.0 / (1.0 + jnp.exp(-gl)))
        xn_ref[...] = xv + _rms_fwd(e, g5_ref[...])

    return _row_call("ple_fwd", body, T, tm, [x, p], [g4, g5, wg, wp], [(D, F32), (D, F32), (D, F32)])


def _loss_fwd_bwd(y, target):
    T, D = y.shape
    tm = _tile(T, ROW_TILE_FWD)

    def body(y_ref, t_ref, dy_ref, loss_ref):
        @pl.when(pl.program_id(0) == 0)
        def _():
            loss_ref[...] = jnp.zeros_like(loss_ref)

        err = y_ref[...] - t_ref[...]
        dy_ref[...] = err * (1.0 / D)
        part = 0.5 * jnp.sum(jnp.mean(err * err, axis=-1, keepdims=True), axis=0, keepdims=True)
        loss_ref[...] += jnp.broadcast_to(part, loss_ref.shape)

    return _row_call("loss", body, T, tm, [y, target], [], [(D, F32)], acc_outs=[((8, LANES), F32)])


def _out_proj_bwd(name, dres, m, g, a_ins, wt, mode):
    T, D = dres.shape
    Ka = wt.shape[1]
    tm, ch = _tile(T, ROW_TILE_BWD), _tile(Ka, COL_CHUNK)
    out_dt = F32 if mode == "plain_f32" else BF16

    def body(dres_ref, m_ref, *rest):
        a_ref = rest[0]
        up_ref = rest[1] if mode == "relu2" else None
        k = len(a_ins)
        g_ref, wt_ref, da_ref, dw_ref, dg_ref = rest[k:k + 5]

        @pl.when(pl.program_id(0) == 0)
        def _():
            dw_ref[...] = jnp.zeros_like(dw_ref)
            dg_ref[...] = jnp.zeros_like(dg_ref)

        dm, dgp = _rms_bwd(m_ref[...], g_ref[...], dres_ref[...])
        dg_ref[...] += dgp
        dmb = dm.astype(BF16)
        for n0 in range(0, Ka, ch):
            da = _dot(dmb, wt_ref[:, n0:n0 + ch])
            if mode == "relu2":
                da = da * (2.0 * jnp.maximum(up_ref[:, n0:n0 + ch].astype(F32), 0.0))
            da_ref[:, n0:n0 + ch] = da.astype(out_dt)
            dw_ref[n0:n0 + ch, :] += _dot_tn(a_ref[:, n0:n0 + ch], dmb)

    return _row_call(name, body, T, tm, [dres, m] + list(a_ins), [g, wt],
                     [(Ka, out_dt)], acc_outs=[((Ka, D), F32), ((1, D), F32)])


def _in_proj_bwd(name, pieces, wts, x, g, dres):
    T, D = x.shape
    tm = _tile(T, ROW_TILE_BWD)
    k = len(pieces)
    widths = [pc.shape[1] for pc in pieces]

    def body(*refs):
        pc_refs = refs[:k]
        x_ref, dres_ref, g_ref = refs[k:k + 3]
        wt_refs = refs[k + 3:2 * k + 3]
        dx_ref = refs[2 * k + 3]
        dw_refs = refs[2 * k + 4:3 * k + 4]
        dg_ref = refs[3 * k + 4]

        @pl.when(pl.program_id(0) == 0)
        def _():
            for r in dw_refs:
                r[...] = jnp.zeros_like(r)
            dg_ref[...] = jnp.zeros_like(dg_ref)

        xv, gv = x_ref[...], g_ref[...]
        hb = _rms_fwd(xv, gv).astype(BF16)
        dh = None
        for pc_ref, wt_ref, dw_ref, n in zip(pc_refs, wt_refs, dw_refs, widths):
            d = _dot(pc_ref[...], wt_ref[...])
            dh = d if dh is None else dh + d
            ch = _tile(n, COL_CHUNK)
            for n0 in range(0, n, ch):
                dw_ref[:, n0:n0 + ch] += _dot_tn(hb, pc_ref[:, n0:n0 + ch])
        dxn, dgp = _rms_bwd(xv, gv, dh)
        dg_ref[...] += dgp
        dx_ref[...] = dres_ref[...] + dxn

    return _row_call(name, body, T, tm, list(pieces) + [x, dres], [g] + list(wts), [(D, F32)],
                     acc_outs=[((D, n), F32) for n in widths] + [((1, D), F32)])


def _conv_bwd(dy, b, c, u, zc, cw):
    T, D = dy.shape
    tm = _tile(T, ROW_TILE_BWD)

    def body(dy_ref, b_ref, c_ref, u_ref, zc_ref, cw_ref, db_ref, dc_ref, du_ref, dcw_ref, head_ref):
        @pl.when(pl.program_id(0) == 0)
        def _():
            head_ref[...] = jnp.zeros_like(head_ref)
            dcw_ref[...] = jnp.zeros_like(dcw_ref)

        dyv, cv, uv = dy_ref[...], c_ref[...], u_ref[...]
        db_ref[...] = (dyv * zc_ref[...]).astype(BF16)
        dzc = dyv * b_ref[...]
        row = lax.broadcasted_iota(jnp.int32, (tm, 1), 0)
        h0, h1 = head_ref[0:1, :], head_ref[1:2, :]
        d1 = jnp.where(row == tm - 1, h0, pltpu.roll(dzc, tm - 1, axis=0))
        d2 = jnp.where(row == tm - 1, h1, jnp.where(row == tm - 2, h0, pltpu.roll(dzc, tm - 2, axis=0)))
        dz = cw_ref[2:3, :] * dzc + cw_ref[1:2, :] * d1 + cw_ref[0:1, :] * d2
        dc_ref[...] = (dz * uv).astype(BF16)
        du_ref[...] = (dz * cv).astype(BF16)
        z = cv * uv
        dcw_ref[0:1, :] += jnp.sum(d2 * z, axis=0, keepdims=True)
        dcw_ref[1:2, :] += jnp.sum(d1 * z, axis=0, keepdims=True)
        dcw_ref[2:3, :] += jnp.sum(dzc * z, axis=0, keepdims=True)
        head_ref[...] = dzc[0:8, :]

    return _row_call("conv_bwd", body, T, tm, [dy, b, c, u, zc], [cw], [(D, BF16), (D, BF16), (D, BF16)],
                     acc_outs=[((8, D), F32)], scratch=[pltpu.VMEM((8, D), F32)], reverse=True)


def _ple_bwd(dres, x, p, gl, pe, g4, g5, wgt):
    T, D = x.shape
    P = p.shape[1]
    tm = _tile(T, ROW_TILE_BWD)

    def body(dres_ref, x_ref, p_ref, gl_ref, pe_ref, g4_ref, g5_ref, wgt_ref,
             dx_ref, dwp_ref, dwg_ref, dg4_ref, dg5_ref):
        @pl.when(pl.program_id(0) == 0)
        def _():
            for r in (dwp_ref, dwg_ref, dg4_ref, dg5_ref):
                r[...] = jnp.zeros_like(r)

        dr, xv, pe_v = dres_ref[...], x_ref[...], pe_ref[...]
        gate = 1.0 / (1.0 + jnp.exp(-gl_ref[...]))
        de, dg5p = _rms_bwd(pe_v * gate, g5_ref[...], dr)
        dg5_ref[...] += dg5p
        dpe = (de * gate).astype(BF16)
        dgl = (de * pe_v * gate * (1.0 - gate)).astype(BF16)
        dwp_ref[...] += _dot_tn(p_ref[...].astype(BF16), dpe)
        g4v = g4_ref[...]
        dwg_ref[...] += _dot_tn(_rms_fwd(xv, g4v).astype(BF16), dgl)
        dxn, dg4p = _rms_bwd(xv, g4v, _dot(dgl, wgt_ref[...]))
        dg4_ref[...] += dg4p
        dx_ref[...] = dr + dxn

    return _row_call("ple_bwd", body, T, tm, [dres, x, p, gl, pe], [g4, g5, wgt], [(D, F32)],
                     acc_outs=[((P, D), F32), ((D, D), F32), ((1, D), F32), ((1, D), F32)])


def _scan_lanes(v, reverse):
    n = v.shape[1]
    lane = lax.broadcasted_iota(jnp.int32, v.shape, 1)
    s = 1
    while s < n:
        if reverse:
            v = v + jnp.where(lane < n - s, pltpu.roll(v, n - s, axis=1), 0.0)
        else:
            v = v + jnp.where(lane >= s, pltpu.roll(v, s, axis=1), 0.0)
        s *= 2
    return v


def _gate_cumsum(lf_t):
    def body(lf_ref, *piece_refs):
        rest = _scan_lanes(lf_ref[...], reverse=False)
        for r in piece_refs:
            piece = rest.astype(BF16)
            r[...] = piece
            rest = rest - piece.astype(F32)

    return pl.pallas_call(body, name="gate_cumsum", out_shape=[jax.ShapeDtypeStruct(lf_t.shape, BF16)] * N_AUG,
                          compiler_params=pltpu.CompilerParams(vmem_limit_bytes=VMEM_LIMIT))(lf_t)


def _gate_bwd(drow_t, dcol_t, fl_t):
    H = fl_t.shape[0]

    def body(dr_ref, dc_ref, fl_ref, dfl_ref, dbf_ref):
        dlf = _scan_lanes(dr_ref[...] - dc_ref[...], reverse=True)
        dfl = dlf * (1.0 / (1.0 + jnp.exp(fl_ref[...])))
        dfl_ref[...] = dfl
        dbf_ref[...] = jnp.sum(dfl, axis=1, keepdims=True)

    return pl.pallas_call(
        body, name="gate_bwd",
        out_shape=(jax.ShapeDtypeStruct(fl_t.shape, F32), jax.ShapeDtypeStruct((H, 1), F32)),
        compiler_params=pltpu.CompilerParams(vmem_limit_bytes=VMEM_LIMIT))(drow_t, dcol_t, fl_t)


def _aug_operands(pieces, H, D):
    T = pieces[0].shape[1]
    dh = D // H
    one = jnp.ones((H, T), BF16)
    qa = jnp.stack(list(pieces) + [one] * N_AUG, axis=-1)
    ka = jnp.stack([one] * N_AUG + [-pc for pc in pieces], axis=-1)

    def place(a):
        a = jnp.pad(a, ((0, 0), (0, 0), (0, dh - 2 * N_AUG))).reshape(H // 2, 2, T, dh)
        return jnp.transpose(a[:, ::-1], (2, 0, 1, 3)).reshape(T, D)

    return place(qa), place(ka)


def _strip_kind(diag, k0, ksz, q0, qsz):
    if not diag or k0 + ksz - 1 <= q0:
        return "full"
    return "skip" if k0 > q0 + qsz - 1 else "partial"


def _score_strip(km, qm, k0, ksz, q0, qsz, kind):
    st = _dot_nt(km[k0:k0 + ksz, :], qm[q0:q0 + qsz, :])
    if kind == "partial":
        keep = (k0 + lax.broadcasted_iota(jnp.int32, (ksz, qsz), 0)) <= (q0 + lax.broadcasted_iota(jnp.int32, (ksz, qsz), 1))
        st = jnp.where(keep, st, NEG_INF)
    return st


def _fold8(v, op):
    rows, n = v.shape
    v3 = v.reshape(rows // 8, 8, n)
    out = v3[0]
    for r in range(1, rows // 8):
        out = op(out, v3[r])
    return out


def _causal_tables(nb, q_outer):
    a, b = [], []
    for o in range(nb):
        inner = range(o + 1) if q_outer else range(o, nb)
        for n in inner:
            a.append(o)
            b.append(n)
    return jnp.asarray(np.array(a, np.int32)), jnp.asarray(np.array(b, np.int32))


def _flash_fwd(q, k, v, qaug, kaug, H):
    T, D = q.shape
    Hp = H // 2
    W = D // Hp
    dh = W // 2
    t = _tile(T, ATTN_TILE)
    nb = T // t
    ii, jj = _causal_tables(nb, q_outer=True)
    n_steps = int(ii.shape[0])

    ksz, qsz = _tile(t, ATTN_KEY_STRIP), _tile(t, ATTN_QUERY_STRIP)

    def body(ii_ref, jj_ref, q_ref, k_ref, v_ref, qa_ref, ka_ref, o_ref, lsea_ref, lseb_ref,
             qm_s, m_s, l_s, acc_s, p_s):
        n = pl.program_id(1)
        i, j = ii_ref[n], jj_ref[n]
        in_a = lax.broadcasted_iota(jnp.int32, (1, W), 1) < dh
        top = lax.broadcasted_iota(jnp.int32, (W, 1), 0) < dh

        @pl.when(j == 0)
        def _():
            qv, qa = q_ref[...], qa_ref[...]
            qm_s[0] = jnp.where(in_a, qv, qa)
            qm_s[1] = jnp.where(in_a, qa, qv)
            m_s[...] = jnp.full(m_s.shape, NEG_INF, F32)
            l_s[...] = jnp.zeros_like(l_s)
            acc_s[...] = jnp.zeros_like(acc_s)

        def step(diag):
            kv, ka, vv = k_ref[...], ka_ref[...], v_ref[...]
            kms = (jnp.where(in_a, kv, ka), jnp.where(in_a, ka, kv))
            vt = vv.T
            alphas = {}

            def softmax_stage(hh, q0):
                cols = slice(q0, q0 + qsz)
                kinds = [(k0, _strip_kind(diag, k0, ksz, q0, qsz)) for k0 in range(0, t, ksz)]
                strips, part = {}, None
                for k0, kind in kinds:
                    if kind != "skip":
                        strips[k0] = _score_strip(kms[hh], qm_s[hh], k0, ksz, q0, qsz, kind)
                        p8 = _fold8(strips[k0], jnp.maximum)
                        part = p8 if part is None else jnp.maximum(part, p8)
                m_prev = m_s[hh, :, cols]
                m_new = jnp.maximum(m_prev, jnp.max(part, axis=0, keepdims=True))
                l8 = None
                for k0, kind in kinds:
                    if kind == "skip":
                        p_s[hh, k0:k0 + ksz, cols] = jnp.zeros((ksz, qsz), BF16)
                        continue
                    pt = jnp.exp(strips[k0] - m_new)
                    s8 = _fold8(pt, jnp.add)
                    l8 = s8 if l8 is None else l8 + s8
                    p_s[hh, k0:k0 + ksz, cols] = pt.astype(BF16)
                alpha = jnp.exp(m_prev - m_new)
                l_s[hh, :, cols] = alpha * l_s[hh, :, cols] + jnp.sum(l8, axis=0, keepdims=True)
                m_s[hh, :, cols] = m_new
                alphas[(hh, q0)] = alpha

            def value_stage(hh, q0):
                cols, rows = slice(q0, q0 + qsz), slice(hh * dh, (hh + 1) * dh)
                acc_s[rows, cols] = acc_s[rows, cols] * alphas[(hh, q0)] + _dot(vt[rows, :], p_s[hh, :, cols])

            units = [(hh, q0) for q0 in range(0, t, qsz) for hh in range(2)]
            for u, unit in enumerate(units):
                softmax_stage(*unit)
                if u >= 1:
                    value_stage(*units[u - 1])
            value_stage(*units[-1])

        @pl.when(i != j)
        def _():
            step(False)

        @pl.when(i == j)
        def _():
            step(True)
            inv = jnp.where(top, 1.0 / l_s[0], 1.0 / l_s[1])
            o_ref[...] = (acc_s[...] * inv).T.astype(BF16)
            lsea_ref[...] = m_s[0] + jnp.log(l_s[0])
            lseb_ref[...] = m_s[1] + jnp.log(l_s[1])

    qspec = pl.BlockSpec((t, W), lambda hp, n, ii, jj: (ii[n], hp))
    kspec = pl.BlockSpec((t, W), lambda hp, n, ii, jj: (jj[n], hp))
    rspec = pl.BlockSpec((None, 1, t), lambda hp, n, ii, jj: (hp, 0, ii[n]))
    grid_spec = pltpu.PrefetchScalarGridSpec(
        num_scalar_prefetch=2, grid=(Hp, n_steps),
        in_specs=[qspec, kspec, kspec, qspec, kspec],
        out_specs=[qspec, rspec, rspec],
        scratch_shapes=[pltpu.VMEM((2, t, W), BF16), pltpu.VMEM((2, 1, t), F32), pltpu.VMEM((2, 1, t), F32),
                        pltpu.VMEM((W, t), F32), pltpu.VMEM((2, t, t), BF16)],
    )
    return pl.pallas_call(
        body, name="flash_fwd", grid_spec=grid_spec,
        out_shape=(jax.ShapeDtypeStruct((T, D), BF16), jax.ShapeDtypeStruct((Hp, 1, T), F32),
                   jax.ShapeDtypeStruct((Hp, 1, T), F32)),
        compiler_params=_params(2),
    )(ii, jj, q, k, v, qaug, kaug)


def _attn_delta(do, o, head_sel):
    T, D = do.shape
    H = head_sel.shape[0]
    tm = _tile(T, ROW_TILE_FWD)

    def body(do_ref, o_ref, sel_ref, out_ref):
        prod = do_ref[...].astype(F32) * o_ref[...].astype(F32)
        out_ref[...] = lax.dot_general(sel_ref[...], prod, (((1,), (1,)), ((), ())),
                                       precision=lax.Precision.HIGHEST, preferred_element_type=F32)

    return pl.pallas_call(
        body, name="attn_delta", grid=(T // tm,),
        in_specs=[pl.BlockSpec((tm, D), lambda i: (i, 0)), pl.BlockSpec((tm, D), lambda i: (i, 0)),
                  pl.BlockSpec((H, D), lambda i: (0, 0))],
        out_specs=pl.BlockSpec((H, tm), lambda i: (0, i)),
        out_shape=jax.ShapeDtypeStruct((H, T), F32), compiler_params=_params())(do, o, head_sel)


def _flash_bwd(q, k, v, qaug, kaug, do, lse_a, lse_b, delta, qscale):
    T, D = q.shape
    Hp = lse_a.shape[0]
    W = D // Hp
    dh = W // 2
    t = _tile(T, ATTN_TILE)
    nb = T // t
    jj, ii = _causal_tables(nb, q_outer=False)
    n_steps = int(ii.shape[0])
    ksz, qsz = _tile(t, ATTN_KEY_STRIP), _tile(t, ATTN_QUERY_STRIP)
    row_lane = (dh, 0)
    col_lane = (dh + N_AUG, N_AUG)

    def body(jj_ref, ii_ref, q_ref, k_ref, v_ref, qa_ref, ka_ref, do_ref, lsea_ref, lseb_ref, dla_ref, dlb_ref,
             dqt_ref, dkt_ref, dvt_ref, dcola_ref, dcolb_ref, drowa_ref, drowb_ref,
             dqt_acc, dk_acc, dv_acc, p2_s, ds2_s):
        n = pl.program_id(1)
        i, j = ii_ref[n], jj_ref[n]
        in_a = lax.broadcasted_iota(jnp.int32, (1, W), 1) < dh
        top = lax.broadcasted_iota(jnp.int32, (W, 1), 0) < dh

        @pl.when(n == 0)
        def _():
            dqt_acc[...] = jnp.zeros_like(dqt_acc)

        @pl.when(i == j)
        def _():
            dk_acc[...] = jnp.zeros_like(dk_acc)
            dv_acc[...] = jnp.zeros_like(dv_acc)

        def step(diag):
            qv, qa, kv, ka = q_ref[...], qa_ref[...], k_ref[...], ka_ref[...]
            vv, dov = v_ref[...], do_ref[...]
            zb = jnp.zeros_like(dov)
            kms = (jnp.where(in_a, kv, ka), jnp.where(in_a, ka, kv))
            qms = (jnp.where(in_a, qv, qa), jnp.where(in_a, qa, qv))
            doms = (jnp.where(in_a, dov, zb), jnp.where(in_a, zb, dov))
            kts, qts, dots = [tuple(a.T for a in pair) for pair in (kms, qms, doms)]
            lses, dls = (lsea_ref[...], lseb_ref[...]), (dla_ref[...], dlb_ref[...])
            dvs, dks = [], ([], [])

            def score_stage(hh, q0):
                cols = slice(q0, q0 + qsz)
                for k0 in range(0, t, ksz):
                    kind = _strip_kind(diag, k0, ksz, q0, qsz)
                    if kind == "skip":
                        p2_s[hh, k0:k0 + ksz, cols] = jnp.zeros((ksz, qsz), BF16)
                        ds2_s[hh, k0:k0 + ksz, cols] = jnp.zeros((ksz, qsz), BF16)
                        continue
                    pt = jnp.exp(_score_strip(kms[hh], qms[hh], k0, ksz, q0, qsz, kind) - lses[hh][:, cols])
                    dst = pt * (_dot_nt(vv[k0:k0 + ksz, :], doms[hh][cols, :]) - dls[hh][:, cols])
                    p2_s[hh, k0:k0 + ksz, cols] = pt.astype(BF16)
                    ds2_s[hh, k0:k0 + ksz, cols] = dst.astype(BF16)

            def grad_stage(hh, q0):
                cols = slice(q0, q0 + qsz)
                ptb, dsb = p2_s[hh, :, cols], ds2_s[hh, :, cols]
                dvs.append(_dot_nt(dots[hh][:, cols], ptb))
                dks[hh].append(_dot_nt(qts[hh][:, cols], dsb))
                dqt_acc[i, hh, :, cols] += _dot(kts[hh], dsb)

            units = [(hh, q0) for q0 in range(0, t, qsz) for hh in range(2)]
            for u, unit in enumerate(units):
                score_stage(*unit)
                if u >= 1:
                    grad_stage(*units[u - 1])
            grad_stage(*units[-1])
            dv_acc[...] += sum(dvs[1:], dvs[0])
            for hh in range(2):
                dk_acc[hh] += sum(dks[hh][1:], dks[hh][0])

        @pl.when(i != j)
        def _():
            step(False)

        @pl.when(i == j)
        def _():
            step(True)

        @pl.when(i == nb - 1)
        def _():
            dka, dkb = dk_acc[0], dk_acc[1]
            dkt_ref[...] = jnp.where(top, dka, dkb).astype(BF16)
            dvt_ref[...] = dv_acc[...].astype(BF16)
            dcola_ref[...] = dka[col_lane[0]:col_lane[0] + 1, :]
            dcolb_ref[...] = dkb[col_lane[1]:col_lane[1] + 1, :]

        @pl.when(n == n_steps - 1)
        def _():
            for b in range(nb):
                blk_a, blk_b = dqt_acc[b, 0], dqt_acc[b, 1]
                dqt_ref[:, b * t:(b + 1) * t] = (jnp.where(top, blk_a, blk_b) * qscale).astype(BF16)
                drowa_ref[:, b * t:(b + 1) * t] = blk_a[row_lane[0]:row_lane[0] + 1, :]
                drowb_ref[:, b * t:(b + 1) * t] = blk_b[row_lane[1]:row_lane[1] + 1, :]

    qspec = pl.BlockSpec((t, W), lambda hp, n, jj, ii: (ii[n], hp))
    kspec = pl.BlockSpec((t, W), lambda hp, n, jj, ii: (jj[n], hp))
    ktspec = pl.BlockSpec((W, t), lambda hp, n, jj, ii: (hp, jj[n]))
    pair_row = pl.BlockSpec((None, 1, t), lambda hp, n, jj, ii: (hp, 0, ii[n]))
    key_row = pl.BlockSpec((None, 1, t), lambda hp, n, jj, ii: (hp, 0, jj[n]))
    whole_row = pl.BlockSpec((None, 1, T), lambda hp, n, jj, ii: (hp, 0, 0))
    grid_spec = pltpu.PrefetchScalarGridSpec(
        num_scalar_prefetch=2, grid=(Hp, n_steps),
        in_specs=[
            qspec, kspec, kspec, qspec, kspec, qspec, pair_row, pair_row,
            pl.BlockSpec((None, 1, t), lambda hp, n, jj, ii: (2 * hp, 0, ii[n])),
            pl.BlockSpec((None, 1, t), lambda hp, n, jj, ii: (2 * hp + 1, 0, ii[n])),
        ],
        out_specs=[
            pl.BlockSpec((W, T), lambda hp, n, jj, ii: (hp, 0)),
            ktspec, ktspec, key_row, key_row, whole_row, whole_row,
        ],
        scratch_shapes=[pltpu.VMEM((nb, 2, W, t), F32), pltpu.VMEM((2, W, t), F32), pltpu.VMEM((W, t), F32),
                        pltpu.VMEM((2, t, t), BF16), pltpu.VMEM((2, t, t), BF16)],
    )
    return pl.pallas_call(
        body, name="flash_bwd", grid_spec=grid_spec,
        out_shape=(jax.ShapeDtypeStruct((D, T), BF16), jax.ShapeDtypeStruct((D, T), BF16),
                   jax.ShapeDtypeStruct((D, T), BF16), jax.ShapeDtypeStruct((Hp, 1, T), F32),
                   jax.ShapeDtypeStruct((Hp, 1, T), F32),
                   jax.ShapeDtypeStruct((Hp, 1, T), F32), jax.ShapeDtypeStruct((Hp, 1, T), F32)),
        compiler_params=_params(2),
    )(jj, ii, q, k, v, qaug, kaug, do, lse_a, lse_b, delta, delta)


HBM_SPEC = pl.BlockSpec(memory_space=pltpu.HBM)


def _mesh_pos():
    return lax.axis_index("x"), lax.axis_index("y"), lax.axis_index("c")


def _all_gather_shards(shards):
    n = len(shards)

    def body(*refs):
        in_refs, out_refs = refs[:n], refs[n:2 * n]
        send1, recv1, send2, recv2, lsem = refs[2 * n:]
        x, y, c = _mesh_pos()
        me = 2 * x + y
        chips = [(1 - x, y), (x, 1 - y), (1 - x, 1 - y)]
        local = [pltpu.make_async_copy(in_refs[t], out_refs[t].at[me], lsem.at[t]) for t in range(n)]
        for cp in local:
            cp.start()

        def half(t, chip_idx, pc):
            hr = shards[t].shape[0] // 2
            return out_refs[t].at[chip_idx, pl.ds(pc * hr, hr), :]

        first = []
        for t in range(n):
            hr = shards[t].shape[0] // 2
            for kk, (cx, cy) in enumerate(chips):
                first.append(pltpu.make_async_remote_copy(
                    src_ref=in_refs[t].at[pl.ds(c * hr, hr), :], dst_ref=half(t, me, c),
                    send_sem=send1.at[3 * t + kk], recv_sem=recv1.at[3 * t + kk],
                    device_id=(cx, cy, c), device_id_type=MESH))
        for cp in first:
            cp.start()
        passed = []
        for t in range(n):
            for kk, (cx, cy) in enumerate(chips):
                src_chip = 2 * cx + cy
                landed = half(t, src_chip, c)
                pltpu.make_async_remote_copy(
                    src_ref=landed, dst_ref=landed, send_sem=send1.at[3 * t + kk], recv_sem=recv1.at[3 * t + kk],
                    device_id=(cx, cy, c), device_id_type=MESH).wait_recv()
                fwd = pltpu.make_async_remote_copy(
                    src_ref=landed, dst_ref=landed, send_sem=send2.at[3 * t + kk], recv_sem=recv2.at[3 * t + kk],
                    device_id=(x, y, 1 - c), device_id_type=MESH)
                fwd.start()
                passed.append(fwd)
        for t in range(n):
            for kk, (cx, cy) in enumerate(chips):
                other = half(t, 2 * cx + cy, 1 - c)
                pltpu.make_async_remote_copy(
                    src_ref=other, dst_ref=other, send_sem=send2.at[3 * t + kk], recv_sem=recv2.at[3 * t + kk],
                    device_id=(x, y, 1 - c), device_id_type=MESH).wait_recv()
        for cp in first + passed:
            cp.wait_send()
        for cp in local:
            cp.wait()

    return pl.pallas_call(
        body, name="weights_all_gather",
        out_shape=[jax.ShapeDtypeStruct((N_CHIPS,) + s.shape, s.dtype) for s in shards],
        in_specs=[HBM_SPEC] * n, out_specs=[HBM_SPEC] * n,
        scratch_shapes=[pltpu.SemaphoreType.DMA((3 * n,)), pltpu.SemaphoreType.DMA((3 * n,)),
                        pltpu.SemaphoreType.DMA((3 * n,)), pltpu.SemaphoreType.DMA((3 * n,)),
                        pltpu.SemaphoreType.DMA((n,))],
    )(*shards)


def _sibling_swap_halves(g):
    _, M, C = g.shape
    hr = M // 2

    def body(g_ref, r_ref, ssem, rsem):
        x, y, c = _mesh_pos()
        cps = [pltpu.make_async_remote_copy(
            src_ref=g_ref.at[s, pl.ds((1 - c) * hr, hr), :], dst_ref=r_ref.at[s],
            send_sem=ssem.at[s], recv_sem=rsem.at[s], device_id=(x, y, 1 - c), device_id_type=MESH)
            for s in range(N_CHIPS)]
        for cp in cps:
            cp.start()
        for cp in cps:
            cp.wait()

    return pl.pallas_call(
        body, name="grads_sibling_swap", out_shape=jax.ShapeDtypeStruct((N_CHIPS, hr, C), F32),
        in_specs=[HBM_SPEC], out_specs=HBM_SPEC,
        scratch_shapes=[pltpu.SemaphoreType.DMA((N_CHIPS,)), pltpu.SemaphoreType.DMA((N_CHIPS,))],
    )(g)


def _pair_add(g, r, c_idx):
    _, M, C = g.shape
    hr = M // 2
    tr = PACK_ROW_TILE
    nbk = hr // tr

    def body(c_ref, g_ref, r_ref, o_ref):
        o_ref[...] = (g_ref[...] + r_ref[...]).astype(BF16)

    grid_spec = pltpu.PrefetchScalarGridSpec(
        num_scalar_prefetch=1, grid=(N_CHIPS, nbk),
        in_specs=[pl.BlockSpec((None, tr, C), lambda s, i, c: (s, c[0] * nbk + i, 0)),
                  pl.BlockSpec((None, tr, C), lambda s, i, c: (s, i, 0))],
        out_specs=pl.BlockSpec((None, tr, C), lambda s, i, c: (s, i, 0)),
    )
    return pl.pallas_call(body, name="grads_pair_add", grid_spec=grid_spec,
                          out_shape=jax.ShapeDtypeStruct((N_CHIPS, hr, C), BF16),
                          compiler_params=_params(2))(c_idx, g, r)


def _chip_scatter(pp):
    def body(p_ref, r_ref, ssem, rsem, lsem):
        x, y, c = _mesh_pos()
        me = 2 * x + y
        chips = [(1 - x, y), (x, 1 - y), (1 - x, 1 - y)]
        own = pltpu.make_async_copy(p_ref.at[me], r_ref.at[me], lsem)
        own.start()
        cps = [pltpu.make_async_remote_copy(
            src_ref=p_ref.at[2 * cx + cy], dst_ref=r_ref.at[me], send_sem=ssem.at[kk], recv_sem=rsem.at[kk],
            device_id=(cx, cy, c), device_id_type=MESH) for kk, (cx, cy) in enumerate(chips)]
        for cp in cps:
            cp.start()
        for kk, (cx, cy) in enumerate(chips):
            got = r_ref.at[2 * cx + cy]
            pltpu.make_async_remote_copy(
                src_ref=got, dst_ref=got, send_sem=ssem.at[kk], recv_sem=rsem.at[kk],
                device_id=(cx, cy, c), device_id_type=MESH).wait_recv()
        for cp in cps:
            cp.wait_send()
        own.wait()

    return pl.pallas_call(
        body, name="grads_chip_scatter", out_shape=jax.ShapeDtypeStruct(pp.shape, pp.dtype),
        in_specs=[HBM_SPEC], out_specs=HBM_SPEC,
        scratch_shapes=[pltpu.SemaphoreType.DMA((3,)), pltpu.SemaphoreType.DMA((3,)), pltpu.SemaphoreType.DMA],
    )(pp)


def _chip_sum(r):
    _, hr, C = r.shape
    tr = PACK_ROW_TILE

    def body(r_ref, o_ref):
        r0, r1, r2, r3 = (r_ref[s].astype(F32) for s in range(N_CHIPS))
        o_ref[...] = ((r0 + r1) + r2) + r3

    return pl.pallas_call(
        body, name="grads_chip_sum", grid=(hr // tr,),
        in_specs=[pl.BlockSpec((N_CHIPS, tr, C), lambda i: (0, i, 0))],
        out_specs=pl.BlockSpec((tr, C), lambda i: (i, 0)),
        out_shape=jax.ShapeDtypeStruct((hr, C), F32), compiler_params=_params())(r)


def _sibling_join(qh):
    hr, C = qh.shape

    nloc = LOCAL_COPY_PIECES if hr % (8 * LOCAL_COPY_PIECES) == 0 else 1
    pr = hr // nloc

    def body(q_ref, o_ref, ssem, rsem, lsem):
        x, y, c = _mesh_pos()
        mine = o_ref.at[pl.ds(c * hr, hr), :]
        own = [pltpu.make_async_copy(q_ref.at[pl.ds(k * pr, pr), :], o_ref.at[pl.ds(c * hr + k * pr, pr), :],
                                     lsem.at[k]) for k in range(nloc)]
        for cp in own:
            cp.start()
        cp = pltpu.make_async_remote_copy(src_ref=q_ref, dst_ref=mine, send_sem=ssem, recv_sem=rsem,
                                          device_id=(x, y, 1 - c), device_id_type=MESH)
        cp.start()
        theirs = o_ref.at[pl.ds((1 - c) * hr, hr), :]
        pltpu.make_async_remote_copy(src_ref=theirs, dst_ref=theirs, send_sem=ssem, recv_sem=rsem,
                                     device_id=(x, y, 1 - c), device_id_type=MESH).wait_recv()
        cp.wait_send()
        for lc in own:
            lc.wait()

    return pl.pallas_call(
        body, name="grads_sibling_join", out_shape=jax.ShapeDtypeStruct((2 * hr, C), F32),
        in_specs=[HBM_SPEC], out_specs=HBM_SPEC,
        scratch_shapes=[pltpu.SemaphoreType.DMA, pltpu.SemaphoreType.DMA, pltpu.SemaphoreType.DMA((nloc,))],
    )(qh)


def _adamw(w, g, m, v):
    M, C = w.shape
    tr = PACK_ROW_TILE

    def body(w_ref, g_ref, m_ref, v_ref, d_ref, mo_ref, vo_ref):
        gv = g_ref[...]
        mn = ADAM_B1 * m_ref[...] + (1.0 - ADAM_B1) * gv
        vn = ADAM_B2 * v_ref[...] + (1.0 - ADAM_B2) * (gv * gv)
        m_hat = mn / (1.0 - ADAM_B1 ** ADAM_STEP)
        v_hat = vn / (1.0 - ADAM_B2 ** ADAM_STEP)
        d_ref[...] = -ADAM_LR * (m_hat / (jnp.sqrt(v_hat) + ADAM_EPS) + ADAM_WD * w_ref[...])
        mo_ref[...] = mn
        vo_ref[...] = vn

    spec = pl.BlockSpec((tr, C), lambda i: (i, 0))
    return pl.pallas_call(
        body, name="adamw", grid=(M // tr,), in_specs=[spec] * 4, out_specs=[spec] * 3,
        out_shape=[jax.ShapeDtypeStruct((M, C), F32)] * 3, compiler_params=_params())(w, g, m, v)


WEIGHT_NAMES = ("norm_g", "w_attn_in", "b_forget", "w_attn_out", "w_conv_in", "conv_w", "w_conv_out",
                "w_mlp_up", "w_mlp_down", "w_ple_proj", "w_ple_gate")
COL_SHARDED = ("norm_g", "w_attn_in", "w_conv_in", "conv_w", "w_mlp_up", "w_ple_proj")
ROW_SHARDED = ("w_attn_out", "w_conv_out", "w_mlp_down", "w_ple_gate")


def _unshard(name, gathered, shard_shape):
    a = gathered.reshape((N_CHIPS,) + tuple(shard_shape))
    if name in COL_SHARDED:
        a = jnp.moveaxis(a, 0, -2)
        return a.reshape(a.shape[:-2] + (N_CHIPS * shard_shape[-1],))
    a = jnp.moveaxis(a, 0, 1)
    return a.reshape((shard_shape[0], N_CHIPS * shard_shape[1], shard_shape[2]))


def _to_shard_major(name, full):
    if name == "b_forget":
        return jnp.broadcast_to(full.reshape(1, -1), (N_CHIPS, full.size))
    if name in COL_SHARDED:
        a = full.reshape(full.shape[:-1] + (N_CHIPS, full.shape[-1] // N_CHIPS))
        a = jnp.moveaxis(a, -2, 0)
    else:
        a = full.reshape((full.shape[0], N_CHIPS, full.shape[1] // N_CHIPS, full.shape[2]))
        a = jnp.moveaxis(a, 1, 0)
    return a.reshape(N_CHIPS, -1)


def _pack_layout(shard_shapes):
    offs, row = {}, 0
    for name in WEIGHT_NAMES:
        n = int(np.prod(shard_shapes[name]))
        rows = -(-n // PACK_COLS)
        offs[name] = (row, rows, n)
        row += rows
    total = -(-row // PACK_ROW_ALIGN) * PACK_ROW_ALIGN
    return offs, total


def _pack(flat_by_name, offs, total_rows, lead=()):
    parts, row = [], 0
    for name in WEIGHT_NAMES:
        _, rows, n = offs[name]
        a = flat_by_name[name].astype(F32)
        pad = rows * PACK_COLS - n
        if pad:
            a = jnp.pad(a, [(0, 0)] * len(lead) + [(0, pad)])
        parts.append(a.reshape(tuple(lead) + (rows, PACK_COLS)))
        row += rows
    if total_rows > row:
        parts.append(jnp.zeros(tuple(lead) + (total_rows - row, PACK_COLS), F32))
    return jnp.concatenate(parts, axis=len(lead))


def _unpack(packed, offs, shard_shapes):
    out = {}
    for name in WEIGHT_NAMES:
        row, rows, n = offs[name]
        out[name] = packed[row:row + rows].reshape(-1)[:n].reshape(shard_shapes[name])
    return out


def kernel(x, p, norm_g, w_attn_in, b_forget, w_attn_out, w_conv_in, conv_w, w_conv_out, w_mlp_up, w_mlp_down, w_ple_proj, w_ple_gate, loss_target, m_norm_g, m_w_attn_in, m_b_forget, m_w_attn_out, m_w_conv_in, m_conv_w, m_w_conv_out, m_w_mlp_up, m_w_mlp_down, m_w_ple_proj, m_w_ple_gate, v_norm_g, v_w_attn_in, v_b_forget, v_w_attn_out, v_w_conv_in, v_conv_w, v_w_conv_out, v_w_mlp_up, v_w_mlp_down, v_w_ple_proj, v_w_ple_gate):
    w_local = dict(norm_g=norm_g, w_attn_in=w_attn_in, b_forget=b_forget, w_attn_out=w_attn_out,
                   w_conv_in=w_conv_in, conv_w=conv_w, w_conv_out=w_conv_out, w_mlp_up=w_mlp_up,
                   w_mlp_down=w_mlp_down, w_ple_proj=w_ple_proj, w_ple_gate=w_ple_gate)
    m_local = dict(norm_g=m_norm_g, w_attn_in=m_w_attn_in, b_forget=m_b_forget, w_attn_out=m_w_attn_out,
                   w_conv_in=m_w_conv_in, conv_w=m_conv_w, w_conv_out=m_w_conv_out, w_mlp_up=m_w_mlp_up,
                   w_mlp_down=m_w_mlp_down, w_ple_proj=m_w_ple_proj, w_ple_gate=m_w_ple_gate)
    v_local = dict(norm_g=v_norm_g, w_attn_in=v_w_attn_in, b_forget=v_b_forget, w_attn_out=v_w_attn_out,
                   w_conv_in=v_w_conv_in, conv_w=v_conv_w, w_conv_out=v_w_conv_out, w_mlp_up=v_w_mlp_up,
                   w_mlp_down=v_w_mlp_down, w_ple_proj=v_w_ple_proj, w_ple_gate=v_w_ple_gate)
    shard_shapes = {k: tuple(a.shape) for k, a in w_local.items()}

    xs = x[0]
    target = loss_target[0]
    T, D = xs.shape
    depth = p.shape[0]
    H = b_forget.shape[1]
    qscale = float(D // H) ** -0.5
    head_sel = (jnp.arange(D)[None, :] // (D // H) == jnp.arange(H)[:, None]).astype(F32)

    big = [n for n in WEIGHT_NAMES if n not in ("norm_g", "conv_w", "b_forget")]
    small = jnp.concatenate([norm_g.reshape(-1), conv_w.reshape(-1)])
    n_small = small.shape[0]
    small_rows = -(-n_small // (LANES * 16)) * 16
    small = jnp.pad(small, (0, small_rows * LANES - n_small)).reshape(small_rows, LANES)
    shards = [w_local[n].astype(BF16).reshape(-1, shard_shapes[n][-1]) for n in big] + [small]
    gathered = _all_gather_shards(shards)
    full = {n: _unshard(n, g, shard_shapes[n]) for n, g in zip(big, gathered[:-1])}
    gs = gathered[-1].reshape(N_CHIPS, -1)
    full["norm_g"] = _unshard("norm_g", gs[:, :norm_g.size], shard_shapes["norm_g"])
    full["conv_w"] = _unshard("conv_w", gs[:, norm_g.size:n_small], shard_shapes["conv_w"])
    gains = full["norm_g"]

    def gain(i, k):
        return gains[i, k].reshape(1, D)

    def taps(j):
        return jnp.pad(full["conv_w"][j], ((0, 5), (0, 0)))

    saved = []
    h = xs
    for i in range(depth):
        j = i // 2
        s = {"x0": h}
        if i % 2 == 0:
            w_in = full["w_attn_in"][j]
            wqkv = w_in[:, :3 * D]
            wf = jnp.pad(w_in[:, 3 * D:], ((0, 0), (0, LANES - H)))
            bf = jnp.pad(b_forget[j].reshape(1, H), ((0, 0), (0, LANES - H)))
            q, k, v, fl, lf = _attn_in_fwd(h, gain(i, 0), wqkv, wf, bf, qscale)
            qaug, kaug = _aug_operands(_gate_cumsum(lf[:, :H].T), H, D)
            o, lse_a, lse_b = _flash_fwd(q, k, v, qaug, kaug, H)
            s.update(q=q, k=k, v=v, fl=fl, qaug=qaug, kaug=kaug, o=o, lse_a=lse_a, lse_b=lse_b, wqkv=wqkv, wf=wf)
            mix_in, w_out = o, full["w_attn_out"][j]
        else:
            b, c, u, zc, y = _conv_in_fwd(h, gain(i, 0), full["w_conv_in"][j], taps(j))
            s.update(b=b, c=c, u=u, zc=zc, y=y)
            mix_in, w_out = y, full["w_conv_out"][j]
        m1, x1 = _out_proj_fwd("mixer_out_fwd", mix_in, w_out, gain(i, 1), h)
        up, a = _mlp_up_fwd(x1, gain(i, 2), full["w_mlp_up"][i])
        m3, x2 = _out_proj_fwd("mlp_down_fwd", a, full["w_mlp_down"][i], gain(i, 3), x1)
        gl, pe, x3 = _ple_fwd(x2, p[i, 0], gain(i, 4), gain(i, 5), full["w_ple_gate"][i], full["w_ple_proj"][i])
        s.update(m1=m1, x1=x1, up=up, a=a, m3=m3, x2=x2, gl=gl, pe=pe, w_out=w_out)
        saved.append(s)
        h = x3

    dh, loss_blk = _loss_fwd_bwd(h, target)
    loss = lax.psum(loss_blk[0, 0], ("x", "y", "c"))

    g_gain = [[None] * 6 for _ in range(depth)]
    grads = {n: [None] * w_local[n].shape[0] for n in WEIGHT_NAMES if n != "norm_g"}
    for i in reversed(range(depth)):
        j = i // 2
        s = saved[i]
        dx2, dwp, dwg, g_gain[i][4], g_gain[i][5] = _ple_bwd(
            dh, s["x2"], p[i, 0], s["gl"], s["pe"], gain(i, 4), gain(i, 5), full["w_ple_gate"][i].T)
        grads["w_ple_proj"][i], grads["w_ple_gate"][i] = dwp, dwg
        dup, dwd, g_gain[i][3] = _out_proj_bwd(
            "mlp_down_bwd", dx2, s["m3"], gain(i, 3), [s["a"], s["up"]], full["w_mlp_down"][i].T, "relu2")
        grads["w_mlp_down"][i] = dwd
        dx1, dwu, g_gain[i][2] = _in_proj_bwd(
            "mlp_up_bwd", [dup], [full["w_mlp_up"][i].T], s["x1"], gain(i, 2), dx2)
        grads["w_mlp_up"][i] = dwu
        if i % 2 == 0:
            do, dwo, g_gain[i][1] = _out_proj_bwd(
                "attn_out_bwd", dx1, s["m1"], gain(i, 1), [s["o"]], s["w_out"].T, "plain")
            grads["w_attn_out"][j] = dwo
            delta = _attn_delta(do, s["o"], head_sel).reshape(H, 1, T)
            dqt, dkt, dvt, dca, dcb, dra, drb = _flash_bwd(s["q"], s["k"], s["v"], s["qaug"], s["kaug"], do,
                                                           s["lse_a"], s["lse_b"], delta, qscale)
            dcol = jnp.concatenate([dca, dcb], axis=1).reshape(H, T)
            drow = jnp.concatenate([dra, drb], axis=1).reshape(H, T)
            dfl_t, dbf = _gate_bwd(drow, dcol, s["fl"][:, :H].T)
            grads["b_forget"][j] = dbf.reshape(H)
            dfl = jnp.pad(dfl_t.T, ((0, 0), (0, LANES - H))).astype(BF16)
            wqkv_t = s["wqkv"].T
            dh, dwq, dwk, dwv, dwf, g_gain[i][0] = _in_proj_bwd(
                "attn_in_bwd", [dqt.T, dkt.T, dvt.T, dfl], [wqkv_t[:D], wqkv_t[D:2 * D], wqkv_t[2 * D:], s["wf"].T],
                s["x0"], gain(i, 0), dx1)
            grads["w_attn_in"][j] = jnp.concatenate([dwq, dwk, dwv, dwf[:, :H]], axis=1)
        else:
            dy, dwo, g_gain[i][1] = _out_proj_bwd(
                "conv_out_bwd", dx1, s["m1"], gain(i, 1), [s["y"]], s["w_out"].T, "plain_f32")
            grads["w_conv_out"][j] = dwo
            db, dc, du, dcw = _conv_bwd(dy, s["b"], s["c"], s["u"], s["zc"], taps(j))
            grads["conv_w"][j] = dcw[:conv_w.shape[1]]
            w_t = full["w_conv_in"][j].T
            dh, dwb, dwc, dwu2, g_gain[i][0] = _in_proj_bwd(
                "conv_in_bwd", [db, dc, du], [w_t[:D], w_t[D:2 * D], w_t[2 * D:]], s["x0"], gain(i, 0), dx1)
            grads["w_conv_in"][j] = jnp.concatenate([dwb, dwc, dwu2], axis=1)
    grad_x = dh.reshape(x.shape)

    grad_full = {n: jnp.stack(grads[n]) for n in grads}
    grad_full["norm_g"] = jnp.stack([jnp.concatenate(row, axis=0) for row in g_gain])

    offs, total_rows = _pack_layout(shard_shapes)
    g_packed = _pack({n: _to_shard_major(n, grad_full[n]) for n in WEIGHT_NAMES}, offs, total_rows, lead=(N_CHIPS,))
    c_idx = lax.axis_index("c").astype(jnp.int32).reshape(1)
    pair = _pair_add(g_packed, _sibling_swap_halves(g_packed), c_idx)
    g_red = _sibling_join(_chip_sum(_chip_scatter(pair)))
    w_p = _pack({n: w_local[n].reshape(-1) for n in WEIGHT_NAMES}, offs, total_rows)
    m_p = _pack({n: m_local[n].reshape(-1) for n in WEIGHT_NAMES}, offs, total_rows)
    v_p = _pack({n: v_local[n].reshape(-1) for n in WEIGHT_NAMES}, offs, total_rows)
    d_p, mn_p, vn_p = _adamw(w_p, g_red, m_p, v_p)
    g_out = _unpack(g_red, offs, shard_shapes)
    d_out = _unpack(d_p, offs, shard_shapes)
    m_out = _unpack(mn_p, offs, shard_shapes)
    v_out = _unpack(vn_p, offs, shard_shapes)
    return (loss, grad_x, *[g_out[n] for n in WEIGHT_NAMES], *[d_out[n] for n in WEIGHT_NAMES],
            *[m_out[n] for n in WEIGHT_NAMES], *[v_out[n] for n in WEIGHT_NAMES])
```

```python
import numpy as np
import jax
import jax.numpy as jnp
from jax import lax
from jax.experimental import pallas as pl
from jax.experimental.pallas import tpu as pltpu

F32 = jnp.float32
BF16 = jnp.bfloat16
MESH = pl.DeviceIdType.MESH

RMS_EPS = 1e-6
NEG_INF = -1e30
ADAM_LR = 0.001
ADAM_B1 = 0.9
ADAM_B2 = 0.999
ADAM_EPS = 1e-08
ADAM_WD = 0.01
ADAM_STEP = 10

N_CHIPS = 4
LANES = 128
ROW_TILE_FWD = 512
ROW_TILE_BWD = 256
ATTN_TILE = 512
ATTN_KEY_STRIP = 128
ATTN_QUERY_STRIP = 256
COL_CHUNK = 512
PACK_COLS = 1024
PACK_ROW_TILE = 256
PACK_ROW_ALIGN = 2 * PACK_ROW_TILE
N_AUG = 3
VMEM_LIMIT = 56 * 1024 * 1024


def _tile(n, pref):
    return pref if n % pref == 0 else n


def _dot(a, b):
    return jnp.dot(a, b, preferred_element_type=F32)


def _dot_tn(a, b):
    return lax.dot_general(a, b, (((0,), (0,)), ((), ())), preferred_element_type=F32)


def _dot_nt(a, b):
    return lax.dot_general(a, b, (((1,), (1,)), ((), ())), preferred_element_type=F32)


def _rms_fwd(x, g):
    r = lax.rsqrt(jnp.mean(x * x, axis=-1, keepdims=True) + RMS_EPS)
    return (x * r) * g


def _rms_bwd(x, g, dy):
    r = lax.rsqrt(jnp.mean(x * x, axis=-1, keepdims=True) + RMS_EPS)
    xh = x * r
    dyg = dy * g
    dx = r * (dyg - xh * jnp.mean(dyg * xh, axis=-1, keepdims=True))
    return dx, jnp.sum(dy * xh, axis=0, keepdims=True)


def _params(n_axes=1):
    return pltpu.CompilerParams(dimension_semantics=("arbitrary",) * n_axes, vmem_limit_bytes=VMEM_LIMIT)


def _row_call(name, body, n_rows, tm, row_ins, const_ins, row_outs, acc_outs=(), scratch=(), reverse=False):
    nb = n_rows // tm
    rmap = (lambda i: (nb - 1 - i, 0)) if reverse else (lambda i: (i, 0))

    def whole(shape):
        nd = len(shape)
        return pl.BlockSpec(tuple(shape), lambda i: (0,) * nd)

    in_specs = [pl.BlockSpec((tm, a.shape[1]), rmap) for a in row_ins] + [whole(a.shape) for a in const_ins]
    out_shape = [jax.ShapeDtypeStruct((n_rows, w), dt) for (w, dt) in row_outs]
    out_shape += [jax.ShapeDtypeStruct(tuple(s), dt) for (s, dt) in acc_outs]
    out_specs = [pl.BlockSpec((tm, w), rmap) for (w, _) in row_outs] + [whole(s) for (s, _) in acc_outs]
    return pl.pallas_call(
        body, name=name, grid=(nb,), in_specs=in_specs, out_specs=out_specs, out_shape=out_shape,
        scratch_shapes=list(scratch), compiler_params=_params(),
    )(*row_ins, *const_ins)


def _attn_in_fwd(x, g, wqkv, wf, bf, qscale):
    T, D = x.shape
    tm, ch = _tile(T, ROW_TILE_FWD), _tile(D, COL_CHUNK)

    def body(x_ref, g_ref, w_ref, wf_ref, bf_ref, q_ref, k_ref, v_ref, fl_ref, lf_ref):
        h = _rms_fwd(x_ref[...], g_ref[...]).astype(BF16)
        for part, o_ref in enumerate((q_ref, k_ref, v_ref)):
            for n0 in range(0, D, ch):
                r = _dot(h, w_ref[:, part * D + n0:part * D + n0 + ch])
                if part == 0:
                    r = r * qscale
                o_ref[:, n0:n0 + ch] = r.astype(BF16)
        fl = _dot(h, wf_ref[...]) + bf_ref[...]
        fl_ref[...] = fl
        lf_ref[...] = jnp.minimum(fl, 0.0) - jnp.log1p(jnp.exp(-jnp.abs(fl)))

    return _row_call("attn_in_fwd", body, T, tm, [x], [g, wqkv, wf, bf],
                     [(D, BF16), (D, BF16), (D, BF16), (LANES, F32), (LANES, F32)])


def _conv_in_fwd(x, g, w, cw):
    T, D = x.shape
    tm, ch = _tile(T, ROW_TILE_FWD), _tile(D, COL_CHUNK)

    def body(x_ref, g_ref, w_ref, cw_ref, b_ref, c_ref, u_ref, zc_ref, y_ref, tail_ref):
        i = pl.program_id(0)

        @pl.when(i == 0)
        def _():
            tail_ref[...] = jnp.zeros_like(tail_ref)

        h = _rms_fwd(x_ref[...], g_ref[...]).astype(BF16)
        for part, o_ref in enumerate((b_ref, c_ref, u_ref)):
            for n0 in range(0, D, ch):
                o_ref[:, n0:n0 + ch] = _dot(h, w_ref[:, part * D + n0:part * D + n0 + ch])
        z = c_ref[...] * u_ref[...]
        row = lax.broadcasted_iota(jnp.int32, (tm, 1), 0)
        t6, t7 = tail_ref[6:7, :], tail_ref[7:8, :]
        z1 = jnp.where(row == 0, t7, pltpu.roll(z, 1, axis=0))
        z2 = jnp.where(row == 0, t6, jnp.where(row == 1, t7, pltpu.roll(z, 2, axis=0)))
        zc = cw_ref[0:1, :] * z2 + cw_ref[1:2, :] * z1 + cw_ref[2:3, :] * z
        zc_ref[...] = zc
        y_ref[...] = (b_ref[...] * zc).astype(BF16)
        tail_ref[...] = z[tm - 8:tm, :]

    return _row_call("conv_in_fwd", body, T, tm, [x], [g, w, cw],
                     [(D, F32), (D, F32), (D, F32), (D, F32), (D, BF16)], scratch=[pltpu.VMEM((8, D), F32)])


def _mlp_up_fwd(x, g, w):
    T, D = x.shape
    F = w.shape[1]
    tm, ch = _tile(T, ROW_TILE_FWD), _tile(F, COL_CHUNK)

    def body(x_ref, g_ref, w_ref, up_ref, a_ref):
        h = _rms_fwd(x_ref[...], g_ref[...]).astype(BF16)
        for n0 in range(0, F, ch):
            r = _dot(h, w_ref[:, n0:n0 + ch])
            up_ref[:, n0:n0 + ch] = r.astype(BF16)
            rl = jnp.maximum(r, 0.0)
            a_ref[:, n0:n0 + ch] = (rl * rl).astype(BF16)

    return _row_call("mlp_up_fwd", body, T, tm, [x], [g, w], [(F, BF16), (F, BF16)])


def _out_proj_fwd(name, a, w, g, x):
    T, D = x.shape
    tm = _tile(T, ROW_TILE_FWD)

    def body(a_ref, x_ref, w_ref, g_ref, m_ref, xn_ref):
        m = _dot(a_ref[...], w_ref[...])
        m_ref[...] = m
        xn_ref[...] = x_ref[...] + _rms_fwd(m, g_ref[...])

    return _row_call(name, body, T, tm, [a, x], [w, g], [(D, F32), (D, F32)])


def _ple_fwd(x, p, g4, g5, wg, wp):
    T, D = x.shape
    tm = _tile(T, ROW_TILE_FWD)

    def body(x_ref, p_ref, g4_ref, g5_ref, wg_ref, wp_ref, gl_ref, pe_ref, xn_ref):
        xv = x_ref[...]
        gl = _dot(_rms_fwd(xv, g4_ref[...]).astype(BF16), wg_ref[...])
        pe = _dot(p_ref[...].astype(BF16), wp_ref[...])
        gl_ref[...] = gl
        pe_ref[...] = pe
        e = pe * (1.0 / (1.0 + jnp.exp(-gl)))
        xn_ref[...] = xv + _rms_fwd(e, g5_ref[...])

    return _row_call("ple_fwd", body, T, tm, [x, p], [g4, g5, wg, wp], [(D, F32), (D, F32), (D, F32)])


def _loss_fwd_bwd(y, target):
    T, D = y.shape
    tm = _tile(T, ROW_TILE_FWD)

    def body(y_ref, t_ref, dy_ref, loss_ref):
        @pl.when(pl.program_id(0) == 0)
        def _():
            loss_ref[...] = jnp.zeros_like(loss_ref)

        err = y_ref[...] - t_ref[...]
        dy_ref[...] = err * (1.0 / D)
        part = 0.5 * jnp.sum(jnp.mean(err * err, axis=-1, keepdims=True), axis=0, keepdims=True)
        loss_ref[...] += jnp.broadcast_to(part, loss_ref.shape)

    return _row_call("loss", body, T, tm, [y, target], [], [(D, F32)], acc_outs=[((8, LANES), F32)])


def _out_proj_bwd(name, dres, m, g, a_ins, wt, mode):
    T, D = dres.shape
    Ka = wt.shape[1]
    tm, ch = _tile(T, ROW_TILE_BWD), _tile(Ka, COL_CHUNK)
    out_dt = F32 if mode == "plain_f32" else BF16

    def body(dres_ref, m_ref, *rest):
        a_ref = rest[0]
        up_ref = rest[1] if mode == "relu2" else None
        k = len(a_ins)
        g_ref, wt_ref, da_ref, dw_ref, dg_ref = rest[k:k + 5]

        @pl.when(pl.program_id(0) == 0)
        def _():
            dw_ref[...] = jnp.zeros_like(dw_ref)
            dg_ref[...] = jnp.zeros_like(dg_ref)

        dm, dgp = _rms_bwd(m_ref[...], g_ref[...], dres_ref[...])
        dg_ref[...] += dgp
        dmb = dm.astype(BF16)
        for n0 in range(0, Ka, ch):
            da = _dot(dmb, wt_ref[:, n0:n0 + ch])
            if mode == "relu2":
                da = da * (2.0 * jnp.maximum(up_ref[:, n0:n0 + ch].astype(F32), 0.0))
            da_ref[:, n0:n0 + ch] = da.astype(out_dt)
            dw_ref[n0:n0 + ch, :] += _dot_tn(a_ref[:, n0:n0 + ch], dmb)

    return _row_call(name, body, T, tm, [dres, m] + list(a_ins), [g, wt],
                     [(Ka, out_dt)], acc_outs=[((Ka, D), F32), ((1, D), F32)])


def _in_proj_bwd(name, pieces, wts, x, g, dres):
    T, D = x.shape
    tm = _tile(T, ROW_TILE_BWD)
    k = len(pieces)
    widths = [pc.shape[1] for pc in pieces]

    def body(*refs):
        pc_refs = refs[:k]
        x_ref, dres_ref, g_ref = refs[k:k + 3]
        wt_refs = refs[k + 3:2 * k + 3]
        dx_ref = refs[2 * k + 3]
        dw_refs = refs[2 * k + 4:3 * k + 4]
        dg_ref = refs[3 * k + 4]

        @pl.when(pl.program_id(0) == 0)
        def _():
            for r in dw_refs:
                r[...] = jnp.zeros_like(r)
            dg_ref[...] = jnp.zeros_like(dg_ref)

        xv, gv = x_ref[...], g_ref[...]
        hb = _rms_fwd(xv, gv).astype(BF16)
        dh = None
        for pc_ref, wt_ref, dw_ref, n in zip(pc_refs, wt_refs, dw_refs, widths):
            d = _dot(pc_ref[...], wt_ref[...])
            dh = d if dh is None else dh + d
            ch = _tile(n, COL_CHUNK)
            for n0 in range(0, n, ch):
                dw_ref[:, n0:n0 + ch] += _dot_tn(hb, pc_ref[:, n0:n0 + ch])
        dxn, dgp = _rms_bwd(xv, gv, dh)
        dg_ref[...] += dgp
        dx_ref[...] = dres_ref[...] + dxn

    return _row_call(name, body, T, tm, list(pieces) + [x, dres], [g] + list(wts), [(D, F32)],
                     acc_outs=[((D, n), F32) for n in widths] + [((1, D), F32)])


def _conv_bwd(dy, b, c, u, zc, cw):
    T, D = dy.shape
    tm = _tile(T, ROW_TILE_BWD)

    def body(dy_ref, b_ref, c_ref, u_ref, zc_ref, cw_ref, db_ref, dc_ref, du_ref, dcw_ref, head_ref):
        @pl.when(pl.program_id(0) == 0)
        def _():
            head_ref[...] = jnp.zeros_like(head_ref)
            dcw_ref[...] = jnp.zeros_like(dcw_ref)

        dyv, cv, uv = dy_ref[...], c_ref[...], u_ref[...]
        db_ref[...] = (dyv * zc_ref[...]).astype(BF16)
        dzc = dyv * b_ref[...]
        row = lax.broadcasted_iota(jnp.int32, (tm, 1), 0)
        h0, h1 = head_ref[0:1, :], head_ref[1:2, :]
        d1 = jnp.where(row == tm - 1, h0, pltpu.roll(dzc, tm - 1, axis=0))
        d2 = jnp.where(row == tm - 1, h1, jnp.where(row == tm - 2, h0, pltpu.roll(dzc, tm - 2, axis=0)))
        dz = cw_ref[2:3, :] * dzc + cw_ref[1:2, :] * d1 + cw_ref[0:1, :] * d2
        dc_ref[...] = (dz * uv).astype(BF16)
        du_ref[...] = (dz * cv).astype(BF16)
        z = cv * uv
        dcw_ref[0:1, :] += jnp.sum(d2 * z, axis=0, keepdims=True)
        dcw_ref[1:2, :] += jnp.sum(d1 * z, axis=0, keepdims=True)
        dcw_ref[2:3, :] += jnp.sum(dzc * z, axis=0, keepdims=True)
        head_ref[...] = dzc[0:8, :]

    return _row_call("conv_bwd", body, T, tm, [dy, b, c, u, zc], [cw], [(D, BF16), (D, BF16), (D, BF16)],
                     acc_outs=[((8, D), F32)], scratch=[pltpu.VMEM((8, D), F32)], reverse=True)


def _ple_bwd(dres, x, p, gl, pe, g4, g5, wgt):
    T, D = x.shape
    P = p.shape[1]
    tm = _tile(T, ROW_TILE_BWD)

    def body(dres_ref, x_ref, p_ref, gl_ref, pe_ref, g4_ref, g5_ref, wgt_ref,
             dx_ref, dwp_ref, dwg_ref, dg4_ref, dg5_ref):
        @pl.when(pl.program_id(0) == 0)
        def _():
            for r in (dwp_ref, dwg_ref, dg4_ref, dg5_ref):
                r[...] = jnp.zeros_like(r)

        dr, xv, pe_v = dres_ref[...], x_ref[...], pe_ref[...]
        gate = 1.0 / (1.0 + jnp.exp(-gl_ref[...]))
        de, dg5p = _rms_bwd(pe_v * gate, g5_ref[...], dr)
        dg5_ref[...] += dg5p
        dpe = (de * gate).astype(BF16)
        dgl = (de * pe_v * gate * (1.0 - gate)).astype(BF16)
        dwp_ref[...] += _dot_tn(p_ref[...].astype(BF16), dpe)
        g4v = g4_ref[...]
        dwg_ref[...] += _dot_tn(_rms_fwd(xv, g4v).astype(BF16), dgl)
        dxn, dg4p = _rms_bwd(xv, g4v, _dot(dgl, wgt_ref[...]))
        dg4_ref[...] += dg4p
        dx_ref[...] = dr + dxn

    return _row_call("ple_bwd", body, T, tm, [dres, x, p, gl, pe], [g4, g5, wgt], [(D, F32)],
                     acc_outs=[((P, D), F32), ((D, D), F32), ((1, D), F32), ((1, D), F32)])


def _scan_lanes(v, reverse):
    n = v.shape[1]
    lane = lax.broadcasted_iota(jnp.int32, v.shape, 1)
    s = 1
    while s < n:
        if reverse:
            v = v + jnp.where(lane < n - s, pltpu.roll(v, n - s, axis=1), 0.0)
        else:
            v = v + jnp.where(lane >= s, pltpu.roll(v, s, axis=1), 0.0)
        s *= 2
    return v


def _gate_cumsum(lf_t):
    def body(lf_ref, *piece_refs):
        rest = _scan_lanes(lf_ref[...], reverse=False)
        for r in piece_refs:
            piece = rest.astype(BF16)
            r[...] = piece
            rest = rest - piece.astype(F32)

    return pl.pallas_call(body, name="gate_cumsum", out_shape=[jax.ShapeDtypeStruct(lf_t.shape, BF16)] * N_AUG,
                          compiler_params=pltpu.CompilerParams(vmem_limit_bytes=VMEM_LIMIT))(lf_t)


def _gate_bwd(drow_t, dcol_t, fl_t):
    H = fl_t.shape[0]

    def body(dr_ref, dc_ref, fl_ref, dfl_ref, dbf_ref):
        dlf = _scan_lanes(dr_ref[...] - dc_ref[...], reverse=True)
        dfl = dlf * (1.0 / (1.0 + jnp.exp(fl_ref[...])))
        dfl_ref[...] = dfl
        dbf_ref[...] = jnp.sum(dfl, axis=1, keepdims=True)

    return pl.pallas_call(
        body, name="gate_bwd",
        out_shape=(jax.ShapeDtypeStruct(fl_t.shape, F32), jax.ShapeDtypeStruct((H, 1), F32)),
        compiler_params=pltpu.CompilerParams(vmem_limit_bytes=VMEM_LIMIT))(drow_t, dcol_t, fl_t)


def _aug_operands(pieces, H, D):
    T = pieces[0].shape[1]
    dh = D // H
    one = jnp.ones((H, T), BF16)
    qa = jnp.stack(list(pieces) + [one] * N_AUG, axis=-1)
    ka = jnp.stack([one] * N_AUG + [-pc for pc in pieces], axis=-1)

    def place(a):
        a = jnp.pad(a, ((0, 0), (0, 0), (0, dh - 2 * N_AUG))).reshape(H // 2, 2, T, dh)
        return jnp.transpose(a[:, ::-1], (2, 0, 1, 3)).reshape(T, D)

    return place(qa), place(ka)


def _strip_kind(diag, k0, ksz, q0, qsz):
    if not diag or k0 + ksz - 1 <= q0:
        return "full"
    return "skip" if k0 > q0 + qsz - 1 else "partial"


def _score_strip(km, qm, k0, ksz, q0, qsz, kind):
    st = _dot_nt(km[k0:k0 + ksz, :], qm[q0:q0 + qsz, :])
    if kind == "partial":
        keep = (k0 + lax.broadcasted_iota(jnp.int32, (ksz, qsz), 0)) <= (q0 + lax.broadcasted_iota(jnp.int32, (ksz, qsz), 1))
        st = jnp.where(keep, st, NEG_INF)
    return st


def _fold8(v, op):
    rows, n = v.shape
    v3 = v.reshape(rows // 8, 8, n)
    out = v3[0]
    for r in range(1, rows // 8):
        out = op(out, v3[r])
    return out


def _causal_tables(nb, q_outer):
    a, b = [], []
    for o in range(nb):
        inner = range(o + 1) if q_outer else range(o, nb)
        for n in inner:
            a.append(o)
            b.append(n)
    return jnp.asarray(np.array(a, np.int32)), jnp.asarray(np.array(b, np.int32))


def _flash_fwd(q, k, v, qaug, kaug, H):
    T, D = q.shape
    Hp = H // 2
    W = D // Hp
    dh = W // 2
    t = _tile(T, ATTN_TILE)
    nb = T // t
    ii, jj = _causal_tables(nb, q_outer=True)
    n_steps = int(ii.shape[0])

    ksz, qsz = _tile(t, ATTN_KEY_STRIP), _tile(t, ATTN_QUERY_STRIP)

    def body(ii_ref, jj_ref, q_ref, k_ref, v_ref, qa_ref, ka_ref, o_ref, lsea_ref, lseb_ref,
             qm_s, m_s, l_s, acc_s, p_s):
        n = pl.program_id(1)
        i, j = ii_ref[n], jj_ref[n]
        in_a = lax.broadcasted_iota(jnp.int32, (1, W), 1) < dh
        top = lax.broadcasted_iota(jnp.int32, (W, 1), 0) < dh

        @pl.when(j == 0)
        def _():
            qv, qa = q_ref[...], qa_ref[...]
            qm_s[0] = jnp.where(in_a, qv, qa)
            qm_s[1] = jnp.where(in_a, qa, qv)
            m_s[...] = jnp.full(m_s.shape, NEG_INF, F32)
            l_s[...] = jnp.zeros_like(l_s)
            acc_s[...] = jnp.zeros_like(acc_s)

        def step(diag):
            kv, ka, vv = k_ref[...], ka_ref[...], v_ref[...]
            kms = (jnp.where(in_a, kv, ka), jnp.where(in_a, ka, kv))
            vt = vv.T
            alphas = {}

            def softmax_stage(hh, q0):
                cols = slice(q0, q0 + qsz)
                kinds = [(k0, _strip_kind(diag, k0, ksz, q0, qsz)) for k0 in range(0, t, ksz)]
                strips, part = {}, None
                for k0, kind in kinds:
                    if kind != "skip":
                        strips[k0] = _score_strip(kms[hh], qm_s[hh], k0, ksz, q0, qsz, kind)
                        p8 = _fold8(strips[k0], jnp.maximum)
                        part = p8 if part is None else jnp.maximum(part, p8)
                m_prev = m_s[hh, :, cols]
                m_new = jnp.maximum(m_prev, jnp.max(part, axis=0, keepdims=True))
                l8 = None
                for k0, kind in kinds:
                    if kind == "skip":
                        p_s[hh, k0:k0 + ksz, cols] = jnp.zeros((ksz, qsz), BF16)
                        continue
                    pt = jnp.exp(strips[k0] - m_new)
                    s8 = _fold8(pt, jnp.add)
                    l8 = s8 if l8 is None else l8 + s8
                    p_s[hh, k0:k0 + ksz, cols] = pt.astype(BF16)
                alpha = jnp.exp(m_prev - m_new)
                l_s[hh, :, cols] = alpha * l_s[hh, :, cols] + jnp.sum(l8, axis=0, keepdims=True)
                m_s[hh, :, cols] = m_new
                alphas[(hh, q0)] = alpha

            def value_stage(hh, q0):
                cols, rows = slice(q0, q0 + qsz), slice(hh * dh, (hh + 1) * dh)
                acc_s[rows, cols] = acc_s[rows, cols] * alphas[(hh, q0)] + _dot(vt[rows, :], p_s[hh, :, cols])

            units = [(hh, q0) for q0 in range(0, t, qsz) for hh in range(2)]
            for u, unit in enumerate(units):
                softmax_stage(*unit)
                if u >= 1:
                    value_stage(*units[u - 1])
            value_stage(*units[-1])

        @pl.when(i != j)
        def _():
            step(False)

        @pl.when(i == j)
        def _():
            step(True)
            inv = jnp.where(top, 1.0 / l_s[0], 1.0 / l_s[1])
            o_ref[...] = (acc_s[...] * inv).T.astype(BF16)
            lsea_ref[...] = m_s[0] + jnp.log(l_s[0])
            lseb_ref[...] = m_s[1] + jnp.log(l_s[1])

    qspec = pl.BlockSpec((t, W), lambda hp, n, ii, jj: (ii[n], hp))
    kspec = pl.BlockSpec((t, W), lambda hp, n, ii, jj: (jj[n], hp))
    rspec = pl.BlockSpec((None, 1, t), lambda hp, n, ii, jj: (hp, 0, ii[n]))
    grid_spec = pltpu.PrefetchScalarGridSpec(
        num_scalar_prefetch=2, grid=(Hp, n_steps),
        in_specs=[qspec, kspec, kspec, qspec, kspec],
        out_specs=[qspec, rspec, rspec],
        scratch_shapes=[pltpu.VMEM((2, t, W), BF16), pltpu.VMEM((2, 1, t), F32), pltpu.VMEM((2, 1, t), F32),
                        pltpu.VMEM((W, t), F32), pltpu.VMEM((2, t, t), BF16)],
    )
    return pl.pallas_call(
        body, name="flash_fwd", grid_spec=grid_spec,
        out_shape=(jax.ShapeDtypeStruct((T, D), BF16), jax.ShapeDtypeStruct((Hp, 1, T), F32),
                   jax.ShapeDtypeStruct((Hp, 1, T), F32)),
        compiler_params=_params(2),
    )(ii, jj, q, k, v, qaug, kaug)


def _attn_delta(do, o, head_sel):
    T, D = do.shape
    H = head_sel.shape[0]
    tm = _tile(T, ROW_TILE_FWD)

    def body(do_ref, o_ref, sel_ref, out_ref):
        prod = do_ref[...].astype(F32) * o_ref[...].astype(F32)
        out_ref[...] = lax.dot_general(sel_ref[...], prod, (((1,), (1,)), ((), ())),
                                       precision=lax.Precision.HIGHEST, preferred_element_type=F32)

    return pl.pallas_call(
        body, name="attn_delta", grid=(T // tm,),
        in_specs=[pl.BlockSpec((tm, D), lambda i: (i, 0)), pl.BlockSpec((tm, D), lambda i: (i, 0)),
                  pl.BlockSpec((H, D), lambda i: (0, 0))],
        out_specs=pl.BlockSpec((H, tm), lambda i: (0, i)),
        out_shape=jax.ShapeDtypeStruct((H, T), F32), compiler_params=_params())(do, o, head_sel)


def _flash_bwd(q, k, v, qaug, kaug, do, lse_a, lse_b, delta, qscale):
    T, D = q.shape
    Hp = lse_a.shape[0]
    W = D // Hp
    dh = W // 2
    t = _tile(T, ATTN_TILE)
    nb = T // t
    jj, ii = _causal_tables(nb, q_outer=False)
    n_steps = int(ii.shape[0])
    ksz, qsz = _tile(t, ATTN_KEY_STRIP), _tile(t, ATTN_QUERY_STRIP)
    row_lane = (dh, 0)
    col_lane = (dh + N_AUG, N_AUG)

    def body(jj_ref, ii_ref, q_ref, k_ref, v_ref, qa_ref, ka_ref, do_ref, lsea_ref, lseb_ref, dla_ref, dlb_ref,
             dqt_ref, dkt_ref, dvt_ref, dcola_ref, dcolb_ref, drowa_ref, drowb_ref,
             dqt_acc, dk_acc, dv_acc, p2_s, ds2_s):
        n = pl.program_id(1)
        i, j = ii_ref[n], jj_ref[n]
        in_a = lax.broadcasted_iota(jnp.int32, (1, W), 1) < dh
        top = lax.broadcasted_iota(jnp.int32, (W, 1), 0) < dh

        @pl.when(n == 0)
        def _():
            dqt_acc[...] = jnp.zeros_like(dqt_acc)

        @pl.when(i == j)
        def _():
            dk_acc[...] = jnp.zeros_like(dk_acc)
            dv_acc[...] = jnp.zeros_like(dv_acc)

        def step(diag):
            qv, qa, kv, ka = q_ref[...], qa_ref[...], k_ref[...], ka_ref[...]
            vv, dov = v_ref[...], do_ref[...]
            zb = jnp.zeros_like(dov)
            kms = (jnp.where(in_a, kv, ka), jnp.where(in_a, ka, kv))
            qms = (jnp.where(in_a, qv, qa), jnp.where(in_a, qa, qv))
            doms = (jnp.where(in_a, dov, zb), jnp.where(in_a, zb, dov))
            kts, qts, dots = [tuple(a.T for a in pair) for pair in (kms, qms, doms)]
            lses, dls = (lsea_ref[...], lseb_ref[...]), (dla_ref[...], dlb_ref[...])
            dvs, dks = [], ([], [])

            def score_stage(hh, q0):
                cols = slice(q0, q0 + qsz)
                for k0 in range(0, t, ksz):
                    kind = _strip_kind(diag, k0, ksz, q0, qsz)
                    if kind == "skip":
                        p2_s[hh, k0:k0 + ksz, cols] = jnp.zeros((ksz, qsz), BF16)
                        ds2_s[hh, k0:k0 + ksz, cols] = jnp.zeros((ksz, qsz), BF16)
                        continue
                    pt = jnp.exp(_score_strip(kms[hh], qms[hh], k0, ksz, q0, qsz, kind) - lses[hh][:, cols])
                    dst = pt * (_dot_nt(vv[k0:k0 + ksz, :], doms[hh][cols, :]) - dls[hh][:, cols])
                    p2_s[hh, k0:k0 + ksz, cols] = pt.astype(BF16)
                    ds2_s[hh, k0:k0 + ksz, cols] = dst.astype(BF16)

            def grad_stage(hh, q0):
                cols = slice(q0, q0 + qsz)
                ptb, dsb = p2_s[hh, :, cols], ds2_s[hh, :, cols]
                dvs.append(_dot_nt(dots[hh][:, cols], ptb))
                dks[hh].append(_dot_nt(qts[hh][:, cols], dsb))
                dqt_acc[i, hh, :, cols] += _dot(kts[hh], dsb)

            units = [(hh, q0) for q0 in range(0, t, qsz) for hh in range(2)]
            for u, unit in enumerate(units):
                score_stage(*unit)
                if u >= 1:
                    grad_stage(*units[u - 1])
            grad_stage(*units[-1])
            dv_acc[...] += sum(dvs[1:], dvs[0])
            for hh in range(2):
                dk_acc[hh] += sum(dks[hh][1:], dks[hh][0])

        @pl.when(i != j)
        def _():
            step(False)

        @pl.when(i == j)
        def _():
            step(True)

        @pl.when(i == nb - 1)
        def _():
            dka, dkb = dk_acc[0], dk_acc[1]
            dkt_ref[...] = jnp.where(top, dka, dkb).astype(BF16)
            dvt_ref[...] = dv_acc[...].astype(BF16)
            dcola_ref[...] = dka[col_lane[0]:col_lane[0] + 1, :]
            dcolb_ref[...] = dkb[col_lane[1]:col_lane[1] + 1, :]

        @pl.when(n == n_steps - 1)
        def _():
            for b in range(nb):
                blk_a, blk_b = dqt_acc[b, 0], dqt_acc[b, 1]
                dqt_ref[:, b * t:(b + 1) * t] = (jnp.where(top, blk_a, blk_b) * qscale).astype(BF16)
                drowa_ref[:, b * t:(b + 1) * t] = blk_a[row_lane[0]:row_lane[0] + 1, :]
                drowb_ref[:, b * t:(b + 1) * t] = blk_b[row_lane[1]:row_lane[1] + 1, :]

    qspec = pl.BlockSpec((t, W), lambda hp, n, jj, ii: (ii[n], hp))
    kspec = pl.BlockSpec((t, W), lambda hp, n, jj, ii: (jj[n], hp))
    ktspec = pl.BlockSpec((W, t), lambda hp, n, jj, ii: (hp, jj[n]))
    pair_row = pl.BlockSpec((None, 1, t), lambda hp, n, jj, ii: (hp, 0, ii[n]))
    key_row = pl.BlockSpec((None, 1, t), lambda hp, n, jj, ii: (hp, 0, jj[n]))
    whole_row = pl.BlockSpec((None, 1, T), lambda hp, n, jj, ii: (hp, 0, 0))
    grid_spec = pltpu.PrefetchScalarGridSpec(
        num_scalar_prefetch=2, grid=(Hp, n_steps),
        in_specs=[
            qspec, kspec, kspec, qspec, kspec, qspec, pair_row, pair_row,
            pl.BlockSpec((None, 1, t), lambda hp, n, jj, ii: (2 * hp, 0, ii[n])),
            pl.BlockSpec((None, 1, t), lambda hp, n, jj, ii: (2 * hp + 1, 0, ii[n])),
        ],
        out_specs=[
            pl.BlockSpec((W, T), lambda hp, n, jj, ii: (hp, 0)),
            ktspec, ktspec, key_row, key_row, whole_row, whole_row,
        ],
        scratch_shapes=[pltpu.VMEM((nb, 2, W, t), F32), pltpu.VMEM((2, W, t), F32), pltpu.VMEM((W, t), F32),
                        pltpu.VMEM((2, t, t), BF16), pltpu.VMEM((2, t, t), BF16)],
    )
    return pl.pallas_call(
        body, name="flash_bwd", grid_spec=grid_spec,
        out_shape=(jax.ShapeDtypeStruct((D, T), BF16), jax.ShapeDtypeStruct((D, T), BF16),
                   jax.ShapeDtypeStruct((D, T), BF16), jax.ShapeDtypeStruct((Hp, 1, T), F32),
                   jax.ShapeDtypeStruct((Hp, 1, T), F32),
                   jax.ShapeDtypeStruct((Hp, 1, T), F32), jax.ShapeDtypeStruct((Hp, 1, T), F32)),
        compiler_params=_params(2),
    )(jj, ii, q, k, v, qaug, kaug, do, lse_a, lse_b, delta, delta)


HBM_SPEC = pl.BlockSpec(memory_space=pltpu.HBM)


def _mesh_pos():
    return lax.axis_index("x"), lax.axis_index("y"), lax.axis_index("c")


def _all_gather_shards(shards):
    n = len(shards)

    def body(*refs):
        in_refs, out_refs = refs[:n], refs[n:2 * n]
        send1, recv1, send2, recv2, lsem = refs[2 * n:]
        x, y, c = _mesh_pos()
        me = 2 * x + y
        chips = [(1 - x, y), (x, 1 - y), (1 - x, 1 - y)]
        local = [pltpu.make_async_copy(in_refs[t], out_refs[t].at[me], lsem.at[t]) for t in range(n)]
        for cp in local:
            cp.start()

        def half(t, chip_idx, pc):
            hr = shards[t].shape[0] // 2
            return out_refs[t].at[chip_idx, pl.ds(pc * hr, hr), :]

        first = []
        for t in range(n):
            hr = shards[t].shape[0] // 2
            for kk, (cx, cy) in enumerate(chips):
                first.append(pltpu.make_async_remote_copy(
                    src_ref=in_refs[t].at[pl.ds(c * hr, hr), :], dst_ref=half(t, me, c),
                    send_sem=send1.at[3 * t + kk], recv_sem=recv1.at[3 * t + kk],
                    device_id=(cx, cy, c), device_id_type=MESH))
        for cp in first:
            cp.start()
        passed = []
        for t in range(n):
            for kk, (cx, cy) in enumerate(chips):
                src_chip = 2 * cx + cy
                landed = half(t, src_chip, c)
                pltpu.make_async_remote_copy(
                    src_ref=landed, dst_ref=landed, send_sem=send1.at[3 * t + kk], recv_sem=recv1.at[3 * t + kk],
                    device_id=(cx, cy, c), device_id_type=MESH).wait_recv()
                fwd = pltpu.make_async_remote_copy(
                    src_ref=landed, dst_ref=landed, send_sem=send2.at[3 * t + kk], recv_sem=recv2.at[3 * t + kk],
                    device_id=(x, y, 1 - c), device_id_type=MESH)
                fwd.start()
                passed.append(fwd)
        for t in range(n):
            for kk, (cx, cy) in enumerate(chips):
                other = half(t, 2 * cx + cy, 1 - c)
                pltpu.make_async_remote_copy(
                    src_ref=other, dst_ref=other, send_sem=send2.at[3 * t + kk], recv_sem=recv2.at[3 * t + kk],
                    device_id=(x, y, 1 - c), device_id_type=MESH).wait_recv()
        for cp in first + passed:
            cp.wait_send()
        for cp in local:
            cp.wait()

    return pl.pallas_call(
        body, name="weights_all_gather",
        out_shape=[jax.ShapeDtypeStruct((N_CHIPS,) + s.shape, s.dtype) for s in shards],
        in_specs=[HBM_SPEC] * n, out_specs=[HBM_SPEC] * n,
        scratch_shapes=[pltpu.SemaphoreType.DMA((3 * n,)), pltpu.SemaphoreType.DMA((3 * n,)),
                        pltpu.SemaphoreType.DMA((3 * n,)), pltpu.SemaphoreType.DMA((3 * n,)),
                        pltpu.SemaphoreType.DMA((n,))],
    )(*shards)


def _sibling_swap_halves(g):
    _, M, C = g.shape
    hr = M // 2

    def body(g_ref, r_ref, ssem, rsem):
        x, y, c = _mesh_pos()
        cps = [pltpu.make_async_remote_copy(
            src_ref=g_ref.at[s, pl.ds((1 - c) * hr, hr), :], dst_ref=r_ref.at[s],
            send_sem=ssem.at[s], recv_sem=rsem.at[s], device_id=(x, y, 1 - c), device_id_type=MESH)
            for s in range(N_CHIPS)]
        for cp in cps:
            cp.start()
        for cp in cps:
            cp.wait()

    return pl.pallas_call(
        body, name="grads_sibling_swap", out_shape=jax.ShapeDtypeStruct((N_CHIPS, hr, C), F32),
        in_specs=[HBM_SPEC], out_specs=HBM_SPEC,
        scratch_shapes=[pltpu.SemaphoreType.DMA((N_CHIPS,)), pltpu.SemaphoreType.DMA((N_CHIPS,))],
    )(g)


def _pair_add(g, r, c_idx):
    _, M, C = g.shape
    hr = M // 2
    tr = PACK_ROW_TILE
    nbk = hr // tr

    def body(c_ref, g_ref, r_ref, o_ref):
        o_ref[...] = (g_ref[...] + r_ref[...]).astype(BF16)

    grid_spec = pltpu.PrefetchScalarGridSpec(
        num_scalar_prefetch=1, grid=(N_CHIPS, nbk),
        in_specs=[pl.BlockSpec((None, tr, C), lambda s, i, c: (s, c[0] * nbk + i, 0)),
                  pl.BlockSpec((None, tr, C), lambda s, i, c: (s, i, 0))],
        out_specs=pl.BlockSpec((None, tr, C), lambda s, i, c: (s, i, 0)),
    )
    return pl.pallas_call(body, name="grads_pair_add", grid_spec=grid_spec,
                          out_shape=jax.ShapeDtypeStruct((N_CHIPS, hr, C), BF16),
                          compiler_params=_params(2))(c_idx, g, r)


def _chip_scatter(pp):
    def body(p_ref, r_ref, ssem, rsem, lsem):
        x, y, c = _mesh_pos()
        me = 2 * x + y
        chips = [(1 - x, y), (x, 1 - y), (1 - x, 1 - y)]
        own = pltpu.make_async_copy(p_ref.at[me], r_ref.at[me], lsem)
        own.start()
        cps = [pltpu.make_async_remote_copy(
            src_ref=p_ref.at[2 * cx + cy], dst_ref=r_ref.at[me], send_sem=ssem.at[kk], recv_sem=rsem.at[kk],
            device_id=(cx, cy, c), device_id_type=MESH) for kk, (cx, cy) in enumerate(chips)]
        for cp in cps:
            cp.start()
        for kk, (cx, cy) in enumerate(chips):
            got = r_ref.at[2 * cx + cy]
            pltpu.make_async_remote_copy(
                src_ref=got, dst_ref=got, send_sem=ssem.at[kk], recv_sem=rsem.at[kk],
                device_id=(cx, cy, c), device_id_type=MESH).wait_recv()
        for cp in cps:
            cp.wait_send()
        own.wait()

    return pl.pallas_call(
        body, name="grads_chip_scatter", out_shape=jax.ShapeDtypeStruct(pp.shape, pp.dtype),
        in_specs=[HBM_SPEC], out_specs=HBM_SPEC,
        scratch_shapes=[pltpu.SemaphoreType.DMA((3,)), pltpu.SemaphoreType.DMA((3,)), pltpu.SemaphoreType.DMA],
    )(pp)


def _chip_sum(r, c_idx):
    _, hr, C = r.shape
    tr = PACK_ROW_TILE
    nbk = hr // tr

    def body(c_ref, r_ref, o_ref):
        r0, r1, r2, r3 = (r_ref[s].astype(F32) for s in range(N_CHIPS))
        o_ref[...] = ((r0 + r1) + r2) + r3

    grid_spec = pltpu.PrefetchScalarGridSpec(
        num_scalar_prefetch=1, grid=(nbk,),
        in_specs=[pl.BlockSpec((N_CHIPS, tr, C), lambda i, c: (0, i, 0))],
        out_specs=pl.BlockSpec((tr, C), lambda i, c: (c[0] * nbk + i, 0)),
    )
    return pl.pallas_call(body, name="grads_chip_sum", grid_spec=grid_spec,
                          out_shape=jax.ShapeDtypeStruct((2 * hr, C), F32), compiler_params=_params())(c_idx, r)


def _sibling_join(buf):
    hr = buf.shape[0] // 2

    def body(in_ref, o_ref, ssem, rsem):
        del in_ref
        x, y, c = _mesh_pos()
        mine = o_ref.at[pl.ds(c * hr, hr), :]
        cp = pltpu.make_async_remote_copy(src_ref=mine, dst_ref=mine, send_sem=ssem, recv_sem=rsem,
                                          device_id=(x, y, 1 - c), device_id_type=MESH)
        cp.start()
        theirs = o_ref.at[pl.ds((1 - c) * hr, hr), :]
        pltpu.make_async_remote_copy(src_ref=theirs, dst_ref=theirs, send_sem=ssem, recv_sem=rsem,
                                     device_id=(x, y, 1 - c), device_id_type=MESH).wait_recv()
        cp.wait_send()

    return pl.pallas_call(
        body, name="grads_sibling_join", out_shape=jax.ShapeDtypeStruct(buf.shape, F32),
        in_specs=[HBM_SPEC], out_specs=HBM_SPEC, input_output_aliases={0: 0},
        scratch_shapes=[pltpu.SemaphoreType.DMA, pltpu.SemaphoreType.DMA],
    )(buf)


def _adamw(w, g, m, v):
    M, C = w.shape
    tr = _tile(M, PACK_ROW_TILE)

    def body(w_ref, g_ref, m_ref, v_ref, d_ref, mo_ref, vo_ref):
        gv = g_ref[...]
        mn = ADAM_B1 * m_ref[...] + (1.0 - ADAM_B1) * gv
        vn = ADAM_B2 * v_ref[...] + (1.0 - ADAM_B2) * (gv * gv)
        m_hat = mn / (1.0 - ADAM_B1 ** ADAM_STEP)
        v_hat = vn / (1.0 - ADAM_B2 ** ADAM_STEP)
        d_ref[...] = -ADAM_LR * (m_hat / (jnp.sqrt(v_hat) + ADAM_EPS) + ADAM_WD * w_ref[...])
        mo_ref[...] = mn
        vo_ref[...] = vn

    spec = pl.BlockSpec((tr, C), lambda i: (i, 0))
    return pl.pallas_call(
        body, name="adamw", grid=(M // tr,), in_specs=[spec] * 4, out_specs=[spec] * 3,
        out_shape=[jax.ShapeDtypeStruct((M, C), F32)] * 3, compiler_params=_params())(w, g, m, v)


WEIGHT_NAMES = ("norm_g", "w_attn_in", "b_forget", "w_attn_out", "w_conv_in", "conv_w", "w_conv_out",
                "w_mlp_up", "w_mlp_down", "w_ple_proj", "w_ple_gate")
COL_SHARDED = ("norm_g", "w_attn_in", "w_conv_in", "conv_w", "w_mlp_up", "w_ple_proj")
ROW_SHARDED = ("w_attn_out", "w_conv_out", "w_mlp_down", "w_ple_gate")


def _unshard(name, gathered, shard_shape):
    a = gathered.reshape((N_CHIPS,) + tuple(shard_shape))
    if name in COL_SHARDED:
        a = jnp.moveaxis(a, 0, -2)
        return a.reshape(a.shape[:-2] + (N_CHIPS * shard_shape[-1],))
    a = jnp.moveaxis(a, 0, 1)
    return a.reshape((shard_shape[0], N_CHIPS * shard_shape[1], shard_shape[2]))


def _to_shard_major(name, full):
    if name == "b_forget":
        return jnp.broadcast_to(full.reshape(1, -1), (N_CHIPS, full.size))
    if name in COL_SHARDED:
        a = full.reshape(full.shape[:-1] + (N_CHIPS, full.shape[-1] // N_CHIPS))
        a = jnp.moveaxis(a, -2, 0)
    else:
        a = full.reshape((full.shape[0], N_CHIPS, full.shape[1] // N_CHIPS, full.shape[2]))
        a = jnp.moveaxis(a, 1, 0)
    return a.reshape(N_CHIPS, -1)


def _pack_layout(shard_shapes):
    offs, row = {}, 0
    for name in WEIGHT_NAMES:
        n = int(np.prod(shard_shapes[name]))
        rows = -(-n // PACK_COLS)
        offs[name] = (row, rows, n)
        row += rows
    total = -(-row // PACK_ROW_ALIGN) * PACK_ROW_ALIGN
    return offs, total


def _pack(flat_by_name, offs, total_rows, lead=()):
    parts, row = [], 0
    for name in WEIGHT_NAMES:
        _, rows, n = offs[name]
        a = flat_by_name[name].astype(F32)
        pad = rows * PACK_COLS - n
        if pad:
            a = jnp.pad(a, [(0, 0)] * len(lead) + [(0, pad)])
        parts.append(a.reshape(tuple(lead) + (rows, PACK_COLS)))
        row += rows
    if total_rows > row:
        parts.append(jnp.zeros(tuple(lead) + (total_rows - row, PACK_COLS), F32))
    return jnp.concatenate(parts, axis=len(lead))


def _unpack(packed, offs, shard_shapes):
    out = {}
    for name in WEIGHT_NAMES:
        row, rows, n = offs[name]
        out[name] = packed[row:row + rows].reshape(-1)[:n].reshape(shard_shapes[name])
    return out


def kernel(x, p, norm_g, w_attn_in, b_forget, w_attn_out, w_conv_in, conv_w, w_conv_out, w_mlp_up, w_mlp_down, w_ple_proj, w_ple_gate, loss_target, m_norm_g, m_w_attn_in, m_b_forget, m_w_attn_out, m_w_conv_in, m_conv_w, m_w_conv_out, m_w_mlp_up, m_w_mlp_down, m_w_ple_proj, m_w_ple_gate, v_norm_g, v_w_attn_in, v_b_forget, v_w_attn_out, v_w_conv_in, v_conv_w, v_w_conv_out, v_w_mlp_up, v_w_mlp_down, v_w_ple_proj, v_w_ple_gate):
    w_local = dict(norm_g=norm_g, w_attn_in=w_attn_in, b_forget=b_forget, w_attn_out=w_attn_out,
                   w_conv_in=w_conv_in, conv_w=conv_w, w_conv_out=w_conv_out, w_mlp_up=w_mlp_up,
                   w_mlp_down=w_mlp_down, w_ple_proj=w_ple_proj, w_ple_gate=w_ple_gate)
    m_local = dict(norm_g=m_norm_g, w_attn_in=m_w_attn_in, b_forget=m_b_forget, w_attn_out=m_w_attn_out,
                   w_conv_in=m_w_conv_in, conv_w=m_conv_w, w_conv_out=m_w_conv_out, w_mlp_up=m_w_mlp_up,
                   w_mlp_down=m_w_mlp_down, w_ple_proj=m_w_ple_proj, w_ple_gate=m_w_ple_gate)
    v_local = dict(norm_g=v_norm_g, w_attn_in=v_w_attn_in, b_forget=v_b_forget, w_attn_out=v_w_attn_out,
                   w_conv_in=v_w_conv_in, conv_w=v_conv_w, w_conv_out=v_w_conv_out, w_mlp_up=v_w_mlp_up,
                   w_mlp_down=v_w_mlp_down, w_ple_proj=v_w_ple_proj, w_ple_gate=v_w_ple_gate)
    shard_shapes = {k: tuple(a.shape) for k, a in w_local.items()}

    xs = x[0]
    target = loss_target[0]
    T, D = xs.shape
    depth = p.shape[0]
    H = b_forget.shape[1]
    qscale = float(D // H) ** -0.5
    head_sel = (jnp.arange(D)[None, :] // (D // H) == jnp.arange(H)[:, None]).astype(F32)

    big = [n for n in WEIGHT_NAMES if n not in ("norm_g", "conv_w", "b_forget")]
    small = jnp.concatenate([norm_g.reshape(-1), conv_w.reshape(-1)])
    n_small = small.shape[0]
    small_rows = -(-n_small // (LANES * 16)) * 16
    small = jnp.pad(small, (0, small_rows * LANES - n_small)).reshape(small_rows, LANES)
    shards = [w_local[n].astype(BF16).reshape(-1, shard_shapes[n][-1]) for n in big] + [small]
    gathered = _all_gather_shards(shards)
    full = {n: _unshard(n, g, shard_shapes[n]) for n, g in zip(big, gathered[:-1])}
    gs = gathered[-1].reshape(N_CHIPS, -1)
    full["norm_g"] = _unshard("norm_g", gs[:, :norm_g.size], shard_shapes["norm_g"])
    full["conv_w"] = _unshard("conv_w", gs[:, norm_g.size:n_small], shard_shapes["conv_w"])
    gains = full["norm_g"]

    def gain(i, k):
        return gains[i, k].reshape(1, D)

    def taps(j):
        return jnp.pad(full["conv_w"][j], ((0, 5), (0, 0)))

    saved = []
    h = xs
    for i in range(depth):
        j = i // 2
        s = {"x0": h}
        if i % 2 == 0:
            w_in = full["w_attn_in"][j]
            wqkv = w_in[:, :3 * D]
            wf = jnp.pad(w_in[:, 3 * D:], ((0, 0), (0, LANES - H)))
            bf = jnp.pad(b_forget[j].reshape(1, H), ((0, 0), (0, LANES - H)))
            q, k, v, fl, lf = _attn_in_fwd(h, gain(i, 0), wqkv, wf, bf, qscale)
            qaug, kaug = _aug_operands(_gate_cumsum(lf[:, :H].T), H, D)
            o, lse_a, lse_b = _flash_fwd(q, k, v, qaug, kaug, H)
            s.update(q=q, k=k, v=v, fl=fl, qaug=qaug, kaug=kaug, o=o, lse_a=lse_a, lse_b=lse_b, wqkv=wqkv, wf=wf)
            mix_in, w_out = o, full["w_attn_out"][j]
        else:
            b, c, u, zc, y = _conv_in_fwd(h, gain(i, 0), full["w_conv_in"][j], taps(j))
            s.update(b=b, c=c, u=u, zc=zc, y=y)
            mix_in, w_out = y, full["w_conv_out"][j]
        m1, x1 = _out_proj_fwd("mixer_out_fwd", mix_in, w_out, gain(i, 1), h)
        up, a = _mlp_up_fwd(x1, gain(i, 2), full["w_mlp_up"][i])
        m3, x2 = _out_proj_fwd("mlp_down_fwd", a, full["w_mlp_down"][i], gain(i, 3), x1)
        gl, pe, x3 = _ple_fwd(x2, p[i, 0], gain(i, 4), gain(i, 5), full["w_ple_gate"][i], full["w_ple_proj"][i])
        s.update(m1=m1, x1=x1, up=up, a=a, m3=m3, x2=x2, gl=gl, pe=pe, w_out=w_out)
        saved.append(s)
        h = x3

    dh, loss_blk = _loss_fwd_bwd(h, target)
    loss = lax.psum(loss_blk[0, 0], ("x", "y", "c"))

    g_gain = [[None] * 6 for _ in range(depth)]
    grads = {n: [None] * w_local[n].shape[0] for n in WEIGHT_NAMES if n != "norm_g"}
    for i in reversed(range(depth)):
        j = i // 2
        s = saved[i]
        dx2, dwp, dwg, g_gain[i][4], g_gain[i][5] = _ple_bwd(
            dh, s["x2"], p[i, 0], s["gl"], s["pe"], gain(i, 4), gain(i, 5), full["w_ple_gate"][i].T)
        grads["w_ple_proj"][i], grads["w_ple_gate"][i] = dwp, dwg
        dup, dwd, g_gain[i][3] = _out_proj_bwd(
            "mlp_down_bwd", dx2, s["m3"], gain(i, 3), [s["a"], s["up"]], full["w_mlp_down"][i].T, "relu2")
        grads["w_mlp_down"][i] = dwd
        dx1, dwu, g_gain[i][2] = _in_proj_bwd(
            "mlp_up_bwd", [dup], [full["w_mlp_up"][i].T], s["x1"], gain(i, 2), dx2)
        grads["w_mlp_up"][i] = dwu
        if i % 2 == 0:
            do, dwo, g_gain[i][1] = _out_proj_bwd(
                "attn_out_bwd", dx1, s["m1"], gain(i, 1), [s["o"]], s["w_out"].T, "plain")
            grads["w_attn_out"][j] = dwo
            delta = _attn_delta(do, s["o"], head_sel).reshape(H, 1, T)
            dqt, dkt, dvt, dca, dcb, dra, drb = _flash_bwd(s["q"], s["k"], s["v"], s["qaug"], s["kaug"], do,
                                                           s["lse_a"], s["lse_b"], delta, qscale)
            dcol = jnp.concatenate([dca, dcb], axis=1).reshape(H, T)
            drow = jnp.concatenate([dra, drb], axis=1).reshape(H, T)
            dfl_t, dbf = _gate_bwd(drow, dcol, s["fl"][:, :H].T)
            grads["b_forget"][j] = dbf.reshape(H)
            dfl = jnp.pad(dfl_t.T, ((0, 0), (0, LANES - H))).astype(BF16)
            wqkv_t = s["wqkv"].T
            dh, dwq, dwk, dwv, dwf, g_gain[i][0] = _in_proj_bwd(
                "attn_in_bwd", [dqt.T, dkt.T, dvt.T, dfl], [wqkv_t[:D], wqkv_t[D:2 * D], wqkv_t[2 * D:], s["wf"].T],
                s["x0"], gain(i, 0), dx1)
            grads["w_attn_in"][j] = jnp.concatenate([dwq, dwk, dwv, dwf[:, :H]], axis=1)
        else:
            dy, dwo, g_gain[i][1] = _out_proj_bwd(
                "conv_out_bwd", dx1, s["m1"], gain(i, 1), [s["y"]], s["w_out"].T, "plain_f32")
            grads["w_conv_out"][j] = dwo
            db, dc, du, dcw = _conv_bwd(dy, s["b"], s["c"], s["u"], s["zc"], taps(j))
            grads["conv_w"][j] = dcw[:conv_w.shape[1]]
            w_t = full["w_conv_in"][j].T
            dh, dwb, dwc, dwu2, g_gain[i][0] = _in_proj_bwd(
                "conv_in_bwd", [db, dc, du], [w_t[:D], w_t[D:2 * D], w_t[2 * D:]], s["x0"], gain(i, 0), dx1)
            grads["w_conv_in"][j] = jnp.concatenate([dwb, dwc, dwu2], axis=1)
    grad_x = dh.reshape(x.shape)

    grad_full = {n: jnp.stack(grads[n]) for n in grads}
    grad_full["norm_g"] = jnp.stack([jnp.concatenate(row, axis=0) for row in g_gain])

    offs, total_rows = _pack_layout(shard_shapes)
    g_packed = _pack({n: _to_shard_major(n, grad_full[n]) for n in WEIGHT_NAMES}, offs, total_rows, lead=(N_CHIPS,))
    c_idx = lax.axis_index("c").astype(jnp.int32).reshape(1)
    pair = _pair_add(g_packed, _sibling_swap_halves(g_packed), c_idx)
    g_red = _sibling_join(_chip_sum(_chip_scatter(pair), c_idx))
    g_out = _unpack(g_red, offs, shard_shapes)
    d_out, m_out, v_out = {}, {}, {}
    for n in WEIGHT_NAMES:
        shp = shard_shapes[n]
        two_d = (-1, shp[-1])
        d, mn, vn = _adamw(w_local[n].reshape(two_d), g_out[n].reshape(two_d),
                           m_local[n].reshape(two_d), v_local[n].reshape(two_d))
        d_out[n], m_out[n], v_out[n] = d.reshape(shp), mn.reshape(shp), vn.reshape(shp)
    return (loss, grad_x, *[g_out[n] for n in WEIGHT_NAMES], *[d_out[n] for n in WEIGHT_NAMES],
            *[m_out[n] for n in WEIGHT_NAMES], *[v_out[n] for n in WEIGHT_NAMES])
```

```python
import numpy as np
import jax
import jax.numpy as jnp
from jax import lax
from jax.experimental import pallas as pl
from jax.experimental.pallas import tpu as pltpu

F32 = jnp.float32
BF16 = jnp.bfloat16
MESH = pl.DeviceIdType.MESH

RMS_EPS = 1e-6
NEG_INF = -1e30
ADAM_LR = 0.001
ADAM_B1 = 0.9
ADAM_B2 = 0.999
ADAM_EPS = 1e-08
ADAM_WD = 0.01
ADAM_STEP = 10

N_CHIPS = 4
LANES = 128
ROW_TILE_FWD = 512
ROW_TILE_BWD = 256
ATTN_Q_TILE = 1024
ATTN_K_TILE = 512
ATTN_KEY_STRIP = 128
ATTN_QUERY_STRIP = 256
COL_CHUNK = 512
PACK_COLS = 1024
PACK_ROW_TILE = 256
PACK_ROW_ALIGN = 2 * PACK_ROW_TILE
N_AUG = 3
VMEM_LIMIT = 56 * 1024 * 1024


def _tile(n, pref):
    return pref if n % pref == 0 else n


def _dot(a, b):
    return jnp.dot(a, b, preferred_element_type=F32)


def _dot_tn(a, b):
    return lax.dot_general(a, b, (((0,), (0,)), ((), ())), preferred_element_type=F32)


def _dot_nt(a, b):
    return lax.dot_general(a, b, (((1,), (1,)), ((), ())), preferred_element_type=F32)


def _rms_fwd(x, g):
    r = lax.rsqrt(jnp.mean(x * x, axis=-1, keepdims=True) + RMS_EPS)
    return (x * r) * g


def _rms_bwd(x, g, dy):
    r = lax.rsqrt(jnp.mean(x * x, axis=-1, keepdims=True) + RMS_EPS)
    xh = x * r
    dyg = dy * g
    dx = r * (dyg - xh * jnp.mean(dyg * xh, axis=-1, keepdims=True))
    return dx, jnp.sum(dy * xh, axis=0, keepdims=True)


def _params(n_axes=1):
    return pltpu.CompilerParams(dimension_semantics=("arbitrary",) * n_axes, vmem_limit_bytes=VMEM_LIMIT)


def _row_call(name, body, n_rows, tm, row_ins, const_ins, row_outs, acc_outs=(), scratch=(), reverse=False):
    nb = n_rows // tm
    rmap = (lambda i: (nb - 1 - i, 0)) if reverse else (lambda i: (i, 0))

    def whole(shape):
        nd = len(shape)
        return pl.BlockSpec(tuple(shape), lambda i: (0,) * nd)

    in_specs = [pl.BlockSpec((tm, a.shape[1]), rmap) for a in row_ins] + [whole(a.shape) for a in const_ins]
    out_shape = [jax.ShapeDtypeStruct((n_rows, w), dt) for (w, dt) in row_outs]
    out_shape += [jax.ShapeDtypeStruct(tuple(s), dt) for (s, dt) in acc_outs]
    out_specs = [pl.BlockSpec((tm, w), rmap) for (w, _) in row_outs] + [whole(s) for (s, _) in acc_outs]
    return pl.pallas_call(
        body, name=name, grid=(nb,), in_specs=in_specs, out_specs=out_specs, out_shape=out_shape,
        scratch_shapes=list(scratch), compiler_params=_params(),
    )(*row_ins, *const_ins)


def _attn_in_fwd(x, g, wqkv, wf, bf, qscale):
    T, D = x.shape
    tm, ch = _tile(T, ROW_TILE_FWD), _tile(D, COL_CHUNK)

    def body(x_ref, g_ref, w_ref, wf_ref, bf_ref, q_ref, k_ref, v_ref, fl_ref, lf_ref):
        h = _rms_fwd(x_ref[...], g_ref[...]).astype(BF16)
        for part, o_ref in enumerate((q_ref, k_ref, v_ref)):
            for n0 in range(0, D, ch):
                r = _dot(h, w_ref[:, part * D + n0:part * D + n0 + ch])
                if part == 0:
                    r = r * qscale
                o_ref[:, n0:n0 + ch] = r.astype(BF16)
        fl = _dot(h, wf_ref[...]) + bf_ref[...]
        fl_ref[...] = fl
        lf_ref[...] = jnp.minimum(fl, 0.0) - jnp.log1p(jnp.exp(-jnp.abs(fl)))

    return _row_call("attn_in_fwd", body, T, tm, [x], [g, wqkv, wf, bf],
                     [(D, BF16), (D, BF16), (D, BF16), (LANES, F32), (LANES, F32)])


def _conv_in_fwd(x, g, w, cw):
    T, D = x.shape
    tm, ch = _tile(T, ROW_TILE_FWD), _tile(D, COL_CHUNK)

    def body(x_ref, g_ref, w_ref, cw_ref, b_ref, c_ref, u_ref, zc_ref, y_ref, tail_ref):
        i = pl.program_id(0)

        @pl.when(i == 0)
        def _():
            tail_ref[...] = jnp.zeros_like(tail_ref)

        h = _rms_fwd(x_ref[...], g_ref[...]).astype(BF16)
        for part, o_ref in enumerate((b_ref, c_ref, u_ref)):
            for n0 in range(0, D, ch):
                o_ref[:, n0:n0 + ch] = _dot(h, w_ref[:, part * D + n0:part * D + n0 + ch])
        z = c_ref[...] * u_ref[...]
        row = lax.broadcasted_iota(jnp.int32, (tm, 1), 0)
        t6, t7 = tail_ref[6:7, :], tail_ref[7:8, :]
        z1 = jnp.where(row == 0, t7, pltpu.roll(z, 1, axis=0))
        z2 = jnp.where(row == 0, t6, jnp.where(row == 1, t7, pltpu.roll(z, 2, axis=0)))
        zc = cw_ref[0:1, :] * z2 + cw_ref[1:2, :] * z1 + cw_ref[2:3, :] * z
        zc_ref[...] = zc
        y_ref[...] = (b_ref[...] * zc).astype(BF16)
        tail_ref[...] = z[tm - 8:tm, :]

    return _row_call("conv_in_fwd", body, T, tm, [x], [g, w, cw],
                     [(D, F32), (D, F32), (D, F32), (D, F32), (D, BF16)], scratch=[pltpu.VMEM((8, D), F32)])


def _mlp_up_fwd(x, g, w):
    T, D = x.shape
    F = w.shape[1]
    tm, ch = _tile(T, ROW_TILE_FWD), _tile(F, COL_CHUNK)

    def body(x_ref, g_ref, w_ref, up_ref, a_ref):
        h = _rms_fwd(x_ref[...], g_ref[...]).astype(BF16)
        for n0 in range(0, F, ch):
            r = _dot(h, w_ref[:, n0:n0 + ch])
            up_ref[:, n0:n0 + ch] = r.astype(BF16)
            rl = jnp.maximum(r, 0.0)
            a_ref[:, n0:n0 + ch] = (rl * rl).astype(BF16)

    return _row_call("mlp_up_fwd", body, T, tm, [x], [g, w], [(F, BF16), (F, BF16)])


def _out_proj_fwd(name, a, w, g, x):
    T, D = x.shape
    tm = _tile(T, ROW_TILE_FWD)

    def body(a_ref, x_ref, w_ref, g_ref, m_ref, xn_ref):
        m = _dot(a_ref[...], w_ref[...])
        m_ref[...] = m
        xn_ref[...] = x_ref[...] + _rms_fwd(m, g_ref[...])

    return _row_call(name, body, T, tm, [a, x], [w, g], [(D, F32), (D, F32)])


def _ple_fwd(x, p, g4, g5, wg, wp):
    T, D = x.shape
    tm = _tile(T, ROW_TILE_FWD)

    def body(x_ref, p_ref, g4_ref, g5_ref, wg_ref, wp_ref, gl_ref, pe_ref, xn_ref):
        xv = x_ref[...]
        gl = _dot(_rms_fwd(xv, g4_ref[...]).astype(BF16), wg_ref[...])
        pe = _dot(p_ref[...].astype(BF16), wp_ref[...])
        gl_ref[...] = gl
        pe_ref[...] = pe
        e = pe * (1.0 / (1.0 + jnp.exp(-gl)))
        xn_ref[...] = xv + _rms_fwd(e, g5_ref[...])

    return _row_call("ple_fwd", body, T, tm, [x, p], [g4, g5, wg, wp], [(D, F32), (D, F32), (D, F32)])


def _loss_fwd_bwd(y, target):
    T, D = y.shape
    tm = _tile(T, ROW_TILE_FWD)

    def body(y_ref, t_ref, dy_ref, loss_ref):
        @pl.when(pl.program_id(0) == 0)
        def _():
            loss_ref[...] = jnp.zeros_like(loss_ref)

        err = y_ref[...] - t_ref[...]
        dy_ref[...] = err * (1.0 / D)
        part = 0.5 * jnp.sum(jnp.mean(err * err, axis=-1, keepdims=True), axis=0, keepdims=True)
        loss_ref[...] += jnp.broadcast_to(part, loss_ref.shape)

    return _row_call("loss", body, T, tm, [y, target], [], [(D, F32)], acc_outs=[((8, LANES), F32)])


def _out_proj_bwd(name, dres, m, g, a_ins, wt, mode):
    T, D = dres.shape
    Ka = wt.shape[1]
    tm, ch = _tile(T, ROW_TILE_BWD), _tile(Ka, COL_CHUNK)
    out_dt = F32 if mode == "plain_f32" else BF16

    def body(dres_ref, m_ref, *rest):
        a_ref = rest[0]
        up_ref = rest[1] if mode == "relu2" else None
        k = len(a_ins)
        g_ref, wt_ref, da_ref, dw_ref, dg_ref = rest[k:k + 5]

        @pl.when(pl.program_id(0) == 0)
        def _():
            dw_ref[...] = jnp.zeros_like(dw_ref)
            dg_ref[...] = jnp.zeros_like(dg_ref)

        dm, dgp = _rms_bwd(m_ref[...], g_ref[...], dres_ref[...])
        dg_ref[...] += dgp
        dmb = dm.astype(BF16)
        for n0 in range(0, Ka, ch):
            da = _dot(dmb, wt_ref[:, n0:n0 + ch])
            if mode == "relu2":
                da = da * (2.0 * jnp.maximum(up_ref[:, n0:n0 + ch].astype(F32), 0.0))
            da_ref[:, n0:n0 + ch] = da.astype(out_dt)
            dw_ref[n0:n0 + ch, :] += _dot_tn(a_ref[:, n0:n0 + ch], dmb)

    return _row_call(name, body, T, tm, [dres, m] + list(a_ins), [g, wt],
                     [(Ka, out_dt)], acc_outs=[((Ka, D), F32), ((1, D), F32)])


def _in_proj_bwd(name, pieces, wts, x, g, dres):
    T, D = x.shape
    tm = _tile(T, ROW_TILE_BWD)
    k = len(pieces)
    widths = [pc.shape[1] for pc in pieces]

    def body(*refs):
        pc_refs = refs[:k]
        x_ref, dres_ref, g_ref = refs[k:k + 3]
        wt_refs = refs[k + 3:2 * k + 3]
        dx_ref = refs[2 * k + 3]
        dw_refs = refs[2 * k + 4:3 * k + 4]
        dg_ref = refs[3 * k + 4]

        @pl.when(pl.program_id(0) == 0)
        def _():
            for r in dw_refs:
                r[...] = jnp.zeros_like(r)
            dg_ref[...] = jnp.zeros_like(dg_ref)

        xv, gv = x_ref[...], g_ref[...]
        hb = _rms_fwd(xv, gv).astype(BF16)
        dh = None
        for pc_ref, wt_ref, dw_ref, n in zip(pc_refs, wt_refs, dw_refs, widths):
            d = _dot(pc_ref[...], wt_ref[...])
            dh = d if dh is None else dh + d
            ch = _tile(n, COL_CHUNK)
            for n0 in range(0, n, ch):
                dw_ref[:, n0:n0 + ch] += _dot_tn(hb, pc_ref[:, n0:n0 + ch])
        dxn, dgp = _rms_bwd(xv, gv, dh)
        dg_ref[...] += dgp
        dx_ref[...] = dres_ref[...] + dxn

    return _row_call(name, body, T, tm, list(pieces) + [x, dres], [g] + list(wts), [(D, F32)],
                     acc_outs=[((D, n), F32) for n in widths] + [((1, D), F32)])


def _conv_bwd(dy, b, c, u, zc, cw):
    T, D = dy.shape
    tm = _tile(T, ROW_TILE_BWD)

    def body(dy_ref, b_ref, c_ref, u_ref, zc_ref, cw_ref, db_ref, dc_ref, du_ref, dcw_ref, head_ref):
        @pl.when(pl.program_id(0) == 0)
        def _():
            head_ref[...] = jnp.zeros_like(head_ref)
            dcw_ref[...] = jnp.zeros_like(dcw_ref)

        dyv, cv, uv = dy_ref[...], c_ref[...], u_ref[...]
        db_ref[...] = (dyv * zc_ref[...]).astype(BF16)
        dzc = dyv * b_ref[...]
        row = lax.broadcasted_iota(jnp.int32, (tm, 1), 0)
        h0, h1 = head_ref[0:1, :], head_ref[1:2, :]
        d1 = jnp.where(row == tm - 1, h0, pltpu.roll(dzc, tm - 1, axis=0))
        d2 = jnp.where(row == tm - 1, h1, jnp.where(row == tm - 2, h0, pltpu.roll(dzc, tm - 2, axis=0)))
        dz = cw_ref[2:3, :] * dzc + cw_ref[1:2, :] * d1 + cw_ref[0:1, :] * d2
        dc_ref[...] = (dz * uv).astype(BF16)
        du_ref[...] = (dz * cv).astype(BF16)
        z = cv * uv
        dcw_ref[0:1, :] += jnp.sum(d2 * z, axis=0, keepdims=True)
        dcw_ref[1:2, :] += jnp.sum(d1 * z, axis=0, keepdims=True)
        dcw_ref[2:3, :] += jnp.sum(dzc * z, axis=0, keepdims=True)
        head_ref[...] = dzc[0:8, :]

    return _row_call("conv_bwd", body, T, tm, [dy, b, c, u, zc], [cw], [(D, BF16), (D, BF16), (D, BF16)],
                     acc_outs=[((8, D), F32)], scratch=[pltpu.VMEM((8, D), F32)], reverse=True)


def _ple_bwd(dres, x, p, gl, pe, g4, g5, wgt):
    T, D = x.shape
    P = p.shape[1]
    tm = _tile(T, ROW_TILE_BWD)

    def body(dres_ref, x_ref, p_ref, gl_ref, pe_ref, g4_ref, g5_ref, wgt_ref,
             dx_ref, dwp_ref, dwg_ref, dg4_ref, dg5_ref):
        @pl.when(pl.program_id(0) == 0)
        def _():
            for r in (dwp_ref, dwg_ref, dg4_ref, dg5_ref):
                r[...] = jnp.zeros_like(r)

        dr, xv, pe_v = dres_ref[...], x_ref[...], pe_ref[...]
        gate = 1.0 / (1.0 + jnp.exp(-gl_ref[...]))
        de, dg5p = _rms_bwd(pe_v * gate, g5_ref[...], dr)
        dg5_ref[...] += dg5p
        dpe = (de * gate).astype(BF16)
        dgl = (de * pe_v * gate * (1.0 - gate)).astype(BF16)
        dwp_ref[...] += _dot_tn(p_ref[...].astype(BF16), dpe)
        g4v = g4_ref[...]
        dwg_ref[...] += _dot_tn(_rms_fwd(xv, g4v).astype(BF16), dgl)
        dxn, dg4p = _rms_bwd(xv, g4v, _dot(dgl, wgt_ref[...]))
        dg4_ref[...] += dg4p
        dx_ref[...] = dr + dxn

    return _row_call("ple_bwd", body, T, tm, [dres, x, p, gl, pe], [g4, g5, wgt], [(D, F32)],
                     acc_outs=[((P, D), F32), ((D, D), F32), ((1, D), F32), ((1, D), F32)])


def _scan_lanes(v, reverse):
    n = v.shape[1]
    lane = lax.broadcasted_iota(jnp.int32, v.shape, 1)
    s = 1
    while s < n:
        if reverse:
            v = v + jnp.where(lane < n - s, pltpu.roll(v, n - s, axis=1), 0.0)
        else:
            v = v + jnp.where(lane >= s, pltpu.roll(v, s, axis=1), 0.0)
        s *= 2
    return v


def _gate_cumsum(lf_t):
    def body(lf_ref, *piece_refs):
        rest = _scan_lanes(lf_ref[...], reverse=False)
        for r in piece_refs:
            piece = rest.astype(BF16)
            r[...] = piece
            rest = rest - piece.astype(F32)

    return pl.pallas_call(body, name="gate_cumsum", out_shape=[jax.ShapeDtypeStruct(lf_t.shape, BF16)] * N_AUG,
                          compiler_params=pltpu.CompilerParams(vmem_limit_bytes=VMEM_LIMIT))(lf_t)


def _gate_bwd(drow_t, dcol_t, fl_t):
    H = fl_t.shape[0]

    def body(dr_ref, dc_ref, fl_ref, dfl_ref, dbf_ref):
        dlf = _scan_lanes(dr_ref[...] - dc_ref[...], reverse=True)
        dfl = dlf * (1.0 / (1.0 + jnp.exp(fl_ref[...])))
        dfl_ref[...] = dfl
        dbf_ref[...] = jnp.sum(dfl, axis=1, keepdims=True)

    return pl.pallas_call(
        body, name="gate_bwd",
        out_shape=(jax.ShapeDtypeStruct(fl_t.shape, F32), jax.ShapeDtypeStruct((H, 1), F32)),
        compiler_params=pltpu.CompilerParams(vmem_limit_bytes=VMEM_LIMIT))(drow_t, dcol_t, fl_t)


def _aug_operands(pieces, H, D):
    T = pieces[0].shape[1]
    dh = D // H
    one = jnp.ones((H, T), BF16)
    qa = jnp.stack(list(pieces) + [one] * N_AUG, axis=-1)
    ka = jnp.stack([one] * N_AUG + [-pc for pc in pieces], axis=-1)

    def place(a):
        a = jnp.pad(a, ((0, 0), (0, 0), (0, dh - 2 * N_AUG))).reshape(H // 2, 2, T, dh)
        return jnp.transpose(a[:, ::-1], (2, 0, 1, 3)).reshape(T, D)

    return place(qa), place(ka)


def _strip_kind(koff, k0, ksz, q0, qsz):
    if koff is None or koff + k0 + ksz - 1 <= q0:
        return "full"
    return "skip" if koff + k0 > q0 + qsz - 1 else "partial"


def _score_strip(km, qm, koff, k0, ksz, q0, qsz, kind):
    st = _dot_nt(km[k0:k0 + ksz, :], qm[q0:q0 + qsz, :])
    if kind == "partial":
        kpos = koff + k0 + lax.broadcasted_iota(jnp.int32, (ksz, qsz), 0)
        st = jnp.where(kpos <= q0 + lax.broadcasted_iota(jnp.int32, (ksz, qsz), 1), st, NEG_INF)
    return st


def _fold8(v, op):
    rows, n = v.shape
    v3 = v.reshape(rows // 8, 8, n)
    out = v3[0]
    for r in range(1, rows // 8):
        out = op(out, v3[r])
    return out


def _causal_tables(nbq, r, q_outer):
    a, b = [], []
    for o in range(nbq if q_outer else nbq * r):
        inner = range((o + 1) * r) if q_outer else range(o // r, nbq)
        for n in inner:
            a.append(o)
            b.append(n)
    return jnp.asarray(np.array(a, np.int32)), jnp.asarray(np.array(b, np.int32))


def _attn_tiles(T):
    tq = _tile(T, ATTN_Q_TILE)
    tk = _tile(tq, ATTN_K_TILE)
    return tq, tk, _tile(tk, ATTN_KEY_STRIP), _tile(tq, ATTN_QUERY_STRIP)


def _flash_fwd(q, k, v, qaug, kaug, H):
    T, D = q.shape
    Hp = H // 2
    W = D // Hp
    dh = W // 2
    tq, tk, ksz, qsz = _attn_tiles(T)
    r = tq // tk
    ii, jj = _causal_tables(T // tq, r, q_outer=True)
    n_steps = int(ii.shape[0])

    def body(ii_ref, jj_ref, q_ref, k_ref, v_ref, qa_ref, ka_ref, o_ref, lsea_ref, lseb_ref,
             qm_s, m_s, l_s, acc_s, p_s):
        n = pl.program_id(1)
        i, j = ii_ref[n], jj_ref[n]
        d = j - i * r
        in_a = lax.broadcasted_iota(jnp.int32, (1, W), 1) < dh
        top = lax.broadcasted_iota(jnp.int32, (W, 1), 0) < dh

        @pl.when(j == 0)
        def _():
            qv, qa = q_ref[...], qa_ref[...]
            qm_s[0] = jnp.where(in_a, qv, qa)
            qm_s[1] = jnp.where(in_a, qa, qv)
            m_s[...] = jnp.full(m_s.shape, NEG_INF, F32)
            l_s[...] = jnp.zeros_like(l_s)
            acc_s[...] = jnp.zeros_like(acc_s)

        def step(koff):
            kv, ka, vv = k_ref[...], ka_ref[...], v_ref[...]
            kms = (jnp.where(in_a, kv, ka), jnp.where(in_a, ka, kv))
            vt = vv.T
            alphas = {}

            def softmax_stage(hh, q0):
                cols = slice(q0, q0 + qsz)
                kinds = [(k0, _strip_kind(koff, k0, ksz, q0, qsz)) for k0 in range(0, tk, ksz)]
                if all(kind == "skip" for _, kind in kinds):
                    return False
                strips, part = {}, None
                for k0, kind in kinds:
                    if kind != "skip":
                        strips[k0] = _score_strip(kms[hh], qm_s[hh], koff, k0, ksz, q0, qsz, kind)
                        p8 = _fold8(strips[k0], jnp.maximum)
                        part = p8 if part is None else jnp.maximum(part, p8)
                m_prev = m_s[hh, :, cols]
                m_new = jnp.maximum(m_prev, jnp.max(part, axis=0, keepdims=True))
                l8 = None
                for k0, kind in kinds:
                    if kind == "skip":
                        p_s[hh, k0:k0 + ksz, cols] = jnp.zeros((ksz, qsz), BF16)
                        continue
                    pt = jnp.exp(strips[k0] - m_new)
                    s8 = _fold8(pt, jnp.add)
                    l8 = s8 if l8 is None else l8 + s8
                    p_s[hh, k0:k0 + ksz, cols] = pt.astype(BF16)
                alpha = jnp.exp(m_prev - m_new)
                l_s[hh, :, cols] = alpha * l_s[hh, :, cols] + jnp.sum(l8, axis=0, keepdims=True)
                m_s[hh, :, cols] = m_new
                alphas[(hh, q0)] = alpha
                return True

            def value_stage(hh, q0):
                cols, rows = slice(q0, q0 + qsz), slice(hh * dh, (hh + 1) * dh)
                acc_s[rows, cols] = acc_s[rows, cols] * alphas[(hh, q0)] + _dot(vt[rows, :], p_s[hh, :, cols])

            units = [(hh, q0) for q0 in range(0, tq, qsz) for hh in range(2)]
            pending = None
            for unit in units:
                live = softmax_stage(*unit)
                if pending is not None:
                    value_stage(*pending)
                pending = unit if live else None
            if pending is not None:
                value_stage(*pending)

        @pl.when(d < 0)
        def _():
            step(None)

        for dd in range(r):
            @pl.when(d == dd)
            def _():
                step(dd * tk)

        @pl.when(d == r - 1)
        def _():
            inv = jnp.where(top, 1.0 / l_s[0], 1.0 / l_s[1])
            o_ref[...] = (acc_s[...] * inv).T.astype(BF16)
            lsea_ref[...] = m_s[0] + jnp.log(l_s[0])
            lseb_ref[...] = m_s[1] + jnp.log(l_s[1])

    qspec = pl.BlockSpec((tq, W), lambda hp, n, ii, jj: (ii[n], hp))
    kspec = pl.BlockSpec((tk, W), lambda hp, n, ii, jj: (jj[n], hp))
    rspec = pl.BlockSpec((None, 1, tq), lambda hp, n, ii, jj: (hp, 0, ii[n]))
    grid_spec = pltpu.PrefetchScalarGridSpec(
        num_scalar_prefetch=2, grid=(Hp, n_steps),
        in_specs=[qspec, kspec, kspec, qspec, kspec],
        out_specs=[qspec, rspec, rspec],
        scratch_shapes=[pltpu.VMEM((2, tq, W), BF16), pltpu.VMEM((2, 1, tq), F32), pltpu.VMEM((2, 1, tq), F32),
                        pltpu.VMEM((W, tq), F32), pltpu.VMEM((2, tk, tq), BF16)],
    )
    return pl.pallas_call(
        body, name="flash_fwd", grid_spec=grid_spec,
        out_shape=(jax.ShapeDtypeStruct((T, D), BF16), jax.ShapeDtypeStruct((Hp, 1, T), F32),
                   jax.ShapeDtypeStruct((Hp, 1, T), F32)),
        compiler_params=_params(2),
    )(ii, jj, q, k, v, qaug, kaug)


def _attn_delta(do, o, head_sel):
    T, D = do.shape
    H = head_sel.shape[0]
    tm = _tile(T, ROW_TILE_FWD)

    def body(do_ref, o_ref, sel_ref, out_ref):
        prod = do_ref[...].astype(F32) * o_ref[...].astype(F32)
        out_ref[...] = lax.dot_general(sel_ref[...], prod, (((1,), (1,)), ((), ())),
                                       precision=lax.Precision.HIGHEST, preferred_element_type=F32)

    return pl.pallas_call(
        body, name="attn_delta", grid=(T // tm,),
        in_specs=[pl.BlockSpec((tm, D), lambda i: (i, 0)), pl.BlockSpec((tm, D), lambda i: (i, 0)),
                  pl.BlockSpec((H, D), lambda i: (0, 0))],
        out_specs=pl.BlockSpec((H, tm), lambda i: (0, i)),
        out_shape=jax.ShapeDtypeStruct((H, T), F32), compiler_params=_params())(do, o, head_sel)


def _flash_bwd(q, k, v, qaug, kaug, do, lse_a, lse_b, delta, qscale):
    T, D = q.shape
    Hp = lse_a.shape[0]
    W = D // Hp
    dh = W // 2
    tq, tk, ksz, qsz = _attn_tiles(T)
    r = tq // tk
    nbq = T // tq
    jj, ii = _causal_tables(nbq, r, q_outer=False)
    n_steps = int(ii.shape[0])
    row_lane = (dh, 0)
    col_lane = (dh + N_AUG, N_AUG)

    def body(jj_ref, ii_ref, q_ref, k_ref, v_ref, qa_ref, ka_ref, do_ref, lsea_ref, lseb_ref, dla_ref, dlb_ref,
             dqt_ref, dkt_ref, dvt_ref, dcola_ref, dcolb_ref, drowa_ref, drowb_ref,
             dqt_acc, dk_acc, dv_acc, p2_s, ds2_s):
        n = pl.program_id(1)
        i, j = ii_ref[n], jj_ref[n]
        d = j - i * r
        in_a = lax.broadcasted_iota(jnp.int32, (1, W), 1) < dh
        top = lax.broadcasted_iota(jnp.int32, (W, 1), 0) < dh

        @pl.when(n == 0)
        def _():
            dqt_acc[...] = jnp.zeros_like(dqt_acc)

        @pl.when(d >= 0)
        def _():
            dk_acc[...] = jnp.zeros_like(dk_acc)
            dv_acc[...] = jnp.zeros_like(dv_acc)

        def step(koff):
            qv, qa, kv, ka = q_ref[...], qa_ref[...], k_ref[...], ka_ref[...]
            vv, dov = v_ref[...], do_ref[...]
            zb = jnp.zeros_like(dov)
            kms = (jnp.where(in_a, kv, ka), jnp.where(in_a, ka, kv))
            qms = (jnp.where(in_a, qv, qa), jnp.where(in_a, qa, qv))
            doms = (jnp.where(in_a, dov, zb), jnp.where(in_a, zb, dov))
            kts, qts, dots = [tuple(a.T for a in pair) for pair in (kms, qms, doms)]
            lses, dls = (lsea_ref[...], lseb_ref[...]), (dla_ref[...], dlb_ref[...])
            dvs, dks = [], ([], [])

            def score_stage(hh, q0):
                cols = slice(q0, q0 + qsz)
                kinds = [(k0, _strip_kind(koff, k0, ksz, q0, qsz)) for k0 in range(0, tk, ksz)]
                if all(kind == "skip" for _, kind in kinds):
                    return False
                for k0, kind in kinds:
                    if kind == "skip":
                        p2_s[hh, k0:k0 + ksz, cols] = jnp.zeros((ksz, qsz), BF16)
                        ds2_s[hh, k0:k0 + ksz, cols] = jnp.zeros((ksz, qsz), BF16)
                        continue
                    st = _score_strip(kms[hh], qms[hh], koff, k0, ksz, q0, qsz, kind)
                    pt = jnp.exp(st - lses[hh][:, cols])
                    dst = pt * (_dot_nt(vv[k0:k0 + ksz, :], doms[hh][cols, :]) - dls[hh][:, cols])
                    p2_s[hh, k0:k0 + ksz, cols] = pt.astype(BF16)
                    ds2_s[hh, k0:k0 + ksz, cols] = dst.astype(BF16)
                return True

            def grad_stage(hh, q0):
                cols = slice(q0, q0 + qsz)
                ptb, dsb = p2_s[hh, :, cols], ds2_s[hh, :, cols]
                dvs.append(_dot_nt(dots[hh][:, cols], ptb))
                dks[hh].append(_dot_nt(qts[hh][:, cols], dsb))
                dqt_acc[i, hh, :, cols] += _dot(kts[hh], dsb)

            units = [(hh, q0) for q0 in range(0, tq, qsz) for hh in range(2)]
            pending = None
            for unit in units:
                live = score_stage(*unit)
                if pending is not None:
                    grad_stage(*pending)
                pending = unit if live else None
            if pending is not None:
                grad_stage(*pending)
            dv_acc[...] += sum(dvs[1:], dvs[0])
            for hh in range(2):
                dk_acc[hh] += sum(dks[hh][1:], dks[hh][0])

        @pl.when(d < 0)
        def _():
            step(None)

        for dd in range(r):
            @pl.when(d == dd)
            def _():
                step(dd * tk)

        @pl.when(i == nbq - 1)
        def _():
            dka, dkb = dk_acc[0], dk_acc[1]
            dkt_ref[...] = jnp.where(top, dka, dkb).astype(BF16)
            dvt_ref[...] = dv_acc[...].astype(BF16)
            dcola_ref[...] = dka[col_lane[0]:col_lane[0] + 1, :]
            dcolb_ref[...] = dkb[col_lane[1]:col_lane[1] + 1, :]

        @pl.when(n == n_steps - 1)
        def _():
            for b in range(nbq):
                blk_a, blk_b = dqt_acc[b, 0], dqt_acc[b, 1]
                dqt_ref[:, b * tq:(b + 1) * tq] = (jnp.where(top, blk_a, blk_b) * qscale).astype(BF16)
                drowa_ref[:, b * tq:(b + 1) * tq] = blk_a[row_lane[0]:row_lane[0] + 1, :]
                drowb_ref[:, b * tq:(b + 1) * tq] = blk_b[row_lane[1]:row_lane[1] + 1, :]

    qspec = pl.BlockSpec((tq, W), lambda hp, n, jj, ii: (ii[n], hp))
    kspec = pl.BlockSpec((tk, W), lambda hp, n, jj, ii: (jj[n], hp))
    ktspec = pl.BlockSpec((W, tk), lambda hp, n, jj, ii: (hp, jj[n]))
    pair_row = pl.BlockSpec((None, 1, tq), lambda hp, n, jj, ii: (hp, 0, ii[n]))
    key_row = pl.BlockSpec((None, 1, tk), lambda hp, n, jj, ii: (hp, 0, jj[n]))
    whole_row = pl.BlockSpec((None, 1, T), lambda hp, n, jj, ii: (hp, 0, 0))
    grid_spec = pltpu.PrefetchScalarGridSpec(
        num_scalar_prefetch=2, grid=(Hp, n_steps),
        in_specs=[
            qspec, kspec, kspec, qspec, kspec, qspec, pair_row, pair_row,
            pl.BlockSpec((None, 1, tq), lambda hp, n, jj, ii: (2 * hp, 0, ii[n])),
            pl.BlockSpec((None, 1, tq), lambda hp, n, jj, ii: (2 * hp + 1, 0, ii[n])),
        ],
        out_specs=[
            pl.BlockSpec((W, T), lambda hp, n, jj, ii: (hp, 0)),
            ktspec, ktspec, key_row, key_row, whole_row, whole_row,
        ],
        scratch_shapes=[pltpu.VMEM((nbq, 2, W, tq), F32), pltpu.VMEM((2, W, tk), F32), pltpu.VMEM((W, tk), F32),
                        pltpu.VMEM((2, tk, tq), BF16), pltpu.VMEM((2, tk, tq), BF16)],
    )
    return pl.pallas_call(
        body, name="flash_bwd", grid_spec=grid_spec,
        out_shape=(jax.ShapeDtypeStruct((D, T), BF16), jax.ShapeDtypeStruct((D, T), BF16),
                   jax.ShapeDtypeStruct((D, T), BF16), jax.ShapeDtypeStruct((Hp, 1, T), F32),
                   jax.ShapeDtypeStruct((Hp, 1, T), F32),
                   jax.ShapeDtypeStruct((Hp, 1, T), F32), jax.ShapeDtypeStruct((Hp, 1, T), F32)),
        compiler_params=_params(2),
    )(jj, ii, q, k, v, qaug, kaug, do, lse_a, lse_b, delta, delta)


HBM_SPEC = pl.BlockSpec(memory_space=pltpu.HBM)


def _mesh_pos():
    return lax.axis_index("x"), lax.axis_index("y"), lax.axis_index("c")


def _all_gather_shards(shards):
    n = len(shards)

    def body(*refs):
        in_refs, out_refs = refs[:n], refs[n:2 * n]
        send1, recv1, send2, recv2, lsem = refs[2 * n:]
        x, y, c = _mesh_pos()
        me = 2 * x + y
        chips = [(1 - x, y), (x, 1 - y), (1 - x, 1 - y)]
        local = [pltpu.make_async_copy(in_refs[t], out_refs[t].at[me], lsem.at[t]) for t in range(n)]
        for cp in local:
            cp.start()

        def half(t, chip_idx, pc):
            hr = shards[t].shape[0] // 2
            return out_refs[t].at[chip_idx, pl.ds(pc * hr, hr), :]

        first = []
        for t in range(n):
            hr = shards[t].shape[0] // 2
            for kk, (cx, cy) in enumerate(chips):
                first.append(pltpu.make_async_remote_copy(
                    src_ref=in_refs[t].at[pl.ds(c * hr, hr), :], dst_ref=half(t, me, c),
                    send_sem=send1.at[3 * t + kk], recv_sem=recv1.at[3 * t + kk],
                    device_id=(cx, cy, c), device_id_type=MESH))
        for cp in first:
            cp.start()
        passed = []
        for t in range(n):
            for kk, (cx, cy) in enumerate(chips):
                src_chip = 2 * cx + cy
                landed = half(t, src_chip, c)
                pltpu.make_async_remote_copy(
                    src_ref=landed, dst_ref=landed, send_sem=send1.at[3 * t + kk], recv_sem=recv1.at[3 * t + kk],
                    device_id=(cx, cy, c), device_id_type=MESH).wait_recv()
                fwd = pltpu.make_async_remote_copy(
                    src_ref=landed, dst_ref=landed, send_sem=send2.at[3 * t + kk], recv_sem=recv2.at[3 * t + kk],
                    device_id=(x, y, 1 - c), device_id_type=MESH)
                fwd.start()
                passed.append(fwd)
        for t in range(n):
            for kk, (cx, cy) in enumerate(chips):
                other = half(t, 2 * cx + cy, 1 - c)
                pltpu.make_async_remote_copy(
                    src_ref=other, dst_ref=other, send_sem=send2.at[3 * t + kk], recv_sem=recv2.at[3 * t + kk],
                    device_id=(x, y, 1 - c), device_id_type=MESH).wait_recv()
        for cp in first + passed:
            cp.wait_send()
        for cp in local:
            cp.wait()

    return pl.pallas_call(
        body, name="weights_all_gather",
        out_shape=[jax.ShapeDtypeStruct((N_CHIPS,) + s.shape, s.dtype) for s in shards],
        in_specs=[HBM_SPEC] * n, out_specs=[HBM_SPEC] * n,
        scratch_shapes=[pltpu.SemaphoreType.DMA((3 * n,)), pltpu.SemaphoreType.DMA((3 * n,)),
                        pltpu.SemaphoreType.DMA((3 * n,)), pltpu.SemaphoreType.DMA((3 * n,)),
                        pltpu.SemaphoreType.DMA((n,))],
    )(*shards)


def _sibling_swap_halves(g):
    _, M, C = g.shape
    hr = M // 2

    def body(g_ref, r_ref, ssem, rsem):
        x, y, c = _mesh_pos()
        cps = [pltpu.make_async_remote_copy(
            src_ref=g_ref.at[s, pl.ds((1 - c) * hr, hr), :], dst_ref=r_ref.at[s],
            send_sem=ssem.at[s], recv_sem=rsem.at[s], device_id=(x, y, 1 - c), device_id_type=MESH)
            for s in range(N_CHIPS)]
        for cp in cps:
            cp.start()
        for cp in cps:
            cp.wait()

    return pl.pallas_call(
        body, name="grads_sibling_swap", out_shape=jax.ShapeDtypeStruct((N_CHIPS, hr, C), F32),
        in_specs=[HBM_SPEC], out_specs=HBM_SPEC,
        scratch_shapes=[pltpu.SemaphoreType.DMA((N_CHIPS,)), pltpu.SemaphoreType.DMA((N_CHIPS,))],
    )(g)


def _pair_add(g, r, c_idx):
    _, M, C = g.shape
    hr = M // 2
    tr = PACK_ROW_TILE
    nbk = hr // tr

    def body(c_ref, g_ref, r_ref, o_ref):
        o_ref[...] = (g_ref[...] + r_ref[...]).astype(BF16)

    grid_spec = pltpu.PrefetchScalarGridSpec(
        num_scalar_prefetch=1, grid=(N_CHIPS, nbk),
        in_specs=[pl.BlockSpec((None, tr, C), lambda s, i, c: (s, c[0] * nbk + i, 0)),
                  pl.BlockSpec((None, tr, C), lambda s, i, c: (s, i, 0))],
        out_specs=pl.BlockSpec((None, tr, C), lambda s, i, c: (s, i, 0)),
    )
    return pl.pallas_call(body, name="grads_pair_add", grid_spec=grid_spec,
                          out_shape=jax.ShapeDtypeStruct((N_CHIPS, hr, C), BF16),
                          compiler_params=_params(2))(c_idx, g, r)


def _chip_scatter(pp):
    def body(p_ref, r_ref, ssem, rsem, lsem):
        x, y, c = _mesh_pos()
        me = 2 * x + y
        chips = [(1 - x, y), (x, 1 - y), (1 - x, 1 - y)]
        own = pltpu.make_async_copy(p_ref.at[me], r_ref.at[me], lsem)
        own.start()
        cps = [pltpu.make_async_remote_copy(
            src_ref=p_ref.at[2 * cx + cy], dst_ref=r_ref.at[me], send_sem=ssem.at[kk], recv_sem=rsem.at[kk],
            device_id=(cx, cy, c), device_id_type=MESH) for kk, (cx, cy) in enumerate(chips)]
        for cp in cps:
            cp.start()
        for kk, (cx, cy) in enumerate(chips):
            got = r_ref.at[2 * cx + cy]
            pltpu.make_async_remote_copy(
                src_ref=got, dst_ref=got, send_sem=ssem.at[kk], recv_sem=rsem.at[kk],
                device_id=(cx, cy, c), device_id_type=MESH).wait_recv()
        for cp in cps:
            cp.wait_send()
        own.wait()

    return pl.pallas_call(
        body, name="grads_chip_scatter", out_shape=jax.ShapeDtypeStruct(pp.shape, pp.dtype),
        in_specs=[HBM_SPEC], out_specs=HBM_SPEC,
        scratch_shapes=[pltpu.SemaphoreType.DMA((3,)), pltpu.SemaphoreType.DMA((3,)), pltpu.SemaphoreType.DMA],
    )(pp)


def _chip_sum(r, c_idx):
    _, hr, C = r.shape
    tr = PACK_ROW_TILE
    nbk = hr // tr

    def body(c_ref, r_ref, o_ref):
        r0, r1, r2, r3 = (r_ref[s].astype(F32) for s in range(N_CHIPS))
        o_ref[...] = ((r0 + r1) + r2) + r3

    grid_spec = pltpu.PrefetchScalarGridSpec(
        num_scalar_prefetch=1, grid=(nbk,),
        in_specs=[pl.BlockSpec((N_CHIPS, tr, C), lambda i, c: (0, i, 0))],
        out_specs=pl.BlockSpec((tr, C), lambda i, c: (c[0] * nbk + i, 0)),
    )
    return pl.pallas_call(body, name="grads_chip_sum", grid_spec=grid_spec,
                          out_shape=jax.ShapeDtypeStruct((2 * hr, C), F32), compiler_params=_params())(c_idx, r)


def _sibling_join(buf):
    hr = buf.shape[0] // 2

    def body(in_ref, o_ref, ssem, rsem):
        del in_ref
        x, y, c = _mesh_pos()
        mine = o_ref.at[pl.ds(c * hr, hr), :]
        cp = pltpu.make_async_remote_copy(src_ref=mine, dst_ref=mine, send_sem=ssem, recv_sem=rsem,
                                          device_id=(x, y, 1 - c), device_id_type=MESH)
        cp.start()
        theirs = o_ref.at[pl.ds((1 - c) * hr, hr), :]
        pltpu.make_async_remote_copy(src_ref=theirs, dst_ref=theirs, send_sem=ssem, recv_sem=rsem,
                                     device_id=(x, y, 1 - c), device_id_type=MESH).wait_recv()
        cp.wait_send()

    return pl.pallas_call(
        body, name="grads_sibling_join", out_shape=jax.ShapeDtypeStruct(buf.shape, F32),
        in_specs=[HBM_SPEC], out_specs=HBM_SPEC, input_output_aliases={0: 0},
        scratch_shapes=[pltpu.SemaphoreType.DMA, pltpu.SemaphoreType.DMA],
    )(buf)


def _adamw(w, g, m, v):
    M, C = w.shape
    tr = _tile(M, PACK_ROW_TILE)

    def body(w_ref, g_ref, m_ref, v_ref, d_ref, mo_ref, vo_ref):
        gv = g_ref[...]
        mn = ADAM_B1 * m_ref[...] + (1.0 - ADAM_B1) * gv
        vn = ADAM_B2 * v_ref[...] + (1.0 - ADAM_B2) * (gv * gv)
        m_hat = mn / (1.0 - ADAM_B1 ** ADAM_STEP)
        v_hat = vn / (1.0 - ADAM_B2 ** ADAM_STEP)
        d_ref[...] = -ADAM_LR * (m_hat / (jnp.sqrt(v_hat) + ADAM_EPS) + ADAM_WD * w_ref[...])
        mo_ref[...] = mn
        vo_ref[...] = vn

    spec = pl.BlockSpec((tr, C), lambda i: (i, 0))
    return pl.pallas_call(
        body, name="adamw", grid=(M // tr,), in_specs=[spec] * 4, out_specs=[spec] * 3,
        out_shape=[jax.ShapeDtypeStruct((M, C), F32)] * 3, compiler_params=_params())(w, g, m, v)


WEIGHT_NAMES = ("norm_g", "w_attn_in", "b_forget", "w_attn_out", "w_conv_in", "conv_w", "w_conv_out",
                "w_mlp_up", "w_mlp_down", "w_ple_proj", "w_ple_gate")
COL_SHARDED = ("norm_g", "w_attn_in", "w_conv_in", "conv_w", "w_mlp_up", "w_ple_proj")
ROW_SHARDED = ("w_attn_out", "w_conv_out", "w_mlp_down", "w_ple_gate")


def _unshard(name, gathered, shard_shape):
    a = gathered.reshape((N_CHIPS,) + tuple(shard_shape))
    if name in COL_SHARDED:
        a = jnp.moveaxis(a, 0, -2)
        return a.reshape(a.shape[:-2] + (N_CHIPS * shard_shape[-1],))
    a = jnp.moveaxis(a, 0, 1)
    return a.reshape((shard_shape[0], N_CHIPS * shard_shape[1], shard_shape[2]))


def _to_shard_major(name, full):
    if name == "b_forget":
        return jnp.broadcast_to(full.reshape(1, -1), (N_CHIPS, full.size))
    if name in COL_SHARDED:
        a = full.reshape(full.shape[:-1] + (N_CHIPS, full.shape[-1] // N_CHIPS))
        a = jnp.moveaxis(a, -2, 0)
    else:
        a = full.reshape((full.shape[0], N_CHIPS, full.shape[1] // N_CHIPS, full.shape[2]))
        a = jnp.moveaxis(a, 1, 0)
    return a.reshape(N_CHIPS, -1)


def _pack_layout(shard_shapes):
    offs, row = {}, 0
    for name in WEIGHT_NAMES:
        n = int(np.prod(shard_shapes[name]))
        rows = -(-n // PACK_COLS)
        offs[name] = (row, rows, n)
        row += rows
    total = -(-row // PACK_ROW_ALIGN) * PACK_ROW_ALIGN
    return offs, total


def _pack(flat_by_name, offs, total_rows, lead=()):
    parts, row = [], 0
    for name in WEIGHT_NAMES:
        _, rows, n = offs[name]
        a = flat_by_name[name].astype(F32)
        pad = rows * PACK_COLS - n
        if pad:
            a = jnp.pad(a, [(0, 0)] * len(lead) + [(0, pad)])
        parts.append(a.reshape(tuple(lead) + (rows, PACK_COLS)))
        row += rows
    if total_rows > row:
        parts.append(jnp.zeros(tuple(lead) + (total_rows - row, PACK_COLS), F32))
    return jnp.concatenate(parts, axis=len(lead))


def _unpack(packed, offs, shard_shapes):
    out = {}
    for name in WEIGHT_NAMES:
        row, rows, n = offs[name]
        out[name] = packed[row:row + rows].reshape(-1)[:n].reshape(shard_shapes[name])
    return out


def kernel(x, p, norm_g, w_attn_in, b_forget, w_attn_out, w_conv_in, conv_w, w_conv_out, w_mlp_up, w_mlp_down, w_ple_proj, w_ple_gate, loss_target, m_norm_g, m_w_attn_in, m_b_forget, m_w_attn_out, m_w_conv_in, m_conv_w, m_w_conv_out, m_w_mlp_up, m_w_mlp_down, m_w_ple_proj, m_w_ple_gate, v_norm_g, v_w_attn_in, v_b_forget, v_w_attn_out, v_w_conv_in, v_conv_w, v_w_conv_out, v_w_mlp_up, v_w_mlp_down, v_w_ple_proj, v_w_ple_gate):
    w_local = dict(norm_g=norm_g, w_attn_in=w_attn_in, b_forget=b_forget, w_attn_out=w_attn_out,
                   w_conv_in=w_conv_in, conv_w=conv_w, w_conv_out=w_conv_out, w_mlp_up=w_mlp_up,
                   w_mlp_down=w_mlp_down, w_ple_proj=w_ple_proj, w_ple_gate=w_ple_gate)
    m_local = dict(norm_g=m_norm_g, w_attn_in=m_w_attn_in, b_forget=m_b_forget, w_attn_out=m_w_attn_out,
                   w_conv_in=m_w_conv_in, conv_w=m_conv_w, w_conv_out=m_w_conv_out, w_mlp_up=m_w_mlp_up,
                   w_mlp_down=m_w_mlp_down, w_ple_proj=m_w_ple_proj, w_ple_gate=m_w_ple_gate)
    v_local = dict(norm_g=v_norm_g, w_attn_in=v_w_attn_in, b_forget=v_b_forget, w_attn_out=v_w_attn_out,
                   w_conv_in=v_w_conv_in, conv_w=v_conv_w, w_conv_out=v_w_conv_out, w_mlp_up=v_w_mlp_up,
                   w_mlp_down=v_w_mlp_down, w_ple_proj=v_w_ple_proj, w_ple_gate=v_w_ple_gate)
    shard_shapes = {k: tuple(a.shape) for k, a in w_local.items()}

    xs = x[0]
    target = loss_target[0]
    T, D = xs.shape
    depth = p.shape[0]
    H = b_forget.shape[1]
    qscale = float(D // H) ** -0.5
    head_sel = (jnp.arange(D)[None, :] // (D // H) == jnp.arange(H)[:, None]).astype(F32)

    big = [n for n in WEIGHT_NAMES if n not in ("norm_g", "conv_w", "b_forget")]
    small = jnp.concatenate([norm_g.reshape(-1), conv_w.reshape(-1)])
    n_small = small.shape[0]
    small_rows = -(-n_small // (LANES * 16)) * 16
    small = jnp.pad(small, (0, small_rows * LANES - n_small)).reshape(small_rows, LANES)
    shards = [w_local[n].astype(BF16).reshape(-1, shard_shapes[n][-1]) for n in big] + [small]
    gathered = _all_gather_shards(shards)
    full = {n: _unshard(n, g, shard_shapes[n]) for n, g in zip(big, gathered[:-1])}
    gs = gathered[-1].reshape(N_CHIPS, -1)
    full["norm_g"] = _unshard("norm_g", gs[:, :norm_g.size], shard_shapes["norm_g"])
    full["conv_w"] = _unshard("conv_w", gs[:, norm_g.size:n_small], shard_shapes["conv_w"])
    gains = full["norm_g"]

    def gain(i, k):
        return gains[i, k].reshape(1, D)

    def taps(j):
        return jnp.pad(full["conv_w"][j], ((0, 5), (0, 0)))

    saved = []
    h = xs
    for i in range(depth):
        j = i // 2
        s = {"x0": h}
        if i % 2 == 0:
            w_in = full["w_attn_in"][j]
            wqkv = w_in[:, :3 * D]
            wf = jnp.pad(w_in[:, 3 * D:], ((0, 0), (0, LANES - H)))
            bf = jnp.pad(b_forget[j].reshape(1, H), ((0, 0), (0, LANES - H)))
            q, k, v, fl, lf = _attn_in_fwd(h, gain(i, 0), wqkv, wf, bf, qscale)
            qaug, kaug = _aug_operands(_gate_cumsum(lf[:, :H].T), H, D)
            o, lse_a, lse_b = _flash_fwd(q, k, v, qaug, kaug, H)
            s.update(q=q, k=k, v=v, fl=fl, qaug=qaug, kaug=kaug, o=o, lse_a=lse_a, lse_b=lse_b, wqkv=wqkv, wf=wf)
            mix_in, w_out = o, full["w_attn_out"][j]
        else:
            b, c, u, zc, y = _conv_in_fwd(h, gain(i, 0), full["w_conv_in"][j], taps(j))
            s.update(b=b, c=c, u=u, zc=zc, y=y)
            mix_in, w_out = y, full["w_conv_out"][j]
        m1, x1 = _out_proj_fwd("mixer_out_fwd", mix_in, w_out, gain(i, 1), h)
        up, a = _mlp_up_fwd(x1, gain(i, 2), full["w_mlp_up"][i])
        m3, x2 = _out_proj_fwd("mlp_down_fwd", a, full["w_mlp_down"][i], gain(i, 3), x1)
        gl, pe, x3 = _ple_fwd(x2, p[i, 0], gain(i, 4), gain(i, 5), full["w_ple_gate"][i], full["w_ple_proj"][i])
        s.update(m1=m1, x1=x1, up=up, a=a, m3=m3, x2=x2, gl=gl, pe=pe, w_out=w_out)
        saved.append(s)
        h = x3

    dh, loss_blk = _loss_fwd_bwd(h, target)
    loss = lax.psum(loss_blk[0, 0], ("x", "y", "c"))

    g_gain = [[None] * 6 for _ in range(depth)]
    grads = {n: [None] * w_local[n].shape[0] for n in WEIGHT_NAMES if n != "norm_g"}
    for i in reversed(range(depth)):
        j = i // 2
        s = saved[i]
        dx2, dwp, dwg, g_gain[i][4], g_gain[i][5] = _ple_bwd(
            dh, s["x2"], p[i, 0], s["gl"], s["pe"], gain(i, 4), gain(i, 5), full["w_ple_gate"][i].T)
        grads["w_ple_proj"][i], grads["w_ple_gate"][i] = dwp, dwg
        dup, dwd, g_gain[i][3] = _out_proj_bwd(
            "mlp_down_bwd", dx2, s["m3"], gain(i, 3), [s["a"], s["up"]], full["w_mlp_down"][i].T, "relu2")
        grads["w_mlp_down"][i] = dwd
        dx1, dwu, g_gain[i][2] = _in_proj_bwd(
            "mlp_up_bwd", [dup], [full["w_mlp_up"][i].T], s["x1"], gain(i, 2), dx2)
        grads["w_mlp_up"][i] = dwu
        if i % 2 == 0:
            do, dwo, g_gain[i][1] = _out_proj_bwd(
                "attn_out_bwd", dx1, s["m1"], gain(i, 1), [s["o"]], s["w_out"].T, "plain")
            grads["w_attn_out"][j] = dwo
            delta = _attn_delta(do, s["o"], head_sel).reshape(H, 1, T)
            dqt, dkt, dvt, dca, dcb, dra, drb = _flash_bwd(s["q"], s["k"], s["v"], s["qaug"], s["kaug"], do,
                                                           s["lse_a"], s["lse_b"], delta, qscale)
            dcol = jnp.concatenate([dca, dcb], axis=1).reshape(H, T)
            drow = jnp.concatenate([dra, drb], axis=1).reshape(H, T)
            dfl_t, dbf = _gate_bwd(drow, dcol, s["fl"][:, :H].T)
            grads["b_forget"][j] = dbf.reshape(H)
            dfl = jnp.pad(dfl_t.T, ((0, 0), (0, LANES - H))).astype(BF16)
            wqkv_t = s["wqkv"].T
            dh, dwq, dwk, dwv, dwf, g_gain[i][0] = _in_proj_bwd(
                "attn_in_bwd", [dqt.T, dkt.T, dvt.T, dfl], [wqkv_t[:D], wqkv_t[D:2 * D], wqkv_t[2 * D:], s["wf"].T],
                s["x0"], gain(i, 0), dx1)
            grads["w_attn_in"][j] = jnp.concatenate([dwq, dwk, dwv, dwf[:, :H]], axis=1)
        else:
            dy, dwo, g_gain[i][1] = _out_proj_bwd(
                "conv_out_bwd", dx1, s["m1"], gain(i, 1), [s["y"]], s["w_out"].T, "plain_f32")
            grads["w_conv_out"][j] = dwo
            db, dc, du, dcw = _conv_bwd(dy, s["b"], s["c"], s["u"], s["zc"], taps(j))
            grads["conv_w"][j] = dcw[:conv_w.shape[1]]
            w_t = full["w_conv_in"][j].T
            dh, dwb, dwc, dwu2, g_gain[i][0] = _in_proj_bwd(
                "conv_in_bwd", [db, dc, du], [w_t[:D], w_t[D:2 * D], w_t[2 * D:]], s["x0"], gain(i, 0), dx1)
            grads["w_conv_in"][j] = jnp.concatenate([dwb, dwc, dwu2], axis=1)
    grad_x = dh.reshape(x.shape)

    grad_full = {n: jnp.stack(grads[n]) for n in grads}
    grad_full["norm_g"] = jnp.stack([jnp.concatenate(row, axis=0) for row in g_gain])

    offs, total_rows = _pack_layout(shard_shapes)
    g_packed = _pack({n: _to_shard_major(n, grad_full[n]) for n in WEIGHT_NAMES}, offs, total_rows, lead=(N_CHIPS,))
    c_idx = lax.axis_index("c").astype(jnp.int32).reshape(1)
    pair = _pair_add(g_packed, _sibling_swap_halves(g_packed), c_idx)
    g_red = _sibling_join(_chip_sum(_chip_scatter(pair), c_idx))
    g_out = _unpack(g_red, offs, shard_shapes)
    d_out, m_out, v_out = {}, {}, {}
    for n in WEIGHT_NAMES:
        shp = shard_shapes[n]
        two_d = (-1, shp[-1])
        d, mn, vn = _adamw(w_local[n].reshape(two_d), g_out[n].reshape(two_d),
                           m_local[n].reshape(two_d), v_local[n].reshape(two_d))
        d_out[n], m_out[n], v_out[n] = d.reshape(shp), mn.reshape(shp), vn.reshape(shp)
    return (loss, grad_x, *[g_out[n] for n in WEIGHT_NAMES], *[d_out[n] for n in WEIGHT_NAMES],
            *[m_out[n] for n in WEIGHT_NAMES], *[v_out[n] for n in WEIGHT_NAMES])
```

```python
import numpy as np
import jax
import jax.numpy as jnp
from jax import lax
from jax.experimental import pallas as pl
from jax.experimental.pallas import tpu as pltpu

F32 = jnp.float32
BF16 = jnp.bfloat16
MESH = pl.DeviceIdType.MESH

RMS_EPS = 1e-6
NEG_INF = -1e30
ADAM_LR = 0.001
ADAM_B1 = 0.9
ADAM_B2 = 0.999
ADAM_EPS = 1e-08
ADAM_WD = 0.01
ADAM_STEP = 10

N_CHIPS = 4
LANES = 128
ROW_TILE_FWD = 512
ROW_TILE_BWD = 256
ATTN_Q_TILE = 2048
ATTN_K_TILE = 512
ATTN_KEY_STRIP = 128
ATTN_QUERY_STRIP = 256
COL_CHUNK = 512
PACK_COLS = 1024
PACK_ROW_TILE = 256
PACK_ROW_ALIGN = 2 * PACK_ROW_TILE
N_AUG = 3
VMEM_LIMIT = 56 * 1024 * 1024


def _tile(n, pref):
    return pref if n % pref == 0 else n


def _dot(a, b):
    return jnp.dot(a, b, preferred_element_type=F32)


def _dot_tn(a, b):
    return lax.dot_general(a, b, (((0,), (0,)), ((), ())), preferred_element_type=F32)


def _dot_nt(a, b):
    return lax.dot_general(a, b, (((1,), (1,)), ((), ())), preferred_element_type=F32)


def _rms_fwd(x, g):
    r = lax.rsqrt(jnp.mean(x * x, axis=-1, keepdims=True) + RMS_EPS)
    return (x * r) * g


def _rms_bwd(x, g, dy):
    r = lax.rsqrt(jnp.mean(x * x, axis=-1, keepdims=True) + RMS_EPS)
    xh = x * r
    dyg = dy * g
    dx = r * (dyg - xh * jnp.mean(dyg * xh, axis=-1, keepdims=True))
    return dx, jnp.sum(dy * xh, axis=0, keepdims=True)


def _params(n_axes=1):
    return pltpu.CompilerParams(dimension_semantics=("arbitrary",) * n_axes, vmem_limit_bytes=VMEM_LIMIT)


def _row_call(name, body, n_rows, tm, row_ins, const_ins, row_outs, acc_outs=(), scratch=(), reverse=False):
    nb = n_rows // tm
    rmap = (lambda i: (nb - 1 - i, 0)) if reverse else (lambda i: (i, 0))

    def whole(shape):
        nd = len(shape)
        return pl.BlockSpec(tuple(shape), lambda i: (0,) * nd)

    in_specs = [pl.BlockSpec((tm, a.shape[1]), rmap) for a in row_ins] + [whole(a.shape) for a in const_ins]
    out_shape = [jax.ShapeDtypeStruct((n_rows, w), dt) for (w, dt) in row_outs]
    out_shape += [jax.ShapeDtypeStruct(tuple(s), dt) for (s, dt) in acc_outs]
    out_specs = [pl.BlockSpec((tm, w), rmap) for (w, _) in row_outs] + [whole(s) for (s, _) in acc_outs]
    return pl.pallas_call(
        body, name=name, grid=(nb,), in_specs=in_specs, out_specs=out_specs, out_shape=out_shape,
        scratch_shapes=list(scratch), compiler_params=_params(),
    )(*row_ins, *const_ins)


def _attn_in_fwd(x, g, wqkv, wf, bf, qscale):
    T, D = x.shape
    tm, ch = _tile(T, ROW_TILE_FWD), _tile(D, COL_CHUNK)

    def body(x_ref, g_ref, w_ref, wf_ref, bf_ref, q_ref, k_ref, v_ref, fl_ref, lf_ref):
        h = _rms_fwd(x_ref[...], g_ref[...]).astype(BF16)
        for part, o_ref in enumerate((q_ref, k_ref, v_ref)):
            for n0 in range(0, D, ch):
                r = _dot(h, w_ref[:, part * D + n0:part * D + n0 + ch])
                if part == 0:
                    r = r * qscale
                o_ref[:, n0:n0 + ch] = r.astype(BF16)
        fl = _dot(h, wf_ref[...]) + bf_ref[...]
        fl_ref[...] = fl
        lf_ref[...] = jnp.minimum(fl, 0.0) - jnp.log1p(jnp.exp(-jnp.abs(fl)))

    return _row_call("attn_in_fwd", body, T, tm, [x], [g, wqkv, wf, bf],
                     [(D, BF16), (D, BF16), (D, BF16), (LANES, F32), (LANES, F32)])


def _conv_in_fwd(x, g, w, cw):
    T, D = x.shape
    tm, ch = _tile(T, ROW_TILE_FWD), _tile(D, COL_CHUNK)

    def body(x_ref, g_ref, w_ref, cw_ref, b_ref, c_ref, u_ref, zc_ref, y_ref, tail_ref):
        i = pl.program_id(0)

        @pl.when(i == 0)
        def _():
            tail_ref[...] = jnp.zeros_like(tail_ref)

        h = _rms_fwd(x_ref[...], g_ref[...]).astype(BF16)
        for part, o_ref in enumerate((b_ref, c_ref, u_ref)):
            for n0 in range(0, D, ch):
                o_ref[:, n0:n0 + ch] = _dot(h, w_ref[:, part * D + n0:part * D + n0 + ch])
        z = c_ref[...] * u_ref[...]
        row = lax.broadcasted_iota(jnp.int32, (tm, 1), 0)
        t6, t7 = tail_ref[6:7, :], tail_ref[7:8, :]
        z1 = jnp.where(row == 0, t7, pltpu.roll(z, 1, axis=0))
        z2 = jnp.where(row == 0, t6, jnp.where(row == 1, t7, pltpu.roll(z, 2, axis=0)))
        zc = cw_ref[0:1, :] * z2 + cw_ref[1:2, :] * z1 + cw_ref[2:3, :] * z
        zc_ref[...] = zc
        y_ref[...] = (b_ref[...] * zc).astype(BF16)
        tail_ref[...] = z[tm - 8:tm, :]

    return _row_call("conv_in_fwd", body, T, tm, [x], [g, w, cw],
                     [(D, F32), (D, F32), (D, F32), (D, F32), (D, BF16)], scratch=[pltpu.VMEM((8, D), F32)])


def _mlp_up_fwd(x, g, w):
    T, D = x.shape
    F = w.shape[1]
    tm, ch = _tile(T, ROW_TILE_FWD), _tile(F, COL_CHUNK)

    def body(x_ref, g_ref, w_ref, up_ref, a_ref):
        h = _rms_fwd(x_ref[...], g_ref[...]).astype(BF16)
        for n0 in range(0, F, ch):
            r = _dot(h, w_ref[:, n0:n0 + ch])
            up_ref[:, n0:n0 + ch] = r.astype(BF16)
            rl = jnp.maximum(r, 0.0)
            a_ref[:, n0:n0 + ch] = (rl * rl).astype(BF16)

    return _row_call("mlp_up_fwd", body, T, tm, [x], [g, w], [(F, BF16), (F, BF16)])


def _out_proj_fwd(name, a, w, g, x):
    T, D = x.shape
    tm = _tile(T, ROW_TILE_FWD)

    def body(a_ref, x_ref, w_ref, g_ref, m_ref, xn_ref):
        m = _dot(a_ref[...], w_ref[...])
        m_ref[...] = m
        xn_ref[...] = x_ref[...] + _rms_fwd(m, g_ref[...])

    return _row_call(name, body, T, tm, [a, x], [w, g], [(D, F32), (D, F32)])


def _ple_fwd(x, p, g4, g5, wg, wp):
    T, D = x.shape
    tm = _tile(T, ROW_TILE_FWD)

    def body(x_ref, p_ref, g4_ref, g5_ref, wg_ref, wp_ref, gl_ref, pe_ref, xn_ref):
        xv = x_ref[...]
        gl = _dot(_rms_fwd(xv, g4_ref[...]).astype(BF16), wg_ref[...])
        pe = _dot(p_ref[...].astype(BF16), wp_ref[...])
        gl_ref[...] = gl
        pe_ref[...] = pe
        e = pe * (1.0 / (1.0 + jnp.exp(-gl)))
        xn_ref[...] = xv + _rms_fwd(e, g5_ref[...])

    return _row_call("ple_fwd", body, T, tm, [x, p], [g4, g5, wg, wp], [(D, F32), (D, F32), (D, F32)])


def _loss_fwd_bwd(y, target):
    T, D = y.shape
    tm = _tile(T, ROW_TILE_FWD)

    def body(y_ref, t_ref, dy_ref, loss_ref):
        @pl.when(pl.program_id(0) == 0)
        def _():
            loss_ref[...] = jnp.zeros_like(loss_ref)

        err = y_ref[...] - t_ref[...]
        dy_ref[...] = err * (1.0 / D)
        part = 0.5 * jnp.sum(jnp.mean(err * err, axis=-1, keepdims=True), axis=0, keepdims=True)
        loss_ref[...] += jnp.broadcast_to(part, loss_ref.shape)

    return _row_call("loss", body, T, tm, [y, target], [], [(D, F32)], acc_outs=[((8, LANES), F32)])


def _out_proj_bwd(name, dres, m, g, a_ins, wt, mode):
    T, D = dres.shape
    Ka = wt.shape[1]
    tm, ch = _tile(T, ROW_TILE_BWD), _tile(Ka, COL_CHUNK)
    out_dt = F32 if mode == "plain_f32" else BF16

    def body(dres_ref, m_ref, *rest):
        a_ref = rest[0]
        up_ref = rest[1] if mode == "relu2" else None
        k = len(a_ins)
        g_ref, wt_ref, da_ref, dw_ref, dg_ref = rest[k:k + 5]

        @pl.when(pl.program_id(0) == 0)
        def _():
            dw_ref[...] = jnp.zeros_like(dw_ref)
            dg_ref[...] = jnp.zeros_like(dg_ref)

        dm, dgp = _rms_bwd(m_ref[...], g_ref[...], dres_ref[...])
        dg_ref[...] += dgp
        dmb = dm.astype(BF16)
        for n0 in range(0, Ka, ch):
            da = _dot(dmb, wt_ref[:, n0:n0 + ch])
            if mode == "relu2":
                da = da * (2.0 * jnp.maximum(up_ref[:, n0:n0 + ch].astype(F32), 0.0))
            da_ref[:, n0:n0 + ch] = da.astype(out_dt)
            dw_ref[n0:n0 + ch, :] += _dot_tn(a_ref[:, n0:n0 + ch], dmb)

    return _row_call(name, body, T, tm, [dres, m] + list(a_ins), [g, wt],
                     [(Ka, out_dt)], acc_outs=[((Ka, D), F32), ((1, D), F32)])


def _in_proj_bwd(name, pieces, wts, x, g, dres):
    T, D = x.shape
    tm = _tile(T, ROW_TILE_BWD)
    k = len(pieces)
    widths = [pc.shape[1] for pc in pieces]

    def body(*refs):
        pc_refs = refs[:k]
        x_ref, dres_ref, g_ref = refs[k:k + 3]
        wt_refs = refs[k + 3:2 * k + 3]
        dx_ref = refs[2 * k + 3]
        dw_refs = refs[2 * k + 4:3 * k + 4]
        dg_ref = refs[3 * k + 4]

        @pl.when(pl.program_id(0) == 0)
        def _():
            for r in dw_refs:
                r[...] = jnp.zeros_like(r)
            dg_ref[...] = jnp.zeros_like(dg_ref)

        xv, gv = x_ref[...], g_ref[...]
        hb = _rms_fwd(xv, gv).astype(BF16)
        dh = None
        for pc_ref, wt_ref, dw_ref, n in zip(pc_refs, wt_refs, dw_refs, widths):
            d = _dot(pc_ref[...], wt_ref[...])
            dh = d if dh is None else dh + d
            ch = _tile(n, COL_CHUNK)
            for n0 in range(0, n, ch):
                dw_ref[:, n0:n0 + ch] += _dot_tn(hb, pc_ref[:, n0:n0 + ch])
        dxn, dgp = _rms_bwd(xv, gv, dh)
        dg_ref[...] += dgp
        dx_ref[...] = dres_ref[...] + dxn

    return _row_call(name, body, T, tm, list(pieces) + [x, dres], [g] + list(wts), [(D, F32)],
                     acc_outs=[((D, n), F32) for n in widths] + [((1, D), F32)])


def _conv_bwd(dy, b, c, u, zc, cw):
    T, D = dy.shape
    tm = _tile(T, ROW_TILE_BWD)

    def body(dy_ref, b_ref, c_ref, u_ref, zc_ref, cw_ref, db_ref, dc_ref, du_ref, dcw_ref, head_ref):
        @pl.when(pl.program_id(0) == 0)
        def _():
            head_ref[...] = jnp.zeros_like(head_ref)
            dcw_ref[...] = jnp.zeros_like(dcw_ref)

        dyv, cv, uv = dy_ref[...], c_ref[...], u_ref[...]
        db_ref[...] = (dyv * zc_ref[...]).astype(BF16)
        dzc = dyv * b_ref[...]
        row = lax.broadcasted_iota(jnp.int32, (tm, 1), 0)
        h0, h1 = head_ref[0:1, :], head_ref[1:2, :]
        d1 = jnp.where(row == tm - 1, h0, pltpu.roll(dzc, tm - 1, axis=0))
        d2 = jnp.where(row == tm - 1, h1, jnp.where(row == tm - 2, h0, pltpu.roll(dzc, tm - 2, axis=0)))
        dz = cw_ref[2:3, :] * dzc + cw_ref[1:2, :] * d1 + cw_ref[0:1, :] * d2
        dc_ref[...] = (dz * uv).astype(BF16)
        du_ref[...] = (dz * cv).astype(BF16)
        z = cv * uv
        dcw_ref[0:1, :] += jnp.sum(d2 * z, axis=0, keepdims=True)
        dcw_ref[1:2, :] += jnp.sum(d1 * z, axis=0, keepdims=True)
        dcw_ref[2:3, :] += jnp.sum(dzc * z, axis=0, keepdims=True)
        head_ref[...] = dzc[0:8, :]

    return _row_call("conv_bwd", body, T, tm, [dy, b, c, u, zc], [cw], [(D, BF16), (D, BF16), (D, BF16)],
                     acc_outs=[((8, D), F32)], scratch=[pltpu.VMEM((8, D), F32)], reverse=True)


def _ple_bwd(dres, x, p, gl, pe, g4, g5, wgt):
    T, D = x.shape
    P = p.shape[1]
    tm = _tile(T, ROW_TILE_BWD)

    def body(dres_ref, x_ref, p_ref, gl_ref, pe_ref, g4_ref, g5_ref, wgt_ref,
             dx_ref, dwp_ref, dwg_ref, dg4_ref, dg5_ref):
        @pl.when(pl.program_id(0) == 0)
        def _():
            for r in (dwp_ref, dwg_ref, dg4_ref, dg5_ref):
                r[...] = jnp.zeros_like(r)

        dr, xv, pe_v = dres_ref[...], x_ref[...], pe_ref[...]
        gate = 1.0 / (1.0 + jnp.exp(-gl_ref[...]))
        de, dg5p = _rms_bwd(pe_v * gate, g5_ref[...], dr)
        dg5_ref[...] += dg5p
        dpe = (de * gate).astype(BF16)
        dgl = (de * pe_v * gate * (1.0 - gate)).astype(BF16)
        dwp_ref[...] += _dot_tn(p_ref[...].astype(BF16), dpe)
        g4v = g4_ref[...]
        dwg_ref[...] += _dot_tn(_rms_fwd(xv, g4v).astype(BF16), dgl)
        dxn, dg4p = _rms_bwd(xv, g4v, _dot(dgl, wgt_ref[...]))
        dg4_ref[...] += dg4p
        dx_ref[...] = dr + dxn

    return _row_call("ple_bwd", body, T, tm, [dres, x, p, gl, pe], [g4, g5, wgt], [(D, F32)],
                     acc_outs=[((P, D), F32), ((D, D), F32), ((1, D), F32), ((1, D), F32)])


def _scan_lanes(v, reverse):
    n = v.shape[1]
    lane = lax.broadcasted_iota(jnp.int32, v.shape, 1)
    s = 1
    while s < n:
        if reverse:
            v = v + jnp.where(lane < n - s, pltpu.roll(v, n - s, axis=1), 0.0)
        else:
            v = v + jnp.where(lane >= s, pltpu.roll(v, s, axis=1), 0.0)
        s *= 2
    return v


def _gate_cumsum(lf_t):
    def body(lf_ref, *piece_refs):
        rest = _scan_lanes(lf_ref[...], reverse=False)
        for r in piece_refs:
            piece = rest.astype(BF16)
            r[...] = piece
            rest = rest - piece.astype(F32)

    return pl.pallas_call(body, name="gate_cumsum", out_shape=[jax.ShapeDtypeStruct(lf_t.shape, BF16)] * N_AUG,
                          compiler_params=pltpu.CompilerParams(vmem_limit_bytes=VMEM_LIMIT))(lf_t)


def _gate_bwd(drow_t, dcol_t, fl_t):
    H = fl_t.shape[0]

    def body(dr_ref, dc_ref, fl_ref, dfl_ref, dbf_ref):
        dlf = _scan_lanes(dr_ref[...] - dc_ref[...], reverse=True)
        dfl = dlf * (1.0 / (1.0 + jnp.exp(fl_ref[...])))
        dfl_ref[...] = dfl
        dbf_ref[...] = jnp.sum(dfl, axis=1, keepdims=True)

    return pl.pallas_call(
        body, name="gate_bwd",
        out_shape=(jax.ShapeDtypeStruct(fl_t.shape, F32), jax.ShapeDtypeStruct((H, 1), F32)),
        compiler_params=pltpu.CompilerParams(vmem_limit_bytes=VMEM_LIMIT))(drow_t, dcol_t, fl_t)


def _aug_operands(pieces, H, D):
    T = pieces[0].shape[1]
    dh = D // H
    one = jnp.ones((H, T), BF16)
    qa = jnp.stack(list(pieces) + [one] * N_AUG, axis=-1)
    ka = jnp.stack([one] * N_AUG + [-pc for pc in pieces], axis=-1)

    def place(a):
        a = jnp.pad(a, ((0, 0), (0, 0), (0, dh - 2 * N_AUG))).reshape(H // 2, 2, T, dh)
        return jnp.transpose(a[:, ::-1], (2, 0, 1, 3)).reshape(T, D)

    return place(qa), place(ka)


def _strip_kind(koff, k0, ksz, q0, qsz):
    if koff is None or koff + k0 + ksz - 1 <= q0:
        return "full"
    return "skip" if koff + k0 > q0 + qsz - 1 else "partial"


def _score_strip(km, qm, koff, k0, ksz, q0, qsz, kind):
    st = _dot_nt(km[k0:k0 + ksz, :], qm[q0:q0 + qsz, :])
    if kind == "partial":
        kpos = koff + k0 + lax.broadcasted_iota(jnp.int32, (ksz, qsz), 0)
        st = jnp.where(kpos <= q0 + lax.broadcasted_iota(jnp.int32, (ksz, qsz), 1), st, NEG_INF)
    return st


def _fold8(v, op):
    rows, n = v.shape
    v3 = v.reshape(rows // 8, 8, n)
    out = v3[0]
    for r in range(1, rows // 8):
        out = op(out, v3[r])
    return out


def _causal_tables(nbq, r, q_outer):
    a, b = [], []
    for o in range(nbq if q_outer else nbq * r):
        inner = range((o + 1) * r) if q_outer else range(o // r, nbq)
        for n in inner:
            a.append(o)
            b.append(n)
    return jnp.asarray(np.array(a, np.int32)), jnp.asarray(np.array(b, np.int32))


def _attn_tiles(T):
    tq = _tile(T, ATTN_Q_TILE)
    tk = _tile(tq, ATTN_K_TILE)
    return tq, tk, _tile(tk, ATTN_KEY_STRIP), _tile(tq, ATTN_QUERY_STRIP)


def _flash_fwd(q, k, v, qaug, kaug, H):
    T, D = q.shape
    Hp = H // 2
    W = D // Hp
    dh = W // 2
    tq, tk, ksz, qsz = _attn_tiles(T)
    r = tq // tk
    ii, jj = _causal_tables(T // tq, r, q_outer=True)
    n_steps = int(ii.shape[0])

    def body(ii_ref, jj_ref, q_ref, k_ref, v_ref, qa_ref, ka_ref, o_ref, lsea_ref, lseb_ref,
             qm_s, m_s, l_s, acc_s, p_s):
        n = pl.program_id(1)
        i, j = ii_ref[n], jj_ref[n]
        d = j - i * r
        in_a = lax.broadcasted_iota(jnp.int32, (1, W), 1) < dh
        top = lax.broadcasted_iota(jnp.int32, (W, 1), 0) < dh

        @pl.when(j == 0)
        def _():
            qv, qa = q_ref[...], qa_ref[...]
            qm_s[0] = jnp.where(in_a, qv, qa)
            qm_s[1] = jnp.where(in_a, qa, qv)
            m_s[...] = jnp.full(m_s.shape, NEG_INF, F32)
            l_s[...] = jnp.zeros_like(l_s)
            acc_s[...] = jnp.zeros_like(acc_s)

        def step(koff):
            kv, ka, vv = k_ref[...], ka_ref[...], v_ref[...]
            kms = (jnp.where(in_a, kv, ka), jnp.where(in_a, ka, kv))
            vt = vv.T
            alphas = {}

            def softmax_stage(hh, q0):
                cols = slice(q0, q0 + qsz)
                kinds = [(k0, _strip_kind(koff, k0, ksz, q0, qsz)) for k0 in range(0, tk, ksz)]
                if all(kind == "skip" for _, kind in kinds):
                    return False
                strips, part = {}, None
                for k0, kind in kinds:
                    if kind != "skip":
                        strips[k0] = _score_strip(kms[hh], qm_s[hh], koff, k0, ksz, q0, qsz, kind)
                        p8 = _fold8(strips[k0], jnp.maximum)
                        part = p8 if part is None else jnp.maximum(part, p8)
                m_prev = m_s[hh, :, cols]
                m_new = jnp.maximum(m_prev, jnp.max(part, axis=0, keepdims=True))
                l8 = None
                for k0, kind in kinds:
                    if kind == "skip":
                        p_s[hh, k0:k0 + ksz, cols] = jnp.zeros((ksz, qsz), BF16)
                        continue
                    pt = jnp.exp(strips[k0] - m_new)
                    s8 = _fold8(pt, jnp.add)
                    l8 = s8 if l8 is None else l8 + s8
                    p_s[hh, k0:k0 + ksz, cols] = pt.astype(BF16)
                alpha = jnp.exp(m_prev - m_new)
                l_s[hh, :, cols] = alpha * l_s[hh, :, cols] + jnp.sum(l8, axis=0, keepdims=True)
                m_s[hh, :, cols] = m_new
                alphas[(hh, q0)] = alpha
                return True

            def value_stage(hh, q0):
                cols, rows = slice(q0, q0 + qsz), slice(hh * dh, (hh + 1) * dh)
                acc_s[rows, cols] = acc_s[rows, cols] * alphas[(hh, q0)] + _dot(vt[rows, :], p_s[hh, :, cols])

            units = [(hh, q0) for q0 in range(0, tq, qsz) for hh in range(2)]
            pending = None
            for unit in units:
                live = softmax_stage(*unit)
                if pending is not None:
                    value_stage(*pending)
                pending = unit if live else None
            if pending is not None:
                value_stage(*pending)

        @pl.when(d < 0)
        def _():
            step(None)

        for dd in range(r):
            @pl.when(d == dd)
            def _():
                step(dd * tk)

        @pl.when(d == r - 1)
        def _():
            inv = jnp.where(top, 1.0 / l_s[0], 1.0 / l_s[1])
            o_ref[...] = (acc_s[...] * inv).T.astype(BF16)
            lsea_ref[...] = m_s[0] + jnp.log(l_s[0])
            lseb_ref[...] = m_s[1] + jnp.log(l_s[1])

    qspec = pl.BlockSpec((tq, W), lambda hp, n, ii, jj: (ii[n], hp))
    kspec = pl.BlockSpec((tk, W), lambda hp, n, ii, jj: (jj[n], hp))
    rspec = pl.BlockSpec((None, 1, tq), lambda hp, n, ii, jj: (hp, 0, ii[n]))
    grid_spec = pltpu.PrefetchScalarGridSpec(
        num_scalar_prefetch=2, grid=(Hp, n_steps),
        in_specs=[qspec, kspec, kspec, qspec, kspec],
        out_specs=[qspec, rspec, rspec],
        scratch_shapes=[pltpu.VMEM((2, tq, W), BF16), pltpu.VMEM((2, 1, tq), F32), pltpu.VMEM((2, 1, tq), F32),
                        pltpu.VMEM((W, tq), F32), pltpu.VMEM((2, tk, tq), BF16)],
    )
    return pl.pallas_call(
        body, name="flash_fwd", grid_spec=grid_spec,
        out_shape=(jax.ShapeDtypeStruct((T, D), BF16), jax.ShapeDtypeStruct((Hp, 1, T), F32),
                   jax.ShapeDtypeStruct((Hp, 1, T), F32)),
        compiler_params=_params(2),
    )(ii, jj, q, k, v, qaug, kaug)


def _attn_delta(do, o, head_sel):
    T, D = do.shape
    H = head_sel.shape[0]
    tm = _tile(T, ROW_TILE_FWD)

    def body(do_ref, o_ref, sel_ref, out_ref):
        prod = do_ref[...].astype(F32) * o_ref[...].astype(F32)
        out_ref[...] = lax.dot_general(sel_ref[...], prod, (((1,), (1,)), ((), ())),
                                       precision=lax.Precision.HIGHEST, preferred_element_type=F32)

    return pl.pallas_call(
        body, name="attn_delta", grid=(T // tm,),
        in_specs=[pl.BlockSpec((tm, D), lambda i: (i, 0)), pl.BlockSpec((tm, D), lambda i: (i, 0)),
                  pl.BlockSpec((H, D), lambda i: (0, 0))],
        out_specs=pl.BlockSpec((H, tm), lambda i: (0, i)),
        out_shape=jax.ShapeDtypeStruct((H, T), F32), compiler_params=_params())(do, o, head_sel)


def _flash_bwd(q, k, v, qaug, kaug, do, lse_a, lse_b, delta, qscale):
    T, D = q.shape
    Hp = lse_a.shape[0]
    W = D // Hp
    dh = W // 2
    tq, tk, ksz, qsz = _attn_tiles(T)
    r = tq // tk
    nbq = T // tq
    jj, ii = _causal_tables(nbq, r, q_outer=False)
    n_steps = int(ii.shape[0])
    row_lane = (dh, 0)
    col_lane = (dh + N_AUG, N_AUG)

    def body(jj_ref, ii_ref, q_ref, k_ref, v_ref, qa_ref, ka_ref, do_ref, lsea_ref, lseb_ref, dla_ref, dlb_ref,
             dqt_ref, dkt_ref, dvt_ref, dcola_ref, dcolb_ref, drowa_ref, drowb_ref,
             dqt_acc, dk_acc, dv_acc, p2_s, ds2_s):
        n = pl.program_id(1)
        i, j = ii_ref[n], jj_ref[n]
        d = j - i * r
        in_a = lax.broadcasted_iota(jnp.int32, (1, W), 1) < dh
        top = lax.broadcasted_iota(jnp.int32, (W, 1), 0) < dh

        @pl.when(n == 0)
        def _():
            dqt_acc[...] = jnp.zeros_like(dqt_acc)

        @pl.when(d >= 0)
        def _():
            dk_acc[...] = jnp.zeros_like(dk_acc)
            dv_acc[...] = jnp.zeros_like(dv_acc)

        def step(koff):
            qv, qa, kv, ka = q_ref[...], qa_ref[...], k_ref[...], ka_ref[...]
            vv, dov = v_ref[...], do_ref[...]
            zb = jnp.zeros_like(dov)
            kms = (jnp.where(in_a, kv, ka), jnp.where(in_a, ka, kv))
            qms = (jnp.where(in_a, qv, qa), jnp.where(in_a, qa, qv))
            doms = (jnp.where(in_a, dov, zb), jnp.where(in_a, zb, dov))
            kts, qts, dots = [tuple(a.T for a in pair) for pair in (kms, qms, doms)]
            lses, dls = (lsea_ref[...], lseb_ref[...]), (dla_ref[...], dlb_ref[...])
            dvs, dks = [], ([], [])

            def score_stage(hh, q0):
                cols = slice(q0, q0 + qsz)
                kinds = [(k0, _strip_kind(koff, k0, ksz, q0, qsz)) for k0 in range(0, tk, ksz)]
                if all(kind == "skip" for _, kind in kinds):
                    return False
                for k0, kind in kinds:
                    if kind == "skip":
                        p2_s[hh, k0:k0 + ksz, cols] = jnp.zeros((ksz, qsz), BF16)
                        ds2_s[hh, k0:k0 + ksz, cols] = jnp.zeros((ksz, qsz), BF16)
                        continue
                    st = _score_strip(kms[hh], qms[hh], koff, k0, ksz, q0, qsz, kind)
                    pt = jnp.exp(st - lses[hh][:, cols])
                    dst = pt * (_dot_nt(vv[k0:k0 + ksz, :], doms[hh][cols, :]) - dls[hh][:, cols])
                    p2_s[hh, k0:k0 + ksz, cols] = pt.astype(BF16)
                    ds2_s[hh, k0:k0 + ksz, cols] = dst.astype(BF16)
                return True

            def grad_stage(hh, q0):
                cols = slice(q0, q0 + qsz)
                ptb, dsb = p2_s[hh, :, cols], ds2_s[hh, :, cols]
                dvs.append(_dot_nt(dots[hh][:, cols], ptb))
                dks[hh].append(_dot_nt(qts[hh][:, cols], dsb))
                dqt_acc[i, hh, :, cols] += _dot(kts[hh], dsb)

            units = [(hh, q0) for q0 in range(0, tq, qsz) for hh in range(2)]
            pending = None
            for unit in units:
                live = score_stage(*unit)
                if pending is not None:
                    grad_stage(*pending)
                pending = unit if live else None
            if pending is not None:
                grad_stage(*pending)
            dv_acc[...] += sum(dvs[1:], dvs[0])
            for hh in range(2):
                dk_acc[hh] += sum(dks[hh][1:], dks[hh][0])

        @pl.when(d < 0)
        def _():
            step(None)

        for dd in range(r):
            @pl.when(d == dd)
            def _():
                step(dd * tk)

        @pl.when(i == nbq - 1)
        def _():
            dka, dkb = dk_acc[0], dk_acc[1]
            dkt_ref[...] = jnp.where(top, dka, dkb).astype(BF16)
            dvt_ref[...] = dv_acc[...].astype(BF16)
            dcola_ref[...] = dka[col_lane[0]:col_lane[0] + 1, :]
            dcolb_ref[...] = dkb[col_lane[1]:col_lane[1] + 1, :]

        @pl.when(n == n_steps - 1)
        def _():
            for b in range(nbq):
                blk_a, blk_b = dqt_acc[b, 0], dqt_acc[b, 1]
                dqt_ref[:, b * tq:(b + 1) * tq] = (jnp.where(top, blk_a, blk_b) * qscale).astype(BF16)
                drowa_ref[:, b * tq:(b + 1) * tq] = blk_a[row_lane[0]:row_lane[0] + 1, :]
                drowb_ref[:, b * tq:(b + 1) * tq] = blk_b[row_lane[1]:row_lane[1] + 1, :]

    qspec = pl.BlockSpec((tq, W), lambda hp, n, jj, ii: (ii[n], hp))
    kspec = pl.BlockSpec((tk, W), lambda hp, n, jj, ii: (jj[n], hp))
    ktspec = pl.BlockSpec((W, tk), lambda hp, n, jj, ii: (hp, jj[n]))
    pair_row = pl.BlockSpec((None, 1, tq), lambda hp, n, jj, ii: (hp, 0, ii[n]))
    key_row = pl.BlockSpec((None, 1, tk), lambda hp, n, jj, ii: (hp, 0, jj[n]))
    whole_row = pl.BlockSpec((None, 1, T), lambda hp, n, jj, ii: (hp, 0, 0))
    grid_spec = pltpu.PrefetchScalarGridSpec(
        num_scalar_prefetch=2, grid=(Hp, n_steps),
        in_specs=[
            qspec, kspec, kspec, qspec, kspec, qspec, pair_row, pair_row,
            pl.BlockSpec((None, 1, tq), lambda hp, n, jj, ii: (2 * hp, 0, ii[n])),
            pl.BlockSpec((None, 1, tq), lambda hp, n, jj, ii: (2 * hp + 1, 0, ii[n])),
        ],
        out_specs=[
            pl.BlockSpec((W, T), lambda hp, n, jj, ii: (hp, 0)),
            ktspec, ktspec, key_row, key_row, whole_row, whole_row,
        ],
        scratch_shapes=[pltpu.VMEM((nbq, 2, W, tq), F32), pltpu.VMEM((2, W, tk), F32), pltpu.VMEM((W, tk), F32),
                        pltpu.VMEM((2, tk, tq), BF16), pltpu.VMEM((2, tk, tq), BF16)],
    )
    return pl.pallas_call(
        body, name="flash_bwd", grid_spec=grid_spec,
        out_shape=(jax.ShapeDtypeStruct((D, T), BF16), jax.ShapeDtypeStruct((D, T), BF16),
                   jax.ShapeDtypeStruct((D, T), BF16), jax.ShapeDtypeStruct((Hp, 1, T), F32),
                   jax.ShapeDtypeStruct((Hp, 1, T), F32),
                   jax.ShapeDtypeStruct((Hp, 1, T), F32), jax.ShapeDtypeStruct((Hp, 1, T), F32)),
        compiler_params=_params(2),
    )(jj, ii, q, k, v, qaug, kaug, do, lse_a, lse_b, delta, delta)


HBM_SPEC = pl.BlockSpec(memory_space=pltpu.HBM)


def _mesh_pos():
    return lax.axis_index("x"), lax.axis_index("y"), lax.axis_index("c")


def _all_gather_shards(shards):
    n = len(shards)

    def body(*refs):
        in_refs, out_refs = refs[:n], refs[n:2 * n]
        send1, recv1, send2, recv2, lsem = refs[2 * n:]
        x, y, c = _mesh_pos()
        me = 2 * x + y
        chips = [(1 - x, y), (x, 1 - y), (1 - x, 1 - y)]
        local = [pltpu.make_async_copy(in_refs[t], out_refs[t].at[me], lsem.at[t]) for t in range(n)]
        for cp in local:
            cp.start()

        def half(t, chip_idx, pc):
            hr = shards[t].shape[0] // 2
            return out_refs[t].at[chip_idx, pl.ds(pc * hr, hr), :]

        first = []
        for t in range(n):
            hr = shards[t].shape[0] // 2
            for kk, (cx, cy) in enumerate(chips):
                first.append(pltpu.make_async_remote_copy(
                    src_ref=in_refs[t].at[pl.ds(c * hr, hr), :], dst_ref=half(t, me, c),
                    send_sem=send1.at[3 * t + kk], recv_sem=recv1.at[3 * t + kk],
                    device_id=(cx, cy, c), device_id_type=MESH))
        for cp in first:
            cp.start()
        passed = []
        for t in range(n):
            for kk, (cx, cy) in enumerate(chips):
                src_chip = 2 * cx + cy
                landed = half(t, src_chip, c)
                pltpu.make_async_remote_copy(
                    src_ref=landed, dst_ref=landed, send_sem=send1.at[3 * t + kk], recv_sem=recv1.at[3 * t + kk],
                    device_id=(cx, cy, c), device_id_type=MESH).wait_recv()
                fwd = pltpu.make_async_remote_copy(
                    src_ref=landed, dst_ref=landed, send_sem=send2.at[3 * t + kk], recv_sem=recv2.at[3 * t + kk],
                    device_id=(x, y, 1 - c), device_id_type=MESH)
                fwd.start()
                passed.append(fwd)
        for t in range(n):
            for kk, (cx, cy) in enumerate(chips):
                other = half(t, 2 * cx + cy, 1 - c)
                pltpu.make_async_remote_copy(
                    src_ref=other, dst_ref=other, send_sem=send2.at[3 * t + kk], recv_sem=recv2.at[3 * t + kk],
                    device_id=(x, y, 1 - c), device_id_type=MESH).wait_recv()
        for cp in first + passed:
            cp.wait_send()
        for cp in local:
            cp.wait()

    return pl.pallas_call(
        body, name="weights_all_gather",
        out_shape=[jax.ShapeDtypeStruct((N_CHIPS,) + s.shape, s.dtype) for s in shards],
        in_specs=[HBM_SPEC] * n, out_specs=[HBM_SPEC] * n,
        scratch_shapes=[pltpu.SemaphoreType.DMA((3 * n,)), pltpu.SemaphoreType.DMA((3 * n,)),
                        pltpu.SemaphoreType.DMA((3 * n,)), pltpu.SemaphoreType.DMA((3 * n,)),
                        pltpu.SemaphoreType.DMA((n,))],
    )(*shards)


def _sibling_swap_halves(g):
    _, M, C = g.shape
    hr = M // 2

    def body(g_ref, r_ref, ssem, rsem):
        x, y, c = _mesh_pos()
        cps = [pltpu.make_async_remote_copy(
            src_ref=g_ref.at[s, pl.ds((1 - c) * hr, hr), :], dst_ref=r_ref.at[s],
            send_sem=ssem.at[s], recv_sem=rsem.at[s], device_id=(x, y, 1 - c), device_id_type=MESH)
            for s in range(N_CHIPS)]
        for cp in cps:
            cp.start()
        for cp in cps:
            cp.wait()

    return pl.pallas_call(
        body, name="grads_sibling_swap", out_shape=jax.ShapeDtypeStruct((N_CHIPS, hr, C), F32),
        in_specs=[HBM_SPEC], out_specs=HBM_SPEC,
        scratch_shapes=[pltpu.SemaphoreType.DMA((N_CHIPS,)), pltpu.SemaphoreType.DMA((N_CHIPS,))],
    )(g)


def _pair_add(g, r, c_idx):
    _, M, C = g.shape
    hr = M // 2
    tr = PACK_ROW_TILE
    nbk = hr // tr

    def body(c_ref, g_ref, r_ref, o_ref):
        o_ref[...] = (g_ref[...] + r_ref[...]).astype(BF16)

    grid_spec = pltpu.PrefetchScalarGridSpec(
        num_scalar_prefetch=1, grid=(N_CHIPS, nbk),
        in_specs=[pl.BlockSpec((None, tr, C), lambda s, i, c: (s, c[0] * nbk + i, 0)),
                  pl.BlockSpec((None, tr, C), lambda s, i, c: (s, i, 0))],
        out_specs=pl.BlockSpec((None, tr, C), lambda s, i, c: (s, i, 0)),
    )
    return pl.pallas_call(body, name="grads_pair_add", grid_spec=grid_spec,
                          out_shape=jax.ShapeDtypeStruct((N_CHIPS, hr, C), BF16),
                          compiler_params=_params(2))(c_idx, g, r)


def _chip_scatter(pp):
    def body(p_ref, r_ref, ssem, rsem, lsem):
        x, y, c = _mesh_pos()
        me = 2 * x + y
        chips = [(1 - x, y), (x, 1 - y), (1 - x, 1 - y)]
        own = pltpu.make_async_copy(p_ref.at[me], r_ref.at[me], lsem)
        own.start()
        cps = [pltpu.make_async_remote_copy(
            src_ref=p_ref.at[2 * cx + cy], dst_ref=r_ref.at[me], send_sem=ssem.at[kk], recv_sem=rsem.at[kk],
            device_id=(cx, cy, c), device_id_type=MESH) for kk, (cx, cy) in enumerate(chips)]
        for cp in cps:
            cp.start()
        for kk, (cx, cy) in enumerate(chips):
            got = r_ref.at[2 * cx + cy]
            pltpu.make_async_remote_copy(
                src_ref=got, dst_ref=got, send_sem=ssem.at[kk], recv_sem=rsem.at[kk],
                device_id=(cx, cy, c), device_id_type=MESH).wait_recv()
        for cp in cps:
            cp.wait_send()
        own.wait()

    return pl.pallas_call(
        body, name="grads_chip_scatter", out_shape=jax.ShapeDtypeStruct(pp.shape, pp.dtype),
        in_specs=[HBM_SPEC], out_specs=HBM_SPEC,
        scratch_shapes=[pltpu.SemaphoreType.DMA((3,)), pltpu.SemaphoreType.DMA((3,)), pltpu.SemaphoreType.DMA],
    )(pp)


def _chip_sum(r, c_idx):
    _, hr, C = r.shape
    tr = PACK_ROW_TILE
    nbk = hr // tr

    def body(c_ref, r_ref, o_ref):
        r0, r1, r2, r3 = (r_ref[s].astype(F32) for s in range(N_CHIPS))
        o_ref[...] = ((r0 + r1) + r2) + r3

    grid_spec = pltpu.PrefetchScalarGridSpec(
        num_scalar_prefetch=1, grid=(nbk,),
        in_specs=[pl.BlockSpec((N_CHIPS, tr, C), lambda i, c: (0, i, 0))],
        out_specs=pl.BlockSpec((tr, C), lambda i, c: (c[0] * nbk + i, 0)),
    )
    return pl.pallas_call(body, name="grads_chip_sum", grid_spec=grid_spec,
                          out_shape=jax.ShapeDtypeStruct((2 * hr, C), F32), compiler_params=_params())(c_idx, r)


def _sibling_join(buf):
    hr = buf.shape[0] // 2

    def body(in_ref, o_ref, ssem, rsem):
        del in_ref
        x, y, c = _mesh_pos()
        mine = o_ref.at[pl.ds(c * hr, hr), :]
        cp = pltpu.make_async_remote_copy(src_ref=mine, dst_ref=mine, send_sem=ssem, recv_sem=rsem,
                                          device_id=(x, y, 1 - c), device_id_type=MESH)
        cp.start()
        theirs = o_ref.at[pl.ds((1 - c) * hr, hr), :]
        pltpu.make_async_remote_copy(src_ref=theirs, dst_ref=theirs, send_sem=ssem, recv_sem=rsem,
                                     device_id=(x, y, 1 - c), device_id_type=MESH).wait_recv()
        cp.wait_send()

    return pl.pallas_call(
        body, name="grads_sibling_join", out_shape=jax.ShapeDtypeStruct(buf.shape, F32),
        in_specs=[HBM_SPEC], out_specs=HBM_SPEC, input_output_aliases={0: 0},
        scratch_shapes=[pltpu.SemaphoreType.DMA, pltpu.SemaphoreType.DMA],
    )(buf)


def _adamw(w, g, m, v):
    M, C = w.shape
    tr = _tile(M, PACK_ROW_TILE)

    def body(w_ref, g_ref, m_ref, v_ref, d_ref, mo_ref, vo_ref):
        gv = g_ref[...]
        mn = ADAM_B1 * m_ref[...] + (1.0 - ADAM_B1) * gv
        vn = ADAM_B2 * v_ref[...] + (1.0 - ADAM_B2) * (gv * gv)
        m_hat = mn / (1.0 - ADAM_B1 ** ADAM_STEP)
        v_hat = vn / (1.0 - ADAM_B2 ** ADAM_STEP)
        d_ref[...] = -ADAM_LR * (m_hat / (jnp.sqrt(v_hat) + ADAM_EPS) + ADAM_WD * w_ref[...])
        mo_ref[...] = mn
        vo_ref[...] = vn

    spec = pl.BlockSpec((tr, C), lambda i: (i, 0))
    return pl.pallas_call(
        body, name="adamw", grid=(M // tr,), in_specs=[spec] * 4, out_specs=[spec] * 3,
        out_shape=[jax.ShapeDtypeStruct((M, C), F32)] * 3, compiler_params=_params())(w, g, m, v)


WEIGHT_NAMES = ("norm_g", "w_attn_in", "b_forget", "w_attn_out", "w_conv_in", "conv_w", "w_conv_out",
                "w_mlp_up", "w_mlp_down", "w_ple_proj", "w_ple_gate")
COL_SHARDED = ("norm_g", "w_attn_in", "w_conv_in", "conv_w", "w_mlp_up", "w_ple_proj")
ROW_SHARDED = ("w_attn_out", "w_conv_out", "w_mlp_down", "w_ple_gate")


def _unshard(name, gathered, shard_shape):
    a = gathered.reshape((N_CHIPS,) + tuple(shard_shape))
    if name in COL_SHARDED:
        a = jnp.moveaxis(a, 0, -2)
        return a.reshape(a.shape[:-2] + (N_CHIPS * shard_shape[-1],))
    a = jnp.moveaxis(a, 0, 1)
    return a.reshape((shard_shape[0], N_CHIPS * shard_shape[1], shard_shape[2]))


def _to_shard_major(name, full):
    if name == "b_forget":
        return jnp.broadcast_to(full.reshape(1, -1), (N_CHIPS, full.size))
    if name in COL_SHARDED:
        a = full.reshape(full.shape[:-1] + (N_CHIPS, full.shape[-1] // N_CHIPS))
        a = jnp.moveaxis(a, -2, 0)
    else:
        a = full.reshape((full.shape[0], N_CHIPS, full.shape[1] // N_CHIPS, full.shape[2]))
        a = jnp.moveaxis(a, 1, 0)
    return a.reshape(N_CHIPS, -1)


def _pack_layout(shard_shapes):
    offs, row = {}, 0
    for name in WEIGHT_NAMES:
        n = int(np.prod(shard_shapes[name]))
        rows = -(-n // PACK_COLS)
        offs[name] = (row, rows, n)
        row += rows
    total = -(-row // PACK_ROW_ALIGN) * PACK_ROW_ALIGN
    return offs, total


def _pack(flat_by_name, offs, total_rows, lead=()):
    parts, row = [], 0
    for name in WEIGHT_NAMES:
        _, rows, n = offs[name]
        a = flat_by_name[name].astype(F32)
        pad = rows * PACK_COLS - n
        if pad:
            a = jnp.pad(a, [(0, 0)] * len(lead) + [(0, pad)])
        parts.append(a.reshape(tuple(lead) + (rows, PACK_COLS)))
        row += rows
    if total_rows > row:
        parts.append(jnp.zeros(tuple(lead) + (total_rows - row, PACK_COLS), F32))
    return jnp.concatenate(parts, axis=len(lead))


def _unpack(packed, offs, shard_shapes):
    out = {}
    for name in WEIGHT_NAMES:
        row, rows, n = offs[name]
        out[name] = packed[row:row + rows].reshape(-1)[:n].reshape(shard_shapes[name])
    return out


def kernel(x, p, norm_g, w_attn_in, b_forget, w_attn_out, w_conv_in, conv_w, w_conv_out, w_mlp_up, w_mlp_down, w_ple_proj, w_ple_gate, loss_target, m_norm_g, m_w_attn_in, m_b_forget, m_w_attn_out, m_w_conv_in, m_conv_w, m_w_conv_out, m_w_mlp_up, m_w_mlp_down, m_w_ple_proj, m_w_ple_gate, v_norm_g, v_w_attn_in, v_b_forget, v_w_attn_out, v_w_conv_in, v_conv_w, v_w_conv_out, v_w_mlp_up, v_w_mlp_down, v_w_ple_proj, v_w_ple_gate):
    w_local = dict(norm_g=norm_g, w_attn_in=w_attn_in, b_forget=b_forget, w_attn_out=w_attn_out,
                   w_conv_in=w_conv_in, conv_w=conv_w, w_conv_out=w_conv_out, w_mlp_up=w_mlp_up,
                   w_mlp_down=w_mlp_down, w_ple_proj=w_ple_proj, w_ple_gate=w_ple_gate)
    m_local = dict(norm_g=m_norm_g, w_attn_in=m_w_attn_in, b_forget=m_b_forget, w_attn_out=m_w_attn_out,
                   w_conv_in=m_w_conv_in, conv_w=m_conv_w, w_conv_out=m_w_conv_out, w_mlp_up=m_w_mlp_up,
                   w_mlp_down=m_w_mlp_down, w_ple_proj=m_w_ple_proj, w_ple_gate=m_w_ple_gate)
    v_local = dict(norm_g=v_norm_g, w_attn_in=v_w_attn_in, b_forget=v_b_forget, w_attn_out=v_w_attn_out,
                   w_conv_in=v_w_conv_in, conv_w=v_conv_w, w_conv_out=v_w_conv_out, w_mlp_up=v_w_mlp_up,
                   w_mlp_down=v_w_mlp_down, w_ple_proj=v_w_ple_proj, w_ple_gate=v_w_ple_gate)
    shard_shapes = {k: tuple(a.shape) for k, a in w_local.items()}

    xs = x[0]
    target = loss_target[0]
    T, D = xs.shape
    depth = p.shape[0]
    H = b_forget.shape[1]
    qscale = float(D // H) ** -0.5
    head_sel = (jnp.arange(D)[None, :] // (D // H) == jnp.arange(H)[:, None]).astype(F32)

    big = [n for n in WEIGHT_NAMES if n not in ("norm_g", "conv_w", "b_forget")]
    small = jnp.concatenate([norm_g.reshape(-1), conv_w.reshape(-1)])
    n_small = small.shape[0]
    small_rows = -(-n_small // (LANES * 16)) * 16
    small = jnp.pad(small, (0, small_rows * LANES - n_small)).reshape(small_rows, LANES)
    shards = [w_local[n].astype(BF16).reshape(-1, shard_shapes[n][-1]) for n in big] + [small]
    gathered = _all_gather_shards(shards)
    full = {n: _unshard(n, g, shard_shapes[n]) for n, g in zip(big, gathered[:-1])}
    gs = gathered[-1].reshape(N_CHIPS, -1)
    full["norm_g"] = _unshard("norm_g", gs[:, :norm_g.size], shard_shapes["norm_g"])
    full["conv_w"] = _unshard("conv_w", gs[:, norm_g.size:n_small], shard_shapes["conv_w"])
    gains = full["norm_g"]

    def gain(i, k):
        return gains[i, k].reshape(1, D)

    def taps(j):
        return jnp.pad(full["conv_w"][j], ((0, 5), (0, 0)))

    saved = []
    h = xs
    for i in range(depth):
        j = i // 2
        s = {"x0": h}
        if i % 2 == 0:
            w_in = full["w_attn_in"][j]
            wqkv = w_in[:, :3 * D]
            wf = jnp.pad(w_in[:, 3 * D:], ((0, 0), (0, LANES - H)))
            bf = jnp.pad(b_forget[j].reshape(1, H), ((0, 0), (0, LANES - H)))
            q, k, v, fl, lf = _attn_in_fwd(h, gain(i, 0), wqkv, wf, bf, qscale)
            qaug, kaug = _aug_operands(_gate_cumsum(lf[:, :H].T), H, D)
            o, lse_a, lse_b = _flash_fwd(q, k, v, qaug, kaug, H)
            s.update(q=q, k=k, v=v, fl=fl, qaug=qaug, kaug=kaug, o=o, lse_a=lse_a, lse_b=lse_b, wqkv=wqkv, wf=wf)
            mix_in, w_out = o, full["w_attn_out"][j]
        else:
            b, c, u, zc, y = _conv_in_fwd(h, gain(i, 0), full["w_conv_in"][j], taps(j))
            s.update(b=b, c=c, u=u, zc=zc, y=y)
            mix_in, w_out = y, full["w_conv_out"][j]
        m1, x1 = _out_proj_fwd("mixer_out_fwd", mix_in, w_out, gain(i, 1), h)
        up, a = _mlp_up_fwd(x1, gain(i, 2), full["w_mlp_up"][i])
        m3, x2 = _out_proj_fwd("mlp_down_fwd", a, full["w_mlp_down"][i], gain(i, 3), x1)
        gl, pe, x3 = _ple_fwd(x2, p[i, 0], gain(i, 4), gain(i, 5), full["w_ple_gate"][i], full["w_ple_proj"][i])
        s.update(m1=m1, x1=x1, up=up, a=a, m3=m3, x2=x2, gl=gl, pe=pe, w_out=w_out)
        saved.append(s)
        h = x3

    dh, loss_blk = _loss_fwd_bwd(h, target)
    loss = lax.psum(loss_blk[0, 0], ("x", "y", "c"))

    g_gain = [[None] * 6 for _ in range(depth)]
    grads = {n: [None] * w_local[n].shape[0] for n in WEIGHT_NAMES if n != "norm_g"}
    for i in reversed(range(depth)):
        j = i // 2
        s = saved[i]
        dx2, dwp, dwg, g_gain[i][4], g_gain[i][5] = _ple_bwd(
            dh, s["x2"], p[i, 0], s["gl"], s["pe"], gain(i, 4), gain(i, 5), full["w_ple_gate"][i].T)
        grads["w_ple_proj"][i], grads["w_ple_gate"][i] = dwp, dwg
        dup, dwd, g_gain[i][3] = _out_proj_bwd(
            "mlp_down_bwd", dx2, s["m3"], gain(i, 3), [s["a"], s["up"]], full["w_mlp_down"][i].T, "relu2")
        grads["w_mlp_down"][i] = dwd
        dx1, dwu, g_gain[i][2] = _in_proj_bwd(
            "mlp_up_bwd", [dup], [full["w_mlp_up"][i].T], s["x1"], gain(i, 2), dx2)
        grads["w_mlp_up"][i] = dwu
        if i % 2 == 0:
            do, dwo, g_gain[i][1] = _out_proj_bwd(
                "attn_out_bwd", dx1, s["m1"], gain(i, 1), [s["o"]], s["w_out"].T, "plain")
            grads["w_attn_out"][j] = dwo
            delta = _attn_delta(do, s["o"], head_sel).reshape(H, 1, T)
            dqt, dkt, dvt, dca, dcb, dra, drb = _flash_bwd(s["q"], s["k"], s["v"], s["qaug"], s["kaug"], do,
                                                           s["lse_a"], s["lse_b"], delta, qscale)
            dcol = jnp.concatenate([dca, dcb], axis=1).reshape(H, T)
            drow = jnp.concatenate([dra, drb], axis=1).reshape(H, T)
            dfl_t, dbf = _gate_bwd(drow, dcol, s["fl"][:, :H].T)
            grads["b_forget"][j] = dbf.reshape(H)
            dfl = jnp.pad(dfl_t.T, ((0, 0), (0, LANES - H))).astype(BF16)
            wqkv_t = s["wqkv"].T
            dh, dwq, dwk, dwv, dwf, g_gain[i][0] = _in_proj_bwd(
                "attn_in_bwd", [dqt.T, dkt.T, dvt.T, dfl], [wqkv_t[:D], wqkv_t[D:2 * D], wqkv_t[2 * D:], s["wf"].T],
                s["x0"], gain(i, 0), dx1)
            grads["w_attn_in"][j] = jnp.concatenate([dwq, dwk, dwv, dwf[:, :H]], axis=1)
        else:
            dy, dwo, g_gain[i][1] = _out_proj_bwd(
                "conv_out_bwd", dx1, s["m1"], gain(i, 1), [s["y"]], s["w_out"].T, "plain_f32")
            grads["w_conv_out"][j] = dwo
            db, dc, du, dcw = _conv_bwd(dy, s["b"], s["c"], s["u"], s["zc"], taps(j))
            grads["conv_w"][j] = dcw[:conv_w.shape[1]]
            w_t = full["w_conv_in"][j].T
            dh, dwb, dwc, dwu2, g_gain[i][0] = _in_proj_bwd(
                "conv_in_bwd", [db, dc, du], [w_t[:D], w_t[D:2 * D], w_t[2 * D:]], s["x0"], gain(i, 0), dx1)
            grads["w_conv_in"][j] = jnp.concatenate([dwb, dwc, dwu2], axis=1)
    grad_x = dh.reshape(x.shape)

    grad_full = {n: jnp.stack(grads[n]) for n in grads}
    grad_full["norm_g"] = jnp.stack([jnp.concatenate(row, axis=0) for row in g_gain])

    offs, total_rows = _pack_layout(shard_shapes)
    g_packed = _pack({n: _to_shard_major(n, grad_full[n]) for n in WEIGHT_NAMES}, offs, total_rows, lead=(N_CHIPS,))
    c_idx = lax.axis_index("c").astype(jnp.int32).reshape(1)
    pair = _pair_add(g_packed, _sibling_swap_halves(g_packed), c_idx)
    g_red = _sibling_join(_chip_sum(_chip_scatter(pair), c_idx))
    g_out = _unpack(g_red, offs, shard_shapes)
    d_out, m_out, v_out = {}, {}, {}
    for n in WEIGHT_NAMES:
        shp = shard_shapes[n]
        two_d = (-1, shp[-1])
        d, mn, vn = _adamw(w_local[n].reshape(two_d), g_out[n].reshape(two_d),
                           m_local[n].reshape(two_d), v_local[n].reshape(two_d))
        d_out[n], m_out[n], v_out[n] = d.reshape(shp), mn.reshape(shp), vn.reshape(shp)
    return (loss, grad_x, *[g_out[n] for n in WEIGHT_NAMES], *[d_out[n] for n in WEIGHT_NAMES],
            *[m_out[n] for n in WEIGHT_NAMES], *[v_out[n] for n in WEIGHT_NAMES])
```

```python
import numpy as np
import jax
import jax.numpy as jnp
from jax import lax
from jax.experimental import pallas as pl
from jax.experimental.pallas import tpu as pltpu

F32 = jnp.float32
BF16 = jnp.bfloat16
MESH = pl.DeviceIdType.MESH

RMS_EPS = 1e-6
NEG_INF = -1e30
ADAM_LR = 0.001
ADAM_B1 = 0.9
ADAM_B2 = 0.999
ADAM_EPS = 1e-08
ADAM_WD = 0.01
ADAM_STEP = 10

N_CHIPS = 4
LANES = 128
ROW_TILE_FWD = 512
ROW_TILE_BWD = 256
ATTN_Q_TILE = 2048
ATTN_K_TILE = 512
ATTN_KEY_STRIP = 128
ATTN_QUERY_STRIP = 256
ATTN_ROTATE = 4
COL_CHUNK = 512
PACK_COLS = 1024
PACK_ROW_TILE = 256
PACK_ROW_ALIGN = 2 * PACK_ROW_TILE
AUG_ROWS = 16
N_AUG = 3
VMEM_LIMIT = 56 * 1024 * 1024


def _tile(n, pref):
    return pref if n % pref == 0 else n


def _dot(a, b):
    return jnp.dot(a, b, preferred_element_type=F32)


def _dot_tn(a, b):
    return lax.dot_general(a, b, (((0,), (0,)), ((), ())), preferred_element_type=F32)


def _dot_nt(a, b):
    return lax.dot_general(a, b, (((1,), (1,)), ((), ())), preferred_element_type=F32)


def _rms_fwd(x, g):
    r = lax.rsqrt(jnp.mean(x * x, axis=-1, keepdims=True) + RMS_EPS)
    return (x * r) * g


def _rms_bwd(x, g, dy):
    r = lax.rsqrt(jnp.mean(x * x, axis=-1, keepdims=True) + RMS_EPS)
    xh = x * r
    dyg = dy * g
    dx = r * (dyg - xh * jnp.mean(dyg * xh, axis=-1, keepdims=True))
    return dx, jnp.sum(dy * xh, axis=0, keepdims=True)


def _params(n_axes=1):
    return pltpu.CompilerParams(dimension_semantics=("arbitrary",) * n_axes, vmem_limit_bytes=VMEM_LIMIT)


def _row_call(name, body, n_rows, tm, row_ins, const_ins, row_outs, acc_outs=(), scratch=(), reverse=False):
    nb = n_rows // tm
    rmap = (lambda i: (nb - 1 - i, 0)) if reverse else (lambda i: (i, 0))

    def whole(shape):
        nd = len(shape)
        return pl.BlockSpec(tuple(shape), lambda i: (0,) * nd)

    in_specs = [pl.BlockSpec((tm, a.shape[1]), rmap) for a in row_ins] + [whole(a.shape) for a in const_ins]
    out_shape = [jax.ShapeDtypeStruct((n_rows, w), dt) for (w, dt) in row_outs]
    out_shape += [jax.ShapeDtypeStruct(tuple(s), dt) for (s, dt) in acc_outs]
    out_specs = [pl.BlockSpec((tm, w), rmap) for (w, _) in row_outs] + [whole(s) for (s, _) in acc_outs]
    return pl.pallas_call(
        body, name=name, grid=(nb,), in_specs=in_specs, out_specs=out_specs, out_shape=out_shape,
        scratch_shapes=list(scratch), compiler_params=_params(),
    )(*row_ins, *const_ins)


def _attn_in_fwd(x, g, wqkv, wf, bf, qscale):
    T, D = x.shape
    tm, ch = _tile(T, ROW_TILE_FWD), _tile(D, COL_CHUNK)

    def body(x_ref, g_ref, w_ref, wf_ref, bf_ref, q_ref, k_ref, v_ref, fl_ref, lf_ref):
        h = _rms_fwd(x_ref[...], g_ref[...]).astype(BF16)
        for part, o_ref in enumerate((q_ref, k_ref, v_ref)):
            for n0 in range(0, D, ch):
                r = _dot(h, w_ref[:, part * D + n0:part * D + n0 + ch])
                if part == 0:
                    r = r * qscale
                o_ref[:, n0:n0 + ch] = r.astype(BF16)
        fl = _dot(h, wf_ref[...]) + bf_ref[...]
        fl_ref[...] = fl
        lf_ref[...] = jnp.minimum(fl, 0.0) - jnp.log1p(jnp.exp(-jnp.abs(fl)))

    return _row_call("attn_in_fwd", body, T, tm, [x], [g, wqkv, wf, bf],
                     [(D, BF16), (D, BF16), (D, BF16), (LANES, F32), (LANES, F32)])


def _conv_in_fwd(x, g, w, cw):
    T, D = x.shape
    tm, ch = _tile(T, ROW_TILE_FWD), _tile(D, COL_CHUNK)

    def body(x_ref, g_ref, w_ref, cw_ref, b_ref, c_ref, u_ref, zc_ref, y_ref, tail_ref):
        i = pl.program_id(0)

        @pl.when(i == 0)
        def _():
            tail_ref[...] = jnp.zeros_like(tail_ref)

        h = _rms_fwd(x_ref[...], g_ref[...]).astype(BF16)
        for part, o_ref in enumerate((b_ref, c_ref, u_ref)):
            for n0 in range(0, D, ch):
                o_ref[:, n0:n0 + ch] = _dot(h, w_ref[:, part * D + n0:part * D + n0 + ch])
        z = c_ref[...] * u_ref[...]
        row = lax.broadcasted_iota(jnp.int32, (tm, 1), 0)
        t6, t7 = tail_ref[6:7, :], tail_ref[7:8, :]
        z1 = jnp.where(row == 0, t7, pltpu.roll(z, 1, axis=0))
        z2 = jnp.where(row == 0, t6, jnp.where(row == 1, t7, pltpu.roll(z, 2, axis=0)))
        zc = cw_ref[0:1, :] * z2 + cw_ref[1:2, :] * z1 + cw_ref[2:3, :] * z
        zc_ref[...] = zc
        y_ref[...] = (b_ref[...] * zc).astype(BF16)
        tail_ref[...] = z[tm - 8:tm, :]

    return _row_call("conv_in_fwd", body, T, tm, [x], [g, w, cw],
                     [(D, F32), (D, F32), (D, F32), (D, F32), (D, BF16)], scratch=[pltpu.VMEM((8, D), F32)])


def _mlp_up_fwd(x, g, w):
    T, D = x.shape
    F = w.shape[1]
    tm, ch = _tile(T, ROW_TILE_FWD), _tile(F, COL_CHUNK)

    def body(x_ref, g_ref, w_ref, up_ref, a_ref):
        h = _rms_fwd(x_ref[...], g_ref[...]).astype(BF16)
        for n0 in range(0, F, ch):
            r = _dot(h, w_ref[:, n0:n0 + ch])
            up_ref[:, n0:n0 + ch] = r.astype(BF16)
            rl = jnp.maximum(r, 0.0)
            a_ref[:, n0:n0 + ch] = (rl * rl).astype(BF16)

    return _row_call("mlp_up_fwd", body, T, tm, [x], [g, w], [(F, BF16), (F, BF16)])


def _out_proj_fwd(name, a, w, g, x):
    T, D = x.shape
    tm = _tile(T, ROW_TILE_FWD)

    def body(a_ref, x_ref, w_ref, g_ref, m_ref, xn_ref):
        m = _dot(a_ref[...], w_ref[...])
        m_ref[...] = m
        xn_ref[...] = x_ref[...] + _rms_fwd(m, g_ref[...])

    return _row_call(name, body, T, tm, [a, x], [w, g], [(D, F32), (D, F32)])


def _ple_fwd(x, p, g4, g5, wg, wp):
    T, D = x.shape
    tm = _tile(T, ROW_TILE_FWD)

    def body(x_ref, p_ref, g4_ref, g5_ref, wg_ref, wp_ref, gl_ref, pe_ref, xn_ref):
        xv = x_ref[...]
        gl = _dot(_rms_fwd(xv, g4_ref[...]).astype(BF16), wg_ref[...])
        pe = _dot(p_ref[...].astype(BF16), wp_ref[...])
        gl_ref[...] = gl
        pe_ref[...] = pe
        e = pe * (1.0 / (1.0 + jnp.exp(-gl)))
        xn_ref[...] = xv + _rms_fwd(e, g5_ref[...])

    return _row_call("ple_fwd", body, T, tm, [x, p], [g4, g5, wg, wp], [(D, F32), (D, F32), (D, F32)])


def _loss_fwd_bwd(y, target):
    T, D = y.shape
    tm = _tile(T, ROW_TILE_FWD)

    def body(y_ref, t_ref, dy_ref, loss_ref):
        @pl.when(pl.program_id(0) == 0)
        def _():
            loss_ref[...] = jnp.zeros_like(loss_ref)

        err = y_ref[...] - t_ref[...]
        dy_ref[...] = err * (1.0 / D)
        part = 0.5 * jnp.sum(jnp.mean(err * err, axis=-1, keepdims=True), axis=0, keepdims=True)
        loss_ref[...] += jnp.broadcast_to(part, loss_ref.shape)

    return _row_call("loss", body, T, tm, [y, target], [], [(D, F32)], acc_outs=[((8, LANES), F32)])


def _out_proj_bwd(name, dres, m, g, a_ins, wt, mode):
    T, D = dres.shape
    Ka = wt.shape[1]
    tm, ch = _tile(T, ROW_TILE_BWD), _tile(Ka, COL_CHUNK)
    out_dt = F32 if mode == "plain_f32" else BF16

    def body(dres_ref, m_ref, *rest):
        a_ref = rest[0]
        up_ref = rest[1] if mode == "relu2" else None
        k = len(a_ins)
        g_ref, wt_ref, da_ref, dw_ref, dg_ref = rest[k:k + 5]

        @pl.when(pl.program_id(0) == 0)
        def _():
            dw_ref[...] = jnp.zeros_like(dw_ref)
            dg_ref[...] = jnp.zeros_like(dg_ref)

        dm, dgp = _rms_bwd(m_ref[...], g_ref[...], dres_ref[...])
        dg_ref[...] += dgp
        dmb = dm.astype(BF16)
        for n0 in range(0, Ka, ch):
            da = _dot(dmb, wt_ref[:, n0:n0 + ch])
            if mode == "relu2":
                da = da * (2.0 * jnp.maximum(up_ref[:, n0:n0 + ch].astype(F32), 0.0))
            da_ref[:, n0:n0 + ch] = da.astype(out_dt)
            dw_ref[n0:n0 + ch, :] += _dot_tn(a_ref[:, n0:n0 + ch], dmb)

    return _row_call(name, body, T, tm, [dres, m] + list(a_ins), [g, wt],
                     [(Ka, out_dt)], acc_outs=[((Ka, D), F32), ((1, D), F32)])


def _in_proj_bwd(name, pieces, wts, x, g, dres):
    T, D = x.shape
    tm = _tile(T, ROW_TILE_BWD)
    k = len(pieces)
    widths = [pc.shape[1] for pc in pieces]

    def body(*refs):
        pc_refs = refs[:k]
        x_ref, dres_ref, g_ref = refs[k:k + 3]
        wt_refs = refs[k + 3:2 * k + 3]
        dx_ref = refs[2 * k + 3]
        dw_refs = refs[2 * k + 4:3 * k + 4]
        dg_ref = refs[3 * k + 4]

        @pl.when(pl.program_id(0) == 0)
        def _():
            for r in dw_refs:
                r[...] = jnp.zeros_like(r)
            dg_ref[...] = jnp.zeros_like(dg_ref)

        xv, gv = x_ref[...], g_ref[...]
        hb = _rms_fwd(xv, gv).astype(BF16)
        dh = None
        for pc_ref, wt_ref, dw_ref, n in zip(pc_refs, wt_refs, dw_refs, widths):
            d = _dot(pc_ref[...], wt_ref[...])
            dh = d if dh is None else dh + d
            ch = _tile(n, COL_CHUNK)
            for n0 in range(0, n, ch):
                dw_ref[:, n0:n0 + ch] += _dot_tn(hb, pc_ref[:, n0:n0 + ch])
        dxn, dgp = _rms_bwd(xv, gv, dh)
        dg_ref[...] += dgp
        dx_ref[...] = dres_ref[...] + dxn

    return _row_call(name, body, T, tm, list(pieces) + [x, dres], [g] + list(wts), [(D, F32)],
                     acc_outs=[((D, n), F32) for n in widths] + [((1, D), F32)])


def _conv_bwd(dy, b, c, u, zc, cw):
    T, D = dy.shape
    tm = _tile(T, ROW_TILE_BWD)

    def body(dy_ref, b_ref, c_ref, u_ref, zc_ref, cw_ref, db_ref, dc_ref, du_ref, dcw_ref, head_ref):
        @pl.when(pl.program_id(0) == 0)
        def _():
            head_ref[...] = jnp.zeros_like(head_ref)
            dcw_ref[...] = jnp.zeros_like(dcw_ref)

        dyv, cv, uv = dy_ref[...], c_ref[...], u_ref[...]
        db_ref[...] = (dyv * zc_ref[...]).astype(BF16)
        dzc = dyv * b_ref[...]
        row = lax.broadcasted_iota(jnp.int32, (tm, 1), 0)
        h0, h1 = head_ref[0:1, :], head_ref[1:2, :]
        d1 = jnp.where(row == tm - 1, h0, pltpu.roll(dzc, tm - 1, axis=0))
        d2 = jnp.where(row == tm - 1, h1, jnp.where(row == tm - 2, h0, pltpu.roll(dzc, tm - 2, axis=0)))
        dz = cw_ref[2:3, :] * dzc + cw_ref[1:2, :] * d1 + cw_ref[0:1, :] * d2
        dc_ref[...] = (dz * uv).astype(BF16)
        du_ref[...] = (dz * cv).astype(BF16)
        z = cv * uv
        dcw_ref[0:1, :] += jnp.sum(d2 * z, axis=0, keepdims=True)
        dcw_ref[1:2, :] += jnp.sum(d1 * z, axis=0, keepdims=True)
        dcw_ref[2:3, :] += jnp.sum(dzc * z, axis=0, keepdims=True)
        head_ref[...] = dzc[0:8, :]

    return _row_call("conv_bwd", body, T, tm, [dy, b, c, u, zc], [cw], [(D, BF16), (D, BF16), (D, BF16)],
                     acc_outs=[((8, D), F32)], scratch=[pltpu.VMEM((8, D), F32)], reverse=True)


def _ple_bwd(dres, x, p, gl, pe, g4, g5, wgt):
    T, D = x.shape
    P = p.shape[1]
    tm = _tile(T, ROW_TILE_BWD)

    def body(dres_ref, x_ref, p_ref, gl_ref, pe_ref, g4_ref, g5_ref, wgt_ref,
             dx_ref, dwp_ref, dwg_ref, dg4_ref, dg5_ref):
        @pl.when(pl.program_id(0) == 0)
        def _():
            for r in (dwp_ref, dwg_ref, dg4_ref, dg5_ref):
                r[...] = jnp.zeros_like(r)

        dr, xv, pe_v = dres_ref[...], x_ref[...], pe_ref[...]
        gate = 1.0 / (1.0 + jnp.exp(-gl_ref[...]))
        de, dg5p = _rms_bwd(pe_v * gate, g5_ref[...], dr)
        dg5_ref[...] += dg5p
        dpe = (de * gate).astype(BF16)
        dgl = (de * pe_v * gate * (1.0 - gate)).astype(BF16)
        dwp_ref[...] += _dot_tn(p_ref[...].astype(BF16), dpe)
        g4v = g4_ref[...]
        dwg_ref[...] += _dot_tn(_rms_fwd(xv, g4v).astype(BF16), dgl)
        dxn, dg4p = _rms_bwd(xv, g4v, _dot(dgl, wgt_ref[...]))
        dg4_ref[...] += dg4p
        dx_ref[...] = dr + dxn

    return _row_call("ple_bwd", body, T, tm, [dres, x, p, gl, pe], [g4, g5, wgt], [(D, F32)],
                     acc_outs=[((P, D), F32), ((D, D), F32), ((1, D), F32), ((1, D), F32)])


def _scan_lanes(v, reverse):
    n = v.shape[1]
    lane = lax.broadcasted_iota(jnp.int32, v.shape, 1)
    s = 1
    while s < n:
        if reverse:
            v = v + jnp.where(lane < n - s, pltpu.roll(v, n - s, axis=1), 0.0)
        else:
            v = v + jnp.where(lane >= s, pltpu.roll(v, s, axis=1), 0.0)
        s *= 2
    return v


def _gate_cumsum(lf_t):
    def body(lf_ref, *piece_refs):
        rest = _scan_lanes(lf_ref[...], reverse=False)
        for r in piece_refs:
            piece = rest.astype(BF16)
            r[...] = piece
            rest = rest - piece.astype(F32)

    return pl.pallas_call(body, name="gate_cumsum", out_shape=[jax.ShapeDtypeStruct(lf_t.shape, BF16)] * N_AUG,
                          compiler_params=pltpu.CompilerParams(vmem_limit_bytes=VMEM_LIMIT))(lf_t)


def _gate_bwd(drow_t, dcol_t, fl_t):
    H = fl_t.shape[0]

    def body(dr_ref, dc_ref, fl_ref, dfl_ref, dbf_ref):
        dlf = _scan_lanes(dr_ref[...] - dc_ref[...], reverse=True)
        dfl = dlf * (1.0 / (1.0 + jnp.exp(fl_ref[...])))
        dfl_ref[...] = dfl
        dbf_ref[...] = jnp.sum(dfl, axis=1, keepdims=True)

    return pl.pallas_call(
        body, name="gate_bwd",
        out_shape=(jax.ShapeDtypeStruct(fl_t.shape, F32), jax.ShapeDtypeStruct((H, 1), F32)),
        compiler_params=pltpu.CompilerParams(vmem_limit_bytes=VMEM_LIMIT))(drow_t, dcol_t, fl_t)


def _aug_operands(pieces, H, D):
    T = pieces[0].shape[1]
    dh = D // H
    one = jnp.ones((H, T), BF16)
    qa = jnp.stack(list(pieces) + [one] * N_AUG, axis=-1)
    ka = jnp.stack([one] * N_AUG + [-pc for pc in pieces], axis=-1)

    def place(a):
        a = jnp.pad(a, ((0, 0), (0, 0), (0, dh - 2 * N_AUG))).reshape(H // 2, 2, T, dh)
        return jnp.transpose(a[:, ::-1], (2, 0, 1, 3)).reshape(T, D)

    return place(qa), place(ka)


def _strip_kind(koff, k0, ksz, q0, qsz):
    if koff is None or koff + k0 + ksz - 1 <= q0:
        return "full"
    return "skip" if koff + k0 > q0 + qsz - 1 else "partial"


def _score_strip(km, qm, koff, k0, ksz, q0, qsz, kind):
    st = _dot_nt(km[k0:k0 + ksz, :], qm[q0:q0 + qsz, :])
    if kind == "partial":
        kpos = koff + k0 + lax.broadcasted_iota(jnp.int32, (ksz, qsz), 0)
        st = jnp.where(kpos <= q0 + lax.broadcasted_iota(jnp.int32, (ksz, qsz), 1), st, NEG_INF)
    return st


def _fold8(v, op):
    rows, n = v.shape
    v3 = v.reshape(rows // 8, 8, n)
    out = v3[0]
    for r in range(1, rows // 8):
        out = op(out, v3[r])
    return out


def _causal_tables(nbq, r, q_outer):
    a, b = [], []
    for o in range(nbq if q_outer else nbq * r):
        inner = range((o + 1) * r) if q_outer else range(o // r, nbq)
        for n in inner:
            a.append(o)
            b.append(n)
    return jnp.asarray(np.array(a, np.int32)), jnp.asarray(np.array(b, np.int32))


def _attn_tiles(T):
    tq = _tile(T, ATTN_Q_TILE)
    tk = _tile(tq, ATTN_K_TILE)
    return tq, tk, _tile(tk, ATTN_KEY_STRIP), _tile(tq, ATTN_QUERY_STRIP)


def _flash_fwd(q, k, v, qaug, kaug, H):
    T, D = q.shape
    Hp = H // 2
    W = D // Hp
    dh = W // 2
    tq, tk, ksz, qsz = _attn_tiles(T)
    r = tq // tk
    ii, jj = _causal_tables(T // tq, r, q_outer=True)
    n_steps = int(ii.shape[0])

    def body(ii_ref, jj_ref, q_ref, k_ref, v_ref, qa_ref, ka_ref, o_ref, lsea_ref, lseb_ref,
             qm_s, m_s, l_s, acc_s, p_s):
        n = pl.program_id(1)
        i, j = ii_ref[n], jj_ref[n]
        d = j - i * r
        in_a = lax.broadcasted_iota(jnp.int32, (1, W), 1) < dh
        top = lax.broadcasted_iota(jnp.int32, (W, 1), 0) < dh

        @pl.when(j == 0)
        def _():
            qv, qa = q_ref[...], qa_ref[...]
            qm_s[0] = jnp.where(in_a, qv, qa)
            qm_s[1] = jnp.where(in_a, qa, qv)
            m_s[...] = jnp.full(m_s.shape, NEG_INF, F32)
            l_s[...] = jnp.zeros_like(l_s)
            acc_s[...] = jnp.zeros_like(acc_s)

        def step(koff):
            kv, ka, vv = k_ref[...], ka_ref[...], v_ref[...]
            kms = (jnp.where(in_a, kv, ka), jnp.where(in_a, ka, kv))
            vt = vv.T
            alphas = {}

            def softmax_stage(hh, q0):
                cols = slice(q0, q0 + qsz)
                kinds = [(k0, _strip_kind(koff, k0, ksz, q0, qsz)) for k0 in range(0, tk, ksz)]
                if all(kind == "skip" for _, kind in kinds):
                    return False
                strips, part = {}, None
                for k0, kind in kinds:
                    if kind != "skip":
                        strips[k0] = _score_strip(kms[hh], qm_s[hh], koff, k0, ksz, q0, qsz, kind)
                        p8 = _fold8(strips[k0], jnp.maximum)
                        part = p8 if part is None else jnp.maximum(part, p8)
                m_prev = m_s[hh, :, cols]
                m_new = jnp.maximum(m_prev, jnp.max(part, axis=0, keepdims=True))
                l8 = None
                for k0, kind in kinds:
                    if kind == "skip":
                        p_s[hh, k0:k0 + ksz, cols] = jnp.zeros((ksz, qsz), BF16)
                        continue
                    pt = jnp.exp(strips[k0] - m_new)
                    s8 = _fold8(pt, jnp.add)
                    l8 = s8 if l8 is None else l8 + s8
                    p_s[hh, k0:k0 + ksz, cols] = pt.astype(BF16)
                alpha = jnp.exp(m_prev - m_new)
                l_s[hh, :, cols] = alpha * l_s[hh, :, cols] + jnp.sum(l8, axis=0, keepdims=True)
                m_s[hh, :, cols] = m_new
                alphas[(hh, q0)] = alpha
                return True

            def value_stage(hh, q0):
                cols, rows = slice(q0, q0 + qsz), slice(hh * dh, (hh + 1) * dh)
                acc_s[rows, cols] = acc_s[rows, cols] * alphas[(hh, q0)] + _dot(vt[rows, :], p_s[hh, :, cols])

            units = [(hh, q0) for q0 in range(0, tq, qsz) for hh in range(2)]
            pending = None
            for unit in units:
                live = softmax_stage(*unit)
                if pending is not None:
                    value_stage(*pending)
                pending = unit if live else None
            if pending is not None:
                value_stage(*pending)

        @pl.when(d < 0)
        def _():
            step(None)

        for dd in range(r):
            @pl.when(d == dd)
            def _():
                step(dd * tk)

        @pl.when(d == r - 1)
        def _():
            inv = jnp.where(top, 1.0 / l_s[0], 1.0 / l_s[1])
            o_ref[...] = (acc_s[...] * inv).T.astype(BF16)
            lsea_ref[...] = m_s[0] + jnp.log(l_s[0])
            lseb_ref[...] = m_s[1] + jnp.log(l_s[1])

    qspec = pl.BlockSpec((tq, W), lambda hp, n, ii, jj: (ii[n], hp))
    kspec = pl.BlockSpec((tk, W), lambda hp, n, ii, jj: (jj[n], hp))
    rspec = pl.BlockSpec((None, 1, tq), lambda hp, n, ii, jj: (hp, 0, ii[n]))
    grid_spec = pltpu.PrefetchScalarGridSpec(
        num_scalar_prefetch=2, grid=(Hp, n_steps),
        in_specs=[qspec, kspec, kspec, qspec, kspec],
        out_specs=[qspec, rspec, rspec],
        scratch_shapes=[pltpu.VMEM((2, tq, W), BF16), pltpu.VMEM((2, 1, tq), F32), pltpu.VMEM((2, 1, tq), F32),
                        pltpu.VMEM((W, tq), F32), pltpu.VMEM((2, tk, tq), BF16)],
    )
    return pl.pallas_call(
        body, name="flash_fwd", grid_spec=grid_spec,
        out_shape=(jax.ShapeDtypeStruct((T, D), BF16), jax.ShapeDtypeStruct((Hp, 1, T), F32),
                   jax.ShapeDtypeStruct((Hp, 1, T), F32)),
        compiler_params=_params(2),
    )(ii, jj, q, k, v, qaug, kaug)


def _attn_delta(do, o, head_sel):
    T, D = do.shape
    H = head_sel.shape[0]
    tm = _tile(T, ROW_TILE_FWD)

    def body(do_ref, o_ref, sel_ref, out_ref):
        prod = do_ref[...].astype(F32) * o_ref[...].astype(F32)
        out_ref[...] = lax.dot_general(sel_ref[...], prod, (((1,), (1,)), ((), ())),
                                       precision=lax.Precision.HIGHEST, preferred_element_type=F32)

    return pl.pallas_call(
        body, name="attn_delta", grid=(T // tm,),
        in_specs=[pl.BlockSpec((tm, D), lambda i: (i, 0)), pl.BlockSpec((tm, D), lambda i: (i, 0)),
                  pl.BlockSpec((H, D), lambda i: (0, 0))],
        out_specs=pl.BlockSpec((H, tm), lambda i: (0, i)),
        out_shape=jax.ShapeDtypeStruct((H, T), F32), compiler_params=_params())(do, o, head_sel)


def _flash_bwd(q, k, v, qaug, kaug, do, lse_a, lse_b, delta, qscale):
    T, D = q.shape
    Hp = lse_a.shape[0]
    W = D // Hp
    dh = W // 2
    tq, tk, ksz, qsz = _attn_tiles(T)
    r = tq // tk
    nbq = T // tq
    jj, ii = _causal_tables(nbq, r, q_outer=False)
    n_steps = int(ii.shape[0])
    row_row = dh
    col_row = dh + N_AUG

    def body(jj_ref, ii_ref, q_ref, k_ref, v_ref, qa_ref, ka_ref, do_ref, lsea_ref, lseb_ref, dla_ref, dlb_ref,
             dqt_ref, dkt_ref, dvt_ref, dcola_ref, dcolb_ref, drowa_ref, drowb_ref,
             dqt_acc, dk_acc, dv_acc, *rot):
        n = pl.program_id(1)
        i, j = ii_ref[n], jj_ref[n]
        d = j - i * r
        in_a = lax.broadcasted_iota(jnp.int32, (1, W), 1) < dh
        top = lax.broadcasted_iota(jnp.int32, (W, 1), 0) < dh

        @pl.when(n == 0)
        def _():
            dqt_acc[...] = jnp.zeros_like(dqt_acc)

        @pl.when(d >= 0)
        def _():
            dk_acc[...] = jnp.zeros_like(dk_acc)
            dv_acc[...] = jnp.zeros_like(dv_acc)

        def step(koff):
            qv, qa, kv, ka = q_ref[...], qa_ref[...], k_ref[...], ka_ref[...]
            vv, dov = v_ref[...], do_ref[...]
            zb = jnp.zeros_like(dov)
            kms = (jnp.where(in_a, kv, ka), jnp.where(in_a, ka, kv))
            qms = (jnp.where(in_a, qv, qa), jnp.where(in_a, qa, qv))
            doms = (jnp.where(in_a, dov, zb), jnp.where(in_a, zb, dov))
            def own_first(a, hh):
                return a[:dh + AUG_ROWS] if hh == 0 else jnp.concatenate([a[dh:], a[:AUG_ROWS]], axis=0)

            kxs = tuple(own_first(kms[hh].T, hh) for hh in range(2))
            qxs = tuple(own_first(qms[hh].T, hh) for hh in range(2))
            dot = dov.T
            dxs = (dot[:dh], dot[dh:])
            lses, dls = (lsea_ref[...], lseb_ref[...]), (dla_ref[...], dlb_ref[...])
            dvs, dks = ([], []), ([], [])

            def score_stage(u, hh, q0):
                p_s, ds_s = rot[2 * (u % ATTN_ROTATE)], rot[2 * (u % ATTN_ROTATE) + 1]
                cols = slice(q0, q0 + qsz)
                kinds = [(k0, _strip_kind(koff, k0, ksz, q0, qsz)) for k0 in range(0, tk, ksz)]
                if all(kind == "skip" for _, kind in kinds):
                    return False
                for k0, kind in kinds:
                    if kind == "skip":
                        p_s[k0:k0 + ksz, :] = jnp.zeros((ksz, qsz), BF16)
                        ds_s[k0:k0 + ksz, :] = jnp.zeros((ksz, qsz), BF16)
                        continue
                    st = _score_strip(kms[hh], qms[hh], koff, k0, ksz, q0, qsz, kind)
                    pt = jnp.exp(st - lses[hh][:, cols])
                    dst = pt * (_dot_nt(vv[k0:k0 + ksz, :], doms[hh][cols, :]) - dls[hh][:, cols])
                    p_s[k0:k0 + ksz, :] = pt.astype(BF16)
                    ds_s[k0:k0 + ksz, :] = dst.astype(BF16)
                return True

            def grad_stage(u, hh, q0):
                p_s, ds_s = rot[2 * (u % ATTN_ROTATE)], rot[2 * (u % ATTN_ROTATE) + 1]
                cols = slice(q0, q0 + qsz)
                ptb, dsb = p_s[...], ds_s[...]
                dvs[hh].append(_dot_nt(dxs[hh][:, cols], ptb))
                dks[hh].append(_dot_nt(qxs[hh][:, cols], dsb))
                dqt_acc[i, hh, :, cols] += _dot(kxs[hh], dsb)

            order = [(hh, q0) for q0 in range(0, tq, qsz) for hh in range(2)]
            units = [(u, hh, q0) for u, (hh, q0) in enumerate(order)]
            pending = None
            for unit in units:
                live = score_stage(*unit)
                if pending is not None:
                    grad_stage(*pending)
                pending = unit if live else None
            if pending is not None:
                grad_stage(*pending)
            for hh in range(2):
                dv_acc[hh] += sum(dvs[hh][1:], dvs[hh][0])
                dk_acc[hh] += sum(dks[hh][1:], dks[hh][0])

        @pl.when(d < 0)
        def _():
            step(None)

        for dd in range(r):
            @pl.when(d == dd)
            def _():
                step(dd * tk)

        @pl.when(i == nbq - 1)
        def _():
            dkt_ref[...] = jnp.concatenate([dk_acc[0, :dh], dk_acc[1, :dh]], axis=0).astype(BF16)
            dvt_ref[...] = jnp.concatenate([dv_acc[0], dv_acc[1]], axis=0).astype(BF16)
            dcola_ref[...] = dk_acc[0, col_row:col_row + 1, :]
            dcolb_ref[...] = dk_acc[1, col_row:col_row + 1, :]

        @pl.when(n == n_steps - 1)
        def _():
            for b in range(nbq):
                cols = slice(b * tq, (b + 1) * tq)
                both = jnp.concatenate([dqt_acc[b, 0, :dh], dqt_acc[b, 1, :dh]], axis=0)
                dqt_ref[:, cols] = (both * qscale).astype(BF16)
                drowa_ref[:, cols] = dqt_acc[b, 0, row_row:row_row + 1, :]
                drowb_ref[:, cols] = dqt_acc[b, 1, row_row:row_row + 1, :]

    qspec = pl.BlockSpec((tq, W), lambda hp, n, jj, ii: (ii[n], hp))
    kspec = pl.BlockSpec((tk, W), lambda hp, n, jj, ii: (jj[n], hp))
    ktspec = pl.BlockSpec((W, tk), lambda hp, n, jj, ii: (hp, jj[n]))
    pair_row = pl.BlockSpec((None, 1, tq), lambda hp, n, jj, ii: (hp, 0, ii[n]))
    key_row = pl.BlockSpec((None, 1, tk), lambda hp, n, jj, ii: (hp, 0, jj[n]))
    whole_row = pl.BlockSpec((None, 1, T), lambda hp, n, jj, ii: (hp, 0, 0))
    grid_spec = pltpu.PrefetchScalarGridSpec(
        num_scalar_prefetch=2, grid=(Hp, n_steps),
        in_specs=[
            qspec, kspec, kspec, qspec, kspec, qspec, pair_row, pair_row,
            pl.BlockSpec((None, 1, tq), lambda hp, n, jj, ii: (2 * hp, 0, ii[n])),
            pl.BlockSpec((None, 1, tq), lambda hp, n, jj, ii: (2 * hp + 1, 0, ii[n])),
        ],
        out_specs=[
            pl.BlockSpec((W, T), lambda hp, n, jj, ii: (hp, 0)),
            ktspec, ktspec, key_row, key_row, whole_row, whole_row,
        ],
        scratch_shapes=[pltpu.VMEM((nbq, 2, dh + AUG_ROWS, tq), F32), pltpu.VMEM((2, dh + AUG_ROWS, tk), F32),
                        pltpu.VMEM((2, dh, tk), F32)] + [pltpu.VMEM((tk, qsz), BF16)] * (2 * ATTN_ROTATE),
    )
    return pl.pallas_call(
        body, name="flash_bwd", grid_spec=grid_spec,
        out_shape=(jax.ShapeDtypeStruct((D, T), BF16), jax.ShapeDtypeStruct((D, T), BF16),
                   jax.ShapeDtypeStruct((D, T), BF16), jax.ShapeDtypeStruct((Hp, 1, T), F32),
                   jax.ShapeDtypeStruct((Hp, 1, T), F32),
                   jax.ShapeDtypeStruct((Hp, 1, T), F32), jax.ShapeDtypeStruct((Hp, 1, T), F32)),
        compiler_params=_params(2),
    )(jj, ii, q, k, v, qaug, kaug, do, lse_a, lse_b, delta, delta)


HBM_SPEC = pl.BlockSpec(memory_space=pltpu.HBM)


def _mesh_pos():
    return lax.axis_index("x"), lax.axis_index("y"), lax.axis_index("c")


def _all_gather_shards(shards):
    n = len(shards)

    def body(*refs):
        in_refs, out_refs = refs[:n], refs[n:2 * n]
        send1, recv1, send2, recv2, lsem = refs[2 * n:]
        x, y, c = _mesh_pos()
        me = 2 * x + y
        chips = [(1 - x, y), (x, 1 - y), (1 - x, 1 - y)]
        local = [pltpu.make_async_copy(in_refs[t], out_refs[t].at[me], lsem.at[t]) for t in range(n)]
        for cp in local:
            cp.start()

        def half(t, chip_idx, pc):
            hr = shards[t].shape[0] // 2
            return out_refs[t].at[chip_idx, pl.ds(pc * hr, hr), :]

        first = []
        for t in range(n):
            hr = shards[t].shape[0] // 2
            for kk, (cx, cy) in enumerate(chips):
                first.append(pltpu.make_async_remote_copy(
                    src_ref=in_refs[t].at[pl.ds(c * hr, hr), :], dst_ref=half(t, me, c),
                    send_sem=send1.at[3 * t + kk], recv_sem=recv1.at[3 * t + kk],
                    device_id=(cx, cy, c), device_id_type=MESH))
        for cp in first:
            cp.start()
        passed = []
        for t in range(n):
            for kk, (cx, cy) in enumerate(chips):
                src_chip = 2 * cx + cy
                landed = half(t, src_chip, c)
                pltpu.make_async_remote_copy(
                    src_ref=landed, dst_ref=landed, send_sem=send1.at[3 * t + kk], recv_sem=recv1.at[3 * t + kk],
                    device_id=(cx, cy, c), device_id_type=MESH).wait_recv()
                fwd = pltpu.make_async_remote_copy(
                    src_ref=landed, dst_ref=landed, send_sem=send2.at[3 * t + kk], recv_sem=recv2.at[3 * t + kk],
                    device_id=(x, y, 1 - c), device_id_type=MESH)
                fwd.start()
                passed.append(fwd)
        for t in range(n):
            for kk, (cx, cy) in enumerate(chips):
                other = half(t, 2 * cx + cy, 1 - c)
                pltpu.make_async_remote_copy(
                    src_ref=other, dst_ref=other, send_sem=send2.at[3 * t + kk], recv_sem=recv2.at[3 * t + kk],
                    device_id=(x, y, 1 - c), device_id_type=MESH).wait_recv()
        for cp in first + passed:
            cp.wait_send()
        for cp in local:
            cp.wait()

    return pl.pallas_call(
        body, name="weights_all_gather",
        out_shape=[jax.ShapeDtypeStruct((N_CHIPS,) + s.shape, s.dtype) for s in shards],
        in_specs=[HBM_SPEC] * n, out_specs=[HBM_SPEC] * n,
        scratch_shapes=[pltpu.SemaphoreType.DMA((3 * n,)), pltpu.SemaphoreType.DMA((3 * n,)),
                        pltpu.SemaphoreType.DMA((3 * n,)), pltpu.SemaphoreType.DMA((3 * n,)),
                        pltpu.SemaphoreType.DMA((n,))],
    )(*shards)


def _sibling_swap_halves(g):
    _, M, C = g.shape
    hr = M // 2

    def body(g_ref, r_ref, ssem, rsem):
        x, y, c = _mesh_pos()
        cps = [pltpu.make_async_remote_copy(
            src_ref=g_ref.at[s, pl.ds((1 - c) * hr, hr), :], dst_ref=r_ref.at[s],
            send_sem=ssem.at[s], recv_sem=rsem.at[s], device_id=(x, y, 1 - c), device_id_type=MESH)
            for s in range(N_CHIPS)]
        for cp in cps:
            cp.start()
        for cp in cps:
            cp.wait()

    return pl.pallas_call(
        body, name="grads_sibling_swap", out_shape=jax.ShapeDtypeStruct((N_CHIPS, hr, C), F32),
        in_specs=[HBM_SPEC], out_specs=HBM_SPEC,
        scratch_shapes=[pltpu.SemaphoreType.DMA((N_CHIPS,)), pltpu.SemaphoreType.DMA((N_CHIPS,))],
    )(g)


def _pair_add(g, r, c_idx):
    _, M, C = g.shape
    hr = M // 2
    tr = PACK_ROW_TILE
    nbk = hr // tr

    def body(c_ref, g_ref, r_ref, o_ref):
        o_ref[...] = (g_ref[...] + r_ref[...]).astype(BF16)

    grid_spec = pltpu.PrefetchScalarGridSpec(
        num_scalar_prefetch=1, grid=(N_CHIPS, nbk),
        in_specs=[pl.BlockSpec((None, tr, C), lambda s, i, c: (s, c[0] * nbk + i, 0)),
                  pl.BlockSpec((None, tr, C), lambda s, i, c: (s, i, 0))],
        out_specs=pl.BlockSpec((None, tr, C), lambda s, i, c: (s, i, 0)),
    )
    return pl.pallas_call(body, name="grads_pair_add", grid_spec=grid_spec,
                          out_shape=jax.ShapeDtypeStruct((N_CHIPS, hr, C), BF16),
                          compiler_params=_params(2))(c_idx, g, r)


def _chip_scatter(pp):
    def body(p_ref, r_ref, ssem, rsem, lsem):
        x, y, c = _mesh_pos()
        me = 2 * x + y
        chips = [(1 - x, y), (x, 1 - y), (1 - x, 1 - y)]
        own = pltpu.make_async_copy(p_ref.at[me], r_ref.at[me], lsem)
        own.start()
        cps = [pltpu.make_async_remote_copy(
            src_ref=p_ref.at[2 * cx + cy], dst_ref=r_ref.at[me], send_sem=ssem.at[kk], recv_sem=rsem.at[kk],
            device_id=(cx, cy, c), device_id_type=MESH) for kk, (cx, cy) in enumerate(chips)]
        for cp in cps:
            cp.start()
        for kk, (cx, cy) in enumerate(chips):
            got = r_ref.at[2 * cx + cy]
            pltpu.make_async_remote_copy(
                src_ref=got, dst_ref=got, send_sem=ssem.at[kk], recv_sem=rsem.at[kk],
                device_id=(cx, cy, c), device_id_type=MESH).wait_recv()
        for cp in cps:
            cp.wait_send()
        own.wait()

    return pl.pallas_call(
        body, name="grads_chip_scatter", out_shape=jax.ShapeDtypeStruct(pp.shape, pp.dtype),
        in_specs=[HBM_SPEC], out_specs=HBM_SPEC,
        scratch_shapes=[pltpu.SemaphoreType.DMA((3,)), pltpu.SemaphoreType.DMA((3,)), pltpu.SemaphoreType.DMA],
    )(pp)


def _chip_sum(r, c_idx):
    _, hr, C = r.shape
    tr = PACK_ROW_TILE
    nbk = hr // tr

    def body(c_ref, r_ref, o_ref):
        r0, r1, r2, r3 = (r_ref[s].astype(F32) for s in range(N_CHIPS))
        o_ref[...] = ((r0 + r1) + r2) + r3

    grid_spec = pltpu.PrefetchScalarGridSpec(
        num_scalar_prefetch=1, grid=(nbk,),
        in_specs=[pl.BlockSpec((N_CHIPS, tr, C), lambda i, c: (0, i, 0))],
        out_specs=pl.BlockSpec((tr, C), lambda i, c: (c[0] * nbk + i, 0)),
    )
    return pl.pallas_call(body, name="grads_chip_sum", grid_spec=grid_spec,
                          out_shape=jax.ShapeDtypeStruct((2 * hr, C), F32), compiler_params=_params())(c_idx, r)


def _sibling_join(buf):
    hr = buf.shape[0] // 2

    def body(in_ref, o_ref, ssem, rsem):
        del in_ref
        x, y, c = _mesh_pos()
        mine = o_ref.at[pl.ds(c * hr, hr), :]
        cp = pltpu.make_async_remote_copy(src_ref=mine, dst_ref=mine, send_sem=ssem, recv_sem=rsem,
                                          device_id=(x, y, 1 - c), device_id_type=MESH)
        cp.start()
        theirs = o_ref.at[pl.ds((1 - c) * hr, hr), :]
        pltpu.make_async_remote_copy(src_ref=theirs, dst_ref=theirs, send_sem=ssem, recv_sem=rsem,
                                     device_id=(x, y, 1 - c), device_id_type=MESH).wait_recv()
        cp.wait_send()

    return pl.pallas_call(
        body, name="grads_sibling_join", out_shape=jax.ShapeDtypeStruct(buf.shape, F32),
        in_specs=[HBM_SPEC], out_specs=HBM_SPEC, input_output_aliases={0: 0},
        scratch_shapes=[pltpu.SemaphoreType.DMA, pltpu.SemaphoreType.DMA],
    )(buf)


def _adamw(w, g, m, v):
    M, C = w.shape
    tr = _tile(M, PACK_ROW_TILE)

    def body(w_ref, g_ref, m_ref, v_ref, d_ref, mo_ref, vo_ref):
        gv = g_ref[...]
        mn = ADAM_B1 * m_ref[...] + (1.0 - ADAM_B1) * gv
        vn = ADAM_B2 * v_ref[...] + (1.0 - ADAM_B2) * (gv * gv)
        m_hat = mn / (1.0 - ADAM_B1 ** ADAM_STEP)
        v_hat = vn / (1.0 - ADAM_B2 ** ADAM_STEP)
        d_ref[...] = -ADAM_LR * (m_hat / (jnp.sqrt(v_hat) + ADAM_EPS) + ADAM_WD * w_ref[...])
        mo_ref[...] = mn
        vo_ref[...] = vn

    spec = pl.BlockSpec((tr, C), lambda i: (i, 0))
    return pl.pallas_call(
        body, name="adamw", grid=(M // tr,), in_specs=[spec] * 4, out_specs=[spec] * 3,
        out_shape=[jax.ShapeDtypeStruct((M, C), F32)] * 3, compiler_params=_params())(w, g, m, v)


WEIGHT_NAMES = ("norm_g", "w_attn_in", "b_forget", "w_attn_out", "w_conv_in", "conv_w", "w_conv_out",
                "w_mlp_up", "w_mlp_down", "w_ple_proj", "w_ple_gate")
COL_SHARDED = ("norm_g", "w_attn_in", "w_conv_in", "conv_w", "w_mlp_up", "w_ple_proj")
ROW_SHARDED = ("w_attn_out", "w_conv_out", "w_mlp_down", "w_ple_gate")


def _unshard(name, gathered, shard_shape):
    a = gathered.reshape((N_CHIPS,) + tuple(shard_shape))
    if name in COL_SHARDED:
        a = jnp.moveaxis(a, 0, -2)
        return a.reshape(a.shape[:-2] + (N_CHIPS * shard_shape[-1],))
    a = jnp.moveaxis(a, 0, 1)
    return a.reshape((shard_shape[0], N_CHIPS * shard_shape[1], shard_shape[2]))


def _to_shard_major(name, full):
    if name == "b_forget":
        return jnp.broadcast_to(full.reshape(1, -1), (N_CHIPS, full.size))
    if name in COL_SHARDED:
        a = full.reshape(full.shape[:-1] + (N_CHIPS, full.shape[-1] // N_CHIPS))
        a = jnp.moveaxis(a, -2, 0)
    else:
        a = full.reshape((full.shape[0], N_CHIPS, full.shape[1] // N_CHIPS, full.shape[2]))
        a = jnp.moveaxis(a, 1, 0)
    return a.reshape(N_CHIPS, -1)


def _pack_layout(shard_shapes):
    offs, row = {}, 0
    for name in WEIGHT_NAMES:
        n = int(np.prod(shard_shapes[name]))
        rows = -(-n // PACK_COLS)
        offs[name] = (row, rows, n)
        row += rows
    total = -(-row // PACK_ROW_ALIGN) * PACK_ROW_ALIGN
    return offs, total


def _pack(flat_by_name, offs, total_rows, lead=()):
    parts, row = [], 0
    for name in WEIGHT_NAMES:
        _, rows, n = offs[name]
        a = flat_by_name[name].astype(F32)
        pad = rows * PACK_COLS - n
        if pad:
            a = jnp.pad(a, [(0, 0)] * len(lead) + [(0, pad)])
        parts.append(a.reshape(tuple(lead) + (rows, PACK_COLS)))
        row += rows
    if total_rows > row:
        parts.append(jnp.zeros(tuple(lead) + (total_rows - row, PACK_COLS), F32))
    return jnp.concatenate(parts, axis=len(lead))


def _unpack(packed, offs, shard_shapes):
    out = {}
    for name in WEIGHT_NAMES:
        row, rows, n = offs[name]
        out[name] = packed[row:row + rows].reshape(-1)[:n].reshape(shard_shapes[name])
    return out


def kernel(x, p, norm_g, w_attn_in, b_forget, w_attn_out, w_conv_in, conv_w, w_conv_out, w_mlp_up, w_mlp_down, w_ple_proj, w_ple_gate, loss_target, m_norm_g, m_w_attn_in, m_b_forget, m_w_attn_out, m_w_conv_in, m_conv_w, m_w_conv_out, m_w_mlp_up, m_w_mlp_down, m_w_ple_proj, m_w_ple_gate, v_norm_g, v_w_attn_in, v_b_forget, v_w_attn_out, v_w_conv_in, v_conv_w, v_w_conv_out, v_w_mlp_up, v_w_mlp_down, v_w_ple_proj, v_w_ple_gate):
    w_local = dict(norm_g=norm_g, w_attn_in=w_attn_in, b_forget=b_forget, w_attn_out=w_attn_out,
                   w_conv_in=w_conv_in, conv_w=conv_w, w_conv_out=w_conv_out, w_mlp_up=w_mlp_up,
                   w_mlp_down=w_mlp_down, w_ple_proj=w_ple_proj, w_ple_gate=w_ple_gate)
    m_local = dict(norm_g=m_norm_g, w_attn_in=m_w_attn_in, b_forget=m_b_forget, w_attn_out=m_w_attn_out,
                   w_conv_in=m_w_conv_in, conv_w=m_conv_w, w_conv_out=m_w_conv_out, w_mlp_up=m_w_mlp_up,
                   w_mlp_down=m_w_mlp_down, w_ple_proj=m_w_ple_proj, w_ple_gate=m_w_ple_gate)
    v_local = dict(norm_g=v_norm_g, w_attn_in=v_w_attn_in, b_forget=v_b_forget, w_attn_out=v_w_attn_out,
                   w_conv_in=v_w_conv_in, conv_w=v_conv_w, w_conv_out=v_w_conv_out, w_mlp_up=v_w_mlp_up,
                   w_mlp_down=v_w_mlp_down, w_ple_proj=v_w_ple_proj, w_ple_gate=v_w_ple_gate)
    shard_shapes = {k: tuple(a.shape) for k, a in w_local.items()}

    xs = x[0]
    target = loss_target[0]
    T, D = xs.shape
    depth = p.shape[0]
    H = b_forget.shape[1]
    qscale = float(D // H) ** -0.5
    head_sel = (jnp.arange(D)[None, :] // (D // H) == jnp.arange(H)[:, None]).astype(F32)

    big = [n for n in WEIGHT_NAMES if n not in ("norm_g", "conv_w", "b_forget")]
    small = jnp.concatenate([norm_g.reshape(-1), conv_w.reshape(-1)])
    n_small = small.shape[0]
    small_rows = -(-n_small // (LANES * 16)) * 16
    small = jnp.pad(small, (0, small_rows * LANES - n_small)).reshape(small_rows, LANES)
    shards = [w_local[n].astype(BF16).reshape(-1, shard_shapes[n][-1]) for n in big] + [small]
    gathered = _all_gather_shards(shards)
    full = {n: _unshard(n, g, shard_shapes[n]) for n, g in zip(big, gathered[:-1])}
    gs = gathered[-1].reshape(N_CHIPS, -1)
    full["norm_g"] = _unshard("norm_g", gs[:, :norm_g.size], shard_shapes["norm_g"])
    full["conv_w"] = _unshard("conv_w", gs[:, norm_g.size:n_small], shard_shapes["conv_w"])
    gains = full["norm_g"]

    def gain(i, k):
        return gains[i, k].reshape(1, D)

    def taps(j):
        return jnp.pad(full["conv_w"][j], ((0, 5), (0, 0)))

    saved = []
    h = xs
    for i in range(depth):
        j = i // 2
        s = {"x0": h}
        if i % 2 == 0:
            w_in = full["w_attn_in"][j]
            wqkv = w_in[:, :3 * D]
            wf = jnp.pad(w_in[:, 3 * D:], ((0, 0), (0, LANES - H)))
            bf = jnp.pad(b_forget[j].reshape(1, H), ((0, 0), (0, LANES - H)))
            q, k, v, fl, lf = _attn_in_fwd(h, gain(i, 0), wqkv, wf, bf, qscale)
            qaug, kaug = _aug_operands(_gate_cumsum(lf[:, :H].T), H, D)
            o, lse_a, lse_b = _flash_fwd(q, k, v, qaug, kaug, H)
            s.update(q=q, k=k, v=v, fl=fl, qaug=qaug, kaug=kaug, o=o, lse_a=lse_a, lse_b=lse_b, wqkv=wqkv, wf=wf)
            mix_in, w_out = o, full["w_attn_out"][j]
        else:
            b, c, u, zc, y = _conv_in_fwd(h, gain(i, 0), full["w_conv_in"][j], taps(j))
            s.update(b=b, c=c, u=u, zc=zc, y=y)
            mix_in, w_out = y, full["w_conv_out"][j]
        m1, x1 = _out_proj_fwd("mixer_out_fwd", mix_in, w_out, gain(i, 1), h)
        up, a = _mlp_up_fwd(x1, gain(i, 2), full["w_mlp_up"][i])
        m3, x2 = _out_proj_fwd("mlp_down_fwd", a, full["w_mlp_down"][i], gain(i, 3), x1)
        gl, pe, x3 = _ple_fwd(x2, p[i, 0], gain(i, 4), gain(i, 5), full["w_ple_gate"][i], full["w_ple_proj"][i])
        s.update(m1=m1, x1=x1, up=up, a=a, m3=m3, x2=x2, gl=gl, pe=pe, w_out=w_out)
        saved.append(s)
        h = x3

    dh, loss_blk = _loss_fwd_bwd(h, target)
    loss = lax.psum(loss_blk[0, 0], ("x", "y", "c"))

    g_gain = [[None] * 6 for _ in range(depth)]
    grads = {n: [None] * w_local[n].shape[0] for n in WEIGHT_NAMES if n != "norm_g"}
    for i in reversed(range(depth)):
        j = i // 2
        s = saved[i]
        dx2, dwp, dwg, g_gain[i][4], g_gain[i][5] = _ple_bwd(
            dh, s["x2"], p[i, 0], s["gl"], s["pe"], gain(i, 4), gain(i, 5), full["w_ple_gate"][i].T)
        grads["w_ple_proj"][i], grads["w_ple_gate"][i] = dwp, dwg
        dup, dwd, g_gain[i][3] = _out_proj_bwd(
            "mlp_down_bwd", dx2, s["m3"], gain(i, 3), [s["a"], s["up"]], full["w_mlp_down"][i].T, "relu2")
        grads["w_mlp_down"][i] = dwd
        dx1, dwu, g_gain[i][2] = _in_proj_bwd(
            "mlp_up_bwd", [dup], [full["w_mlp_up"][i].T], s["x1"], gain(i, 2), dx2)
        grads["w_mlp_up"][i] = dwu
        if i % 2 == 0:
            do, dwo, g_gain[i][1] = _out_proj_bwd(
                "attn_out_bwd", dx1, s["m1"], gain(i, 1), [s["o"]], s["w_out"].T, "plain")
            grads["w_attn_out"][j] = dwo
            delta = _attn_delta(do, s["o"], head_sel).reshape(H, 1, T)
            dqt, dkt, dvt, dca, dcb, dra, drb = _flash_bwd(s["q"], s["k"], s["v"], s["qaug"], s["kaug"], do,
                                                           s["lse_a"], s["lse_b"], delta, qscale)
            dcol = jnp.concatenate([dca, dcb], axis=1).reshape(H, T)
            drow = jnp.concatenate([dra, drb], axis=1).reshape(H, T)
            dfl_t, dbf = _gate_bwd(drow, dcol, s["fl"][:, :H].T)
            grads["b_forget"][j] = dbf.reshape(H)
            dfl = jnp.pad(dfl_t.T, ((0, 0), (0, LANES - H))).astype(BF16)
            wqkv_t = s["wqkv"].T
            dh, dwq, dwk, dwv, dwf, g_gain[i][0] = _in_proj_bwd(
                "attn_in_bwd", [dqt.T, dkt.T, dvt.T, dfl], [wqkv_t[:D], wqkv_t[D:2 * D], wqkv_t[2 * D:], s["wf"].T],
                s["x0"], gain(i, 0), dx1)
            grads["w_attn_in"][j] = jnp.concatenate([dwq, dwk, dwv, dwf[:, :H]], axis=1)
        else:
            dy, dwo, g_gain[i][1] = _out_proj_bwd(
                "conv_out_bwd", dx1, s["m1"], gain(i, 1), [s["y"]], s["w_out"].T, "plain_f32")
            grads["w_conv_out"][j] = dwo
            db, dc, du, dcw = _conv_bwd(dy, s["b"], s["c"], s["u"], s["zc"], taps(j))
            grads["conv_w"][j] = dcw[:conv_w.shape[1]]
            w_t = full["w_conv_in"][j].T
            dh, dwb, dwc, dwu2, g_gain[i][0] = _in_proj_bwd(
                "conv_in_bwd", [db, dc, du], [w_t[:D], w_t[D:2 * D], w_t[2 * D:]], s["x0"], gain(i, 0), dx1)
            grads["w_conv_in"][j] = jnp.concatenate([dwb, dwc, dwu2], axis=1)
    grad_x = dh.reshape(x.shape)

    grad_full = {n: jnp.stack(grads[n]) for n in grads}
    grad_full["norm_g"] = jnp.stack([jnp.concatenate(row, axis=0) for row in g_gain])

    offs, total_rows = _pack_layout(shard_shapes)
    g_packed = _pack({n: _to_shard_major(n, grad_full[n]) for n in WEIGHT_NAMES}, offs, total_rows, lead=(N_CHIPS,))
    c_idx = lax.axis_index("c").astype(jnp.int32).reshape(1)
    pair = _pair_add(g_packed, _sibling_swap_halves(g_packed), c_idx)
    g_red = _sibling_join(_chip_sum(_chip_scatter(pair), c_idx))
    g_out = _unpack(g_red, offs, shard_shapes)
    d_out, m_out, v_out = {}, {}, {}
    for n in WEIGHT_NAMES:
        shp = shard_shapes[n]
        two_d = (-1, shp[-1])
        d, mn, vn = _adamw(w_local[n].reshape(two_d), g_out[n].reshape(two_d),
                           m_local[n].reshape(two_d), v_local[n].reshape(two_d))
        d_out[n], m_out[n], v_out[n] = d.reshape(shp), mn.reshape(shp), vn.reshape(shp)
    return (loss, grad_x, *[g_out[n] for n in WEIGHT_NAMES], *[d_out[n] for n in WEIGHT_NAMES],
            *[m_out[n] for n in WEIGHT_NAMES], *[v_out[n] for n in WEIGHT_NAMES])
```

```python
import numpy as np
import jax
import jax.numpy as jnp
from jax import lax
from jax.experimental import pallas as pl
from jax.experimental.pallas import tpu as pltpu

F32 = jnp.float32
BF16 = jnp.bfloat16
MESH = pl.DeviceIdType.MESH

RMS_EPS = 1e-6
NEG_INF = -1e30
ADAM_LR = 0.001
ADAM_B1 = 0.9
ADAM_B2 = 0.999
ADAM_EPS = 1e-08
ADAM_WD = 0.01
ADAM_STEP = 10

N_CHIPS = 4
LANES = 128
ROW_TILE_FWD = 512
ROW_TILE_BWD = 256
ATTN_Q_TILE = 4096
ATTN_K_TILE = 512
ATTN_KEY_STRIP = 128
ATTN_QUERY_STRIP = 256
ATTN_ROTATE = 4
COL_CHUNK = 512
PACK_ROW_TILE = 256
AUG_ROWS = 16
N_AUG = 3
VMEM_LIMIT = 56 * 1024 * 1024


def _tile(n, pref):
    return pref if n % pref == 0 else n


def _dot(a, b):
    return jnp.dot(a, b, preferred_element_type=F32)


def _dot_tn(a, b):
    return lax.dot_general(a, b, (((0,), (0,)), ((), ())), preferred_element_type=F32)


def _dot_nt(a, b):
    return lax.dot_general(a, b, (((1,), (1,)), ((), ())), preferred_element_type=F32)


def _rms_fwd(x, g):
    r = lax.rsqrt(jnp.mean(x * x, axis=-1, keepdims=True) + RMS_EPS)
    return (x * r) * g


def _rms_bwd(x, g, dy):
    r = lax.rsqrt(jnp.mean(x * x, axis=-1, keepdims=True) + RMS_EPS)
    xh = x * r
    dyg = dy * g
    dx = r * (dyg - xh * jnp.mean(dyg * xh, axis=-1, keepdims=True))
    return dx, jnp.sum(dy * xh, axis=0, keepdims=True)


def _params(n_axes=1):
    return pltpu.CompilerParams(dimension_semantics=("arbitrary",) * n_axes, vmem_limit_bytes=VMEM_LIMIT)


def _row_call(name, body, n_rows, tm, row_ins, const_ins, row_outs, acc_outs=(), scratch=(), reverse=False):
    nb = n_rows // tm
    rmap = (lambda i: (nb - 1 - i, 0)) if reverse else (lambda i: (i, 0))

    def whole(shape):
        nd = len(shape)
        return pl.BlockSpec(tuple(shape), lambda i: (0,) * nd)

    in_specs = [pl.BlockSpec((tm, a.shape[1]), rmap) for a in row_ins] + [whole(a.shape) for a in const_ins]
    out_shape = [jax.ShapeDtypeStruct((n_rows, w), dt) for (w, dt) in row_outs]
    out_shape += [jax.ShapeDtypeStruct(tuple(s), dt) for (s, dt) in acc_outs]
    out_specs = [pl.BlockSpec((tm, w), rmap) for (w, _) in row_outs] + [whole(s) for (s, _) in acc_outs]
    return pl.pallas_call(
        body, name=name, grid=(nb,), in_specs=in_specs, out_specs=out_specs, out_shape=out_shape,
        scratch_shapes=list(scratch), compiler_params=_params(),
    )(*row_ins, *const_ins)


def _attn_in_fwd(x, g, wqkv, wf, bf, qscale):
    T, D = x.shape
    tm, ch = _tile(T, ROW_TILE_FWD), _tile(D, COL_CHUNK)

    def body(x_ref, g_ref, w_ref, wf_ref, bf_ref, q_ref, k_ref, v_ref, fl_ref, lf_ref):
        h = _rms_fwd(x_ref[...], g_ref[...]).astype(BF16)
        for part, o_ref in enumerate((q_ref, k_ref, v_ref)):
            for n0 in range(0, D, ch):
                r = _dot(h, w_ref[:, part * D + n0:part * D + n0 + ch])
                if part == 0:
                    r = r * qscale
                o_ref[:, n0:n0 + ch] = r.astype(BF16)
        fl = _dot(h, wf_ref[...]) + bf_ref[...]
        fl_ref[...] = fl
        lf_ref[...] = jnp.minimum(fl, 0.0) - jnp.log1p(jnp.exp(-jnp.abs(fl)))

    return _row_call("attn_in_fwd", body, T, tm, [x], [g, wqkv, wf, bf],
                     [(D, BF16), (D, BF16), (D, BF16), (LANES, F32), (LANES, F32)])


def _conv_in_fwd(x, g, w, cw):
    T, D = x.shape
    tm, ch = _tile(T, ROW_TILE_FWD), _tile(D, COL_CHUNK)

    def body(x_ref, g_ref, w_ref, cw_ref, b_ref, c_ref, u_ref, zc_ref, y_ref, tail_ref):
        i = pl.program_id(0)

        @pl.when(i == 0)
        def _():
            tail_ref[...] = jnp.zeros_like(tail_ref)

        h = _rms_fwd(x_ref[...], g_ref[...]).astype(BF16)
        for part, o_ref in enumerate((b_ref, c_ref, u_ref)):
            for n0 in range(0, D, ch):
                o_ref[:, n0:n0 + ch] = _dot(h, w_ref[:, part * D + n0:part * D + n0 + ch])
        z = c_ref[...] * u_ref[...]
        row = lax.broadcasted_iota(jnp.int32, (tm, 1), 0)
        t6, t7 = tail_ref[6:7, :], tail_ref[7:8, :]
        z1 = jnp.where(row == 0, t7, pltpu.roll(z, 1, axis=0))
        z2 = jnp.where(row == 0, t6, jnp.where(row == 1, t7, pltpu.roll(z, 2, axis=0)))
        zc = cw_ref[0:1, :] * z2 + cw_ref[1:2, :] * z1 + cw_ref[2:3, :] * z
        zc_ref[...] = zc
        y_ref[...] = (b_ref[...] * zc).astype(BF16)
        tail_ref[...] = z[tm - 8:tm, :]

    return _row_call("conv_in_fwd", body, T, tm, [x], [g, w, cw],
                     [(D, F32), (D, F32), (D, F32), (D, F32), (D, BF16)], scratch=[pltpu.VMEM((8, D), F32)])


def _mlp_up_fwd(x, g, w):
    T, D = x.shape
    F = w.shape[1]
    tm, ch = _tile(T, ROW_TILE_FWD), _tile(F, COL_CHUNK)

    def body(x_ref, g_ref, w_ref, up_ref, a_ref):
        h = _rms_fwd(x_ref[...], g_ref[...]).astype(BF16)
        for n0 in range(0, F, ch):
            r = _dot(h, w_ref[:, n0:n0 + ch])
            up_ref[:, n0:n0 + ch] = r.astype(BF16)
            rl = jnp.maximum(r, 0.0)
            a_ref[:, n0:n0 + ch] = (rl * rl).astype(BF16)

    return _row_call("mlp_up_fwd", body, T, tm, [x], [g, w], [(F, BF16), (F, BF16)])


def _out_proj_fwd(name, a, w, g, x):
    T, D = x.shape
    tm = _tile(T, ROW_TILE_FWD)

    def body(a_ref, x_ref, w_ref, g_ref, m_ref, xn_ref):
        m = _dot(a_ref[...], w_ref[...])
        m_ref[...] = m
        xn_ref[...] = x_ref[...] + _rms_fwd(m, g_ref[...])

    return _row_call(name, body, T, tm, [a, x], [w, g], [(D, F32), (D, F32)])


def _ple_fwd(x, p, g4, g5, wg, wp):
    T, D = x.shape
    tm = _tile(T, ROW_TILE_FWD)

    def body(x_ref, p_ref, g4_ref, g5_ref, wg_ref, wp_ref, gl_ref, pe_ref, xn_ref):
        xv = x_ref[...]
        gl = _dot(_rms_fwd(xv, g4_ref[...]).astype(BF16), wg_ref[...])
        pe = _dot(p_ref[...].astype(BF16), wp_ref[...])
        gl_ref[...] = gl
        pe_ref[...] = pe
        e = pe * (1.0 / (1.0 + jnp.exp(-gl)))
        xn_ref[...] = xv + _rms_fwd(e, g5_ref[...])

    return _row_call("ple_fwd", body, T, tm, [x, p], [g4, g5, wg, wp], [(D, F32), (D, F32), (D, F32)])


def _loss_fwd_bwd(y, target):
    T, D = y.shape
    tm = _tile(T, ROW_TILE_FWD)

    def body(y_ref, t_ref, dy_ref, loss_ref):
        @pl.when(pl.program_id(0) == 0)
        def _():
            loss_ref[...] = jnp.zeros_like(loss_ref)

        err = y_ref[...] - t_ref[...]
        dy_ref[...] = err * (1.0 / D)
        part = 0.5 * jnp.sum(jnp.mean(err * err, axis=-1, keepdims=True), axis=0, keepdims=True)
        loss_ref[...] += jnp.broadcast_to(part, loss_ref.shape)

    return _row_call("loss", body, T, tm, [y, target], [], [(D, F32)], acc_outs=[((8, LANES), F32)])


def _out_proj_bwd(name, dres, m, g, a_ins, wt, mode):
    T, D = dres.shape
    Ka = wt.shape[1]
    tm, ch = _tile(T, ROW_TILE_BWD), _tile(Ka, COL_CHUNK)
    out_dt = F32 if mode == "plain_f32" else BF16

    def body(dres_ref, m_ref, *rest):
        a_ref = rest[0]
        up_ref = rest[1] if mode == "relu2" else None
        k = len(a_ins)
        g_ref, wt_ref, da_ref, dw_ref, dg_ref = rest[k:k + 5]

        @pl.when(pl.program_id(0) == 0)
        def _():
            dw_ref[...] = jnp.zeros_like(dw_ref)
            dg_ref[...] = jnp.zeros_like(dg_ref)

        dm, dgp = _rms_bwd(m_ref[...], g_ref[...], dres_ref[...])
        dg_ref[...] += dgp
        dmb = dm.astype(BF16)
        for n0 in range(0, Ka, ch):
            da = _dot(dmb, wt_ref[:, n0:n0 + ch])
            if mode == "relu2":
                da = da * (2.0 * jnp.maximum(up_ref[:, n0:n0 + ch].astype(F32), 0.0))
            da_ref[:, n0:n0 + ch] = da.astype(out_dt)
            dw_ref[n0:n0 + ch, :] += _dot_tn(a_ref[:, n0:n0 + ch], dmb)

    return _row_call(name, body, T, tm, [dres, m] + list(a_ins), [g, wt],
                     [(Ka, out_dt)], acc_outs=[((Ka, D), F32), ((1, D), F32)])


def _in_proj_bwd(name, pieces, wts, x, g, dres):
    T, D = x.shape
    tm = _tile(T, ROW_TILE_BWD)
    k = len(pieces)
    widths = [pc.shape[1] for pc in pieces]

    def body(*refs):
        pc_refs = refs[:k]
        x_ref, dres_ref, g_ref = refs[k:k + 3]
        wt_refs = refs[k + 3:2 * k + 3]
        dx_ref = refs[2 * k + 3]
        dw_refs = refs[2 * k + 4:3 * k + 4]
        dg_ref = refs[3 * k + 4]

        @pl.when(pl.program_id(0) == 0)
        def _():
            for r in dw_refs:
                r[...] = jnp.zeros_like(r)
            dg_ref[...] = jnp.zeros_like(dg_ref)

        xv, gv = x_ref[...], g_ref[...]
        hb = _rms_fwd(xv, gv).astype(BF16)
        dh = None
        for pc_ref, wt_ref, dw_ref, n in zip(pc_refs, wt_refs, dw_refs, widths):
            d = _dot(pc_ref[...], wt_ref[...])
            dh = d if dh is None else dh + d
            ch = _tile(n, COL_CHUNK)
            for n0 in range(0, n, ch):
                dw_ref[:, n0:n0 + ch] += _dot_tn(hb, pc_ref[:, n0:n0 + ch])
        dxn, dgp = _rms_bwd(xv, gv, dh)
        dg_ref[...] += dgp
        dx_ref[...] = dres_ref[...] + dxn

    return _row_call(name, body, T, tm, list(pieces) + [x, dres], [g] + list(wts), [(D, F32)],
                     acc_outs=[((D, n), F32) for n in widths] + [((1, D), F32)])


def _conv_bwd(dy, b, c, u, zc, cw):
    T, D = dy.shape
    tm = _tile(T, ROW_TILE_BWD)

    def body(dy_ref, b_ref, c_ref, u_ref, zc_ref, cw_ref, db_ref, dc_ref, du_ref, dcw_ref, head_ref):
        @pl.when(pl.program_id(0) == 0)
        def _():
            head_ref[...] = jnp.zeros_like(head_ref)
            dcw_ref[...] = jnp.zeros_like(dcw_ref)

        dyv, cv, uv = dy_ref[...], c_ref[...], u_ref[...]
        db_ref[...] = (dyv * zc_ref[...]).astype(BF16)
        dzc = dyv * b_ref[...]
        row = lax.broadcasted_iota(jnp.int32, (tm, 1), 0)
        h0, h1 = head_ref[0:1, :], head_ref[1:2, :]
        d1 = jnp.where(row == tm - 1, h0, pltpu.roll(dzc, tm - 1, axis=0))
        d2 = jnp.where(row == tm - 1, h1, jnp.where(row == tm - 2, h0, pltpu.roll(dzc, tm - 2, axis=0)))
        dz = cw_ref[2:3, :] * dzc + cw_ref[1:2, :] * d1 + cw_ref[0:1, :] * d2
        dc_ref[...] = (dz * uv).astype(BF16)
        du_ref[...] = (dz * cv).astype(BF16)
        z = cv * uv
        dcw_ref[0:1, :] += jnp.sum(d2 * z, axis=0, keepdims=True)
        dcw_ref[1:2, :] += jnp.sum(d1 * z, axis=0, keepdims=True)
        dcw_ref[2:3, :] += jnp.sum(dzc * z, axis=0, keepdims=True)
        head_ref[...] = dzc[0:8, :]

    return _row_call("conv_bwd", body, T, tm, [dy, b, c, u, zc], [cw], [(D, BF16), (D, BF16), (D, BF16)],
                     acc_outs=[((8, D), F32)], scratch=[pltpu.VMEM((8, D), F32)], reverse=True)


def _ple_bwd(dres, x, p, gl, pe, g4, g5, wgt):
    T, D = x.shape
    P = p.shape[1]
    tm = _tile(T, ROW_TILE_BWD)

    def body(dres_ref, x_ref, p_ref, gl_ref, pe_ref, g4_ref, g5_ref, wgt_ref,
             dx_ref, dwp_ref, dwg_ref, dg4_ref, dg5_ref):
        @pl.when(pl.program_id(0) == 0)
        def _():
            for r in (dwp_ref, dwg_ref, dg4_ref, dg5_ref):
                r[...] = jnp.zeros_like(r)

        dr, xv, pe_v = dres_ref[...], x_ref[...], pe_ref[...]
        gate = 1.0 / (1.0 + jnp.exp(-gl_ref[...]))
        de, dg5p = _rms_bwd(pe_v * gate, g5_ref[...], dr)
        dg5_ref[...] += dg5p
        dpe = (de * gate).astype(BF16)
        dgl = (de * pe_v * gate * (1.0 - gate)).astype(BF16)
        dwp_ref[...] += _dot_tn(p_ref[...].astype(BF16), dpe)
        g4v = g4_ref[...]
        dwg_ref[...] += _dot_tn(_rms_fwd(xv, g4v).astype(BF16), dgl)
        dxn, dg4p = _rms_bwd(xv, g4v, _dot(dgl, wgt_ref[...]))
        dg4_ref[...] += dg4p
        dx_ref[...] = dr + dxn

    return _row_call("ple_bwd", body, T, tm, [dres, x, p, gl, pe], [g4, g5, wgt], [(D, F32)],
                     acc_outs=[((P, D), F32), ((D, D), F32), ((1, D), F32), ((1, D), F32)])


def _scan_lanes(v, reverse):
    n = v.shape[1]
    lane = lax.broadcasted_iota(jnp.int32, v.shape, 1)
    s = 1
    while s < n:
        if reverse:
            v = v + jnp.where(lane < n - s, pltpu.roll(v, n - s, axis=1), 0.0)
        else:
            v = v + jnp.where(lane >= s, pltpu.roll(v, s, axis=1), 0.0)
        s *= 2
    return v


def _gate_cumsum(lf_t):
    def body(lf_ref, *piece_refs):
        rest = _scan_lanes(lf_ref[...], reverse=False)
        for r in piece_refs:
            piece = rest.astype(BF16)
            r[...] = piece
            rest = rest - piece.astype(F32)

    return pl.pallas_call(body, name="gate_cumsum", out_shape=[jax.ShapeDtypeStruct(lf_t.shape, BF16)] * N_AUG,
                          compiler_params=pltpu.CompilerParams(vmem_limit_bytes=VMEM_LIMIT))(lf_t)


def _gate_bwd(drow_t, dcol_t, fl_t):
    H = fl_t.shape[0]

    def body(dr_ref, dc_ref, fl_ref, dfl_ref, dbf_ref):
        dlf = _scan_lanes(dr_ref[...] - dc_ref[...], reverse=True)
        dfl = dlf * (1.0 / (1.0 + jnp.exp(fl_ref[...])))
        dfl_ref[...] = dfl
        dbf_ref[...] = jnp.sum(dfl, axis=1, keepdims=True)

    return pl.pallas_call(
        body, name="gate_bwd",
        out_shape=(jax.ShapeDtypeStruct(fl_t.shape, F32), jax.ShapeDtypeStruct((H, 1), F32)),
        compiler_params=pltpu.CompilerParams(vmem_limit_bytes=VMEM_LIMIT))(drow_t, dcol_t, fl_t)


def _aug_operands(pieces, H, D):
    T = pieces[0].shape[1]
    dh = D // H
    one = jnp.ones((H, T), BF16)
    qa = jnp.stack(list(pieces) + [one] * N_AUG, axis=-1)
    ka = jnp.stack([one] * N_AUG + [-pc for pc in pieces], axis=-1)

    def place(a):
        a = jnp.pad(a, ((0, 0), (0, 0), (0, dh - 2 * N_AUG))).reshape(H // 2, 2, T, dh)
        return jnp.transpose(a[:, ::-1], (2, 0, 1, 3)).reshape(T, D)

    return place(qa), place(ka)


def _strip_kind(koff, k0, ksz, q0, qsz):
    if koff is None or koff + k0 + ksz - 1 <= q0:
        return "full"
    return "skip" if koff + k0 > q0 + qsz - 1 else "partial"


def _score_strip(km, qm, koff, k0, ksz, q0, qsz, kind):
    st = _dot_nt(km[k0:k0 + ksz, :], qm[q0:q0 + qsz, :])
    if kind == "partial":
        kpos = koff + k0 + lax.broadcasted_iota(jnp.int32, (ksz, qsz), 0)
        st = jnp.where(kpos <= q0 + lax.broadcasted_iota(jnp.int32, (ksz, qsz), 1), st, NEG_INF)
    return st


def _fold8(v, op):
    rows, n = v.shape
    v3 = v.reshape(rows // 8, 8, n)
    out = v3[0]
    for r in range(1, rows // 8):
        out = op(out, v3[r])
    return out


def _causal_tables(nbq, r, q_outer):
    a, b = [], []
    for o in range(nbq if q_outer else nbq * r):
        inner = range((o + 1) * r) if q_outer else range(o // r, nbq)
        for n in inner:
            a.append(o)
            b.append(n)
    return jnp.asarray(np.array(a, np.int32)), jnp.asarray(np.array(b, np.int32))


def _attn_tiles(T):
    tq = _tile(T, ATTN_Q_TILE)
    tk = _tile(tq, ATTN_K_TILE)
    return tq, tk, _tile(tk, ATTN_KEY_STRIP), _tile(tq, ATTN_QUERY_STRIP)


def _flash_fwd(q, k, v, qaug, kaug, H):
    T, D = q.shape
    Hp = H // 2
    W = D // Hp
    dh = W // 2
    tq, tk, ksz, qsz = _attn_tiles(T)
    r = tq // tk
    ii, jj = _causal_tables(T // tq, r, q_outer=True)
    n_steps = int(ii.shape[0])

    def body(ii_ref, jj_ref, q_ref, k_ref, v_ref, qa_ref, ka_ref, o_ref, lsea_ref, lseb_ref,
             qm_s, m_s, l_s, acc_s, p_s):
        n = pl.program_id(1)
        i, j = ii_ref[n], jj_ref[n]
        d = j - i * r
        in_a = lax.broadcasted_iota(jnp.int32, (1, W), 1) < dh
        top = lax.broadcasted_iota(jnp.int32, (W, 1), 0) < dh

        @pl.when(j == 0)
        def _():
            qv, qa = q_ref[...], qa_ref[...]
            qm_s[0] = jnp.where(in_a, qv, qa)
            qm_s[1] = jnp.where(in_a, qa, qv)
            m_s[...] = jnp.full(m_s.shape, NEG_INF, F32)
            l_s[...] = jnp.zeros_like(l_s)
            acc_s[...] = jnp.zeros_like(acc_s)

        def step(koff):
            kv, ka, vv = k_ref[...], ka_ref[...], v_ref[...]
            kms = (jnp.where(in_a, kv, ka), jnp.where(in_a, ka, kv))
            vt = vv.T
            alphas = {}

            def softmax_stage(hh, q0):
                cols = slice(q0, q0 + qsz)
                kinds = [(k0, _strip_kind(koff, k0, ksz, q0, qsz)) for k0 in range(0, tk, ksz)]
                if all(kind == "skip" for _, kind in kinds):
                    return False
                strips, part = {}, None
                for k0, kind in kinds:
                    if kind != "skip":
                        strips[k0] = _score_strip(kms[hh], qm_s[hh], koff, k0, ksz, q0, qsz, kind)
                        p8 = _fold8(strips[k0], jnp.maximum)
                        part = p8 if part is None else jnp.maximum(part, p8)
                m_prev = m_s[hh, :, cols]
                m_new = jnp.maximum(m_prev, jnp.max(part, axis=0, keepdims=True))
                l8 = None
                for k0, kind in kinds:
                    if kind == "skip":
                        p_s[hh, k0:k0 + ksz, cols] = jnp.zeros((ksz, qsz), BF16)
                        continue
                    pt = jnp.exp(strips[k0] - m_new)
                    s8 = _fold8(pt, jnp.add)
                    l8 = s8 if l8 is None else l8 + s8
                    p_s[hh, k0:k0 + ksz, cols] = pt.astype(BF16)
                alpha = jnp.exp(m_prev - m_new)
                l_s[hh, :, cols] = alpha * l_s[hh, :, cols] + jnp.sum(l8, axis=0, keepdims=True)
                m_s[hh, :, cols] = m_new
                alphas[(hh, q0)] = alpha
                return True

            def value_stage(hh, q0):
                cols, rows = slice(q0, q0 + qsz), slice(hh * dh, (hh + 1) * dh)
                acc_s[rows, cols] = acc_s[rows, cols] * alphas[(hh, q0)] + _dot(vt[rows, :], p_s[hh, :, cols])

            units = [(hh, q0) for q0 in range(0, tq, qsz) for hh in range(2)]
            pending = None
            for unit in units:
                live = softmax_stage(*unit)
                if pending is not None:
                    value_stage(*pending)
                pending = unit if live else None
            if pending is not None:
                value_stage(*pending)

        @pl.when(d < 0)
        def _():
            step(None)

        for dd in range(r):
            @pl.when(d == dd)
            def _():
                step(dd * tk)

        @pl.when(d == r - 1)
        def _():
            inv = jnp.where(top, 1.0 / l_s[0], 1.0 / l_s[1])
            o_ref[...] = (acc_s[...] * inv).T.astype(BF16)
            lsea_ref[...] = m_s[0] + jnp.log(l_s[0])
            lseb_ref[...] = m_s[1] + jnp.log(l_s[1])

    qspec = pl.BlockSpec((tq, W), lambda hp, n, ii, jj: (ii[n], hp))
    kspec = pl.BlockSpec((tk, W), lambda hp, n, ii, jj: (jj[n], hp))
    rspec = pl.BlockSpec((None, 1, tq), lambda hp, n, ii, jj: (hp, 0, ii[n]))
    grid_spec = pltpu.PrefetchScalarGridSpec(
        num_scalar_prefetch=2, grid=(Hp, n_steps),
        in_specs=[qspec, kspec, kspec, qspec, kspec],
        out_specs=[qspec, rspec, rspec],
        scratch_shapes=[pltpu.VMEM((2, tq, W), BF16), pltpu.VMEM((2, 1, tq), F32), pltpu.VMEM((2, 1, tq), F32),
                        pltpu.VMEM((W, tq), F32), pltpu.VMEM((2, tk, tq), BF16)],
    )
    return pl.pallas_call(
        body, name="flash_fwd", grid_spec=grid_spec,
        out_shape=(jax.ShapeDtypeStruct((T, D), BF16), jax.ShapeDtypeStruct((Hp, 1, T), F32),
                   jax.ShapeDtypeStruct((Hp, 1, T), F32)),
        compiler_params=_params(2),
    )(ii, jj, q, k, v, qaug, kaug)


def _attn_delta(do, o, head_sel):
    T, D = do.shape
    H = head_sel.shape[0]
    tm = _tile(T, ROW_TILE_FWD)

    def body(do_ref, o_ref, sel_ref, out_ref):
        prod = do_ref[...].astype(F32) * o_ref[...].astype(F32)
        out_ref[...] = lax.dot_general(sel_ref[...], prod, (((1,), (1,)), ((), ())),
                                       precision=lax.Precision.HIGHEST, preferred_element_type=F32)

    return pl.pallas_call(
        body, name="attn_delta", grid=(T // tm,),
        in_specs=[pl.BlockSpec((tm, D), lambda i: (i, 0)), pl.BlockSpec((tm, D), lambda i: (i, 0)),
                  pl.BlockSpec((H, D), lambda i: (0, 0))],
        out_specs=pl.BlockSpec((H, tm), lambda i: (0, i)),
        out_shape=jax.ShapeDtypeStruct((H, T), F32), compiler_params=_params())(do, o, head_sel)


def _flash_bwd(q, k, v, qaug, kaug, do, lse_a, lse_b, delta, qscale):
    T, D = q.shape
    Hp = lse_a.shape[0]
    W = D // Hp
    dh = W // 2
    tq, tk, ksz, qsz = _attn_tiles(T)
    r = tq // tk
    nbq = T // tq
    jj, ii = _causal_tables(nbq, r, q_outer=False)
    n_steps = int(ii.shape[0])
    row_row = dh
    col_row = dh + N_AUG

    def body(jj_ref, ii_ref, q_ref, k_ref, v_ref, qa_ref, ka_ref, do_ref, lsea_ref, lseb_ref, dla_ref, dlb_ref,
             dqt_ref, dkt_ref, dvt_ref, dcola_ref, dcolb_ref, drowa_ref, drowb_ref,
             dqt_acc, dk_acc, dv_acc, *rot):
        n = pl.program_id(1)
        i, j = ii_ref[n], jj_ref[n]
        d = j - i * r
        in_a = lax.broadcasted_iota(jnp.int32, (1, W), 1) < dh
        top = lax.broadcasted_iota(jnp.int32, (W, 1), 0) < dh

        @pl.when(n == 0)
        def _():
            dqt_acc[...] = jnp.zeros_like(dqt_acc)

        @pl.when(d >= 0)
        def _():
            dk_acc[...] = jnp.zeros_like(dk_acc)
            dv_acc[...] = jnp.zeros_like(dv_acc)

        def step(koff):
            qv, qa, kv, ka = q_ref[...], qa_ref[...], k_ref[...], ka_ref[...]
            vv, dov = v_ref[...], do_ref[...]
            zb = jnp.zeros_like(dov)
            kms = (jnp.where(in_a, kv, ka), jnp.where(in_a, ka, kv))
            qms = (jnp.where(in_a, qv, qa), jnp.where(in_a, qa, qv))
            doms = (jnp.where(in_a, dov, zb), jnp.where(in_a, zb, dov))
            def own_first(a, hh):
                return a[:dh + AUG_ROWS] if hh == 0 else jnp.concatenate([a[dh:], a[:AUG_ROWS]], axis=0)

            kxs = tuple(own_first(kms[hh].T, hh) for hh in range(2))
            qxs = tuple(own_first(qms[hh].T, hh) for hh in range(2))
            dot = dov.T
            dxs = (dot[:dh], dot[dh:])
            lses, dls = (lsea_ref[...], lseb_ref[...]), (dla_ref[...], dlb_ref[...])
            dvs, dks = ([], []), ([], [])

            def score_stage(u, hh, q0):
                p_s, ds_s = rot[2 * (u % ATTN_ROTATE)], rot[2 * (u % ATTN_ROTATE) + 1]
                cols = slice(q0, q0 + qsz)
                kinds = [(k0, _strip_kind(koff, k0, ksz, q0, qsz)) for k0 in range(0, tk, ksz)]
                if all(kind == "skip" for _, kind in kinds):
                    return False
                for k0, kind in kinds:
                    if kind == "skip":
                        p_s[k0:k0 + ksz, :] = jnp.zeros((ksz, qsz), BF16)
                        ds_s[k0:k0 + ksz, :] = jnp.zeros((ksz, qsz), BF16)
                        continue
                    st = _score_strip(kms[hh], qms[hh], koff, k0, ksz, q0, qsz, kind)
                    pt = jnp.exp(st - lses[hh][:, cols])
                    dst = pt * (_dot_nt(vv[k0:k0 + ksz, :], doms[hh][cols, :]) - dls[hh][:, cols])
                    p_s[k0:k0 + ksz, :] = pt.astype(BF16)
                    ds_s[k0:k0 + ksz, :] = dst.astype(BF16)
                return True

            def grad_stage(u, hh, q0):
                p_s, ds_s = rot[2 * (u % ATTN_ROTATE)], rot[2 * (u % ATTN_ROTATE) + 1]
                cols = slice(q0, q0 + qsz)
                ptb, dsb = p_s[...], ds_s[...]
                dvs[hh].append(_dot_nt(dxs[hh][:, cols], ptb))
                dks[hh].append(_dot_nt(qxs[hh][:, cols], dsb))
                dqt_acc[i, hh, :, cols] += _dot(kxs[hh], dsb)

            order = [(hh, q0) for q0 in range(0, tq, qsz) for hh in range(2)]
            units = [(u, hh, q0) for u, (hh, q0) in enumerate(order)]
            pending = None
            for unit in units:
                live = score_stage(*unit)
                if pending is not None:
                    grad_stage(*pending)
                pending = unit if live else None
            if pending is not None:
                grad_stage(*pending)
            for hh in range(2):
                dv_acc[hh] += sum(dvs[hh][1:], dvs[hh][0])
                dk_acc[hh] += sum(dks[hh][1:], dks[hh][0])

        @pl.when(d < 0)
        def _():
            step(None)

        for dd in range(r):
            @pl.when(d == dd)
            def _():
                step(dd * tk)

        @pl.when(i == nbq - 1)
        def _():
            dkt_ref[...] = jnp.concatenate([dk_acc[0, :dh], dk_acc[1, :dh]], axis=0).astype(BF16)
            dvt_ref[...] = jnp.concatenate([dv_acc[0], dv_acc[1]], axis=0).astype(BF16)
            dcola_ref[...] = dk_acc[0, col_row:col_row + 1, :]
            dcolb_ref[...] = dk_acc[1, col_row:col_row + 1, :]

        @pl.when(n == n_steps - 1)
        def _():
            for b in range(nbq):
                cols = slice(b * tq, (b + 1) * tq)
                both = jnp.concatenate([dqt_acc[b, 0, :dh], dqt_acc[b, 1, :dh]], axis=0)
                dqt_ref[:, cols] = (both * qscale).astype(BF16)
                drowa_ref[:, cols] = dqt_acc[b, 0, row_row:row_row + 1, :]
                drowb_ref[:, cols] = dqt_acc[b, 1, row_row:row_row + 1, :]

    qspec = pl.BlockSpec((tq, W), lambda hp, n, jj, ii: (ii[n], hp))
    kspec = pl.BlockSpec((tk, W), lambda hp, n, jj, ii: (jj[n], hp))
    ktspec = pl.BlockSpec((W, tk), lambda hp, n, jj, ii: (hp, jj[n]))
    pair_row = pl.BlockSpec((None, 1, tq), lambda hp, n, jj, ii: (hp, 0, ii[n]))
    key_row = pl.BlockSpec((None, 1, tk), lambda hp, n, jj, ii: (hp, 0, jj[n]))
    whole_row = pl.BlockSpec((None, 1, T), lambda hp, n, jj, ii: (hp, 0, 0))
    grid_spec = pltpu.PrefetchScalarGridSpec(
        num_scalar_prefetch=2, grid=(Hp, n_steps),
        in_specs=[
            qspec, kspec, kspec, qspec, kspec, qspec, pair_row, pair_row,
            pl.BlockSpec((None, 1, tq), lambda hp, n, jj, ii: (2 * hp, 0, ii[n])),
            pl.BlockSpec((None, 1, tq), lambda hp, n, jj, ii: (2 * hp + 1, 0, ii[n])),
        ],
        out_specs=[
            pl.BlockSpec((W, T), lambda hp, n, jj, ii: (hp, 0)),
            ktspec, ktspec, key_row, key_row, whole_row, whole_row,
        ],
        scratch_shapes=[pltpu.VMEM((nbq, 2, dh + AUG_ROWS, tq), F32), pltpu.VMEM((2, dh + AUG_ROWS, tk), F32),
                        pltpu.VMEM((2, dh, tk), F32)] + [pltpu.VMEM((tk, qsz), BF16)] * (2 * ATTN_ROTATE),
    )
    return pl.pallas_call(
        body, name="flash_bwd", grid_spec=grid_spec,
        out_shape=(jax.ShapeDtypeStruct((D, T), BF16), jax.ShapeDtypeStruct((D, T), BF16),
                   jax.ShapeDtypeStruct((D, T), BF16), jax.ShapeDtypeStruct((Hp, 1, T), F32),
                   jax.ShapeDtypeStruct((Hp, 1, T), F32),
                   jax.ShapeDtypeStruct((Hp, 1, T), F32), jax.ShapeDtypeStruct((Hp, 1, T), F32)),
        compiler_params=_params(2),
    )(jj, ii, q, k, v, qaug, kaug, do, lse_a, lse_b, delta, delta)


HBM_SPEC = pl.BlockSpec(memory_space=pltpu.HBM)


def _mesh_pos():
    return lax.axis_index("x"), lax.axis_index("y"), lax.axis_index("c")


def _all_gather_shards(shards):
    n = len(shards)

    def body(*refs):
        in_refs, out_refs = refs[:n], refs[n:2 * n]
        send1, recv1, send2, recv2, lsem = refs[2 * n:]
        x, y, c = _mesh_pos()
        me = 2 * x + y
        chips = [(1 - x, y), (x, 1 - y), (1 - x, 1 - y)]
        local = [pltpu.make_async_copy(in_refs[t], out_refs[t].at[me], lsem.at[t]) for t in range(n)]
        for cp in local:
            cp.start()

        def half(t, chip_idx, pc):
            hr = shards[t].shape[0] // 2
            return out_refs[t].at[chip_idx, pl.ds(pc * hr, hr), :]

        first = []
        for t in range(n):
            hr = shards[t].shape[0] // 2
            for kk, (cx, cy) in enumerate(chips):
                first.append(pltpu.make_async_remote_copy(
                    src_ref=in_refs[t].at[pl.ds(c * hr, hr), :], dst_ref=half(t, me, c),
                    send_sem=send1.at[3 * t + kk], recv_sem=recv1.at[3 * t + kk],
                    device_id=(cx, cy, c), device_id_type=MESH))
        for cp in first:
            cp.start()
        passed = []
        for t in range(n):
            for kk, (cx, cy) in enumerate(chips):
                src_chip = 2 * cx + cy
                landed = half(t, src_chip, c)
                pltpu.make_async_remote_copy(
                    src_ref=landed, dst_ref=landed, send_sem=send1.at[3 * t + kk], recv_sem=recv1.at[3 * t + kk],
                    device_id=(cx, cy, c), device_id_type=MESH).wait_recv()
                fwd = pltpu.make_async_remote_copy(
                    src_ref=landed, dst_ref=landed, send_sem=send2.at[3 * t + kk], recv_sem=recv2.at[3 * t + kk],
                    device_id=(x, y, 1 - c), device_id_type=MESH)
                fwd.start()
                passed.append(fwd)
        for t in range(n):
            for kk, (cx, cy) in enumerate(chips):
                other = half(t, 2 * cx + cy, 1 - c)
                pltpu.make_async_remote_copy(
                    src_ref=other, dst_ref=other, send_sem=send2.at[3 * t + kk], recv_sem=recv2.at[3 * t + kk],
                    device_id=(x, y, 1 - c), device_id_type=MESH).wait_recv()
        for cp in first + passed:
            cp.wait_send()
        for cp in local:
            cp.wait()

    return pl.pallas_call(
        body, name="weights_all_gather",
        out_shape=[jax.ShapeDtypeStruct((N_CHIPS,) + s.shape, s.dtype) for s in shards],
        in_specs=[HBM_SPEC] * n, out_specs=[HBM_SPEC] * n,
        scratch_shapes=[pltpu.SemaphoreType.DMA((3 * n,)), pltpu.SemaphoreType.DMA((3 * n,)),
                        pltpu.SemaphoreType.DMA((3 * n,)), pltpu.SemaphoreType.DMA((3 * n,)),
                        pltpu.SemaphoreType.DMA((n,))],
    )(*shards)


def _sibling_swap_halves(gs):
    n = len(gs)

    def body(*refs):
        g_refs, r_refs = refs[:n], refs[n:2 * n]
        ssem, rsem = refs[2 * n:]
        x, y, c = _mesh_pos()
        cps = []
        for t in range(n):
            hr = gs[t].shape[1] // 2
            cps += [pltpu.make_async_remote_copy(
                src_ref=g_refs[t].at[s, pl.ds((1 - c) * hr, hr), :], dst_ref=r_refs[t].at[s],
                send_sem=ssem.at[N_CHIPS * t + s], recv_sem=rsem.at[N_CHIPS * t + s],
                device_id=(x, y, 1 - c), device_id_type=MESH) for s in range(N_CHIPS)]
        for cp in cps:
            cp.start()
        for cp in cps:
            cp.wait()

    return pl.pallas_call(
        body, name="grads_sibling_swap",
        out_shape=[jax.ShapeDtypeStruct((N_CHIPS, g.shape[1] // 2, g.shape[2]), F32) for g in gs],
        in_specs=[HBM_SPEC] * n, out_specs=[HBM_SPEC] * n,
        scratch_shapes=[pltpu.SemaphoreType.DMA((N_CHIPS * n,)), pltpu.SemaphoreType.DMA((N_CHIPS * n,))],
    )(*gs)


def _pair_add(g, r, c_idx):
    _, M, C = g.shape
    hr = M // 2
    tr = _tile(hr, PACK_ROW_TILE)
    nbk = hr // tr

    def body(c_ref, g_ref, r_ref, o_ref):
        o_ref[...] = (g_ref[...] + r_ref[...]).astype(BF16)

    grid_spec = pltpu.PrefetchScalarGridSpec(
        num_scalar_prefetch=1, grid=(N_CHIPS, nbk),
        in_specs=[pl.BlockSpec((None, tr, C), lambda s, i, c: (s, c[0] * nbk + i, 0)),
                  pl.BlockSpec((None, tr, C), lambda s, i, c: (s, i, 0))],
        out_specs=pl.BlockSpec((None, tr, C), lambda s, i, c: (s, i, 0)),
    )
    return pl.pallas_call(body, name="grads_pair_add", grid_spec=grid_spec,
                          out_shape=jax.ShapeDtypeStruct((N_CHIPS, hr, C), BF16),
                          compiler_params=_params(2))(c_idx, g, r)


def _chip_scatter(pps):
    n = len(pps)

    def body(*refs):
        p_refs, r_refs = refs[:n], refs[n:2 * n]
        ssem, rsem, lsem = refs[2 * n:]
        x, y, c = _mesh_pos()
        me = 2 * x + y
        chips = [(1 - x, y), (x, 1 - y), (1 - x, 1 - y)]
        own = [pltpu.make_async_copy(p_refs[t].at[me], r_refs[t].at[me], lsem.at[t]) for t in range(n)]
        for cp in own:
            cp.start()
        cps = [pltpu.make_async_remote_copy(
            src_ref=p_refs[t].at[2 * cx + cy], dst_ref=r_refs[t].at[me],
            send_sem=ssem.at[3 * t + kk], recv_sem=rsem.at[3 * t + kk],
            device_id=(cx, cy, c), device_id_type=MESH) for t in range(n) for kk, (cx, cy) in enumerate(chips)]
        for cp in cps:
            cp.start()
        for t in range(n):
            for kk, (cx, cy) in enumerate(chips):
                got = r_refs[t].at[2 * cx + cy]
                pltpu.make_async_remote_copy(
                    src_ref=got, dst_ref=got, send_sem=ssem.at[3 * t + kk], recv_sem=rsem.at[3 * t + kk],
                    device_id=(cx, cy, c), device_id_type=MESH).wait_recv()
        for cp in cps:
            cp.wait_send()
        for cp in own:
            cp.wait()

    return pl.pallas_call(
        body, name="grads_chip_scatter", out_shape=[jax.ShapeDtypeStruct(pp.shape, pp.dtype) for pp in pps],
        in_specs=[HBM_SPEC] * n, out_specs=[HBM_SPEC] * n,
        scratch_shapes=[pltpu.SemaphoreType.DMA((3 * n,)), pltpu.SemaphoreType.DMA((3 * n,)),
                        pltpu.SemaphoreType.DMA((n,))],
    )(*pps)


def _chip_sum(r, c_idx):
    _, hr, C = r.shape
    tr = _tile(hr, PACK_ROW_TILE)
    nbk = hr // tr

    def body(c_ref, r_ref, o_ref):
        r0, r1, r2, r3 = (r_ref[s].astype(F32) for s in range(N_CHIPS))
        o_ref[...] = ((r0 + r1) + r2) + r3

    grid_spec = pltpu.PrefetchScalarGridSpec(
        num_scalar_prefetch=1, grid=(nbk,),
        in_specs=[pl.BlockSpec((N_CHIPS, tr, C), lambda i, c: (0, i, 0))],
        out_specs=pl.BlockSpec((tr, C), lambda i, c: (c[0] * nbk + i, 0)),
    )
    return pl.pallas_call(body, name="grads_chip_sum", grid_spec=grid_spec,
                          out_shape=jax.ShapeDtypeStruct((2 * hr, C), F32), compiler_params=_params())(c_idx, r)


def _sibling_join(bufs):
    n = len(bufs)

    def body(*refs):
        o_refs = refs[n:2 * n]
        ssem, rsem = refs[2 * n:]
        x, y, c = _mesh_pos()
        cps = []
        for t in range(n):
            hr = bufs[t].shape[0] // 2
            mine = o_refs[t].at[pl.ds(c * hr, hr), :]
            cps.append(pltpu.make_async_remote_copy(src_ref=mine, dst_ref=mine, send_sem=ssem.at[t], recv_sem=rsem.at[t],
                                                    device_id=(x, y, 1 - c), device_id_type=MESH))
        for cp in cps:
            cp.start()
        for t in range(n):
            hr = bufs[t].shape[0] // 2
            theirs = o_refs[t].at[pl.ds((1 - c) * hr, hr), :]
            pltpu.make_async_remote_copy(src_ref=theirs, dst_ref=theirs, send_sem=ssem.at[t], recv_sem=rsem.at[t],
                                         device_id=(x, y, 1 - c), device_id_type=MESH).wait_recv()
        for cp in cps:
            cp.wait_send()

    return pl.pallas_call(
        body, name="grads_sibling_join", out_shape=[jax.ShapeDtypeStruct(b.shape, F32) for b in bufs],
        in_specs=[HBM_SPEC] * n, out_specs=[HBM_SPEC] * n, input_output_aliases={t: t for t in range(n)},
        scratch_shapes=[pltpu.SemaphoreType.DMA((n,)), pltpu.SemaphoreType.DMA((n,))],
    )(*bufs)


def _adamw(w, g, m, v):
    M, C = w.shape
    tr = _tile(M, PACK_ROW_TILE)

    def body(w_ref, g_ref, m_ref, v_ref, d_ref, mo_ref, vo_ref):
        gv = g_ref[...]
        mn = ADAM_B1 * m_ref[...] + (1.0 - ADAM_B1) * gv
        vn = ADAM_B2 * v_ref[...] + (1.0 - ADAM_B2) * (gv * gv)
        m_hat = mn / (1.0 - ADAM_B1 ** ADAM_STEP)
        v_hat = vn / (1.0 - ADAM_B2 ** ADAM_STEP)
        d_ref[...] = -ADAM_LR * (m_hat / (jnp.sqrt(v_hat) + ADAM_EPS) + ADAM_WD * w_ref[...])
        mo_ref[...] = mn
        vo_ref[...] = vn

    spec = pl.BlockSpec((tr, C), lambda i: (i, 0))
    return pl.pallas_call(
        body, name="adamw", grid=(M // tr,), in_specs=[spec] * 4, out_specs=[spec] * 3,
        out_shape=[jax.ShapeDtypeStruct((M, C), F32)] * 3, compiler_params=_params())(w, g, m, v)


WEIGHT_NAMES = ("norm_g", "w_attn_in", "b_forget", "w_attn_out", "w_conv_in", "conv_w", "w_conv_out",
                "w_mlp_up", "w_mlp_down", "w_ple_proj", "w_ple_gate")
COL_SHARDED = ("norm_g", "w_attn_in", "w_conv_in", "conv_w", "w_mlp_up", "w_ple_proj")
ROW_SHARDED = ("w_attn_out", "w_conv_out", "w_mlp_down", "w_ple_gate")


def _unshard(name, gathered, shard_shape):
    a = gathered.reshape((N_CHIPS,) + tuple(shard_shape))
    if name in COL_SHARDED:
        a = jnp.moveaxis(a, 0, -2)
        return a.reshape(a.shape[:-2] + (N_CHIPS * shard_shape[-1],))
    a = jnp.moveaxis(a, 0, 1)
    return a.reshape((shard_shape[0], N_CHIPS * shard_shape[1], shard_shape[2]))


def _to_shard_major(name, full):
    if name == "b_forget":
        return jnp.broadcast_to(full.reshape(1, -1), (N_CHIPS, full.size))
    if name in COL_SHARDED:
        a = full.reshape(full.shape[:-1] + (N_CHIPS, full.shape[-1] // N_CHIPS))
        a = jnp.moveaxis(a, -2, 0)
    else:
        a = full.reshape((full.shape[0], N_CHIPS, full.shape[1] // N_CHIPS, full.shape[2]))
        a = jnp.moveaxis(a, 1, 0)
    return a.reshape(N_CHIPS, -1)


def _reduce_scatter(tensors, c_idx):
    swapped = _sibling_swap_halves(tensors)
    pairs = [_pair_add(g, r, c_idx) for g, r in zip(tensors, swapped)]
    return _sibling_join([_chip_sum(r, c_idx) for r in _chip_scatter(pairs)])


def kernel(x, p, norm_g, w_attn_in, b_forget, w_attn_out, w_conv_in, conv_w, w_conv_out, w_mlp_up, w_mlp_down, w_ple_proj, w_ple_gate, loss_target, m_norm_g, m_w_attn_in, m_b_forget, m_w_attn_out, m_w_conv_in, m_conv_w, m_w_conv_out, m_w_mlp_up, m_w_mlp_down, m_w_ple_proj, m_w_ple_gate, v_norm_g, v_w_attn_in, v_b_forget, v_w_attn_out, v_w_conv_in, v_conv_w, v_w_conv_out, v_w_mlp_up, v_w_mlp_down, v_w_ple_proj, v_w_ple_gate):
    w_local = dict(norm_g=norm_g, w_attn_in=w_attn_in, b_forget=b_forget, w_attn_out=w_attn_out,
                   w_conv_in=w_conv_in, conv_w=conv_w, w_conv_out=w_conv_out, w_mlp_up=w_mlp_up,
                   w_mlp_down=w_mlp_down, w_ple_proj=w_ple_proj, w_ple_gate=w_ple_gate)
    m_local = dict(norm_g=m_norm_g, w_attn_in=m_w_attn_in, b_forget=m_b_forget, w_attn_out=m_w_attn_out,
                   w_conv_in=m_w_conv_in, conv_w=m_conv_w, w_conv_out=m_w_conv_out, w_mlp_up=m_w_mlp_up,
                   w_mlp_down=m_w_mlp_down, w_ple_proj=m_w_ple_proj, w_ple_gate=m_w_ple_gate)
    v_local = dict(norm_g=v_norm_g, w_attn_in=v_w_attn_in, b_forget=v_b_forget, w_attn_out=v_w_attn_out,
                   w_conv_in=v_w_conv_in, conv_w=v_conv_w, w_conv_out=v_w_conv_out, w_mlp_up=v_w_mlp_up,
                   w_mlp_down=v_w_mlp_down, w_ple_proj=v_w_ple_proj, w_ple_gate=v_w_ple_gate)
    shard_shapes = {k: tuple(a.shape) for k, a in w_local.items()}

    xs = x[0]
    target = loss_target[0]
    T, D = xs.shape
    depth = p.shape[0]
    H = b_forget.shape[1]
    qscale = float(D // H) ** -0.5
    head_sel = (jnp.arange(D)[None, :] // (D // H) == jnp.arange(H)[:, None]).astype(F32)

    big = [n for n in WEIGHT_NAMES if n not in ("norm_g", "conv_w", "b_forget")]
    small = jnp.concatenate([norm_g.reshape(-1), conv_w.reshape(-1)])
    n_small = small.shape[0]
    small_rows = -(-n_small // (LANES * 16)) * 16
    small = jnp.pad(small, (0, small_rows * LANES - n_small)).reshape(small_rows, LANES)
    shards = [w_local[n].astype(BF16).reshape(-1, shard_shapes[n][-1]) for n in big] + [small]
    gathered = _all_gather_shards(shards)
    full = {n: _unshard(n, g, shard_shapes[n]) for n, g in zip(big, gathered[:-1])}
    gs = gathered[-1].reshape(N_CHIPS, -1)
    full["norm_g"] = _unshard("norm_g", gs[:, :norm_g.size], shard_shapes["norm_g"])
    full["conv_w"] = _unshard("conv_w", gs[:, norm_g.size:n_small], shard_shapes["conv_w"])
    gains = full["norm_g"]

    def gain(i, k):
        return gains[i, k].reshape(1, D)

    def taps(j):
        return jnp.pad(full["conv_w"][j], ((0, 5), (0, 0)))

    saved = []
    h = xs
    for i in range(depth):
        j = i // 2
        s = {"x0": h}
        if i % 2 == 0:
            w_in = full["w_attn_in"][j]
            wqkv = w_in[:, :3 * D]
            wf = jnp.pad(w_in[:, 3 * D:], ((0, 0), (0, LANES - H)))
            bf = jnp.pad(b_forget[j].reshape(1, H), ((0, 0), (0, LANES - H)))
            q, k, v, fl, lf = _attn_in_fwd(h, gain(i, 0), wqkv, wf, bf, qscale)
            qaug, kaug = _aug_operands(_gate_cumsum(lf[:, :H].T), H, D)
            o, lse_a, lse_b = _flash_fwd(q, k, v, qaug, kaug, H)
            s.update(q=q, k=k, v=v, fl=fl, qaug=qaug, kaug=kaug, o=o, lse_a=lse_a, lse_b=lse_b, wqkv=wqkv, wf=wf)
            mix_in, w_out = o, full["w_attn_out"][j]
        else:
            b, c, u, zc, y = _conv_in_fwd(h, gain(i, 0), full["w_conv_in"][j], taps(j))
            s.update(b=b, c=c, u=u, zc=zc, y=y)
            mix_in, w_out = y, full["w_conv_out"][j]
        m1, x1 = _out_proj_fwd("mixer_out_fwd", mix_in, w_out, gain(i, 1), h)
        up, a = _mlp_up_fwd(x1, gain(i, 2), full["w_mlp_up"][i])
        m3, x2 = _out_proj_fwd("mlp_down_fwd", a, full["w_mlp_down"][i], gain(i, 3), x1)
        gl, pe, x3 = _ple_fwd(x2, p[i, 0], gain(i, 4), gain(i, 5), full["w_ple_gate"][i], full["w_ple_proj"][i])
        s.update(m1=m1, x1=x1, up=up, a=a, m3=m3, x2=x2, gl=gl, pe=pe, w_out=w_out)
        saved.append(s)
        h = x3

    dh, loss_blk = _loss_fwd_bwd(h, target)
    loss = lax.psum(loss_blk[0, 0], ("x", "y", "c"))

    g_gain = [[None] * 6 for _ in range(depth)]
    grads = {n: [None] * w_local[n].shape[0] for n in WEIGHT_NAMES if n != "norm_g"}
    for i in reversed(range(depth)):
        j = i // 2
        s = saved[i]
        dx2, dwp, dwg, g_gain[i][4], g_gain[i][5] = _ple_bwd(
            dh, s["x2"], p[i, 0], s["gl"], s["pe"], gain(i, 4), gain(i, 5), full["w_ple_gate"][i].T)
        grads["w_ple_proj"][i], grads["w_ple_gate"][i] = dwp, dwg
        dup, dwd, g_gain[i][3] = _out_proj_bwd(
            "mlp_down_bwd", dx2, s["m3"], gain(i, 3), [s["a"], s["up"]], full["w_mlp_down"][i].T, "relu2")
        grads["w_mlp_down"][i] = dwd
        dx1, dwu, g_gain[i][2] = _in_proj_bwd(
            "mlp_up_bwd", [dup], [full["w_mlp_up"][i].T], s["x1"], gain(i, 2), dx2)
        grads["w_mlp_up"][i] = dwu
        if i % 2 == 0:
            do, dwo, g_gain[i][1] = _out_proj_bwd(
                "attn_out_bwd", dx1, s["m1"], gain(i, 1), [s["o"]], s["w_out"].T, "plain")
            grads["w_attn_out"][j] = dwo
            delta = _attn_delta(do, s["o"], head_sel).reshape(H, 1, T)
            dqt, dkt, dvt, dca, dcb, dra, drb = _flash_bwd(s["q"], s["k"], s["v"], s["qaug"], s["kaug"], do,
                                                           s["lse_a"], s["lse_b"], delta, qscale)
            dcol = jnp.concatenate([dca, dcb], axis=1).reshape(H, T)
            drow = jnp.concatenate([dra, drb], axis=1).reshape(H, T)
            dfl_t, dbf = _gate_bwd(drow, dcol, s["fl"][:, :H].T)
            grads["b_forget"][j] = dbf.reshape(H)
            dfl = jnp.pad(dfl_t.T, ((0, 0), (0, LANES - H))).astype(BF16)
            wqkv_t = s["wqkv"].T
            dh, dwq, dwk, dwv, dwf, g_gain[i][0] = _in_proj_bwd(
                "attn_in_bwd", [dqt.T, dkt.T, dvt.T, dfl], [wqkv_t[:D], wqkv_t[D:2 * D], wqkv_t[2 * D:], s["wf"].T],
                s["x0"], gain(i, 0), dx1)
            grads["w_attn_in"][j] = jnp.concatenate([dwq, dwk, dwv, dwf[:, :H]], axis=1)
        else:
            dy, dwo, g_gain[i][1] = _out_proj_bwd(
                "conv_out_bwd", dx1, s["m1"], gain(i, 1), [s["y"]], s["w_out"].T, "plain_f32")
            grads["w_conv_out"][j] = dwo
            db, dc, du, dcw = _conv_bwd(dy, s["b"], s["c"], s["u"], s["zc"], taps(j))
            grads["conv_w"][j] = dcw[:conv_w.shape[1]]
            w_t = full["w_conv_in"][j].T
            dh, dwb, dwc, dwu2, g_gain[i][0] = _in_proj_bwd(
                "conv_in_bwd", [db, dc, du], [w_t[:D], w_t[D:2 * D], w_t[2 * D:]], s["x0"], gain(i, 0), dx1)
            grads["w_conv_in"][j] = jnp.concatenate([dwb, dwc, dwu2], axis=1)
    grad_x = dh.reshape(x.shape)

    grad_full = {n: jnp.stack(grads[n]) for n in grads}
    grad_full["norm_g"] = jnp.stack([jnp.concatenate(row, axis=0) for row in g_gain])

    c_idx = lax.axis_index("c").astype(jnp.int32).reshape(1)
    tiny = ("norm_g", "conv_w", "b_forget")
    tiny_flat = jnp.concatenate([_to_shard_major(n, grad_full[n]) for n in tiny], axis=1)
    n_tiny = tiny_flat.shape[1]
    tiny_rows = -(-n_tiny // (LANES * 32)) * 32
    tiny_t = jnp.pad(tiny_flat, ((0, 0), (0, tiny_rows * LANES - n_tiny))).reshape(N_CHIPS, tiny_rows, LANES)
    tensors = [_to_shard_major(n, grad_full[n]).reshape(N_CHIPS, -1, shard_shapes[n][-1]) for n in big]
    reduced = _reduce_scatter(tensors + [tiny_t], c_idx)
    g_out = {n: r.reshape(shard_shapes[n]) for n, r in zip(big, reduced[:-1])}
    tiny_red, off = reduced[-1].reshape(-1), 0
    for n in tiny:
        size = int(np.prod(shard_shapes[n]))
        g_out[n] = tiny_red[off:off + size].reshape(shard_shapes[n])
        off += size
    d_out, m_out, v_out = {}, {}, {}
    for n in WEIGHT_NAMES:
        shp = shard_shapes[n]
        two_d = (-1, shp[-1])
        d, mn, vn = _adamw(w_local[n].reshape(two_d), g_out[n].reshape(two_d),
                           m_local[n].reshape(two_d), v_local[n].reshape(two_d))
        d_out[n], m_out[n], v_out[n] = d.reshape(shp), mn.reshape(shp), vn.reshape(shp)
    return (loss, grad_x, *[g_out[n] for n in WEIGHT_NAMES], *[d_out[n] for n in WEIGHT_NAMES],
            *[m_out[n] for n in WEIGHT_NAMES], *[v_out[n] for n in WEIGHT_NAMES])
```

```python
import numpy as np
import jax
import jax.numpy as jnp
from jax import lax
from jax.experimental import pallas as pl
from jax.experimental.pallas import tpu as pltpu

F32 = jnp.float32
BF16 = jnp.bfloat16
MESH = pl.DeviceIdType.MESH

RMS_EPS = 1e-6
NEG_INF = -1e30
ADAM_LR = 0.001
ADAM_B1 = 0.9
ADAM_B2 = 0.999
ADAM_EPS = 1e-08
ADAM_WD = 0.01
ADAM_STEP = 10

N_CHIPS = 4
LANES = 128
ROW_TILE_FWD = 512
ROW_TILE_BWD = 256
ATTN_Q_TILE_FWD = 4096
ATTN_Q_TILE_BWD = 2048
ATTN_K_TILE = 512
ATTN_KEY_STRIP = 128
ATTN_QUERY_STRIP = 256
ATTN_ROTATE = 4
COL_CHUNK = 512
PACK_ROW_TILE = 256
AUG_ROWS = 16
N_AUG = 3
VMEM_LIMIT = 56 * 1024 * 1024


def _tile(n, pref):
    return pref if n % pref == 0 else n


def _dot(a, b):
    return jnp.dot(a, b, preferred_element_type=F32)


def _dot_tn(a, b):
    return lax.dot_general(a, b, (((0,), (0,)), ((), ())), preferred_element_type=F32)


def _dot_nt(a, b):
    return lax.dot_general(a, b, (((1,), (1,)), ((), ())), preferred_element_type=F32)


def _rms_fwd(x, g):
    r = lax.rsqrt(jnp.mean(x * x, axis=-1, keepdims=True) + RMS_EPS)
    return (x * r) * g


def _rms_bwd(x, g, dy):
    r = lax.rsqrt(jnp.mean(x * x, axis=-1, keepdims=True) + RMS_EPS)
    xh = x * r
    dyg = dy * g
    dx = r * (dyg - xh * jnp.mean(dyg * xh, axis=-1, keepdims=True))
    return dx, jnp.sum(dy * xh, axis=0, keepdims=True)


def _params(n_axes=1):
    return pltpu.CompilerParams(dimension_semantics=("arbitrary",) * n_axes, vmem_limit_bytes=VMEM_LIMIT)


def _row_call(name, body, n_rows, tm, row_ins, const_ins, row_outs, acc_outs=(), scratch=(), reverse=False):
    nb = n_rows // tm
    rmap = (lambda i: (nb - 1 - i, 0)) if reverse else (lambda i: (i, 0))

    def whole(shape):
        nd = len(shape)
        return pl.BlockSpec(tuple(shape), lambda i: (0,) * nd)

    in_specs = [pl.BlockSpec((tm, a.shape[1]), rmap) for a in row_ins] + [whole(a.shape) for a in const_ins]
    out_shape = [jax.ShapeDtypeStruct((n_rows, w), dt) for (w, dt) in row_outs]
    out_shape += [jax.ShapeDtypeStruct(tuple(s), dt) for (s, dt) in acc_outs]
    out_specs = [pl.BlockSpec((tm, w), rmap) for (w, _) in row_outs] + [whole(s) for (s, _) in acc_outs]
    return pl.pallas_call(
        body, name=name, grid=(nb,), in_specs=in_specs, out_specs=out_specs, out_shape=out_shape,
        scratch_shapes=list(scratch), compiler_params=_params(),
    )(*row_ins, *const_ins)


def _attn_in_fwd(x, g, wqkv, wf, bf, qscale):
    T, D = x.shape
    tm, ch = _tile(T, ROW_TILE_FWD), _tile(D, COL_CHUNK)

    def body(x_ref, g_ref, w_ref, wf_ref, bf_ref, q_ref, k_ref, v_ref, fl_ref, lf_ref):
        h = _rms_fwd(x_ref[...], g_ref[...]).astype(BF16)
        for part, o_ref in enumerate((q_ref, k_ref, v_ref)):
            for n0 in range(0, D, ch):
                r = _dot(h, w_ref[:, part * D + n0:part * D + n0 + ch])
                if part == 0:
                    r = r * qscale
                o_ref[:, n0:n0 + ch] = r.astype(BF16)
        fl = _dot(h, wf_ref[...]) + bf_ref[...]
        fl_ref[...] = fl
        lf_ref[...] = jnp.minimum(fl, 0.0) - jnp.log1p(jnp.exp(-jnp.abs(fl)))

    return _row_call("attn_in_fwd", body, T, tm, [x], [g, wqkv, wf, bf],
                     [(D, BF16), (D, BF16), (D, BF16), (LANES, F32), (LANES, F32)])


def _conv_in_fwd(x, g, w, cw):
    T, D = x.shape
    tm, ch = _tile(T, ROW_TILE_FWD), _tile(D, COL_CHUNK)

    def body(x_ref, g_ref, w_ref, cw_ref, b_ref, c_ref, u_ref, zc_ref, y_ref, tail_ref):
        i = pl.program_id(0)

        @pl.when(i == 0)
        def _():
            tail_ref[...] = jnp.zeros_like(tail_ref)

        h = _rms_fwd(x_ref[...], g_ref[...]).astype(BF16)
        for part, o_ref in enumerate((b_ref, c_ref, u_ref)):
            for n0 in range(0, D, ch):
                o_ref[:, n0:n0 + ch] = _dot(h, w_ref[:, part * D + n0:part * D + n0 + ch])
        z = c_ref[...] * u_ref[...]
        row = lax.broadcasted_iota(jnp.int32, (tm, 1), 0)
        t6, t7 = tail_ref[6:7, :], tail_ref[7:8, :]
        z1 = jnp.where(row == 0, t7, pltpu.roll(z, 1, axis=0))
        z2 = jnp.where(row == 0, t6, jnp.where(row == 1, t7, pltpu.roll(z, 2, axis=0)))
        zc = cw_ref[0:1, :] * z2 + cw_ref[1:2, :] * z1 + cw_ref[2:3, :] * z
        zc_ref[...] = zc
        y_ref[...] = (b_ref[...] * zc).astype(BF16)
        tail_ref[...] = z[tm - 8:tm, :]

    return _row_call("conv_in_fwd", body, T, tm, [x], [g, w, cw],
                     [(D, F32), (D, F32), (D, F32), (D, F32), (D, BF16)], scratch=[pltpu.VMEM((8, D), F32)])


def _mlp_up_fwd(x, g, w):
    T, D = x.shape
    F = w.shape[1]
    tm, ch = _tile(T, ROW_TILE_FWD), _tile(F, COL_CHUNK)

    def body(x_ref, g_ref, w_ref, up_ref, a_ref):
        h = _rms_fwd(x_ref[...], g_ref[...]).astype(BF16)
        for n0 in range(0, F, ch):
            r = _dot(h, w_ref[:, n0:n0 + ch])
            up_ref[:, n0:n0 + ch] = r.astype(BF16)
            rl = jnp.maximum(r, 0.0)
            a_ref[:, n0:n0 + ch] = (rl * rl).astype(BF16)

    return _row_call("mlp_up_fwd", body, T, tm, [x], [g, w], [(F, BF16), (F, BF16)])


def _out_proj_fwd(name, a, w, g, x):
    T, D = x.shape
    tm = _tile(T, ROW_TILE_FWD)

    def body(a_ref, x_ref, w_ref, g_ref, m_ref, xn_ref):
        m = _dot(a_ref[...], w_ref[...])
        m_ref[...] = m
        xn_ref[...] = x_ref[...] + _rms_fwd(m, g_ref[...])

    return _row_call(name, body, T, tm, [a, x], [w, g], [(D, F32), (D, F32)])


def _ple_fwd(x, p, g4, g5, wg, wp):
    T, D = x.shape
    tm = _tile(T, ROW_TILE_FWD)

    def body(x_ref, p_ref, g4_ref, g5_ref, wg_ref, wp_ref, gl_ref, pe_ref, xn_ref):
        xv = x_ref[...]
        gl = _dot(_rms_fwd(xv, g4_ref[...]).astype(BF16), wg_ref[...])
        pe = _dot(p_ref[...].astype(BF16), wp_ref[...])
        gl_ref[...] = gl
        pe_ref[...] = pe
        e = pe * (1.0 / (1.0 + jnp.exp(-gl)))
        xn_ref[...] = xv + _rms_fwd(e, g5_ref[...])

    return _row_call("ple_fwd", body, T, tm, [x, p], [g4, g5, wg, wp], [(D, F32), (D, F32), (D, F32)])


def _loss_fwd_bwd(y, target):
    T, D = y.shape
    tm = _tile(T, ROW_TILE_FWD)

    def body(y_ref, t_ref, dy_ref, loss_ref):
        @pl.when(pl.program_id(0) == 0)
        def _():
            loss_ref[...] = jnp.zeros_like(loss_ref)

        err = y_ref[...] - t_ref[...]
        dy_ref[...] = err * (1.0 / D)
        part = 0.5 * jnp.sum(jnp.mean(err * err, axis=-1, keepdims=True), axis=0, keepdims=True)
        loss_ref[...] += jnp.broadcast_to(part, loss_ref.shape)

    return _row_call("loss", body, T, tm, [y, target], [], [(D, F32)], acc_outs=[((8, LANES), F32)])


def _out_proj_bwd(name, dres, m, g, a_ins, wt, mode):
    T, D = dres.shape
    Ka = wt.shape[1]
    tm, ch = _tile(T, ROW_TILE_BWD), _tile(Ka, COL_CHUNK)
    out_dt = F32 if mode == "plain_f32" else BF16

    def body(dres_ref, m_ref, *rest):
        a_ref = rest[0]
        up_ref = rest[1] if mode == "relu2" else None
        k = len(a_ins)
        g_ref, wt_ref, da_ref, dw_ref, dg_ref = rest[k:k + 5]

        @pl.when(pl.program_id(0) == 0)
        def _():
            dw_ref[...] = jnp.zeros_like(dw_ref)
            dg_ref[...] = jnp.zeros_like(dg_ref)

        dm, dgp = _rms_bwd(m_ref[...], g_ref[...], dres_ref[...])
        dg_ref[...] += dgp
        dmb = dm.astype(BF16)
        for n0 in range(0, Ka, ch):
            da = _dot(dmb, wt_ref[:, n0:n0 + ch])
            if mode == "relu2":
                da = da * (2.0 * jnp.maximum(up_ref[:, n0:n0 + ch].astype(F32), 0.0))
            da_ref[:, n0:n0 + ch] = da.astype(out_dt)
            dw_ref[n0:n0 + ch, :] += _dot_tn(a_ref[:, n0:n0 + ch], dmb)

    return _row_call(name, body, T, tm, [dres, m] + list(a_ins), [g, wt],
                     [(Ka, out_dt)], acc_outs=[((Ka, D), F32), ((1, D), F32)])


def _in_proj_bwd(name, pieces, wts, x, g, dres):
    T, D = x.shape
    tm = _tile(T, ROW_TILE_BWD)
    k = len(pieces)
    widths = [pc.shape[1] for pc in pieces]

    def body(*refs):
        pc_refs = refs[:k]
        x_ref, dres_ref, g_ref = refs[k:k + 3]
        wt_refs = refs[k + 3:2 * k + 3]
        dx_ref = refs[2 * k + 3]
        dw_refs = refs[2 * k + 4:3 * k + 4]
        dg_ref = refs[3 * k + 4]

        @pl.when(pl.program_id(0) == 0)
        def _():
            for r in dw_refs:
                r[...] = jnp.zeros_like(r)
            dg_ref[...] = jnp.zeros_like(dg_ref)

        xv, gv = x_ref[...], g_ref[...]
        hb = _rms_fwd(xv, gv).astype(BF16)
        dh = None
        for pc_ref, wt_ref, dw_ref, n in zip(pc_refs, wt_refs, dw_refs, widths):
            d = _dot(pc_ref[...], wt_ref[...])
            dh = d if dh is None else dh + d
            ch = _tile(n, COL_CHUNK)
            for n0 in range(0, n, ch):
                dw_ref[:, n0:n0 + ch] += _dot_tn(hb, pc_ref[:, n0:n0 + ch])
        dxn, dgp = _rms_bwd(xv, gv, dh)
        dg_ref[...] += dgp
        dx_ref[...] = dres_ref[...] + dxn

    return _row_call(name, body, T, tm, list(pieces) + [x, dres], [g] + list(wts), [(D, F32)],
                     acc_outs=[((D, n), F32) for n in widths] + [((1, D), F32)])


def _conv_bwd(dy, b, c, u, zc, cw):
    T, D = dy.shape
    tm = _tile(T, ROW_TILE_BWD)

    def body(dy_ref, b_ref, c_ref, u_ref, zc_ref, cw_ref, db_ref, dc_ref, du_ref, dcw_ref, head_ref):
        @pl.when(pl.program_id(0) == 0)
        def _():
            head_ref[...] = jnp.zeros_like(head_ref)
            dcw_ref[...] = jnp.zeros_like(dcw_ref)

        dyv, cv, uv = dy_ref[...], c_ref[...], u_ref[...]
        db_ref[...] = (dyv * zc_ref[...]).astype(BF16)
        dzc = dyv * b_ref[...]
        row = lax.broadcasted_iota(jnp.int32, (tm, 1), 0)
        h0, h1 = head_ref[0:1, :], head_ref[1:2, :]
        d1 = jnp.where(row == tm - 1, h0, pltpu.roll(dzc, tm - 1, axis=0))
        d2 = jnp.where(row == tm - 1, h1, jnp.where(row == tm - 2, h0, pltpu.roll(dzc, tm - 2, axis=0)))
        dz = cw_ref[2:3, :] * dzc + cw_ref[1:2, :] * d1 + cw_ref[0:1, :] * d2
        dc_ref[...] = (dz * uv).astype(BF16)
        du_ref[...] = (dz * cv).astype(BF16)
        z = cv * uv
        dcw_ref[0:1, :] += jnp.sum(d2 * z, axis=0, keepdims=True)
        dcw_ref[1:2, :] += jnp.sum(d1 * z, axis=0, keepdims=True)
        dcw_ref[2:3, :] += jnp.sum(dzc * z, axis=0, keepdims=True)
        head_ref[...] = dzc[0:8, :]

    return _row_call("conv_bwd", body, T, tm, [dy, b, c, u, zc], [cw], [(D, BF16), (D, BF16), (D, BF16)],
                     acc_outs=[((8, D), F32)], scratch=[pltpu.VMEM((8, D), F32)], reverse=True)


def _ple_bwd(dres, x, p, gl, pe, g4, g5, wgt):
    T, D = x.shape
    P = p.shape[1]
    tm = _tile(T, ROW_TILE_BWD)

    def body(dres_ref, x_ref, p_ref, gl_ref, pe_ref, g4_ref, g5_ref, wgt_ref,
             dx_ref, dwp_ref, dwg_ref, dg4_ref, dg5_ref):
        @pl.when(pl.program_id(0) == 0)
        def _():
            for r in (dwp_ref, dwg_ref, dg4_ref, dg5_ref):
                r[...] = jnp.zeros_like(r)

        dr, xv, pe_v = dres_ref[...], x_ref[...], pe_ref[...]
        gate = 1.0 / (1.0 + jnp.exp(-gl_ref[...]))
        de, dg5p = _rms_bwd(pe_v * gate, g5_ref[...], dr)
        dg5_ref[...] += dg5p
        dpe = (de * gate).astype(BF16)
        dgl = (de * pe_v * gate * (1.0 - gate)).astype(BF16)
        dwp_ref[...] += _dot_tn(p_ref[...].astype(BF16), dpe)
        g4v = g4_ref[...]
        dwg_ref[...] += _dot_tn(_rms_fwd(xv, g4v).astype(BF16), dgl)
        dxn, dg4p = _rms_bwd(xv, g4v, _dot(dgl, wgt_ref[...]))
        dg4_ref[...] += dg4p
        dx_ref[...] = dr + dxn

    return _row_call("ple_bwd", body, T, tm, [dres, x, p, gl, pe], [g4, g5, wgt], [(D, F32)],
                     acc_outs=[((P, D), F32), ((D, D), F32), ((1, D), F32), ((1, D), F32)])


def _scan_lanes(v, reverse):
    n = v.shape[1]
    lane = lax.broadcasted_iota(jnp.int32, v.shape, 1)
    s = 1
    while s < n:
        if reverse:
            v = v + jnp.where(lane < n - s, pltpu.roll(v, n - s, axis=1), 0.0)
        else:
            v = v + jnp.where(lane >= s, pltpu.roll(v, s, axis=1), 0.0)
        s *= 2
    return v


def _gate_cumsum(lf_t):
    def body(lf_ref, *piece_refs):
        rest = _scan_lanes(lf_ref[...], reverse=False)
        for r in piece_refs:
            piece = rest.astype(BF16)
            r[...] = piece
            rest = rest - piece.astype(F32)

    return pl.pallas_call(body, name="gate_cumsum", out_shape=[jax.ShapeDtypeStruct(lf_t.shape, BF16)] * N_AUG,
                          compiler_params=pltpu.CompilerParams(vmem_limit_bytes=VMEM_LIMIT))(lf_t)


def _gate_bwd(drow_t, dcol_t, fl_t):
    H = fl_t.shape[0]

    def body(dr_ref, dc_ref, fl_ref, dfl_ref, dbf_ref):
        dlf = _scan_lanes(dr_ref[...] - dc_ref[...], reverse=True)
        dfl = dlf * (1.0 / (1.0 + jnp.exp(fl_ref[...])))
        dfl_ref[...] = dfl
        dbf_ref[...] = jnp.sum(dfl, axis=1, keepdims=True)

    return pl.pallas_call(
        body, name="gate_bwd",
        out_shape=(jax.ShapeDtypeStruct(fl_t.shape, F32), jax.ShapeDtypeStruct((H, 1), F32)),
        compiler_params=pltpu.CompilerParams(vmem_limit_bytes=VMEM_LIMIT))(drow_t, dcol_t, fl_t)


def _aug_operands(pieces, H, D):
    T = pieces[0].shape[1]
    dh = D // H
    one = jnp.ones((H, T), BF16)
    qa = jnp.stack(list(pieces) + [one] * N_AUG, axis=-1)
    ka = jnp.stack([one] * N_AUG + [-pc for pc in pieces], axis=-1)

    def place(a):
        a = jnp.pad(a, ((0, 0), (0, 0), (0, dh - 2 * N_AUG))).reshape(H // 2, 2, T, dh)
        return jnp.transpose(a[:, ::-1], (2, 0, 1, 3)).reshape(T, D)

    return place(qa), place(ka)


def _strip_kind(koff, k0, ksz, q0, qsz):
    if koff is None or koff + k0 + ksz - 1 <= q0:
        return "full"
    return "skip" if koff + k0 > q0 + qsz - 1 else "partial"


def _score_strip(km, qm, koff, k0, ksz, q0, qsz, kind):
    st = _dot_nt(km[k0:k0 + ksz, :], qm[q0:q0 + qsz, :])
    if kind == "partial":
        kpos = koff + k0 + lax.broadcasted_iota(jnp.int32, (ksz, qsz), 0)
        st = jnp.where(kpos <= q0 + lax.broadcasted_iota(jnp.int32, (ksz, qsz), 1), st, NEG_INF)
    return st


def _fold8(v, op):
    rows, n = v.shape
    v3 = v.reshape(rows // 8, 8, n)
    out = v3[0]
    for r in range(1, rows // 8):
        out = op(out, v3[r])
    return out


def _causal_tables(nbq, r, q_outer):
    a, b = [], []
    for o in range(nbq if q_outer else nbq * r):
        inner = range((o + 1) * r) if q_outer else range(o // r, nbq)
        for n in inner:
            a.append(o)
            b.append(n)
    return jnp.asarray(np.array(a, np.int32)), jnp.asarray(np.array(b, np.int32))


def _attn_tiles(T, q_tile):
    tq = _tile(T, q_tile)
    tk = _tile(tq, ATTN_K_TILE)
    return tq, tk, _tile(tk, ATTN_KEY_STRIP), _tile(tq, ATTN_QUERY_STRIP)


def _flash_fwd(q, k, v, qaug, kaug, H):
    T, D = q.shape
    Hp = H // 2
    W = D // Hp
    dh = W // 2
    tq, tk, ksz, qsz = _attn_tiles(T, ATTN_Q_TILE_FWD)
    r = tq // tk
    ii, jj = _causal_tables(T // tq, r, q_outer=True)
    n_steps = int(ii.shape[0])

    def body(ii_ref, jj_ref, q_ref, k_ref, v_ref, qa_ref, ka_ref, o_ref, lsea_ref, lseb_ref,
             qm_s, m_s, l_s, acc_s, p_s):
        n = pl.program_id(1)
        i, j = ii_ref[n], jj_ref[n]
        d = j - i * r
        in_a = lax.broadcasted_iota(jnp.int32, (1, W), 1) < dh
        top = lax.broadcasted_iota(jnp.int32, (W, 1), 0) < dh

        @pl.when(j == 0)
        def _():
            qv, qa = q_ref[...], qa_ref[...]
            qm_s[0] = jnp.where(in_a, qv, qa)
            qm_s[1] = jnp.where(in_a, qa, qv)
            m_s[...] = jnp.full(m_s.shape, NEG_INF, F32)
            l_s[...] = jnp.zeros_like(l_s)
            acc_s[...] = jnp.zeros_like(acc_s)

        def step(koff):
            kv, ka, vv = k_ref[...], ka_ref[...], v_ref[...]
            kms = (jnp.where(in_a, kv, ka), jnp.where(in_a, ka, kv))
            vt = vv.T
            alphas = {}

            def softmax_stage(hh, q0):
                cols = slice(q0, q0 + qsz)
                kinds = [(k0, _strip_kind(koff, k0, ksz, q0, qsz)) for k0 in range(0, tk, ksz)]
                if all(kind == "skip" for _, kind in kinds):
                    return False
                strips, part = {}, None
                for k0, kind in kinds:
                    if kind != "skip":
                        strips[k0] = _score_strip(kms[hh], qm_s[hh], koff, k0, ksz, q0, qsz, kind)
                        p8 = _fold8(strips[k0], jnp.maximum)
                        part = p8 if part is None else jnp.maximum(part, p8)
                m_prev = m_s[hh, :, cols]
                m_new = jnp.maximum(m_prev, jnp.max(part, axis=0, keepdims=True))
                l8 = None
                for k0, kind in kinds:
                    if kind == "skip":
                        p_s[hh, k0:k0 + ksz, cols] = jnp.zeros((ksz, qsz), BF16)
                        continue
                    pt = jnp.exp(strips[k0] - m_new)
                    s8 = _fold8(pt, jnp.add)
                    l8 = s8 if l8 is None else l8 + s8
                    p_s[hh, k0:k0 + ksz, cols] = pt.astype(BF16)
                alpha = jnp.exp(m_prev - m_new)
                l_s[hh, :, cols] = alpha * l_s[hh, :, cols] + jnp.sum(l8, axis=0, keepdims=True)
                m_s[hh, :, cols] = m_new
                alphas[(hh, q0)] = alpha
                return True

            def value_stage(hh, q0):
                cols, rows = slice(q0, q0 + qsz), slice(hh * dh, (hh + 1) * dh)
                acc_s[rows, cols] = acc_s[rows, cols] * alphas[(hh, q0)] + _dot(vt[rows, :], p_s[hh, :, cols])

            units = [(hh, q0) for q0 in range(0, tq, qsz) for hh in range(2)]
            pending = None
            for unit in units:
                live = softmax_stage(*unit)
                if pending is not None:
                    value_stage(*pending)
                pending = unit if live else None
            if pending is not None:
                value_stage(*pending)

        @pl.when(d < 0)
        def _():
            step(None)

        for dd in range(r):
            @pl.when(d == dd)
            def _():
                step(dd * tk)

        @pl.when(d == r - 1)
        def _():
            inv = jnp.where(top, 1.0 / l_s[0], 1.0 / l_s[1])
            o_ref[...] = (acc_s[...] * inv).T.astype(BF16)
            lsea_ref[...] = m_s[0] + jnp.log(l_s[0])
            lseb_ref[...] = m_s[1] + jnp.log(l_s[1])

    qspec = pl.BlockSpec((tq, W), lambda hp, n, ii, jj: (ii[n], hp))
    kspec = pl.BlockSpec((tk, W), lambda hp, n, ii, jj: (jj[n], hp))
    rspec = pl.BlockSpec((None, 1, tq), lambda hp, n, ii, jj: (hp, 0, ii[n]))
    grid_spec = pltpu.PrefetchScalarGridSpec(
        num_scalar_prefetch=2, grid=(Hp, n_steps),
        in_specs=[qspec, kspec, kspec, qspec, kspec],
        out_specs=[qspec, rspec, rspec],
        scratch_shapes=[pltpu.VMEM((2, tq, W), BF16), pltpu.VMEM((2, 1, tq), F32), pltpu.VMEM((2, 1, tq), F32),
                        pltpu.VMEM((W, tq), F32), pltpu.VMEM((2, tk, tq), BF16)],
    )
    return pl.pallas_call(
        body, name="flash_fwd", grid_spec=grid_spec,
        out_shape=(jax.ShapeDtypeStruct((T, D), BF16), jax.ShapeDtypeStruct((Hp, 1, T), F32),
                   jax.ShapeDtypeStruct((Hp, 1, T), F32)),
        compiler_params=_params(2),
    )(ii, jj, q, k, v, qaug, kaug)


def _attn_delta(do, o, head_sel):
    T, D = do.shape
    H = head_sel.shape[0]
    tm = _tile(T, ROW_TILE_FWD)

    def body(do_ref, o_ref, sel_ref, out_ref):
        prod = do_ref[...].astype(F32) * o_ref[...].astype(F32)
        out_ref[...] = lax.dot_general(sel_ref[...], prod, (((1,), (1,)), ((), ())),
                                       precision=lax.Precision.HIGHEST, preferred_element_type=F32)

    return pl.pallas_call(
        body, name="attn_delta", grid=(T // tm,),
        in_specs=[pl.BlockSpec((tm, D), lambda i: (i, 0)), pl.BlockSpec((tm, D), lambda i: (i, 0)),
                  pl.BlockSpec((H, D), lambda i: (0, 0))],
        out_specs=pl.BlockSpec((H, tm), lambda i: (0, i)),
        out_shape=jax.ShapeDtypeStruct((H, T), F32), compiler_params=_params())(do, o, head_sel)


def _flash_bwd(q, k, v, qaug, kaug, do, lse_a, lse_b, delta, qscale):
    T, D = q.shape
    Hp = lse_a.shape[0]
    W = D // Hp
    dh = W // 2
    tq, tk, ksz, qsz = _attn_tiles(T, ATTN_Q_TILE_BWD)
    r = tq // tk
    nbq = T // tq
    jj, ii = _causal_tables(nbq, r, q_outer=False)
    n_steps = int(ii.shape[0])
    row_row = dh
    col_row = dh + N_AUG

    def body(jj_ref, ii_ref, q_ref, k_ref, v_ref, qa_ref, ka_ref, do_ref, lsea_ref, lseb_ref, dla_ref, dlb_ref,
             dqt_ref, dkt_ref, dvt_ref, dcola_ref, dcolb_ref, drowa_ref, drowb_ref,
             dqt_acc, dk_acc, dv_acc, *rot):
        n = pl.program_id(1)
        i, j = ii_ref[n], jj_ref[n]
        d = j - i * r
        in_a = lax.broadcasted_iota(jnp.int32, (1, W), 1) < dh
        top = lax.broadcasted_iota(jnp.int32, (W, 1), 0) < dh

        @pl.when(n == 0)
        def _():
            dqt_acc[...] = jnp.zeros_like(dqt_acc)

        @pl.when(d >= 0)
        def _():
            dk_acc[...] = jnp.zeros_like(dk_acc)
            dv_acc[...] = jnp.zeros_like(dv_acc)

        def step(koff):
            qv, qa, kv, ka = q_ref[...], qa_ref[...], k_ref[...], ka_ref[...]
            vv, dov = v_ref[...], do_ref[...]
            zb = jnp.zeros_like(dov)
            kms = (jnp.where(in_a, kv, ka), jnp.where(in_a, ka, kv))
            qms = (jnp.where(in_a, qv, qa), jnp.where(in_a, qa, qv))
            doms = (jnp.where(in_a, dov, zb), jnp.where(in_a, zb, dov))
            def own_first(a, hh):
                return a[:dh + AUG_ROWS] if hh == 0 else jnp.concatenate([a[dh:], a[:AUG_ROWS]], axis=0)

            kxs = tuple(own_first(kms[hh].T, hh) for hh in range(2))
            qxs = tuple(own_first(qms[hh].T, hh) for hh in range(2))
            dot = dov.T
            dxs = (dot[:dh], dot[dh:])
            lses, dls = (lsea_ref[...], lseb_ref[...]), (dla_ref[...], dlb_ref[...])
            dvs, dks = ([], []), ([], [])

            def score_stage(u, hh, q0):
                p_s, ds_s = rot[2 * (u % ATTN_ROTATE)], rot[2 * (u % ATTN_ROTATE) + 1]
                cols = slice(q0, q0 + qsz)
                kinds = [(k0, _strip_kind(koff, k0, ksz, q0, qsz)) for k0 in range(0, tk, ksz)]
                if all(kind == "skip" for _, kind in kinds):
                    return False
                for k0, kind in kinds:
                    if kind == "skip":
                        p_s[k0:k0 + ksz, :] = jnp.zeros((ksz, qsz), BF16)
                        ds_s[k0:k0 + ksz, :] = jnp.zeros((ksz, qsz), BF16)
                        continue
                    st = _score_strip(kms[hh], qms[hh], koff, k0, ksz, q0, qsz, kind)
                    pt = jnp.exp(st - lses[hh][:, cols])
                    dst = pt * (_dot_nt(vv[k0:k0 + ksz, :], doms[hh][cols, :]) - dls[hh][:, cols])
                    p_s[k0:k0 + ksz, :] = pt.astype(BF16)
                    ds_s[k0:k0 + ksz, :] = dst.astype(BF16)
                return True

            def grad_stage(u, hh, q0):
                p_s, ds_s = rot[2 * (u % ATTN_ROTATE)], rot[2 * (u % ATTN_ROTATE) + 1]
                cols = slice(q0, q0 + qsz)
                ptb, dsb = p_s[...], ds_s[...]
                dvs[hh].append(_dot_nt(dxs[hh][:, cols], ptb))
                dks[hh].append(_dot_nt(qxs[hh][:, cols], dsb))
                dqt_acc[i, hh, :, cols] += _dot(kxs[hh], dsb)

            order = [(hh, q0) for q0 in range(0, tq, qsz) for hh in range(2)]
            units = [(u, hh, q0) for u, (hh, q0) in enumerate(order)]
            pending = None
            for unit in units:
                live = score_stage(*unit)
                if pending is not None:
                    grad_stage(*pending)
                pending = unit if live else None
            if pending is not None:
                grad_stage(*pending)
            for hh in range(2):
                dv_acc[hh] += sum(dvs[hh][1:], dvs[hh][0])
                dk_acc[hh] += sum(dks[hh][1:], dks[hh][0])

        @pl.when(d < 0)
        def _():
            step(None)

        for dd in range(r):
            @pl.when(d == dd)
            def _():
                step(dd * tk)

        @pl.when(i == nbq - 1)
        def _():
            dkt_ref[...] = jnp.concatenate([dk_acc[0, :dh], dk_acc[1, :dh]], axis=0).astype(BF16)
            dvt_ref[...] = jnp.concatenate([dv_acc[0], dv_acc[1]], axis=0).astype(BF16)
            dcola_ref[...] = dk_acc[0, col_row:col_row + 1, :]
            dcolb_ref[...] = dk_acc[1, col_row:col_row + 1, :]

        @pl.when(n == n_steps - 1)
        def _():
            for b in range(nbq):
                cols = slice(b * tq, (b + 1) * tq)
                both = jnp.concatenate([dqt_acc[b, 0, :dh], dqt_acc[b, 1, :dh]], axis=0)
                dqt_ref[:, cols] = (both * qscale).astype(BF16)
                drowa_ref[:, cols] = dqt_acc[b, 0, row_row:row_row + 1, :]
                drowb_ref[:, cols] = dqt_acc[b, 1, row_row:row_row + 1, :]

    qspec = pl.BlockSpec((tq, W), lambda hp, n, jj, ii: (ii[n], hp))
    kspec = pl.BlockSpec((tk, W), lambda hp, n, jj, ii: (jj[n], hp))
    ktspec = pl.BlockSpec((W, tk), lambda hp, n, jj, ii: (hp, jj[n]))
    pair_row = pl.BlockSpec((None, 1, tq), lambda hp, n, jj, ii: (hp, 0, ii[n]))
    key_row = pl.BlockSpec((None, 1, tk), lambda hp, n, jj, ii: (hp, 0, jj[n]))
    whole_row = pl.BlockSpec((None, 1, T), lambda hp, n, jj, ii: (hp, 0, 0))
    grid_spec = pltpu.PrefetchScalarGridSpec(
        num_scalar_prefetch=2, grid=(Hp, n_steps),
        in_specs=[
            qspec, kspec, kspec, qspec, kspec, qspec, pair_row, pair_row,
            pl.BlockSpec((None, 1, tq), lambda hp, n, jj, ii: (2 * hp, 0, ii[n])),
            pl.BlockSpec((None, 1, tq), lambda hp, n, jj, ii: (2 * hp + 1, 0, ii[n])),
        ],
        out_specs=[
            pl.BlockSpec((W, T), lambda hp, n, jj, ii: (hp, 0)),
            ktspec, ktspec, key_row, key_row, whole_row, whole_row,
        ],
        scratch_shapes=[pltpu.VMEM((nbq, 2, dh + AUG_ROWS, tq), F32), pltpu.VMEM((2, dh + AUG_ROWS, tk), F32),
                        pltpu.VMEM((2, dh, tk), F32)] + [pltpu.VMEM((tk, qsz), BF16)] * (2 * ATTN_ROTATE),
    )
    return pl.pallas_call(
        body, name="flash_bwd", grid_spec=grid_spec,
        out_shape=(jax.ShapeDtypeStruct((D, T), BF16), jax.ShapeDtypeStruct((D, T), BF16),
                   jax.ShapeDtypeStruct((D, T), BF16), jax.ShapeDtypeStruct((Hp, 1, T), F32),
                   jax.ShapeDtypeStruct((Hp, 1, T), F32),
                   jax.ShapeDtypeStruct((Hp, 1, T), F32), jax.ShapeDtypeStruct((Hp, 1, T), F32)),
        compiler_params=_params(2),
    )(jj, ii, q, k, v, qaug, kaug, do, lse_a, lse_b, delta, delta)


HBM_SPEC = pl.BlockSpec(memory_space=pltpu.HBM)


def _mesh_pos():
    return lax.axis_index("x"), lax.axis_index("y"), lax.axis_index("c")


def _all_gather_shards(shards):
    n = len(shards)

    def body(*refs):
        in_refs, out_refs = refs[:n], refs[n:2 * n]
        send1, recv1, send2, recv2, lsem = refs[2 * n:]
        x, y, c = _mesh_pos()
        me = 2 * x + y
        chips = [(1 - x, y), (x, 1 - y), (1 - x, 1 - y)]
        local = [pltpu.make_async_copy(in_refs[t], out_refs[t].at[me], lsem.at[t]) for t in range(n)]
        for cp in local:
            cp.start()

        def half(t, chip_idx, pc):
            hr = shards[t].shape[0] // 2
            return out_refs[t].at[chip_idx, pl.ds(pc * hr, hr), :]

        first = []
        for t in range(n):
            hr = shards[t].shape[0] // 2
            for kk, (cx, cy) in enumerate(chips):
                first.append(pltpu.make_async_remote_copy(
                    src_ref=in_refs[t].at[pl.ds(c * hr, hr), :], dst_ref=half(t, me, c),
                    send_sem=send1.at[3 * t + kk], recv_sem=recv1.at[3 * t + kk],
                    device_id=(cx, cy, c), device_id_type=MESH))
        for cp in first:
            cp.start()
        passed = []
        for t in range(n):
            for kk, (cx, cy) in enumerate(chips):
                src_chip = 2 * cx + cy
                landed = half(t, src_chip, c)
                pltpu.make_async_remote_copy(
                    src_ref=landed, dst_ref=landed, send_sem=send1.at[3 * t + kk], recv_sem=recv1.at[3 * t + kk],
                    device_id=(cx, cy, c), device_id_type=MESH).wait_recv()
                fwd = pltpu.make_async_remote_copy(
                    src_ref=landed, dst_ref=landed, send_sem=send2.at[3 * t + kk], recv_sem=recv2.at[3 * t + kk],
                    device_id=(x, y, 1 - c), device_id_type=MESH)
                fwd.start()
                passed.append(fwd)
        for t in range(n):
            for kk, (cx, cy) in enumerate(chips):
                other = half(t, 2 * cx + cy, 1 - c)
                pltpu.make_async_remote_copy(
                    src_ref=other, dst_ref=other, send_sem=send2.at[3 * t + kk], recv_sem=recv2.at[3 * t + kk],
                    device_id=(x, y, 1 - c), device_id_type=MESH).wait_recv()
        for cp in first + passed:
            cp.wait_send()
        for cp in local:
            cp.wait()

    return pl.pallas_call(
        body, name="weights_all_gather",
        out_shape=[jax.ShapeDtypeStruct((N_CHIPS,) + s.shape, s.dtype) for s in shards],
        in_specs=[HBM_SPEC] * n, out_specs=[HBM_SPEC] * n,
        scratch_shapes=[pltpu.SemaphoreType.DMA((3 * n,)), pltpu.SemaphoreType.DMA((3 * n,)),
                        pltpu.SemaphoreType.DMA((3 * n,)), pltpu.SemaphoreType.DMA((3 * n,)),
                        pltpu.SemaphoreType.DMA((n,))],
    )(*shards)


def _sibling_swap_halves(gs):
    n = len(gs)

    def body(*refs):
        g_refs, r_refs = refs[:n], refs[n:2 * n]
        ssem, rsem = refs[2 * n:]
        x, y, c = _mesh_pos()
        cps = []
        for t in range(n):
            hr = gs[t].shape[1] // 2
            cps += [pltpu.make_async_remote_copy(
                src_ref=g_refs[t].at[s, pl.ds((1 - c) * hr, hr), :], dst_ref=r_refs[t].at[s],
                send_sem=ssem.at[N_CHIPS * t + s], recv_sem=rsem.at[N_CHIPS * t + s],
                device_id=(x, y, 1 - c), device_id_type=MESH) for s in range(N_CHIPS)]
        for cp in cps:
            cp.start()
        for cp in cps:
            cp.wait()

    return pl.pallas_call(
        body, name="grads_sibling_swap",
        out_shape=[jax.ShapeDtypeStruct((N_CHIPS, g.shape[1] // 2, g.shape[2]), F32) for g in gs],
        in_specs=[HBM_SPEC] * n, out_specs=[HBM_SPEC] * n,
        scratch_shapes=[pltpu.SemaphoreType.DMA((N_CHIPS * n,)), pltpu.SemaphoreType.DMA((N_CHIPS * n,))],
    )(*gs)


def _pair_add(g, r, c_idx):
    _, M, C = g.shape
    hr = M // 2
    tr = _tile(hr, PACK_ROW_TILE)
    nbk = hr // tr

    def body(c_ref, g_ref, r_ref, o_ref):
        o_ref[...] = (g_ref[...] + r_ref[...]).astype(BF16)

    grid_spec = pltpu.PrefetchScalarGridSpec(
        num_scalar_prefetch=1, grid=(N_CHIPS, nbk),
        in_specs=[pl.BlockSpec((None, tr, C), lambda s, i, c: (s, c[0] * nbk + i, 0)),
                  pl.BlockSpec((None, tr, C), lambda s, i, c: (s, i, 0))],
        out_specs=pl.BlockSpec((None, tr, C), lambda s, i, c: (s, i, 0)),
    )
    return pl.pallas_call(body, name="grads_pair_add", grid_spec=grid_spec,
                          out_shape=jax.ShapeDtypeStruct((N_CHIPS, hr, C), BF16),
                          compiler_params=_params(2))(c_idx, g, r)


def _chip_scatter(pps):
    n = len(pps)

    def body(*refs):
        p_refs, r_refs = refs[:n], refs[n:2 * n]
        ssem, rsem, lsem = refs[2 * n:]
        x, y, c = _mesh_pos()
        me = 2 * x + y
        chips = [(1 - x, y), (x, 1 - y), (1 - x, 1 - y)]
        own = [pltpu.make_async_copy(p_refs[t].at[me], r_refs[t].at[me], lsem.at[t]) for t in range(n)]
        for cp in own:
            cp.start()
        cps = [pltpu.make_async_remote_copy(
            src_ref=p_refs[t].at[2 * cx + cy], dst_ref=r_refs[t].at[me],
            send_sem=ssem.at[3 * t + kk], recv_sem=rsem.at[3 * t + kk],
            device_id=(cx, cy, c), device_id_type=MESH) for t in range(n) for kk, (cx, cy) in enumerate(chips)]
        for cp in cps:
            cp.start()
        for t in range(n):
            for kk, (cx, cy) in enumerate(chips):
                got = r_refs[t].at[2 * cx + cy]
                pltpu.make_async_remote_copy(
                    src_ref=got, dst_ref=got, send_sem=ssem.at[3 * t + kk], recv_sem=rsem.at[3 * t + kk],
                    device_id=(cx, cy, c), device_id_type=MESH).wait_recv()
        for cp in cps:
            cp.wait_send()
        for cp in own:
            cp.wait()

    return pl.pallas_call(
        body, name="grads_chip_scatter", out_shape=[jax.ShapeDtypeStruct(pp.shape, pp.dtype) for pp in pps],
        in_specs=[HBM_SPEC] * n, out_specs=[HBM_SPEC] * n,
        scratch_shapes=[pltpu.SemaphoreType.DMA((3 * n,)), pltpu.SemaphoreType.DMA((3 * n,)),
                        pltpu.SemaphoreType.DMA((n,))],
    )(*pps)


def _chip_sum(r, c_idx):
    _, hr, C = r.shape
    tr = _tile(hr, PACK_ROW_TILE)
    nbk = hr // tr

    def body(c_ref, r_ref, o_ref):
        r0, r1, r2, r3 = (r_ref[s].astype(F32) for s in range(N_CHIPS))
        o_ref[...] = ((r0 + r1) + r2) + r3

    grid_spec = pltpu.PrefetchScalarGridSpec(
        num_scalar_prefetch=1, grid=(nbk,),
        in_specs=[pl.BlockSpec((N_CHIPS, tr, C), lambda i, c: (0, i, 0))],
        out_specs=pl.BlockSpec((tr, C), lambda i, c: (c[0] * nbk + i, 0)),
    )
    return pl.pallas_call(body, name="grads_chip_sum", grid_spec=grid_spec,
                          out_shape=jax.ShapeDtypeStruct((2 * hr, C), F32), compiler_params=_params())(c_idx, r)


def _sibling_join(bufs):
    n = len(bufs)

    def body(*refs):
        o_refs = refs[n:2 * n]
        ssem, rsem = refs[2 * n:]
        x, y, c = _mesh_pos()
        cps = []
        for t in range(n):
            hr = bufs[t].shape[0] // 2
            mine = o_refs[t].at[pl.ds(c * hr, hr), :]
            cps.append(pltpu.make_async_remote_copy(src_ref=mine, dst_ref=mine, send_sem=ssem.at[t], recv_sem=rsem.at[t],
                                                    device_id=(x, y, 1 - c), device_id_type=MESH))
        for cp in cps:
            cp.start()
        for t in range(n):
            hr = bufs[t].shape[0] // 2
            theirs = o_refs[t].at[pl.ds((1 - c) * hr, hr), :]
            pltpu.make_async_remote_copy(src_ref=theirs, dst_ref=theirs, send_sem=ssem.at[t], recv_sem=rsem.at[t],
                                         device_id=(x, y, 1 - c), device_id_type=MESH).wait_recv()
        for cp in cps:
            cp.wait_send()

    return pl.pallas_call(
        body, name="grads_sibling_join", out_shape=[jax.ShapeDtypeStruct(b.shape, F32) for b in bufs],
        in_specs=[HBM_SPEC] * n, out_specs=[HBM_SPEC] * n, input_output_aliases={t: t for t in range(n)},
        scratch_shapes=[pltpu.SemaphoreType.DMA((n,)), pltpu.SemaphoreType.DMA((n,))],
    )(*bufs)


def _adamw(w, g, m, v):
    M, C = w.shape
    tr = _tile(M, PACK_ROW_TILE)

    def body(w_ref, g_ref, m_ref, v_ref, d_ref, mo_ref, vo_ref):
        gv = g_ref[...]
        mn = ADAM_B1 * m_ref[...] + (1.0 - ADAM_B1) * gv
        vn = ADAM_B2 * v_ref[...] + (1.0 - ADAM_B2) * (gv * gv)
        m_hat = mn / (1.0 - ADAM_B1 ** ADAM_STEP)
        v_hat = vn / (1.0 - ADAM_B2 ** ADAM_STEP)
        d_ref[...] = -ADAM_LR * (m_hat / (jnp.sqrt(v_hat) + ADAM_EPS) + ADAM_WD * w_ref[...])
        mo_ref[...] = mn
        vo_ref[...] = vn

    spec = pl.BlockSpec((tr, C), lambda i: (i, 0))
    return pl.pallas_call(
        body, name="adamw", grid=(M // tr,), in_specs=[spec] * 4, out_specs=[spec] * 3,
        out_shape=[jax.ShapeDtypeStruct((M, C), F32)] * 3, compiler_params=_params())(w, g, m, v)


WEIGHT_NAMES = ("norm_g", "w_attn_in", "b_forget", "w_attn_out", "w_conv_in", "conv_w", "w_conv_out",
                "w_mlp_up", "w_mlp_down", "w_ple_proj", "w_ple_gate")
COL_SHARDED = ("norm_g", "w_attn_in", "w_conv_in", "conv_w", "w_mlp_up", "w_ple_proj")
ROW_SHARDED = ("w_attn_out", "w_conv_out", "w_mlp_down", "w_ple_gate")


def _unshard(name, gathered, shard_shape):
    a = gathered.reshape((N_CHIPS,) + tuple(shard_shape))
    if name in COL_SHARDED:
        a = jnp.moveaxis(a, 0, -2)
        return a.reshape(a.shape[:-2] + (N_CHIPS * shard_shape[-1],))
    a = jnp.moveaxis(a, 0, 1)
    return a.reshape((shard_shape[0], N_CHIPS * shard_shape[1], shard_shape[2]))


def _to_shard_major(name, full):
    if name == "b_forget":
        return jnp.broadcast_to(full.reshape(1, -1), (N_CHIPS, full.size))
    if name in COL_SHARDED:
        a = full.reshape(full.shape[:-1] + (N_CHIPS, full.shape[-1] // N_CHIPS))
        a = jnp.moveaxis(a, -2, 0)
    else:
        a = full.reshape((full.shape[0], N_CHIPS, full.shape[1] // N_CHIPS, full.shape[2]))
        a = jnp.moveaxis(a, 1, 0)
    return a.reshape(N_CHIPS, -1)


def _reduce_scatter(tensors, c_idx):
    swapped = _sibling_swap_halves(tensors)
    pairs = [_pair_add(g, r, c_idx) for g, r in zip(tensors, swapped)]
    return _sibling_join([_chip_sum(r, c_idx) for r in _chip_scatter(pairs)])


def kernel(x, p, norm_g, w_attn_in, b_forget, w_attn_out, w_conv_in, conv_w, w_conv_out, w_mlp_up, w_mlp_down, w_ple_proj, w_ple_gate, loss_target, m_norm_g, m_w_attn_in, m_b_forget, m_w_attn_out, m_w_conv_in, m_conv_w, m_w_conv_out, m_w_mlp_up, m_w_mlp_down, m_w_ple_proj, m_w_ple_gate, v_norm_g, v_w_attn_in, v_b_forget, v_w_attn_out, v_w_conv_in, v_conv_w, v_w_conv_out, v_w_mlp_up, v_w_mlp_down, v_w_ple_proj, v_w_ple_gate):
    w_local = dict(norm_g=norm_g, w_attn_in=w_attn_in, b_forget=b_forget, w_attn_out=w_attn_out,
                   w_conv_in=w_conv_in, conv_w=conv_w, w_conv_out=w_conv_out, w_mlp_up=w_mlp_up,
                   w_mlp_down=w_mlp_down, w_ple_proj=w_ple_proj, w_ple_gate=w_ple_gate)
    m_local = dict(norm_g=m_norm_g, w_attn_in=m_w_attn_in, b_forget=m_b_forget, w_attn_out=m_w_attn_out,
                   w_conv_in=m_w_conv_in, conv_w=m_conv_w, w_conv_out=m_w_conv_out, w_mlp_up=m_w_mlp_up,
                   w_mlp_down=m_w_mlp_down, w_ple_proj=m_w_ple_proj, w_ple_gate=m_w_ple_gate)
    v_local = dict(norm_g=v_norm_g, w_attn_in=v_w_attn_in, b_forget=v_b_forget, w_attn_out=v_w_attn_out,
                   w_conv_in=v_w_conv_in, conv_w=v_conv_w, w_conv_out=v_w_conv_out, w_mlp_up=v_w_mlp_up,
                   w_mlp_down=v_w_mlp_down, w_ple_proj=v_w_ple_proj, w_ple_gate=v_w_ple_gate)
    shard_shapes = {k: tuple(a.shape) for k, a in w_local.items()}

    xs = x[0]
    target = loss_target[0]
    T, D = xs.shape
    depth = p.shape[0]
    H = b_forget.shape[1]
    qscale = float(D // H) ** -0.5
    head_sel = (jnp.arange(D)[None, :] // (D // H) == jnp.arange(H)[:, None]).astype(F32)

    big = [n for n in WEIGHT_NAMES if n not in ("norm_g", "conv_w", "b_forget")]
    small = jnp.concatenate([norm_g.reshape(-1), conv_w.reshape(-1)])
    n_small = small.shape[0]
    small_rows = -(-n_small // (LANES * 16)) * 16
    small = jnp.pad(small, (0, small_rows * LANES - n_small)).reshape(small_rows, LANES)
    shards = [w_local[n].astype(BF16).reshape(-1, shard_shapes[n][-1]) for n in big] + [small]
    gathered = _all_gather_shards(shards)
    full = {n: _unshard(n, g, shard_shapes[n]) for n, g in zip(big, gathered[:-1])}
    gs = gathered[-1].reshape(N_CHIPS, -1)
    full["norm_g"] = _unshard("norm_g", gs[:, :norm_g.size], shard_shapes["norm_g"])
    full["conv_w"] = _unshard("conv_w", gs[:, norm_g.size:n_small], shard_shapes["conv_w"])
    gains = full["norm_g"]

    def gain(i, k):
        return gains[i, k].reshape(1, D)

    def taps(j):
        return jnp.pad(full["conv_w"][j], ((0, 5), (0, 0)))

    saved = []
    h = xs
    for i in range(depth):
        j = i // 2
        s = {"x0": h}
        if i % 2 == 0:
            w_in = full["w_attn_in"][j]
            wqkv = w_in[:, :3 * D]
            wf = jnp.pad(w_in[:, 3 * D:], ((0, 0), (0, LANES - H)))
            bf = jnp.pad(b_forget[j].reshape(1, H), ((0, 0), (0, LANES - H)))
            q, k, v, fl, lf = _attn_in_fwd(h, gain(i, 0), wqkv, wf, bf, qscale)
            qaug, kaug = _aug_operands(_gate_cumsum(lf[:, :H].T), H, D)
            o, lse_a, lse_b = _flash_fwd(q, k, v, qaug, kaug, H)
            s.update(q=q, k=k, v=v, fl=fl, qaug=qaug, kaug=kaug, o=o, lse_a=lse_a, lse_b=lse_b, wqkv=wqkv, wf=wf)
            mix_in, w_out = o, full["w_attn_out"][j]
        else:
            b, c, u, zc, y = _conv_in_fwd(h, gain(i, 0), full["w_conv_in"][j], taps(j))
            s.update(b=b, c=c, u=u, zc=zc, y=y)
            mix_in, w_out = y, full["w_conv_out"][j]
        m1, x1 = _out_proj_fwd("mixer_out_fwd", mix_in, w_out, gain(i, 1), h)
        up, a = _mlp_up_fwd(x1, gain(i, 2), full["w_mlp_up"][i])
        m3, x2 = _out_proj_fwd("mlp_down_fwd", a, full["w_mlp_down"][i], gain(i, 3), x1)
        gl, pe, x3 = _ple_fwd(x2, p[i, 0], gain(i, 4), gain(i, 5), full["w_ple_gate"][i], full["w_ple_proj"][i])
        s.update(m1=m1, x1=x1, up=up, a=a, m3=m3, x2=x2, gl=gl, pe=pe, w_out=w_out)
        saved.append(s)
        h = x3

    dh, loss_blk = _loss_fwd_bwd(h, target)
    loss = lax.psum(loss_blk[0, 0], ("x", "y", "c"))

    g_gain = [[None] * 6 for _ in range(depth)]
    grads = {n: [None] * w_local[n].shape[0] for n in WEIGHT_NAMES if n != "norm_g"}
    for i in reversed(range(depth)):
        j = i // 2
        s = saved[i]
        dx2, dwp, dwg, g_gain[i][4], g_gain[i][5] = _ple_bwd(
            dh, s["x2"], p[i, 0], s["gl"], s["pe"], gain(i, 4), gain(i, 5), full["w_ple_gate"][i].T)
        grads["w_ple_proj"][i], grads["w_ple_gate"][i] = dwp, dwg
        dup, dwd, g_gain[i][3] = _out_proj_bwd(
            "mlp_down_bwd", dx2, s["m3"], gain(i, 3), [s["a"], s["up"]], full["w_mlp_down"][i].T, "relu2")
        grads["w_mlp_down"][i] = dwd
        dx1, dwu, g_gain[i][2] = _in_proj_bwd(
            "mlp_up_bwd", [dup], [full["w_mlp_up"][i].T], s["x1"], gain(i, 2), dx2)
        grads["w_mlp_up"][i] = dwu
        if i % 2 == 0:
            do, dwo, g_gain[i][1] = _out_proj_bwd(
                "attn_out_bwd", dx1, s["m1"], gain(i, 1), [s["o"]], s["w_out"].T, "plain")
            grads["w_attn_out"][j] = dwo
            delta = _attn_delta(do, s["o"], head_sel).reshape(H, 1, T)
            dqt, dkt, dvt, dca, dcb, dra, drb = _flash_bwd(s["q"], s["k"], s["v"], s["qaug"], s["kaug"], do,
                                                           s["lse_a"], s["lse_b"], delta, qscale)
            dcol = jnp.concatenate([dca, dcb], axis=1).reshape(H, T)
            drow = jnp.concatenate([dra, drb], axis=1).reshape(H, T)
            dfl_t, dbf = _gate_bwd(drow, dcol, s["fl"][:, :H].T)
            grads["b_forget"][j] = dbf.reshape(H)
            dfl = jnp.pad(dfl_t.T, ((0, 0), (0, LANES - H))).astype(BF16)
            wqkv_t = s["wqkv"].T
            dh, dwq, dwk, dwv, dwf, g_gain[i][0] = _in_proj_bwd(
                "attn_in_bwd", [dqt.T, dkt.T, dvt.T, dfl], [wqkv_t[:D], wqkv_t[D:2 * D], wqkv_t[2 * D:], s["wf"].T],
                s["x0"], gain(i, 0), dx1)
            grads["w_attn_in"][j] = jnp.concatenate([dwq, dwk, dwv, dwf[:, :H]], axis=1)
        else:
            dy, dwo, g_gain[i][1] = _out_proj_bwd(
                "conv_out_bwd", dx1, s["m1"], gain(i, 1), [s["y"]], s["w_out"].T, "plain_f32")
            grads["w_conv_out"][j] = dwo
            db, dc, du, dcw = _conv_bwd(dy, s["b"], s["c"], s["u"], s["zc"], taps(j))
            grads["conv_w"][j] = dcw[:conv_w.shape[1]]
            w_t = full["w_conv_in"][j].T
            dh, dwb, dwc, dwu2, g_gain[i][0] = _in_proj_bwd(
                "conv_in_bwd", [db, dc, du], [w_t[:D], w_t[D:2 * D], w_t[2 * D:]], s["x0"], gain(i, 0), dx1)
            grads["w_conv_in"][j] = jnp.concatenate([dwb, dwc, dwu2], axis=1)
    grad_x = dh.reshape(x.shape)

    grad_full = {n: jnp.stack(grads[n]) for n in grads}
    grad_full["norm_g"] = jnp.stack([jnp.concatenate(row, axis=0) for row in g_gain])

    c_idx = lax.axis_index("c").astype(jnp.int32).reshape(1)
    tiny = ("norm_g", "conv_w", "b_forget")
    tiny_flat = jnp.concatenate([_to_shard_major(n, grad_full[n]) for n in tiny], axis=1)
    n_tiny = tiny_flat.shape[1]
    tiny_rows = -(-n_tiny // (LANES * 32)) * 32
    tiny_t = jnp.pad(tiny_flat, ((0, 0), (0, tiny_rows * LANES - n_tiny))).reshape(N_CHIPS, tiny_rows, LANES)
    tensors = [_to_shard_major(n, grad_full[n]).reshape(N_CHIPS, -1, shard_shapes[n][-1]) for n in big]
    reduced = _reduce_scatter(tensors + [tiny_t], c_idx)
    g_out = {n: r.reshape(shard_shapes[n]) for n, r in zip(big, reduced[:-1])}
    tiny_red, off = reduced[-1].reshape(-1), 0
    for n in tiny:
        size = int(np.prod(shard_shapes[n]))
        g_out[n] = tiny_red[off:off + size].reshape(shard_shapes[n])
        off += size
    d_out, m_out, v_out = {}, {}, {}
    for n in WEIGHT_NAMES:
        shp = shard_shapes[n]
        two_d = (-1, shp[-1])
        d, mn, vn = _adamw(w_local[n].reshape(two_d), g_out[n].reshape(two_d),
                           m_local[n].reshape(two_d), v_local[n].reshape(two_d))
        d_out[n], m_out[n], v_out[n] = d.reshape(shp), mn.reshape(shp), vn.reshape(shp)
    return (loss, grad_x, *[g_out[n] for n in WEIGHT_NAMES], *[d_out[n] for n in WEIGHT_NAMES],
            *[m_out[n] for n in WEIGHT_NAMES], *[v_out[n] for n in WEIGHT_NAMES])
```

```python
import numpy as np
import jax
import jax.numpy as jnp
from jax import lax
from jax.experimental import pallas as pl
from jax.experimental.pallas import tpu as pltpu

F32 = jnp.float32
BF16 = jnp.bfloat16
MESH = pl.DeviceIdType.MESH

RMS_EPS = 1e-6
NEG_INF = -1e30
ADAM_LR = 0.001
ADAM_B1 = 0.9
ADAM_B2 = 0.999
ADAM_EPS = 1e-08
ADAM_WD = 0.01
ADAM_STEP = 10

N_CHIPS = 4
LANES = 128
ROW_TILE_FWD = 512
ROW_TILE_BWD = 512
ROW_TILE_BWD_FF = 256
ATTN_Q_TILE_FWD = 4096
ATTN_Q_TILE_BWD = 2048
ATTN_K_TILE = 512
ATTN_KEY_STRIP = 128
ATTN_QUERY_STRIP = 256
ATTN_ROTATE = 4
COL_CHUNK = 512
PACK_ROW_TILE = 256
AUG_ROWS = 16
N_AUG = 3
VMEM_LIMIT = 56 * 1024 * 1024


def _tile(n, pref):
    return pref if n % pref == 0 else n


def _dot(a, b):
    return jnp.dot(a, b, preferred_element_type=F32)


def _dot_tn(a, b):
    return lax.dot_general(a, b, (((0,), (0,)), ((), ())), preferred_element_type=F32)


def _dot_nt(a, b):
    return lax.dot_general(a, b, (((1,), (1,)), ((), ())), preferred_element_type=F32)


def _rms_fwd(x, g):
    r = lax.rsqrt(jnp.mean(x * x, axis=-1, keepdims=True) + RMS_EPS)
    return (x * r) * g


def _rms_bwd(x, g, dy):
    r = lax.rsqrt(jnp.mean(x * x, axis=-1, keepdims=True) + RMS_EPS)
    xh = x * r
    dyg = dy * g
    dx = r * (dyg - xh * jnp.mean(dyg * xh, axis=-1, keepdims=True))
    return dx, jnp.sum(dy * xh, axis=0, keepdims=True)


def _params(n_axes=1):
    return pltpu.CompilerParams(dimension_semantics=("arbitrary",) * n_axes, vmem_limit_bytes=VMEM_LIMIT)


def _row_call(name, body, n_rows, tm, row_ins, const_ins, row_outs, acc_outs=(), scratch=(), reverse=False):
    nb = n_rows // tm
    rmap = (lambda i: (nb - 1 - i, 0)) if reverse else (lambda i: (i, 0))

    def whole(shape):
        nd = len(shape)
        return pl.BlockSpec(tuple(shape), lambda i: (0,) * nd)

    in_specs = [pl.BlockSpec((tm, a.shape[1]), rmap) for a in row_ins] + [whole(a.shape) for a in const_ins]
    out_shape = [jax.ShapeDtypeStruct((n_rows, w), dt) for (w, dt) in row_outs]
    out_shape += [jax.ShapeDtypeStruct(tuple(s), dt) for (s, dt) in acc_outs]
    out_specs = [pl.BlockSpec((tm, w), rmap) for (w, _) in row_outs] + [whole(s) for (s, _) in acc_outs]
    return pl.pallas_call(
        body, name=name, grid=(nb,), in_specs=in_specs, out_specs=out_specs, out_shape=out_shape,
        scratch_shapes=list(scratch), compiler_params=_params(),
    )(*row_ins, *const_ins)


def _attn_in_fwd(x, g, wqkv, wf, bf, qscale):
    T, D = x.shape
    tm, ch = _tile(T, ROW_TILE_FWD), _tile(D, COL_CHUNK)

    def body(x_ref, g_ref, w_ref, wf_ref, bf_ref, q_ref, k_ref, v_ref, fl_ref, lf_ref):
        h = _rms_fwd(x_ref[...], g_ref[...]).astype(BF16)
        for part, o_ref in enumerate((q_ref, k_ref, v_ref)):
            for n0 in range(0, D, ch):
                r = _dot(h, w_ref[:, part * D + n0:part * D + n0 + ch])
                if part == 0:
                    r = r * qscale
                o_ref[:, n0:n0 + ch] = r.astype(BF16)
        fl = _dot(h, wf_ref[...]) + bf_ref[...]
        fl_ref[...] = fl
        lf_ref[...] = jnp.minimum(fl, 0.0) - jnp.log1p(jnp.exp(-jnp.abs(fl)))

    return _row_call("attn_in_fwd", body, T, tm, [x], [g, wqkv, wf, bf],
                     [(D, BF16), (D, BF16), (D, BF16), (LANES, F32), (LANES, F32)])


def _conv_in_fwd(x, g, w, cw):
    T, D = x.shape
    tm, ch = _tile(T, ROW_TILE_FWD), _tile(D, COL_CHUNK)

    def body(x_ref, g_ref, w_ref, cw_ref, b_ref, c_ref, u_ref, zc_ref, y_ref, tail_ref):
        i = pl.program_id(0)

        @pl.when(i == 0)
        def _():
            tail_ref[...] = jnp.zeros_like(tail_ref)

        h = _rms_fwd(x_ref[...], g_ref[...]).astype(BF16)
        for part, o_ref in enumerate((b_ref, c_ref, u_ref)):
            for n0 in range(0, D, ch):
                o_ref[:, n0:n0 + ch] = _dot(h, w_ref[:, part * D + n0:part * D + n0 + ch])
        z = c_ref[...] * u_ref[...]
        row = lax.broadcasted_iota(jnp.int32, (tm, 1), 0)
        t6, t7 = tail_ref[6:7, :], tail_ref[7:8, :]
        z1 = jnp.where(row == 0, t7, pltpu.roll(z, 1, axis=0))
        z2 = jnp.where(row == 0, t6, jnp.where(row == 1, t7, pltpu.roll(z, 2, axis=0)))
        zc = cw_ref[0:1, :] * z2 + cw_ref[1:2, :] * z1 + cw_ref[2:3, :] * z
        zc_ref[...] = zc
        y_ref[...] = (b_ref[...] * zc).astype(BF16)
        tail_ref[...] = z[tm - 8:tm, :]

    return _row_call("conv_in_fwd", body, T, tm, [x], [g, w, cw],
                     [(D, F32), (D, F32), (D, F32), (D, F32), (D, BF16)], scratch=[pltpu.VMEM((8, D), F32)])


def _mlp_up_fwd(x, g, w):
    T, D = x.shape
    F = w.shape[1]
    tm, ch = _tile(T, ROW_TILE_FWD), _tile(F, COL_CHUNK)

    def body(x_ref, g_ref, w_ref, up_ref, a_ref):
        h = _rms_fwd(x_ref[...], g_ref[...]).astype(BF16)
        for n0 in range(0, F, ch):
            r = _dot(h, w_ref[:, n0:n0 + ch])
            up_ref[:, n0:n0 + ch] = r.astype(BF16)
            rl = jnp.maximum(r, 0.0)
            a_ref[:, n0:n0 + ch] = (rl * rl).astype(BF16)

    return _row_call("mlp_up_fwd", body, T, tm, [x], [g, w], [(F, BF16), (F, BF16)])


def _out_proj_fwd(name, a, w, g, x):
    T, D = x.shape
    tm = _tile(T, ROW_TILE_FWD)

    def body(a_ref, x_ref, w_ref, g_ref, m_ref, xn_ref):
        m = _dot(a_ref[...], w_ref[...])
        m_ref[...] = m
        xn_ref[...] = x_ref[...] + _rms_fwd(m, g_ref[...])

    return _row_call(name, body, T, tm, [a, x], [w, g], [(D, F32), (D, F32)])


def _ple_fwd(x, p, g4, g5, wg, wp):
    T, D = x.shape
    tm = _tile(T, ROW_TILE_FWD)

    def body(x_ref, p_ref, g4_ref, g5_ref, wg_ref, wp_ref, gl_ref, pe_ref, xn_ref):
        xv = x_ref[...]
        gl = _dot(_rms_fwd(xv, g4_ref[...]).astype(BF16), wg_ref[...])
        pe = _dot(p_ref[...].astype(BF16), wp_ref[...])
        gl_ref[...] = gl
        pe_ref[...] = pe
        e = pe * (1.0 / (1.0 + jnp.exp(-gl)))
        xn_ref[...] = xv + _rms_fwd(e, g5_ref[...])

    return _row_call("ple_fwd", body, T, tm, [x, p], [g4, g5, wg, wp], [(D, F32), (D, F32), (D, F32)])


def _loss_fwd_bwd(y, target):
    T, D = y.shape
    tm = _tile(T, ROW_TILE_FWD)

    def body(y_ref, t_ref, dy_ref, loss_ref):
        @pl.when(pl.program_id(0) == 0)
        def _():
            loss_ref[...] = jnp.zeros_like(loss_ref)

        err = y_ref[...] - t_ref[...]
        dy_ref[...] = err * (1.0 / D)
        part = 0.5 * jnp.sum(jnp.mean(err * err, axis=-1, keepdims=True), axis=0, keepdims=True)
        loss_ref[...] += jnp.broadcast_to(part, loss_ref.shape)

    return _row_call("loss", body, T, tm, [y, target], [], [(D, F32)], acc_outs=[((8, LANES), F32)])


def _out_proj_bwd(name, dres, m, g, a_ins, wt, mode):
    T, D = dres.shape
    Ka = wt.shape[1]
    tm, ch = _tile(T, ROW_TILE_BWD_FF if mode == "relu2" else ROW_TILE_BWD), _tile(Ka, COL_CHUNK)
    out_dt = F32 if mode == "plain_f32" else BF16

    def body(dres_ref, m_ref, *rest):
        a_ref = rest[0]
        up_ref = rest[1] if mode == "relu2" else None
        k = len(a_ins)
        g_ref, wt_ref, da_ref, dw_ref, dg_ref = rest[k:k + 5]

        @pl.when(pl.program_id(0) == 0)
        def _():
            dw_ref[...] = jnp.zeros_like(dw_ref)
            dg_ref[...] = jnp.zeros_like(dg_ref)

        dm, dgp = _rms_bwd(m_ref[...], g_ref[...], dres_ref[...])
        dg_ref[...] += dgp
        dmb = dm.astype(BF16)
        for n0 in range(0, Ka, ch):
            da = _dot(dmb, wt_ref[:, n0:n0 + ch])
            if mode == "relu2":
                da = da * (2.0 * jnp.maximum(up_ref[:, n0:n0 + ch].astype(F32), 0.0))
            da_ref[:, n0:n0 + ch] = da.astype(out_dt)
            dw_ref[n0:n0 + ch, :] += _dot_tn(a_ref[:, n0:n0 + ch], dmb)

    return _row_call(name, body, T, tm, [dres, m] + list(a_ins), [g, wt],
                     [(Ka, out_dt)], acc_outs=[((Ka, D), F32), ((1, D), F32)])


def _in_proj_bwd(name, pieces, wts, x, g, dres):
    T, D = x.shape
    tm = _tile(T, ROW_TILE_BWD)
    k = len(pieces)
    widths = [pc.shape[1] for pc in pieces]

    def body(*refs):
        pc_refs = refs[:k]
        x_ref, dres_ref, g_ref = refs[k:k + 3]
        wt_refs = refs[k + 3:2 * k + 3]
        dx_ref = refs[2 * k + 3]
        dw_refs = refs[2 * k + 4:3 * k + 4]
        dg_ref = refs[3 * k + 4]

        @pl.when(pl.program_id(0) == 0)
        def _():
            for r in dw_refs:
                r[...] = jnp.zeros_like(r)
            dg_ref[...] = jnp.zeros_like(dg_ref)

        xv, gv = x_ref[...], g_ref[...]
        hb = _rms_fwd(xv, gv).astype(BF16)
        dh = None
        for pc_ref, wt_ref, dw_ref, n in zip(pc_refs, wt_refs, dw_refs, widths):
            d = _dot(pc_ref[...], wt_ref[...])
            dh = d if dh is None else dh + d
            ch = _tile(n, COL_CHUNK)
            for n0 in range(0, n, ch):
                dw_ref[:, n0:n0 + ch] += _dot_tn(hb, pc_ref[:, n0:n0 + ch])
        dxn, dgp = _rms_bwd(xv, gv, dh)
        dg_ref[...] += dgp
        dx_ref[...] = dres_ref[...] + dxn

    return _row_call(name, body, T, tm, list(pieces) + [x, dres], [g] + list(wts), [(D, F32)],
                     acc_outs=[((D, n), F32) for n in widths] + [((1, D), F32)])


def _conv_bwd(dy, b, c, u, zc, cw):
    T, D = dy.shape
    tm = _tile(T, ROW_TILE_BWD)

    def body(dy_ref, b_ref, c_ref, u_ref, zc_ref, cw_ref, db_ref, dc_ref, du_ref, dcw_ref, head_ref):
        @pl.when(pl.program_id(0) == 0)
        def _():
            head_ref[...] = jnp.zeros_like(head_ref)
            dcw_ref[...] = jnp.zeros_like(dcw_ref)

        dyv, cv, uv = dy_ref[...], c_ref[...], u_ref[...]
        db_ref[...] = (dyv * zc_ref[...]).astype(BF16)
        dzc = dyv * b_ref[...]
        row = lax.broadcasted_iota(jnp.int32, (tm, 1), 0)
        h0, h1 = head_ref[0:1, :], head_ref[1:2, :]
        d1 = jnp.where(row == tm - 1, h0, pltpu.roll(dzc, tm - 1, axis=0))
        d2 = jnp.where(row == tm - 1, h1, jnp.where(row == tm - 2, h0, pltpu.roll(dzc, tm - 2, axis=0)))
        dz = cw_ref[2:3, :] * dzc + cw_ref[1:2, :] * d1 + cw_ref[0:1, :] * d2
        dc_ref[...] = (dz * uv).astype(BF16)
        du_ref[...] = (dz * cv).astype(BF16)
        z = cv * uv
        dcw_ref[0:1, :] += jnp.sum(d2 * z, axis=0, keepdims=True)
        dcw_ref[1:2, :] += jnp.sum(d1 * z, axis=0, keepdims=True)
        dcw_ref[2:3, :] += jnp.sum(dzc * z, axis=0, keepdims=True)
        head_ref[...] = dzc[0:8, :]

    return _row_call("conv_bwd", body, T, tm, [dy, b, c, u, zc], [cw], [(D, BF16), (D, BF16), (D, BF16)],
                     acc_outs=[((8, D), F32)], scratch=[pltpu.VMEM((8, D), F32)], reverse=True)


def _ple_bwd(dres, x, p, gl, pe, g4, g5, wgt):
    T, D = x.shape
    P = p.shape[1]
    tm = _tile(T, ROW_TILE_BWD)

    def body(dres_ref, x_ref, p_ref, gl_ref, pe_ref, g4_ref, g5_ref, wgt_ref,
             dx_ref, dwp_ref, dwg_ref, dg4_ref, dg5_ref):
        @pl.when(pl.program_id(0) == 0)
        def _():
            for r in (dwp_ref, dwg_ref, dg4_ref, dg5_ref):
                r[...] = jnp.zeros_like(r)

        dr, xv, pe_v = dres_ref[...], x_ref[...], pe_ref[...]
        gate = 1.0 / (1.0 + jnp.exp(-gl_ref[...]))
        de, dg5p = _rms_bwd(pe_v * gate, g5_ref[...], dr)
        dg5_ref[...] += dg5p
        dpe = (de * gate).astype(BF16)
        dgl = (de * pe_v * gate * (1.0 - gate)).astype(BF16)
        dwp_ref[...] += _dot_tn(p_ref[...].astype(BF16), dpe)
        g4v = g4_ref[...]
        dwg_ref[...] += _dot_tn(_rms_fwd(xv, g4v).astype(BF16), dgl)
        dxn, dg4p = _rms_bwd(xv, g4v, _dot(dgl, wgt_ref[...]))
        dg4_ref[...] += dg4p
        dx_ref[...] = dr + dxn

    return _row_call("ple_bwd", body, T, tm, [dres, x, p, gl, pe], [g4, g5, wgt], [(D, F32)],
                     acc_outs=[((P, D), F32), ((D, D), F32), ((1, D), F32), ((1, D), F32)])


def _scan_lanes(v, reverse):
    n = v.shape[1]
    lane = lax.broadcasted_iota(jnp.int32, v.shape, 1)
    s = 1
    while s < n:
        if reverse:
            v = v + jnp.where(lane < n - s, pltpu.roll(v, n - s, axis=1), 0.0)
        else:
            v = v + jnp.where(lane >= s, pltpu.roll(v, s, axis=1), 0.0)
        s *= 2
    return v


def _gate_cumsum(lf_t):
    def body(lf_ref, *piece_refs):
        rest = _scan_lanes(lf_ref[...], reverse=False)
        for r in piece_refs:
            piece = rest.astype(BF16)
            r[...] = piece
            rest = rest - piece.astype(F32)

    return pl.pallas_call(body, name="gate_cumsum", out_shape=[jax.ShapeDtypeStruct(lf_t.shape, BF16)] * N_AUG,
                          compiler_params=pltpu.CompilerParams(vmem_limit_bytes=VMEM_LIMIT))(lf_t)


def _gate_bwd(drow_t, dcol_t, fl_t):
    H = fl_t.shape[0]

    def body(dr_ref, dc_ref, fl_ref, dfl_ref, dbf_ref):
        dlf = _scan_lanes(dr_ref[...] - dc_ref[...], reverse=True)
        dfl = dlf * (1.0 / (1.0 + jnp.exp(fl_ref[...])))
        dfl_ref[...] = dfl
        dbf_ref[...] = jnp.sum(dfl, axis=1, keepdims=True)

    return pl.pallas_call(
        body, name="gate_bwd",
        out_shape=(jax.ShapeDtypeStruct(fl_t.shape, F32), jax.ShapeDtypeStruct((H, 1), F32)),
        compiler_params=pltpu.CompilerParams(vmem_limit_bytes=VMEM_LIMIT))(drow_t, dcol_t, fl_t)


def _aug_operands(pieces, H, D):
    T = pieces[0].shape[1]
    dh = D // H
    one = jnp.ones((H, T), BF16)
    qa = jnp.stack(list(pieces) + [one] * N_AUG, axis=-1)
    ka = jnp.stack([one] * N_AUG + [-pc for pc in pieces], axis=-1)

    def place(a):
        a = jnp.pad(a, ((0, 0), (0, 0), (0, dh - 2 * N_AUG))).reshape(H // 2, 2, T, dh)
        return jnp.transpose(a[:, ::-1], (2, 0, 1, 3)).reshape(T, D)

    return place(qa), place(ka)


def _strip_kind(koff, k0, ksz, q0, qsz):
    if koff is None or koff + k0 + ksz - 1 <= q0:
        return "full"
    return "skip" if koff + k0 > q0 + qsz - 1 else "partial"


def _score_strip(km, qm, koff, k0, ksz, q0, qsz, kind):
    st = _dot_nt(km[k0:k0 + ksz, :], qm[q0:q0 + qsz, :])
    if kind == "partial":
        kpos = koff + k0 + lax.broadcasted_iota(jnp.int32, (ksz, qsz), 0)
        st = jnp.where(kpos <= q0 + lax.broadcasted_iota(jnp.int32, (ksz, qsz), 1), st, NEG_INF)
    return st


def _fold8(v, op):
    rows, n = v.shape
    v3 = v.reshape(rows // 8, 8, n)
    out = v3[0]
    for r in range(1, rows // 8):
        out = op(out, v3[r])
    return out


def _causal_tables(nbq, r, q_outer):
    a, b = [], []
    for o in range(nbq if q_outer else nbq * r):
        inner = range((o + 1) * r) if q_outer else range(o // r, nbq)
        for n in inner:
            a.append(o)
            b.append(n)
    return jnp.asarray(np.array(a, np.int32)), jnp.asarray(np.array(b, np.int32))


def _attn_tiles(T, q_tile):
    tq = _tile(T, q_tile)
    tk = _tile(tq, ATTN_K_TILE)
    return tq, tk, _tile(tk, ATTN_KEY_STRIP), _tile(tq, ATTN_QUERY_STRIP)


def _flash_fwd(q, k, v, qaug, kaug, H):
    T, D = q.shape
    Hp = H // 2
    W = D // Hp
    dh = W // 2
    tq, tk, ksz, qsz = _attn_tiles(T, ATTN_Q_TILE_FWD)
    r = tq // tk
    ii, jj = _causal_tables(T // tq, r, q_outer=True)
    n_steps = int(ii.shape[0])

    def body(ii_ref, jj_ref, q_ref, k_ref, v_ref, qa_ref, ka_ref, o_ref, lsea_ref, lseb_ref,
             qm_s, m_s, l_s, acc_s, p_s):
        n = pl.program_id(1)
        i, j = ii_ref[n], jj_ref[n]
        d = j - i * r
        in_a = lax.broadcasted_iota(jnp.int32, (1, W), 1) < dh
        top = lax.broadcasted_iota(jnp.int32, (W, 1), 0) < dh

        @pl.when(j == 0)
        def _():
            qv, qa = q_ref[...], qa_ref[...]
            qm_s[0] = jnp.where(in_a, qv, qa)
            qm_s[1] = jnp.where(in_a, qa, qv)
            m_s[...] = jnp.full(m_s.shape, NEG_INF, F32)
            l_s[...] = jnp.zeros_like(l_s)
            acc_s[...] = jnp.zeros_like(acc_s)

        def step(koff):
            kv, ka, vv = k_ref[...], ka_ref[...], v_ref[...]
            kms = (jnp.where(in_a, kv, ka), jnp.where(in_a, ka, kv))
            vt = vv.T
            alphas = {}

            def softmax_stage(hh, q0):
                cols = slice(q0, q0 + qsz)
                kinds = [(k0, _strip_kind(koff, k0, ksz, q0, qsz)) for k0 in range(0, tk, ksz)]
                if all(kind == "skip" for _, kind in kinds):
                    return False
                strips, part = {}, None
                for k0, kind in kinds:
                    if kind != "skip":
                        strips[k0] = _score_strip(kms[hh], qm_s[hh], koff, k0, ksz, q0, qsz, kind)
                        p8 = _fold8(strips[k0], jnp.maximum)
                        part = p8 if part is None else jnp.maximum(part, p8)
                m_prev = m_s[hh, :, cols]
                m_new = jnp.maximum(m_prev, jnp.max(part, axis=0, keepdims=True))
                l8 = None
                for k0, kind in kinds:
                    if kind == "skip":
                        p_s[hh, k0:k0 + ksz, cols] = jnp.zeros((ksz, qsz), BF16)
                        continue
                    pt = jnp.exp(strips[k0] - m_new)
                    s8 = _fold8(pt, jnp.add)
                    l8 = s8 if l8 is None else l8 + s8
                    p_s[hh, k0:k0 + ksz, cols] = pt.astype(BF16)
                alpha = jnp.exp(m_prev - m_new)
                l_s[hh, :, cols] = alpha * l_s[hh, :, cols] + jnp.sum(l8, axis=0, keepdims=True)
                m_s[hh, :, cols] = m_new
                alphas[(hh, q0)] = alpha
                return True

            def value_stage(hh, q0):
                cols, rows = slice(q0, q0 + qsz), slice(hh * dh, (hh + 1) * dh)
                acc_s[rows, cols] = acc_s[rows, cols] * alphas[(hh, q0)] + _dot(vt[rows, :], p_s[hh, :, cols])

            units = [(hh, q0) for q0 in range(0, tq, qsz) for hh in range(2)]
            pending = None
            for unit in units:
                live = softmax_stage(*unit)
                if pending is not None:
                    value_stage(*pending)
                pending = unit if live else None
            if pending is not None:
                value_stage(*pending)

        @pl.when(d < 0)
        def _():
            step(None)

        for dd in range(r):
            @pl.when(d == dd)
            def _():
                step(dd * tk)

        @pl.when(d == r - 1)
        def _():
            inv = jnp.where(top, 1.0 / l_s[0], 1.0 / l_s[1])
            o_ref[...] = (acc_s[...] * inv).T.astype(BF16)
            lsea_ref[...] = m_s[0] + jnp.log(l_s[0])
            lseb_ref[...] = m_s[1] + jnp.log(l_s[1])

    qspec = pl.BlockSpec((tq, W), lambda hp, n, ii, jj: (ii[n], hp))
    kspec = pl.BlockSpec((tk, W), lambda hp, n, ii, jj: (jj[n], hp))
    rspec = pl.BlockSpec((None, 1, tq), lambda hp, n, ii, jj: (hp, 0, ii[n]))
    grid_spec = pltpu.PrefetchScalarGridSpec(
        num_scalar_prefetch=2, grid=(Hp, n_steps),
        in_specs=[qspec, kspec, kspec, qspec, kspec],
        out_specs=[qspec, rspec, rspec],
        scratch_shapes=[pltpu.VMEM((2, tq, W), BF16), pltpu.VMEM((2, 1, tq), F32), pltpu.VMEM((2, 1, tq), F32),
                        pltpu.VMEM((W, tq), F32), pltpu.VMEM((2, tk, tq), BF16)],
    )
    return pl.pallas_call(
        body, name="flash_fwd", grid_spec=grid_spec,
        out_shape=(jax.ShapeDtypeStruct((T, D), BF16), jax.ShapeDtypeStruct((Hp, 1, T), F32),
                   jax.ShapeDtypeStruct((Hp, 1, T), F32)),
        compiler_params=_params(2),
    )(ii, jj, q, k, v, qaug, kaug)


def _attn_delta(do, o, head_sel):
    T, D = do.shape
    H = head_sel.shape[0]
    tm = _tile(T, ROW_TILE_FWD)

    def body(do_ref, o_ref, sel_ref, out_ref):
        prod = do_ref[...].astype(F32) * o_ref[...].astype(F32)
        out_ref[...] = lax.dot_general(sel_ref[...], prod, (((1,), (1,)), ((), ())),
                                       precision=lax.Precision.HIGHEST, preferred_element_type=F32)

    return pl.pallas_call(
        body, name="attn_delta", grid=(T // tm,),
        in_specs=[pl.BlockSpec((tm, D), lambda i: (i, 0)), pl.BlockSpec((tm, D), lambda i: (i, 0)),
                  pl.BlockSpec((H, D), lambda i: (0, 0))],
        out_specs=pl.BlockSpec((H, tm), lambda i: (0, i)),
        out_shape=jax.ShapeDtypeStruct((H, T), F32), compiler_params=_params())(do, o, head_sel)


def _flash_bwd(q, k, v, qaug, kaug, do, lse_a, lse_b, delta, qscale):
    T, D = q.shape
    Hp = lse_a.shape[0]
    W = D // Hp
    dh = W // 2
    tq, tk, ksz, qsz = _attn_tiles(T, ATTN_Q_TILE_BWD)
    r = tq // tk
    nbq = T // tq
    jj, ii = _causal_tables(nbq, r, q_outer=False)
    n_steps = int(ii.shape[0])
    row_row = dh
    col_row = dh + N_AUG

    def body(jj_ref, ii_ref, q_ref, k_ref, v_ref, qa_ref, ka_ref, do_ref, lsea_ref, lseb_ref, dla_ref, dlb_ref,
             dqt_ref, dkt_ref, dvt_ref, dcola_ref, dcolb_ref, drowa_ref, drowb_ref,
             dqt_acc, dk_acc, dv_acc, *rot):
        n = pl.program_id(1)
        i, j = ii_ref[n], jj_ref[n]
        d = j - i * r
        in_a = lax.broadcasted_iota(jnp.int32, (1, W), 1) < dh
        top = lax.broadcasted_iota(jnp.int32, (W, 1), 0) < dh

        @pl.when(n == 0)
        def _():
            dqt_acc[...] = jnp.zeros_like(dqt_acc)

        @pl.when(d >= 0)
        def _():
            dk_acc[...] = jnp.zeros_like(dk_acc)
            dv_acc[...] = jnp.zeros_like(dv_acc)

        def step(koff):
            qv, qa, kv, ka = q_ref[...], qa_ref[...], k_ref[...], ka_ref[...]
            vv, dov = v_ref[...], do_ref[...]
            zb = jnp.zeros_like(dov)
            kms = (jnp.where(in_a, kv, ka), jnp.where(in_a, ka, kv))
            qms = (jnp.where(in_a, qv, qa), jnp.where(in_a, qa, qv))
            doms = (jnp.where(in_a, dov, zb), jnp.where(in_a, zb, dov))
            def own_first(a, hh):
                return a[:dh + AUG_ROWS] if hh == 0 else jnp.concatenate([a[dh:], a[:AUG_ROWS]], axis=0)

            kxs = tuple(own_first(kms[hh].T, hh) for hh in range(2))
            qxs = tuple(own_first(qms[hh].T, hh) for hh in range(2))
            dot = dov.T
            dxs = (dot[:dh], dot[dh:])
            lses, dls = (lsea_ref[...], lseb_ref[...]), (dla_ref[...], dlb_ref[...])
            dvs, dks = ([], []), ([], [])

            def score_stage(u, hh, q0):
                p_s, ds_s = rot[2 * (u % ATTN_ROTATE)], rot[2 * (u % ATTN_ROTATE) + 1]
                cols = slice(q0, q0 + qsz)
                kinds = [(k0, _strip_kind(koff, k0, ksz, q0, qsz)) for k0 in range(0, tk, ksz)]
                if all(kind == "skip" for _, kind in kinds):
                    return False
                for k0, kind in kinds:
                    if kind == "skip":
                        p_s[k0:k0 + ksz, :] = jnp.zeros((ksz, qsz), BF16)
                        ds_s[k0:k0 + ksz, :] = jnp.zeros((ksz, qsz), BF16)
                        continue
                    st = _score_strip(kms[hh], qms[hh], koff, k0, ksz, q0, qsz, kind)
                    pt = jnp.exp(st - lses[hh][:, cols])
                    dst = pt * (_dot_nt(vv[k0:k0 + ksz, :], doms[hh][cols, :]) - dls[hh][:, cols])
                    p_s[k0:k0 + ksz, :] = pt.astype(BF16)
                    ds_s[k0:k0 + ksz, :] = dst.astype(BF16)
                return True

            def grad_stage(u, hh, q0):
                p_s, ds_s = rot[2 * (u % ATTN_ROTATE)], rot[2 * (u % ATTN_ROTATE) + 1]
                cols = slice(q0, q0 + qsz)
                ptb, dsb = p_s[...], ds_s[...]
                dvs[hh].append(_dot_nt(dxs[hh][:, cols], ptb))
                dks[hh].append(_dot_nt(qxs[hh][:, cols], dsb))
                dqt_acc[i, hh, :, cols] += _dot(kxs[hh], dsb)

            order = [(hh, q0) for q0 in range(0, tq, qsz) for hh in range(2)]
            units = [(u, hh, q0) for u, (hh, q0) in enumerate(order)]
            pending = None
            for unit in units:
                live = score_stage(*unit)
                if pending is not None:
                    grad_stage(*pending)
                pending = unit if live else None
            if pending is not None:
                grad_stage(*pending)
            for hh in range(2):
                dv_acc[hh] += sum(dvs[hh][1:], dvs[hh][0])
                dk_acc[hh] += sum(dks[hh][1:], dks[hh][0])

        @pl.when(d < 0)
        def _():
            step(None)

        for dd in range(r):
            @pl.when(d == dd)
            def _():
                step(dd * tk)

        @pl.when(i == nbq - 1)
        def _():
            dkt_ref[...] = jnp.concatenate([dk_acc[0, :dh], dk_acc[1, :dh]], axis=0).astype(BF16)
            dvt_ref[...] = jnp.concatenate([dv_acc[0], dv_acc[1]], axis=0).astype(BF16)
            dcola_ref[...] = dk_acc[0, col_row:col_row + 1, :]
            dcolb_ref[...] = dk_acc[1, col_row:col_row + 1, :]

        @pl.when(n == n_steps - 1)
        def _():
            for b in range(nbq):
                cols = slice(b * tq, (b + 1) * tq)
                both = jnp.concatenate([dqt_acc[b, 0, :dh], dqt_acc[b, 1, :dh]], axis=0)
                dqt_ref[:, cols] = (both * qscale).astype(BF16)
                drowa_ref[:, cols] = dqt_acc[b, 0, row_row:row_row + 1, :]
                drowb_ref[:, cols] = dqt_acc[b, 1, row_row:row_row + 1, :]

    qspec = pl.BlockSpec((tq, W), lambda hp, n, jj, ii: (ii[n], hp))
    kspec = pl.BlockSpec((tk, W), lambda hp, n, jj, ii: (jj[n], hp))
    ktspec = pl.BlockSpec((W, tk), lambda hp, n, jj, ii: (hp, jj[n]))
    pair_row = pl.BlockSpec((None, 1, tq), lambda hp, n, jj, ii: (hp, 0, ii[n]))
    key_row = pl.BlockSpec((None, 1, tk), lambda hp, n, jj, ii: (hp, 0, jj[n]))
    whole_row = pl.BlockSpec((None, 1, T), lambda hp, n, jj, ii: (hp, 0, 0))
    grid_spec = pltpu.PrefetchScalarGridSpec(
        num_scalar_prefetch=2, grid=(Hp, n_steps),
        in_specs=[
            qspec, kspec, kspec, qspec, kspec, qspec, pair_row, pair_row,
            pl.BlockSpec((None, 1, tq), lambda hp, n, jj, ii: (2 * hp, 0, ii[n])),
            pl.BlockSpec((None, 1, tq), lambda hp, n, jj, ii: (2 * hp + 1, 0, ii[n])),
        ],
        out_specs=[
            pl.BlockSpec((W, T), lambda hp, n, jj, ii: (hp, 0)),
            ktspec, ktspec, key_row, key_row, whole_row, whole_row,
        ],
        scratch_shapes=[pltpu.VMEM((nbq, 2, dh + AUG_ROWS, tq), F32), pltpu.VMEM((2, dh + AUG_ROWS, tk), F32),
                        pltpu.VMEM((2, dh, tk), F32)] + [pltpu.VMEM((tk, qsz), BF16)] * (2 * ATTN_ROTATE),
    )
    return pl.pallas_call(
        body, name="flash_bwd", grid_spec=grid_spec,
        out_shape=(jax.ShapeDtypeStruct((D, T), BF16), jax.ShapeDtypeStruct((D, T), BF16),
                   jax.ShapeDtypeStruct((D, T), BF16), jax.ShapeDtypeStruct((Hp, 1, T), F32),
                   jax.ShapeDtypeStruct((Hp, 1, T), F32),
                   jax.ShapeDtypeStruct((Hp, 1, T), F32), jax.ShapeDtypeStruct((Hp, 1, T), F32)),
        compiler_params=_params(2),
    )(jj, ii, q, k, v, qaug, kaug, do, lse_a, lse_b, delta, delta)


HBM_SPEC = pl.BlockSpec(memory_space=pltpu.HBM)


def _mesh_pos():
    return lax.axis_index("x"), lax.axis_index("y"), lax.axis_index("c")


def _all_gather_shards(shards):
    n = len(shards)

    def body(*refs):
        in_refs, out_refs = refs[:n], refs[n:2 * n]
        send1, recv1, send2, recv2 = refs[2 * n:]
        x, y, c = _mesh_pos()
        me = 2 * x + y
        chips = [(1 - x, y), (x, 1 - y), (1 - x, 1 - y)]

        def half(t, chip_idx, pc):
            hr = shards[t].shape[0] // 2
            return out_refs[t].at[chip_idx, pl.ds(pc * hr, hr), :]

        first = []
        for t in range(n):
            hr = shards[t].shape[0] // 2
            for kk, (cx, cy) in enumerate(chips):
                first.append(pltpu.make_async_remote_copy(
                    src_ref=in_refs[t].at[pl.ds(c * hr, hr), :], dst_ref=half(t, me, c),
                    send_sem=send1.at[3 * t + kk], recv_sem=recv1.at[3 * t + kk],
                    device_id=(cx, cy, c), device_id_type=MESH))
        for cp in first:
            cp.start()
        passed = []
        for t in range(n):
            for kk, (cx, cy) in enumerate(chips):
                src_chip = 2 * cx + cy
                landed = half(t, src_chip, c)
                pltpu.make_async_remote_copy(
                    src_ref=landed, dst_ref=landed, send_sem=send1.at[3 * t + kk], recv_sem=recv1.at[3 * t + kk],
                    device_id=(cx, cy, c), device_id_type=MESH).wait_recv()
                fwd = pltpu.make_async_remote_copy(
                    src_ref=landed, dst_ref=landed, send_sem=send2.at[3 * t + kk], recv_sem=recv2.at[3 * t + kk],
                    device_id=(x, y, 1 - c), device_id_type=MESH)
                fwd.start()
                passed.append(fwd)
        for t in range(n):
            for kk, (cx, cy) in enumerate(chips):
                other = half(t, 2 * cx + cy, 1 - c)
                pltpu.make_async_remote_copy(
                    src_ref=other, dst_ref=other, send_sem=send2.at[3 * t + kk], recv_sem=recv2.at[3 * t + kk],
                    device_id=(x, y, 1 - c), device_id_type=MESH).wait_recv()
        for cp in first + passed:
            cp.wait_send()

    gathered = pl.pallas_call(
        body, name="weights_all_gather",
        out_shape=[jax.ShapeDtypeStruct((N_CHIPS,) + s.shape, s.dtype) for s in shards],
        in_specs=[HBM_SPEC] * n, out_specs=[HBM_SPEC] * n,
        scratch_shapes=[pltpu.SemaphoreType.DMA((3 * n,)), pltpu.SemaphoreType.DMA((3 * n,)),
                        pltpu.SemaphoreType.DMA((3 * n,)), pltpu.SemaphoreType.DMA((3 * n,))],
    )(*shards)
    own = lax.broadcasted_iota(jnp.int32, (N_CHIPS, 1, 1), 0) == 2 * lax.axis_index("x") + lax.axis_index("y")
    return [jnp.where(own, s[None], g) for s, g in zip(shards, gathered)]


def _sibling_swap_halves(gs):
    n = len(gs)

    def body(*refs):
        g_refs, r_refs = refs[:n], refs[n:2 * n]
        ssem, rsem = refs[2 * n:]
        x, y, c = _mesh_pos()
        cps = []
        for t in range(n):
            hr = gs[t].shape[1] // 2
            cps += [pltpu.make_async_remote_copy(
                src_ref=g_refs[t].at[s, pl.ds((1 - c) * hr, hr), :], dst_ref=r_refs[t].at[s],
                send_sem=ssem.at[N_CHIPS * t + s], recv_sem=rsem.at[N_CHIPS * t + s],
                device_id=(x, y, 1 - c), device_id_type=MESH) for s in range(N_CHIPS)]
        for cp in cps:
            cp.start()
        for cp in cps:
            cp.wait()

    return pl.pallas_call(
        body, name="grads_sibling_swap",
        out_shape=[jax.ShapeDtypeStruct((N_CHIPS, g.shape[1] // 2, g.shape[2]), F32) for g in gs],
        in_specs=[HBM_SPEC] * n, out_specs=[HBM_SPEC] * n,
        scratch_shapes=[pltpu.SemaphoreType.DMA((N_CHIPS * n,)), pltpu.SemaphoreType.DMA((N_CHIPS * n,))],
    )(*gs)


def _pair_add(g, r, c_idx):
    _, M, C = g.shape
    hr = M // 2
    tr = _tile(hr, PACK_ROW_TILE)
    nbk = hr // tr

    def body(c_ref, g_ref, r_ref, o_ref):
        o_ref[...] = (g_ref[...] + r_ref[...]).astype(BF16)

    grid_spec = pltpu.PrefetchScalarGridSpec(
        num_scalar_prefetch=1, grid=(N_CHIPS, nbk),
        in_specs=[pl.BlockSpec((None, tr, C), lambda s, i, c: (s, c[0] * nbk + i, 0)),
                  pl.BlockSpec((None, tr, C), lambda s, i, c: (s, i, 0))],
        out_specs=pl.BlockSpec((None, tr, C), lambda s, i, c: (s, i, 0)),
    )
    return pl.pallas_call(body, name="grads_pair_add", grid_spec=grid_spec,
                          out_shape=jax.ShapeDtypeStruct((N_CHIPS, hr, C), BF16),
                          compiler_params=_params(2))(c_idx, g, r)


def _chip_scatter(pps):
    n = len(pps)

    def body(*refs):
        p_refs, r_refs = refs[:n], refs[n:2 * n]
        ssem, rsem, lsem = refs[2 * n:]
        x, y, c = _mesh_pos()
        me = 2 * x + y
        chips = [(1 - x, y), (x, 1 - y), (1 - x, 1 - y)]
        own = [pltpu.make_async_copy(p_refs[t].at[me], r_refs[t].at[me], lsem.at[t]) for t in range(n)]
        for cp in own:
            cp.start()
        cps = [pltpu.make_async_remote_copy(
            src_ref=p_refs[t].at[2 * cx + cy], dst_ref=r_refs[t].at[me],
            send_sem=ssem.at[3 * t + kk], recv_sem=rsem.at[3 * t + kk],
            device_id=(cx, cy, c), device_id_type=MESH) for t in range(n) for kk, (cx, cy) in enumerate(chips)]
        for cp in cps:
            cp.start()
        for t in range(n):
            for kk, (cx, cy) in enumerate(chips):
                got = r_refs[t].at[2 * cx + cy]
                pltpu.make_async_remote_copy(
                    src_ref=got, dst_ref=got, send_sem=ssem.at[3 * t + kk], recv_sem=rsem.at[3 * t + kk],
                    device_id=(cx, cy, c), device_id_type=MESH).wait_recv()
        for cp in cps:
            cp.wait_send()
        for cp in own:
            cp.wait()

    return pl.pallas_call(
        body, name="grads_chip_scatter", out_shape=[jax.ShapeDtypeStruct(pp.shape, pp.dtype) for pp in pps],
        in_specs=[HBM_SPEC] * n, out_specs=[HBM_SPEC] * n,
        scratch_shapes=[pltpu.SemaphoreType.DMA((3 * n,)), pltpu.SemaphoreType.DMA((3 * n,)),
                        pltpu.SemaphoreType.DMA((n,))],
    )(*pps)


def _chip_sum(r, c_idx):
    _, hr, C = r.shape
    tr = _tile(hr, PACK_ROW_TILE)
    nbk = hr // tr

    def body(c_ref, r_ref, o_ref):
        r0, r1, r2, r3 = (r_ref[s].astype(F32) for s in range(N_CHIPS))
        o_ref[...] = ((r0 + r1) + r2) + r3

    grid_spec = pltpu.PrefetchScalarGridSpec(
        num_scalar_prefetch=1, grid=(nbk,),
        in_specs=[pl.BlockSpec((N_CHIPS, tr, C), lambda i, c: (0, i, 0))],
        out_specs=pl.BlockSpec((tr, C), lambda i, c: (c[0] * nbk + i, 0)),
    )
    return pl.pallas_call(body, name="grads_chip_sum", grid_spec=grid_spec,
                          out_shape=jax.ShapeDtypeStruct((2 * hr, C), F32), compiler_params=_params())(c_idx, r)


def _sibling_join(bufs):
    n = len(bufs)

    def body(*refs):
        o_refs = refs[n:2 * n]
        ssem, rsem = refs[2 * n:]
        x, y, c = _mesh_pos()
        cps = []
        for t in range(n):
            hr = bufs[t].shape[0] // 2
            mine = o_refs[t].at[pl.ds(c * hr, hr), :]
            cps.append(pltpu.make_async_remote_copy(src_ref=mine, dst_ref=mine, send_sem=ssem.at[t], recv_sem=rsem.at[t],
                                                    device_id=(x, y, 1 - c), device_id_type=MESH))
        for cp in cps:
            cp.start()
        for t in range(n):
            hr = bufs[t].shape[0] // 2
            theirs = o_refs[t].at[pl.ds((1 - c) * hr, hr), :]
            pltpu.make_async_remote_copy(src_ref=theirs, dst_ref=theirs, send_sem=ssem.at[t], recv_sem=rsem.at[t],
                                         device_id=(x, y, 1 - c), device_id_type=MESH).wait_recv()
        for cp in cps:
            cp.wait_send()

    return pl.pallas_call(
        body, name="grads_sibling_join", out_shape=[jax.ShapeDtypeStruct(b.shape, F32) for b in bufs],
        in_specs=[HBM_SPEC] * n, out_specs=[HBM_SPEC] * n, input_output_aliases={t: t for t in range(n)},
        scratch_shapes=[pltpu.SemaphoreType.DMA((n,)), pltpu.SemaphoreType.DMA((n,))],
    )(*bufs)


def _adamw(w, g, m, v):
    M, C = w.shape
    tr = _tile(M, PACK_ROW_TILE)

    def body(w_ref, g_ref, m_ref, v_ref, d_ref, mo_ref, vo_ref):
        gv = g_ref[...]
        mn = ADAM_B1 * m_ref[...] + (1.0 - ADAM_B1) * gv
        vn = ADAM_B2 * v_ref[...] + (1.0 - ADAM_B2) * (gv * gv)
        m_hat = mn / (1.0 - ADAM_B1 ** ADAM_STEP)
        v_hat = vn / (1.0 - ADAM_B2 ** ADAM_STEP)
        d_ref[...] = -ADAM_LR * (m_hat / (jnp.sqrt(v_hat) + ADAM_EPS) + ADAM_WD * w_ref[...])
        mo_ref[...] = mn
        vo_ref[...] = vn

    spec = pl.BlockSpec((tr, C), lambda i: (i, 0))
    return pl.pallas_call(
        body, name="adamw", grid=(M // tr,), in_specs=[spec] * 4, out_specs=[spec] * 3,
        out_shape=[jax.ShapeDtypeStruct((M, C), F32)] * 3, compiler_params=_params())(w, g, m, v)


WEIGHT_NAMES = ("norm_g", "w_attn_in", "b_forget", "w_attn_out", "w_conv_in", "conv_w", "w_conv_out",
                "w_mlp_up", "w_mlp_down", "w_ple_proj", "w_ple_gate")
COL_SHARDED = ("norm_g", "w_attn_in", "w_conv_in", "conv_w", "w_mlp_up", "w_ple_proj")
ROW_SHARDED = ("w_attn_out", "w_conv_out", "w_mlp_down", "w_ple_gate")


def _unshard(name, gathered, shard_shape):
    a = gathered.reshape((N_CHIPS,) + tuple(shard_shape))
    if name in COL_SHARDED:
        a = jnp.moveaxis(a, 0, -2)
        return a.reshape(a.shape[:-2] + (N_CHIPS * shard_shape[-1],))
    a = jnp.moveaxis(a, 0, 1)
    return a.reshape((shard_shape[0], N_CHIPS * shard_shape[1], shard_shape[2]))


def _to_shard_major(name, full):
    if name == "b_forget":
        return jnp.broadcast_to(full.reshape(1, -1), (N_CHIPS, full.size))
    if name in COL_SHARDED:
        a = full.reshape(full.shape[:-1] + (N_CHIPS, full.shape[-1] // N_CHIPS))
        a = jnp.moveaxis(a, -2, 0)
    else:
        a = full.reshape((full.shape[0], N_CHIPS, full.shape[1] // N_CHIPS, full.shape[2]))
        a = jnp.moveaxis(a, 1, 0)
    return a.reshape(N_CHIPS, -1)


def _reduce_scatter(tensors, c_idx):
    swapped = _sibling_swap_halves(tensors)
    pairs = [_pair_add(g, r, c_idx) for g, r in zip(tensors, swapped)]
    return _sibling_join([_chip_sum(r, c_idx) for r in _chip_scatter(pairs)])


def kernel(x, p, norm_g, w_attn_in, b_forget, w_attn_out, w_conv_in, conv_w, w_conv_out, w_mlp_up, w_mlp_down, w_ple_proj, w_ple_gate, loss_target, m_norm_g, m_w_attn_in, m_b_forget, m_w_attn_out, m_w_conv_in, m_conv_w, m_w_conv_out, m_w_mlp_up, m_w_mlp_down, m_w_ple_proj, m_w_ple_gate, v_norm_g, v_w_attn_in, v_b_forget, v_w_attn_out, v_w_conv_in, v_conv_w, v_w_conv_out, v_w_mlp_up, v_w_mlp_down, v_w_ple_proj, v_w_ple_gate):
    w_local = dict(norm_g=norm_g, w_attn_in=w_attn_in, b_forget=b_forget, w_attn_out=w_attn_out,
                   w_conv_in=w_conv_in, conv_w=conv_w, w_conv_out=w_conv_out, w_mlp_up=w_mlp_up,
                   w_mlp_down=w_mlp_down, w_ple_proj=w_ple_proj, w_ple_gate=w_ple_gate)
    m_local = dict(norm_g=m_norm_g, w_attn_in=m_w_attn_in, b_forget=m_b_forget, w_attn_out=m_w_attn_out,
                   w_conv_in=m_w_conv_in, conv_w=m_conv_w, w_conv_out=m_w_conv_out, w_mlp_up=m_w_mlp_up,
                   w_mlp_down=m_w_mlp_down, w_ple_proj=m_w_ple_proj, w_ple_gate=m_w_ple_gate)
    v_local = dict(norm_g=v_norm_g, w_attn_in=v_w_attn_in, b_forget=v_b_forget, w_attn_out=v_w_attn_out,
                   w_conv_in=v_w_conv_in, conv_w=v_conv_w, w_conv_out=v_w_conv_out, w_mlp_up=v_w_mlp_up,
                   w_mlp_down=v_w_mlp_down, w_ple_proj=v_w_ple_proj, w_ple_gate=v_w_ple_gate)
    shard_shapes = {k: tuple(a.shape) for k, a in w_local.items()}

    xs = x[0]
    target = loss_target[0]
    T, D = xs.shape
    depth = p.shape[0]
    H = b_forget.shape[1]
    qscale = float(D // H) ** -0.5
    head_sel = (jnp.arange(D)[None, :] // (D // H) == jnp.arange(H)[:, None]).astype(F32)

    big = [n for n in WEIGHT_NAMES if n not in ("norm_g", "conv_w", "b_forget")]
    small = jnp.concatenate([norm_g.reshape(-1), conv_w.reshape(-1)])
    n_small = small.shape[0]
    small_rows = -(-n_small // (LANES * 16)) * 16
    small = jnp.pad(small, (0, small_rows * LANES - n_small)).reshape(small_rows, LANES)
    shards = [w_local[n].astype(BF16).reshape(-1, shard_shapes[n][-1]) for n in big] + [small]
    gathered = _all_gather_shards(shards)
    full = {n: _unshard(n, g, shard_shapes[n]) for n, g in zip(big, gathered[:-1])}
    gs = gathered[-1].reshape(N_CHIPS, -1)
    full["norm_g"] = _unshard("norm_g", gs[:, :norm_g.size], shard_shapes["norm_g"])
    full["conv_w"] = _unshard("conv_w", gs[:, norm_g.size:n_small], shard_shapes["conv_w"])
    gains = full["norm_g"]

    def gain(i, k):
        return gains[i, k].reshape(1, D)

    def taps(j):
        return jnp.pad(full["conv_w"][j], ((0, 5), (0, 0)))

    saved = []
    h = xs
    for i in range(depth):
        j = i // 2
        s = {"x0": h}
        if i % 2 == 0:
            w_in = full["w_attn_in"][j]
            wqkv = w_in[:, :3 * D]
            wf = jnp.pad(w_in[:, 3 * D:], ((0, 0), (0, LANES - H)))
            bf = jnp.pad(b_forget[j].reshape(1, H), ((0, 0), (0, LANES - H)))
            q, k, v, fl, lf = _attn_in_fwd(h, gain(i, 0), wqkv, wf, bf, qscale)
            qaug, kaug = _aug_operands(_gate_cumsum(lf[:, :H].T), H, D)
            o, lse_a, lse_b = _flash_fwd(q, k, v, qaug, kaug, H)
            s.update(q=q, k=k, v=v, fl=fl, qaug=qaug, kaug=kaug, o=o, lse_a=lse_a, lse_b=lse_b, wqkv=wqkv, wf=wf)
            mix_in, w_out = o, full["w_attn_out"][j]
        else:
            b, c, u, zc, y = _conv_in_fwd(h, gain(i, 0), full["w_conv_in"][j], taps(j))
            s.update(b=b, c=c, u=u, zc=zc, y=y)
            mix_in, w_out = y, full["w_conv_out"][j]
        m1, x1 = _out_proj_fwd("mixer_out_fwd", mix_in, w_out, gain(i, 1), h)
        up, a = _mlp_up_fwd(x1, gain(i, 2), full["w_mlp_up"][i])
        m3, x2 = _out_proj_fwd("mlp_down_fwd", a, full["w_mlp_down"][i], gain(i, 3), x1)
        gl, pe, x3 = _ple_fwd(x2, p[i, 0], gain(i, 4), gain(i, 5), full["w_ple_gate"][i], full["w_ple_proj"][i])
        s.update(m1=m1, x1=x1, up=up, a=a, m3=m3, x2=x2, gl=gl, pe=pe, w_out=w_out)
        saved.append(s)
        h = x3

    dh, loss_blk = _loss_fwd_bwd(h, target)
    loss = lax.psum(loss_blk[0, 0], ("x", "y", "c"))

    g_gain = [[None] * 6 for _ in range(depth)]
    grads = {n: [None] * w_local[n].shape[0] for n in WEIGHT_NAMES if n != "norm_g"}
    for i in reversed(range(depth)):
        j = i // 2
        s = saved[i]
        dx2, dwp, dwg, g_gain[i][4], g_gain[i][5] = _ple_bwd(
            dh, s["x2"], p[i, 0], s["gl"], s["pe"], gain(i, 4), gain(i, 5), full["w_ple_gate"][i].T)
        grads["w_ple_proj"][i], grads["w_ple_gate"][i] = dwp, dwg
        dup, dwd, g_gain[i][3] = _out_proj_bwd(
            "mlp_down_bwd", dx2, s["m3"], gain(i, 3), [s["a"], s["up"]], full["w_mlp_down"][i].T, "relu2")
        grads["w_mlp_down"][i] = dwd
        dx1, dwu, g_gain[i][2] = _in_proj_bwd(
            "mlp_up_bwd", [dup], [full["w_mlp_up"][i].T], s["x1"], gain(i, 2), dx2)
        grads["w_mlp_up"][i] = dwu
        if i % 2 == 0:
            do, dwo, g_gain[i][1] = _out_proj_bwd(
                "attn_out_bwd", dx1, s["m1"], gain(i, 1), [s["o"]], s["w_out"].T, "plain")
            grads["w_attn_out"][j] = dwo
            delta = _attn_delta(do, s["o"], head_sel).reshape(H, 1, T)
            dqt, dkt, dvt, dca, dcb, dra, drb = _flash_bwd(s["q"], s["k"], s["v"], s["qaug"], s["kaug"], do,
                                                           s["lse_a"], s["lse_b"], delta, qscale)
            dcol = jnp.concatenate([dca, dcb], axis=1).reshape(H, T)
            drow = jnp.concatenate([dra, drb], axis=1).reshape(H, T)
            dfl_t, dbf = _gate_bwd(drow, dcol, s["fl"][:, :H].T)
            grads["b_forget"][j] = dbf.reshape(H)
            dfl = jnp.pad(dfl_t.T, ((0, 0), (0, LANES - H))).astype(BF16)
            wqkv_t = s["wqkv"].T
            dh, dwq, dwk, dwv, dwf, g_gain[i][0] = _in_proj_bwd(
                "attn_in_bwd", [dqt.T, dkt.T, dvt.T, dfl], [wqkv_t[:D], wqkv_t[D:2 * D], wqkv_t[2 * D:], s["wf"].T],
                s["x0"], gain(i, 0), dx1)
            grads["w_attn_in"][j] = jnp.concatenate([dwq, dwk, dwv, dwf[:, :H]], axis=1)
        else:
            dy, dwo, g_gain[i][1] = _out_proj_bwd(
                "conv_out_bwd", dx1, s["m1"], gain(i, 1), [s["y"]], s["w_out"].T, "plain_f32")
            grads["w_conv_out"][j] = dwo
            db, dc, du, dcw = _conv_bwd(dy, s["b"], s["c"], s["u"], s["zc"], taps(j))
            grads["conv_w"][j] = dcw[:conv_w.shape[1]]
            w_t = full["w_conv_in"][j].T
            dh, dwb, dwc, dwu2, g_gain[i][0] = _in_proj_bwd(
                "conv_in_bwd", [db, dc, du], [w_t[:D], w_t[D:2 * D], w_t[2 * D:]], s["x0"], gain(i, 0), dx1)
            grads["w_conv_in"][j] = jnp.concatenate([dwb, dwc, dwu2], axis=1)
    grad_x = dh.reshape(x.shape)

    grad_full = {n: jnp.stack(grads[n]) for n in grads}
    grad_full["norm_g"] = jnp.stack([jnp.concatenate(row, axis=0) for row in g_gain])

    c_idx = lax.axis_index("c").astype(jnp.int32).reshape(1)
    tiny = ("norm_g", "conv_w", "b_forget")
    tiny_flat = jnp.concatenate([_to_shard_major(n, grad_full[n]) for n in tiny], axis=1)
    n_tiny = tiny_flat.shape[1]
    tiny_rows = -(-n_tiny // (LANES * 32)) * 32
    tiny_t = jnp.pad(tiny_flat, ((0, 0), (0, tiny_rows * LANES - n_tiny))).reshape(N_CHIPS, tiny_rows, LANES)
    tensors = [_to_shard_major(n, grad_full[n]).reshape(N_CHIPS, -1, shard_shapes[n][-1]) for n in big]
    reduced = _reduce_scatter(tensors + [tiny_t], c_idx)
    g_out = {n: r.reshape(shard_shapes[n]) for n, r in zip(big, reduced[:-1])}
    tiny_red, off = reduced[-1].reshape(-1), 0
    for n in tiny:
        size = int(np.prod(shard_shapes[n]))
        g_out[n] = tiny_red[off:off + size].reshape(shard_shapes[n])
        off += size
    d_out, m_out, v_out = {}, {}, {}
    for n in WEIGHT_NAMES:
        shp = shard_shapes[n]
        two_d = (-1, shp[-1])
        d, mn, vn = _adamw(w_local[n].reshape(two_d), g_out[n].reshape(two_d),
                           m_local[n].reshape(two_d), v_local[n].reshape(two_d))
        d_out[n], m_out[n], v_out[n] = d.reshape(shp), mn.reshape(shp), vn.reshape(shp)
    return (loss, grad_x, *[g_out[n] for n in WEIGHT_NAMES], *[d_out[n] for n in WEIGHT_NAMES],
            *[m_out[n] for n in WEIGHT_NAMES], *[v_out[n] for n in WEIGHT_NAMES])
```

```python
import numpy as np
import jax
import jax.numpy as jnp
from jax import lax
from jax.experimental import pallas as pl
from jax.experimental.pallas import tpu as pltpu

F32 = jnp.float32
BF16 = jnp.bfloat16
MESH = pl.DeviceIdType.MESH

RMS_EPS = 1e-6
NEG_INF = -1e30
ADAM_LR = 0.001
ADAM_B1 = 0.9
ADAM_B2 = 0.999
ADAM_EPS = 1e-08
ADAM_WD = 0.01
ADAM_STEP = 10

N_CHIPS = 4
LANES = 128
ROW_TILE_FWD = 512
ROW_TILE_BWD = 512
ROW_TILE_BWD_FF = 256
ATTN_Q_TILE_FWD = 4096
ATTN_Q_TILE_BWD = 2048
ATTN_K_TILE = 512
ATTN_KEY_STRIP = 128
ATTN_QUERY_STRIP = 256
ATTN_ROTATE = 4
COL_CHUNK = 512
PACK_ROW_TILE = 256
AUG_ROWS = 16
N_AUG = 3
VMEM_LIMIT = 56 * 1024 * 1024


def _tile(n, pref):
    return pref if n % pref == 0 else n


def _dot(a, b):
    return jnp.dot(a, b, preferred_element_type=F32)


def _dot_tn(a, b):
    return lax.dot_general(a, b, (((0,), (0,)), ((), ())), preferred_element_type=F32)


def _dot_nt(a, b):
    return lax.dot_general(a, b, (((1,), (1,)), ((), ())), preferred_element_type=F32)


def _rms_fwd(x, g):
    r = lax.rsqrt(jnp.mean(x * x, axis=-1, keepdims=True) + RMS_EPS)
    return (x * r) * g


def _rms_bwd(x, g, dy):
    r = lax.rsqrt(jnp.mean(x * x, axis=-1, keepdims=True) + RMS_EPS)
    xh = x * r
    dyg = dy * g
    dx = r * (dyg - xh * jnp.mean(dyg * xh, axis=-1, keepdims=True))
    return dx, jnp.sum(dy * xh, axis=0, keepdims=True)


def _params(n_axes=1):
    return pltpu.CompilerParams(dimension_semantics=("arbitrary",) * n_axes, vmem_limit_bytes=VMEM_LIMIT)


def _row_call(name, body, n_rows, tm, row_ins, const_ins, row_outs, acc_outs=(), scratch=(), reverse=False):
    nb = n_rows // tm
    rmap = (lambda i: (nb - 1 - i, 0)) if reverse else (lambda i: (i, 0))

    def whole(shape):
        nd = len(shape)
        return pl.BlockSpec(tuple(shape), lambda i: (0,) * nd)

    in_specs = [pl.BlockSpec((tm, a.shape[1]), rmap) for a in row_ins] + [whole(a.shape) for a in const_ins]
    out_shape = [jax.ShapeDtypeStruct((n_rows, w), dt) for (w, dt) in row_outs]
    out_shape += [jax.ShapeDtypeStruct(tuple(s), dt) for (s, dt) in acc_outs]
    out_specs = [pl.BlockSpec((tm, w), rmap) for (w, _) in row_outs] + [whole(s) for (s, _) in acc_outs]
    return pl.pallas_call(
        body, name=name, grid=(nb,), in_specs=in_specs, out_specs=out_specs, out_shape=out_shape,
        scratch_shapes=list(scratch), compiler_params=_params(),
    )(*row_ins, *const_ins)


def _attn_in_fwd(x, g, wqkv, wf, bf, qscale):
    T, D = x.shape
    tm, ch = _tile(T, ROW_TILE_FWD), _tile(D, COL_CHUNK)

    def body(x_ref, g_ref, w_ref, wf_ref, bf_ref, q_ref, k_ref, v_ref, fl_ref, lf_ref):
        h = _rms_fwd(x_ref[...], g_ref[...]).astype(BF16)
        for part, o_ref in enumerate((q_ref, k_ref, v_ref)):
            for n0 in range(0, D, ch):
                r = _dot(h, w_ref[:, part * D + n0:part * D + n0 + ch])
                if part == 0:
                    r = r * qscale
                o_ref[:, n0:n0 + ch] = r.astype(BF16)
        fl = _dot(h, wf_ref[...]) + bf_ref[...]
        fl_ref[...] = fl
        lf_ref[...] = jnp.minimum(fl, 0.0) - jnp.log1p(jnp.exp(-jnp.abs(fl)))

    return _row_call("attn_in_fwd", body, T, tm, [x], [g, wqkv, wf, bf],
                     [(D, BF16), (D, BF16), (D, BF16), (LANES, F32), (LANES, F32)])


def _conv_in_fwd(x, g, w, cw):
    T, D = x.shape
    tm, ch = _tile(T, ROW_TILE_FWD), _tile(D, COL_CHUNK)

    def body(x_ref, g_ref, w_ref, cw_ref, b_ref, c_ref, u_ref, zc_ref, y_ref, tail_ref):
        i = pl.program_id(0)

        @pl.when(i == 0)
        def _():
            tail_ref[...] = jnp.zeros_like(tail_ref)

        h = _rms_fwd(x_ref[...], g_ref[...]).astype(BF16)
        for part, o_ref in enumerate((b_ref, c_ref, u_ref)):
            for n0 in range(0, D, ch):
                o_ref[:, n0:n0 + ch] = _dot(h, w_ref[:, part * D + n0:part * D + n0 + ch])
        z = c_ref[...] * u_ref[...]
        row = lax.broadcasted_iota(jnp.int32, (tm, 1), 0)
        t6, t7 = tail_ref[6:7, :], tail_ref[7:8, :]
        z1 = jnp.where(row == 0, t7, pltpu.roll(z, 1, axis=0))
        z2 = jnp.where(row == 0, t6, jnp.where(row == 1, t7, pltpu.roll(z, 2, axis=0)))
        zc = cw_ref[0:1, :] * z2 + cw_ref[1:2, :] * z1 + cw_ref[2:3, :] * z
        zc_ref[...] = zc
        y_ref[...] = (b_ref[...] * zc).astype(BF16)
        tail_ref[...] = z[tm - 8:tm, :]

    return _row_call("conv_in_fwd", body, T, tm, [x], [g, w, cw],
                     [(D, F32), (D, F32), (D, F32), (D, F32), (D, BF16)], scratch=[pltpu.VMEM((8, D), F32)])


def _mlp_up_fwd(x, g, w):
    T, D = x.shape
    F = w.shape[1]
    tm, ch = _tile(T, ROW_TILE_FWD), _tile(F, COL_CHUNK)

    def body(x_ref, g_ref, w_ref, up_ref, a_ref):
        h = _rms_fwd(x_ref[...], g_ref[...]).astype(BF16)
        for n0 in range(0, F, ch):
            r = _dot(h, w_ref[:, n0:n0 + ch])
            up_ref[:, n0:n0 + ch] = r.astype(BF16)
            rl = jnp.maximum(r, 0.0)
            a_ref[:, n0:n0 + ch] = (rl * rl).astype(BF16)

    return _row_call("mlp_up_fwd", body, T, tm, [x], [g, w], [(F, BF16), (F, BF16)])


def _out_proj_fwd(name, a, w, g, x):
    T, D = x.shape
    tm = _tile(T, ROW_TILE_FWD)

    def body(a_ref, x_ref, w_ref, g_ref, m_ref, xn_ref):
        m = _dot(a_ref[...], w_ref[...])
        m_ref[...] = m
        xn_ref[...] = x_ref[...] + _rms_fwd(m, g_ref[...])

    return _row_call(name, body, T, tm, [a, x], [w, g], [(D, F32), (D, F32)])


def _ple_fwd(x, p, g4, g5, wg, wp):
    T, D = x.shape
    tm = _tile(T, ROW_TILE_FWD)

    def body(x_ref, p_ref, g4_ref, g5_ref, wg_ref, wp_ref, gl_ref, pe_ref, xn_ref):
        xv = x_ref[...]
        gl = _dot(_rms_fwd(xv, g4_ref[...]).astype(BF16), wg_ref[...])
        pe = _dot(p_ref[...].astype(BF16), wp_ref[...])
        gl_ref[...] = gl
        pe_ref[...] = pe
        e = pe * (1.0 / (1.0 + jnp.exp(-gl)))
        xn_ref[...] = xv + _rms_fwd(e, g5_ref[...])

    return _row_call("ple_fwd", body, T, tm, [x, p], [g4, g5, wg, wp], [(D, F32), (D, F32), (D, F32)])


def _loss_fwd_bwd(y, target):
    T, D = y.shape
    tm = _tile(T, ROW_TILE_FWD)

    def body(y_ref, t_ref, dy_ref, loss_ref):
        @pl.when(pl.program_id(0) == 0)
        def _():
            loss_ref[...] = jnp.zeros_like(loss_ref)

        err = y_ref[...] - t_ref[...]
        dy_ref[...] = err * (1.0 / D)
        part = 0.5 * jnp.sum(jnp.mean(err * err, axis=-1, keepdims=True), axis=0, keepdims=True)
        loss_ref[...] += jnp.broadcast_to(part, loss_ref.shape)

    return _row_call("loss", body, T, tm, [y, target], [], [(D, F32)], acc_outs=[((8, LANES), F32)])


def _out_proj_bwd(name, dres, m, g, a_ins, wt, mode):
    T, D = dres.shape
    Ka = wt.shape[1]
    tm, ch = _tile(T, ROW_TILE_BWD_FF if mode == "relu2" else ROW_TILE_BWD), _tile(Ka, COL_CHUNK)
    out_dt = F32 if mode == "plain_f32" else BF16

    def body(dres_ref, m_ref, *rest):
        a_ref = rest[0]
        up_ref = rest[1] if mode == "relu2" else None
        k = len(a_ins)
        g_ref, wt_ref, da_ref, dw_ref, dg_ref = rest[k:k + 5]

        @pl.when(pl.program_id(0) == 0)
        def _():
            dw_ref[...] = jnp.zeros_like(dw_ref)
            dg_ref[...] = jnp.zeros_like(dg_ref)

        dm, dgp = _rms_bwd(m_ref[...], g_ref[...], dres_ref[...])
        dg_ref[...] += dgp
        dmb = dm.astype(BF16)
        for n0 in range(0, Ka, ch):
            da = _dot(dmb, wt_ref[:, n0:n0 + ch])
            if mode == "relu2":
                da = da * (2.0 * jnp.maximum(up_ref[:, n0:n0 + ch].astype(F32), 0.0))
            da_ref[:, n0:n0 + ch] = da.astype(out_dt)
            dw_ref[n0:n0 + ch, :] += _dot_tn(a_ref[:, n0:n0 + ch], dmb)

    return _row_call(name, body, T, tm, [dres, m] + list(a_ins), [g, wt],
                     [(Ka, out_dt)], acc_outs=[((Ka, D), F32), ((1, D), F32)])


def _in_proj_bwd(name, pieces, wts, x, g, dres):
    T, D = x.shape
    tm = _tile(T, ROW_TILE_BWD)
    k = len(pieces)
    widths = [pc.shape[1] for pc in pieces]

    def body(*refs):
        pc_refs = refs[:k]
        x_ref, dres_ref, g_ref = refs[k:k + 3]
        wt_refs = refs[k + 3:2 * k + 3]
        dx_ref = refs[2 * k + 3]
        dw_refs = refs[2 * k + 4:3 * k + 4]
        dg_ref = refs[3 * k + 4]

        @pl.when(pl.program_id(0) == 0)
        def _():
            for r in dw_refs:
                r[...] = jnp.zeros_like(r)
            dg_ref[...] = jnp.zeros_like(dg_ref)

        xv, gv = x_ref[...], g_ref[...]
        hb = _rms_fwd(xv, gv).astype(BF16)
        dh = None
        for pc_ref, wt_ref, dw_ref, n in zip(pc_refs, wt_refs, dw_refs, widths):
            d = _dot(pc_ref[...], wt_ref[...])
            dh = d if dh is None else dh + d
            ch = _tile(n, COL_CHUNK)
            for n0 in range(0, n, ch):
                dw_ref[:, n0:n0 + ch] += _dot_tn(hb, pc_ref[:, n0:n0 + ch])
        dxn, dgp = _rms_bwd(xv, gv, dh)
        dg_ref[...] += dgp
        dx_ref[...] = dres_ref[...] + dxn

    return _row_call(name, body, T, tm, list(pieces) + [x, dres], [g] + list(wts), [(D, F32)],
                     acc_outs=[((D, n), F32) for n in widths] + [((1, D), F32)])


def _conv_bwd(dy, b, c, u, zc, cw):
    T, D = dy.shape
    tm = _tile(T, ROW_TILE_BWD)

    def body(dy_ref, b_ref, c_ref, u_ref, zc_ref, cw_ref, db_ref, dc_ref, du_ref, dcw_ref, head_ref):
        @pl.when(pl.program_id(0) == 0)
        def _():
            head_ref[...] = jnp.zeros_like(head_ref)
            dcw_ref[...] = jnp.zeros_like(dcw_ref)

        dyv, cv, uv = dy_ref[...], c_ref[...], u_ref[...]
        db_ref[...] = (dyv * zc_ref[...]).astype(BF16)
        dzc = dyv * b_ref[...]
        row = lax.broadcasted_iota(jnp.int32, (tm, 1), 0)
        h0, h1 = head_ref[0:1, :], head_ref[1:2, :]
        d1 = jnp.where(row == tm - 1, h0, pltpu.roll(dzc, tm - 1, axis=0))
        d2 = jnp.where(row == tm - 1, h1, jnp.where(row == tm - 2, h0, pltpu.roll(dzc, tm - 2, axis=0)))
        dz = cw_ref[2:3, :] * dzc + cw_ref[1:2, :] * d1 + cw_ref[0:1, :] * d2
        dc_ref[...] = (dz * uv).astype(BF16)
        du_ref[...] = (dz * cv).astype(BF16)
        z = cv * uv
        dcw_ref[0:1, :] += jnp.sum(d2 * z, axis=0, keepdims=True)
        dcw_ref[1:2, :] += jnp.sum(d1 * z, axis=0, keepdims=True)
        dcw_ref[2:3, :] += jnp.sum(dzc * z, axis=0, keepdims=True)
        head_ref[...] = dzc[0:8, :]

    return _row_call("conv_bwd", body, T, tm, [dy, b, c, u, zc], [cw], [(D, BF16), (D, BF16), (D, BF16)],
                     acc_outs=[((8, D), F32)], scratch=[pltpu.VMEM((8, D), F32)], reverse=True)


def _ple_bwd(dres, x, p, gl, pe, g4, g5, wgt):
    T, D = x.shape
    P = p.shape[1]
    tm = _tile(T, ROW_TILE_BWD)

    def body(dres_ref, x_ref, p_ref, gl_ref, pe_ref, g4_ref, g5_ref, wgt_ref,
             dx_ref, dwp_ref, dwg_ref, dg4_ref, dg5_ref):
        @pl.when(pl.program_id(0) == 0)
        def _():
            for r in (dwp_ref, dwg_ref, dg4_ref, dg5_ref):
                r[...] = jnp.zeros_like(r)

        dr, xv, pe_v = dres_ref[...], x_ref[...], pe_ref[...]
        gate = 1.0 / (1.0 + jnp.exp(-gl_ref[...]))
        de, dg5p = _rms_bwd(pe_v * gate, g5_ref[...], dr)
        dg5_ref[...] += dg5p
        dpe = (de * gate).astype(BF16)
        dgl = (de * pe_v * gate * (1.0 - gate)).astype(BF16)
        dwp_ref[...] += _dot_tn(p_ref[...].astype(BF16), dpe)
        g4v = g4_ref[...]
        dwg_ref[...] += _dot_tn(_rms_fwd(xv, g4v).astype(BF16), dgl)
        dxn, dg4p = _rms_bwd(xv, g4v, _dot(dgl, wgt_ref[...]))
        dg4_ref[...] += dg4p
        dx_ref[...] = dr + dxn

    return _row_call("ple_bwd", body, T, tm, [dres, x, p, gl, pe], [g4, g5, wgt], [(D, F32)],
                     acc_outs=[((P, D), F32), ((D, D), F32), ((1, D), F32), ((1, D), F32)])


def _scan_lanes(v, reverse):
    n = v.shape[1]
    lane = lax.broadcasted_iota(jnp.int32, v.shape, 1)
    s = 1
    while s < n:
        if reverse:
            v = v + jnp.where(lane < n - s, pltpu.roll(v, n - s, axis=1), 0.0)
        else:
            v = v + jnp.where(lane >= s, pltpu.roll(v, s, axis=1), 0.0)
        s *= 2
    return v


def _gate_cumsum(lf_t):
    def body(lf_ref, *piece_refs):
        rest = _scan_lanes(lf_ref[...], reverse=False)
        for r in piece_refs:
            piece = rest.astype(BF16)
            r[...] = piece
            rest = rest - piece.astype(F32)

    return pl.pallas_call(body, name="gate_cumsum", out_shape=[jax.ShapeDtypeStruct(lf_t.shape, BF16)] * N_AUG,
                          compiler_params=pltpu.CompilerParams(vmem_limit_bytes=VMEM_LIMIT))(lf_t)


def _gate_bwd(drow_t, dcol_t, fl_t):
    H = fl_t.shape[0]

    def body(dr_ref, dc_ref, fl_ref, dfl_ref, dbf_ref):
        dlf = _scan_lanes(dr_ref[...] - dc_ref[...], reverse=True)
        dfl = dlf * (1.0 / (1.0 + jnp.exp(fl_ref[...])))
        dfl_ref[...] = dfl
        dbf_ref[...] = jnp.sum(dfl, axis=1, keepdims=True)

    return pl.pallas_call(
        body, name="gate_bwd",
        out_shape=(jax.ShapeDtypeStruct(fl_t.shape, F32), jax.ShapeDtypeStruct((H, 1), F32)),
        compiler_params=pltpu.CompilerParams(vmem_limit_bytes=VMEM_LIMIT))(drow_t, dcol_t, fl_t)


def _aug_operands(pieces, H, D):
    T = pieces[0].shape[1]
    dh = D // H
    one = jnp.ones((H, T), BF16)
    qa = jnp.stack(list(pieces) + [one] * N_AUG, axis=-1)
    ka = jnp.stack([one] * N_AUG + [-pc for pc in pieces], axis=-1)

    def place(a):
        a = jnp.pad(a, ((0, 0), (0, 0), (0, dh - 2 * N_AUG))).reshape(H // 2, 2, T, dh)
        return jnp.transpose(a[:, ::-1], (2, 0, 1, 3)).reshape(T, D)

    return place(qa), place(ka)


def _strip_kind(koff, k0, ksz, q0, qsz):
    if koff is None or koff + k0 + ksz - 1 <= q0:
        return "full"
    return "skip" if koff + k0 > q0 + qsz - 1 else "partial"


def _score_strip(km, qm, koff, k0, ksz, q0, qsz, kind):
    st = _dot_nt(km[k0:k0 + ksz, :], qm[q0:q0 + qsz, :])
    if kind == "partial":
        kpos = koff + k0 + lax.broadcasted_iota(jnp.int32, (ksz, qsz), 0)
        st = jnp.where(kpos <= q0 + lax.broadcasted_iota(jnp.int32, (ksz, qsz), 1), st, NEG_INF)
    return st


def _fold8(v, op):
    rows, n = v.shape
    v3 = v.reshape(rows // 8, 8, n)
    out = v3[0]
    for r in range(1, rows // 8):
        out = op(out, v3[r])
    return out


def _causal_tables(nbq, r, q_outer):
    a, b = [], []
    for o in range(nbq if q_outer else nbq * r):
        inner = range((o + 1) * r) if q_outer else range(o // r, nbq)
        for n in inner:
            a.append(o)
            b.append(n)
    return jnp.asarray(np.array(a, np.int32)), jnp.asarray(np.array(b, np.int32))


def _attn_tiles(T, q_tile, k_tile=ATTN_K_TILE):
    tq = _tile(T, q_tile)
    tk = _tile(tq, k_tile)
    return tq, tk, _tile(tk, ATTN_KEY_STRIP), _tile(tq, ATTN_QUERY_STRIP)


def _flash_fwd(q, k, v, qaug, kaug, H):
    T, D = q.shape
    Hp = H // 2
    W = D // Hp
    dh = W // 2
    tq, tk, ksz, qsz = _attn_tiles(T, ATTN_Q_TILE_FWD)
    r = tq // tk
    ii, jj = _causal_tables(T // tq, r, q_outer=True)
    n_steps = int(ii.shape[0])

    def body(ii_ref, jj_ref, q_ref, k_ref, v_ref, qa_ref, ka_ref, o_ref, lsea_ref, lseb_ref,
             qm_s, m_s, l_s, acc_s, p_s):
        n = pl.program_id(1)
        i, j = ii_ref[n], jj_ref[n]
        d = j - i * r
        in_a = lax.broadcasted_iota(jnp.int32, (1, W), 1) < dh
        top = lax.broadcasted_iota(jnp.int32, (W, 1), 0) < dh

        @pl.when(j == 0)
        def _():
            qv, qa = q_ref[...], qa_ref[...]
            qm_s[0] = jnp.where(in_a, qv, qa)
            qm_s[1] = jnp.where(in_a, qa, qv)
            m_s[...] = jnp.full(m_s.shape, NEG_INF, F32)
            l_s[...] = jnp.zeros_like(l_s)
            acc_s[...] = jnp.zeros_like(acc_s)

        def step(koff):
            kv, ka, vv = k_ref[...], ka_ref[...], v_ref[...]
            kms = (jnp.where(in_a, kv, ka), jnp.where(in_a, ka, kv))
            vt = vv.T
            alphas = {}

            def softmax_stage(hh, q0):
                cols = slice(q0, q0 + qsz)
                kinds = [(k0, _strip_kind(koff, k0, ksz, q0, qsz)) for k0 in range(0, tk, ksz)]
                if all(kind == "skip" for _, kind in kinds):
                    return False
                strips, part = {}, None
                for k0, kind in kinds:
                    if kind != "skip":
                        strips[k0] = _score_strip(kms[hh], qm_s[hh], koff, k0, ksz, q0, qsz, kind)
                        p8 = _fold8(strips[k0], jnp.maximum)
                        part = p8 if part is None else jnp.maximum(part, p8)
                m_prev = m_s[hh, :, cols]
                m_new = jnp.maximum(m_prev, jnp.max(part, axis=0, keepdims=True))
                l8 = None
                for k0, kind in kinds:
                    if kind == "skip":
                        p_s[hh, k0:k0 + ksz, cols] = jnp.zeros((ksz, qsz), BF16)
                        continue
                    pt = jnp.exp(strips[k0] - m_new)
                    s8 = _fold8(pt, jnp.add)
                    l8 = s8 if l8 is None else l8 + s8
                    p_s[hh, k0:k0 + ksz, cols] = pt.astype(BF16)
                alpha = jnp.exp(m_prev - m_new)
                l_s[hh, :, cols] = alpha * l_s[hh, :, cols] + jnp.sum(l8, axis=0, keepdims=True)
                m_s[hh, :, cols] = m_new
                alphas[(hh, q0)] = alpha
                return True

            def value_stage(hh, q0):
                cols, rows = slice(q0, q0 + qsz), slice(hh * dh, (hh + 1) * dh)
                acc_s[rows, cols] = acc_s[rows, cols] * alphas[(hh, q0)] + _dot(vt[rows, :], p_s[hh, :, cols])

            units = [(hh, q0) for q0 in range(0, tq, qsz) for hh in range(2)]
            pending = None
            for unit in units:
                live = softmax_stage(*unit)
                if pending is not None:
                    value_stage(*pending)
                pending = unit if live else None
            if pending is not None:
                value_stage(*pending)

        @pl.when(d < 0)
        def _():
            step(None)

        for dd in range(r):
            @pl.when(d == dd)
            def _():
                step(dd * tk)

        @pl.when(d == r - 1)
        def _():
            inv = jnp.where(top, 1.0 / l_s[0], 1.0 / l_s[1])
            o_ref[...] = (acc_s[...] * inv).T.astype(BF16)
            lsea_ref[...] = m_s[0] + jnp.log(l_s[0])
            lseb_ref[...] = m_s[1] + jnp.log(l_s[1])

    qspec = pl.BlockSpec((tq, W), lambda hp, n, ii, jj: (ii[n], hp))
    kspec = pl.BlockSpec((tk, W), lambda hp, n, ii, jj: (jj[n], hp))
    rspec = pl.BlockSpec((None, 1, tq), lambda hp, n, ii, jj: (hp, 0, ii[n]))
    grid_spec = pltpu.PrefetchScalarGridSpec(
        num_scalar_prefetch=2, grid=(Hp, n_steps),
        in_specs=[qspec, kspec, kspec, qspec, kspec],
        out_specs=[qspec, rspec, rspec],
        scratch_shapes=[pltpu.VMEM((2, tq, W), BF16), pltpu.VMEM((2, 1, tq), F32), pltpu.VMEM((2, 1, tq), F32),
                        pltpu.VMEM((W, tq), F32), pltpu.VMEM((2, tk, tq), BF16)],
    )
    return pl.pallas_call(
        body, name="flash_fwd", grid_spec=grid_spec,
        out_shape=(jax.ShapeDtypeStruct((T, D), BF16), jax.ShapeDtypeStruct((Hp, 1, T), F32),
                   jax.ShapeDtypeStruct((Hp, 1, T), F32)),
        compiler_params=_params(2),
    )(ii, jj, q, k, v, qaug, kaug)


def _attn_delta(do, o, head_sel):
    T, D = do.shape
    H = head_sel.shape[0]
    tm = _tile(T, ROW_TILE_FWD)

    def body(do_ref, o_ref, sel_ref, out_ref):
        prod = do_ref[...].astype(F32) * o_ref[...].astype(F32)
        out_ref[...] = lax.dot_general(sel_ref[...], prod, (((1,), (1,)), ((), ())),
                                       precision=lax.Precision.HIGHEST, preferred_element_type=F32)

    return pl.pallas_call(
        body, name="attn_delta", grid=(T // tm,),
        in_specs=[pl.BlockSpec((tm, D), lambda i: (i, 0)), pl.BlockSpec((tm, D), lambda i: (i, 0)),
                  pl.BlockSpec((H, D), lambda i: (0, 0))],
        out_specs=pl.BlockSpec((H, tm), lambda i: (0, i)),
        out_shape=jax.ShapeDtypeStruct((H, T), F32), compiler_params=_params())(do, o, head_sel)


def _flash_bwd(q, k, v, qaug, kaug, do, lse_a, lse_b, delta, qscale):
    T, D = q.shape
    Hp = lse_a.shape[0]
    W = D // Hp
    dh = W // 2
    tq, tk, ksz, qsz = _attn_tiles(T, ATTN_Q_TILE_BWD, 2 * ATTN_K_TILE)
    r = tq // tk
    nbq = T // tq
    jj, ii = _causal_tables(nbq, r, q_outer=False)
    n_steps = int(ii.shape[0])
    row_row = dh
    col_row = dh + N_AUG

    def body(jj_ref, ii_ref, q_ref, k_ref, v_ref, qa_ref, ka_ref, do_ref, lsea_ref, lseb_ref, dla_ref, dlb_ref,
             dqt_ref, dkt_ref, dvt_ref, dcola_ref, dcolb_ref, drowa_ref, drowb_ref,
             dqt_acc, dk_acc, dv_acc, *rot):
        n = pl.program_id(1)
        i, j = ii_ref[n], jj_ref[n]
        d = j - i * r
        in_a = lax.broadcasted_iota(jnp.int32, (1, W), 1) < dh
        top = lax.broadcasted_iota(jnp.int32, (W, 1), 0) < dh

        @pl.when(n == 0)
        def _():
            dqt_acc[...] = jnp.zeros_like(dqt_acc)

        @pl.when(d >= 0)
        def _():
            dk_acc[...] = jnp.zeros_like(dk_acc)
            dv_acc[...] = jnp.zeros_like(dv_acc)

        def step(koff):
            qv, qa, kv, ka = q_ref[...], qa_ref[...], k_ref[...], ka_ref[...]
            vv, dov = v_ref[...], do_ref[...]
            zb = jnp.zeros_like(dov)
            kms = (jnp.where(in_a, kv, ka), jnp.where(in_a, ka, kv))
            qms = (jnp.where(in_a, qv, qa), jnp.where(in_a, qa, qv))
            doms = (jnp.where(in_a, dov, zb), jnp.where(in_a, zb, dov))
            def own_first(a, hh):
                return a[:dh + AUG_ROWS] if hh == 0 else jnp.concatenate([a[dh:], a[:AUG_ROWS]], axis=0)

            kxs = tuple(own_first(kms[hh].T, hh) for hh in range(2))
            qxs = tuple(own_first(qms[hh].T, hh) for hh in range(2))
            dot = dov.T
            dxs = (dot[:dh], dot[dh:])
            lses, dls = (lsea_ref[...], lseb_ref[...]), (dla_ref[...], dlb_ref[...])
            dvs, dks = ([], []), ([], [])

            def score_stage(u, hh, q0):
                p_s, ds_s = rot[2 * (u % ATTN_ROTATE)], rot[2 * (u % ATTN_ROTATE) + 1]
                cols = slice(q0, q0 + qsz)
                kinds = [(k0, _strip_kind(koff, k0, ksz, q0, qsz)) for k0 in range(0, tk, ksz)]
                if all(kind == "skip" for _, kind in kinds):
                    return False
                for k0, kind in kinds:
                    if kind == "skip":
                        p_s[k0:k0 + ksz, :] = jnp.zeros((ksz, qsz), BF16)
                        ds_s[k0:k0 + ksz, :] = jnp.zeros((ksz, qsz), BF16)
                        continue
                    st = _score_strip(kms[hh], qms[hh], koff, k0, ksz, q0, qsz, kind)
                    pt = jnp.exp(st - lses[hh][:, cols])
                    dst = pt * (_dot_nt(vv[k0:k0 + ksz, :], doms[hh][cols, :]) - dls[hh][:, cols])
                    p_s[k0:k0 + ksz, :] = pt.astype(BF16)
                    ds_s[k0:k0 + ksz, :] = dst.astype(BF16)
                return True

            def grad_stage(u, hh, q0):
                p_s, ds_s = rot[2 * (u % ATTN_ROTATE)], rot[2 * (u % ATTN_ROTATE) + 1]
                cols = slice(q0, q0 + qsz)
                ptb, dsb = p_s[...], ds_s[...]
                dvs[hh].append(_dot_nt(dxs[hh][:, cols], ptb))
                dks[hh].append(_dot_nt(qxs[hh][:, cols], dsb))
                dqt_acc[i, hh, :, cols] += _dot(kxs[hh], dsb)

            order = [(hh, q0) for q0 in range(0, tq, qsz) for hh in range(2)]
            units = [(u, hh, q0) for u, (hh, q0) in enumerate(order)]
            pending = None
            for unit in units:
                live = score_stage(*unit)
                if pending is not None:
                    grad_stage(*pending)
                pending = unit if live else None
            if pending is not None:
                grad_stage(*pending)
            for hh in range(2):
                dv_acc[hh] += sum(dvs[hh][1:], dvs[hh][0])
                dk_acc[hh] += sum(dks[hh][1:], dks[hh][0])

        @pl.when(d < 0)
        def _():
            step(None)

        for dd in range(r):
            @pl.when(d == dd)
            def _():
                step(dd * tk)

        @pl.when(i == nbq - 1)
        def _():
            dkt_ref[...] = jnp.concatenate([dk_acc[0, :dh], dk_acc[1, :dh]], axis=0).astype(BF16)
            dvt_ref[...] = jnp.concatenate([dv_acc[0], dv_acc[1]], axis=0).astype(BF16)
            dcola_ref[...] = dk_acc[0, col_row:col_row + 1, :]
            dcolb_ref[...] = dk_acc[1, col_row:col_row + 1, :]

        @pl.when(n == n_steps - 1)
        def _():
            for b in range(nbq):
                cols = slice(b * tq, (b + 1) * tq)
                both = jnp.concatenate([dqt_acc[b, 0, :dh], dqt_acc[b, 1, :dh]], axis=0)
                dqt_ref[:, cols] = (both * qscale).astype(BF16)
                drowa_ref[:, cols] = dqt_acc[b, 0, row_row:row_row + 1, :]
                drowb_ref[:, cols] = dqt_acc[b, 1, row_row:row_row + 1, :]

    qspec = pl.BlockSpec((tq, W), lambda hp, n, jj, ii: (ii[n], hp))
    kspec = pl.BlockSpec((tk, W), lambda hp, n, jj, ii: (jj[n], hp))
    ktspec = pl.BlockSpec((W, tk), lambda hp, n, jj, ii: (hp, jj[n]))
    pair_row = pl.BlockSpec((None, 1, tq), lambda hp, n, jj, ii: (hp, 0, ii[n]))
    key_row = pl.BlockSpec((None, 1, tk), lambda hp, n, jj, ii: (hp, 0, jj[n]))
    whole_row = pl.BlockSpec((None, 1, T), lambda hp, n, jj, ii: (hp, 0, 0))
    grid_spec = pltpu.PrefetchScalarGridSpec(
        num_scalar_prefetch=2, grid=(Hp, n_steps),
        in_specs=[
            qspec, kspec, kspec, qspec, kspec, qspec, pair_row, pair_row,
            pl.BlockSpec((None, 1, tq), lambda hp, n, jj, ii: (2 * hp, 0, ii[n])),
            pl.BlockSpec((None, 1, tq), lambda hp, n, jj, ii: (2 * hp + 1, 0, ii[n])),
        ],
        out_specs=[
            pl.BlockSpec((W, T), lambda hp, n, jj, ii: (hp, 0)),
            ktspec, ktspec, key_row, key_row, whole_row, whole_row,
        ],
        scratch_shapes=[pltpu.VMEM((nbq, 2, dh + AUG_ROWS, tq), F32), pltpu.VMEM((2, dh + AUG_ROWS, tk), F32),
                        pltpu.VMEM((2, dh, tk), F32)] + [pltpu.VMEM((tk, qsz), BF16)] * (2 * ATTN_ROTATE),
    )
    return pl.pallas_call(
        body, name="flash_bwd", grid_spec=grid_spec,
        out_shape=(jax.ShapeDtypeStruct((D, T), BF16), jax.ShapeDtypeStruct((D, T), BF16),
                   jax.ShapeDtypeStruct((D, T), BF16), jax.ShapeDtypeStruct((Hp, 1, T), F32),
                   jax.ShapeDtypeStruct((Hp, 1, T), F32),
                   jax.ShapeDtypeStruct((Hp, 1, T), F32), jax.ShapeDtypeStruct((Hp, 1, T), F32)),
        compiler_params=_params(2),
    )(jj, ii, q, k, v, qaug, kaug, do, lse_a, lse_b, delta, delta)


HBM_SPEC = pl.BlockSpec(memory_space=pltpu.HBM)


def _mesh_pos():
    return lax.axis_index("x"), lax.axis_index("y"), lax.axis_index("c")


def _all_gather_shards(shards):
    n = len(shards)

    def body(*refs):
        in_refs, out_refs = refs[:n], refs[n:2 * n]
        send1, recv1, send2, recv2 = refs[2 * n:]
        x, y, c = _mesh_pos()
        me = 2 * x + y
        chips = [(1 - x, y), (x, 1 - y), (1 - x, 1 - y)]

        def half(t, chip_idx, pc):
            hr = shards[t].shape[0] // 2
            return out_refs[t].at[chip_idx, pl.ds(pc * hr, hr), :]

        first = []
        for t in range(n):
            hr = shards[t].shape[0] // 2
            for kk, (cx, cy) in enumerate(chips):
                first.append(pltpu.make_async_remote_copy(
                    src_ref=in_refs[t].at[pl.ds(c * hr, hr), :], dst_ref=half(t, me, c),
                    send_sem=send1.at[3 * t + kk], recv_sem=recv1.at[3 * t + kk],
                    device_id=(cx, cy, c), device_id_type=MESH))
        for cp in first:
            cp.start()
        passed = []
        for t in range(n):
            for kk, (cx, cy) in enumerate(chips):
                src_chip = 2 * cx + cy
                landed = half(t, src_chip, c)
                pltpu.make_async_remote_copy(
                    src_ref=landed, dst_ref=landed, send_sem=send1.at[3 * t + kk], recv_sem=recv1.at[3 * t + kk],
                    device_id=(cx, cy, c), device_id_type=MESH).wait_recv()
                fwd = pltpu.make_async_remote_copy(
                    src_ref=landed, dst_ref=landed, send_sem=send2.at[3 * t + kk], recv_sem=recv2.at[3 * t + kk],
                    device_id=(x, y, 1 - c), device_id_type=MESH)
                fwd.start()
                passed.append(fwd)
        for t in range(n):
            for kk, (cx, cy) in enumerate(chips):
                other = half(t, 2 * cx + cy, 1 - c)
                pltpu.make_async_remote_copy(
                    src_ref=other, dst_ref=other, send_sem=send2.at[3 * t + kk], recv_sem=recv2.at[3 * t + kk],
                    device_id=(x, y, 1 - c), device_id_type=MESH).wait_recv()
        for cp in first + passed:
            cp.wait_send()

    gathered = pl.pallas_call(
        body, name="weights_all_gather",
        out_shape=[jax.ShapeDtypeStruct((N_CHIPS,) + s.shape, s.dtype) for s in shards],
        in_specs=[HBM_SPEC] * n, out_specs=[HBM_SPEC] * n,
        scratch_shapes=[pltpu.SemaphoreType.DMA((3 * n,)), pltpu.SemaphoreType.DMA((3 * n,)),
                        pltpu.SemaphoreType.DMA((3 * n,)), pltpu.SemaphoreType.DMA((3 * n,))],
    )(*shards)
    own = lax.broadcasted_iota(jnp.int32, (N_CHIPS, 1, 1), 0) == 2 * lax.axis_index("x") + lax.axis_index("y")
    return [jnp.where(own, s[None], g) for s, g in zip(shards, gathered)]


def _sibling_swap_halves(gs):
    n = len(gs)

    def body(*refs):
        g_refs, r_refs = refs[:n], refs[n:2 * n]
        ssem, rsem = refs[2 * n:]
        x, y, c = _mesh_pos()
        cps = []
        for t in range(n):
            hr = gs[t].shape[1] // 2
            cps += [pltpu.make_async_remote_copy(
                src_ref=g_refs[t].at[s, pl.ds((1 - c) * hr, hr), :], dst_ref=r_refs[t].at[s],
                send_sem=ssem.at[N_CHIPS * t + s], recv_sem=rsem.at[N_CHIPS * t + s],
                device_id=(x, y, 1 - c), device_id_type=MESH) for s in range(N_CHIPS)]
        for cp in cps:
            cp.start()
        for cp in cps:
            cp.wait()

    return pl.pallas_call(
        body, name="grads_sibling_swap",
        out_shape=[jax.ShapeDtypeStruct((N_CHIPS, g.shape[1] // 2, g.shape[2]), F32) for g in gs],
        in_specs=[HBM_SPEC] * n, out_specs=[HBM_SPEC] * n,
        scratch_shapes=[pltpu.SemaphoreType.DMA((N_CHIPS * n,)), pltpu.SemaphoreType.DMA((N_CHIPS * n,))],
    )(*gs)


def _pair_add(g, r, c_idx):
    _, M, C = g.shape
    hr = M // 2
    tr = _tile(hr, PACK_ROW_TILE)
    nbk = hr // tr

    def body(c_ref, g_ref, r_ref, o_ref):
        o_ref[...] = (g_ref[...] + r_ref[...]).astype(BF16)

    grid_spec = pltpu.PrefetchScalarGridSpec(
        num_scalar_prefetch=1, grid=(N_CHIPS, nbk),
        in_specs=[pl.BlockSpec((None, tr, C), lambda s, i, c: (s, c[0] * nbk + i, 0)),
                  pl.BlockSpec((None, tr, C), lambda s, i, c: (s, i, 0))],
        out_specs=pl.BlockSpec((None, tr, C), lambda s, i, c: (s, i, 0)),
    )
    return pl.pallas_call(body, name="grads_pair_add", grid_spec=grid_spec,
                          out_shape=jax.ShapeDtypeStruct((N_CHIPS, hr, C), BF16),
                          compiler_params=_params(2))(c_idx, g, r)


def _chip_scatter(pps):
    n = len(pps)

    def body(*refs):
        p_refs, r_refs = refs[:n], refs[n:2 * n]
        ssem, rsem, lsem = refs[2 * n:]
        x, y, c = _mesh_pos()
        me = 2 * x + y
        chips = [(1 - x, y), (x, 1 - y), (1 - x, 1 - y)]
        own = [pltpu.make_async_copy(p_refs[t].at[me], r_refs[t].at[me], lsem.at[t]) for t in range(n)]
        for cp in own:
            cp.start()
        cps = [pltpu.make_async_remote_copy(
            src_ref=p_refs[t].at[2 * cx + cy], dst_ref=r_refs[t].at[me],
            send_sem=ssem.at[3 * t + kk], recv_sem=rsem.at[3 * t + kk],
            device_id=(cx, cy, c), device_id_type=MESH) for t in range(n) for kk, (cx, cy) in enumerate(chips)]
        for cp in cps:
            cp.start()
        for t in range(n):
            for kk, (cx, cy) in enumerate(chips):
                got = r_refs[t].at[2 * cx + cy]
                pltpu.make_async_remote_copy(
                    src_ref=got, dst_ref=got, send_sem=ssem.at[3 * t + kk], recv_sem=rsem.at[3 * t + kk],
                    device_id=(cx, cy, c), device_id_type=MESH).wait_recv()
        for cp in cps:
            cp.wait_send()
        for cp in own:
            cp.wait()

    return pl.pallas_call(
        body, name="grads_chip_scatter", out_shape=[jax.ShapeDtypeStruct(pp.shape, pp.dtype) for pp in pps],
        in_specs=[HBM_SPEC] * n, out_specs=[HBM_SPEC] * n,
        scratch_shapes=[pltpu.SemaphoreType.DMA((3 * n,)), pltpu.SemaphoreType.DMA((3 * n,)),
                        pltpu.SemaphoreType.DMA((n,))],
    )(*pps)


def _chip_sum(r, c_idx):
    _, hr, C = r.shape
    tr = _tile(hr, PACK_ROW_TILE)
    nbk = hr // tr

    def body(c_ref, r_ref, o_ref):
        r0, r1, r2, r3 = (r_ref[s].astype(F32) for s in range(N_CHIPS))
        o_ref[...] = ((r0 + r1) + r2) + r3

    grid_spec = pltpu.PrefetchScalarGridSpec(
        num_scalar_prefetch=1, grid=(nbk,),
        in_specs=[pl.BlockSpec((N_CHIPS, tr, C), lambda i, c: (0, i, 0))],
        out_specs=pl.BlockSpec((tr, C), lambda i, c: (c[0] * nbk + i, 0)),
    )
    return pl.pallas_call(body, name="grads_chip_sum", grid_spec=grid_spec,
                          out_shape=jax.ShapeDtypeStruct((2 * hr, C), F32), compiler_params=_params())(c_idx, r)


def _sibling_join(bufs):
    n = len(bufs)

    def body(*refs):
        o_refs = refs[n:2 * n]
        ssem, rsem = refs[2 * n:]
        x, y, c = _mesh_pos()
        cps = []
        for t in range(n):
            hr = bufs[t].shape[0] // 2
            mine = o_refs[t].at[pl.ds(c * hr, hr), :]
            cps.append(pltpu.make_async_remote_copy(src_ref=mine, dst_ref=mine, send_sem=ssem.at[t], recv_sem=rsem.at[t],
                                                    device_id=(x, y, 1 - c), device_id_type=MESH))
        for cp in cps:
            cp.start()
        for t in range(n):
            hr = bufs[t].shape[0] // 2
            theirs = o_refs[t].at[pl.ds((1 - c) * hr, hr), :]
            pltpu.make_async_remote_copy(src_ref=theirs, dst_ref=theirs, send_sem=ssem.at[t], recv_sem=rsem.at[t],
                                         device_id=(x, y, 1 - c), device_id_type=MESH).wait_recv()
        for cp in cps:
            cp.wait_send()

    return pl.pallas_call(
        body, name="grads_sibling_join", out_shape=[jax.ShapeDtypeStruct(b.shape, F32) for b in bufs],
        in_specs=[HBM_SPEC] * n, out_specs=[HBM_SPEC] * n, input_output_aliases={t: t for t in range(n)},
        scratch_shapes=[pltpu.SemaphoreType.DMA((n,)), pltpu.SemaphoreType.DMA((n,))],
    )(*bufs)


def _adamw(w, g, m, v):
    M, C = w.shape
    tr = _tile(M, PACK_ROW_TILE)

    def body(w_ref, g_ref, m_ref, v_ref, d_ref, mo_ref, vo_ref):
        gv = g_ref[...]
        mn = ADAM_B1 * m_ref[...] + (1.0 - ADAM_B1) * gv
        vn = ADAM_B2 * v_ref[...] + (1.0 - ADAM_B2) * (gv * gv)
        m_hat = mn / (1.0 - ADAM_B1 ** ADAM_STEP)
        v_hat = vn / (1.0 - ADAM_B2 ** ADAM_STEP)
        d_ref[...] = -ADAM_LR * (m_hat / (jnp.sqrt(v_hat) + ADAM_EPS) + ADAM_WD * w_ref[...])
        mo_ref[...] = mn
        vo_ref[...] = vn

    spec = pl.BlockSpec((tr, C), lambda i: (i, 0))
    return pl.pallas_call(
        body, name="adamw", grid=(M // tr,), in_specs=[spec] * 4, out_specs=[spec] * 3,
        out_shape=[jax.ShapeDtypeStruct((M, C), F32)] * 3, compiler_params=_params())(w, g, m, v)


WEIGHT_NAMES = ("norm_g", "w_attn_in", "b_forget", "w_attn_out", "w_conv_in", "conv_w", "w_conv_out",
                "w_mlp_up", "w_mlp_down", "w_ple_proj", "w_ple_gate")
COL_SHARDED = ("norm_g", "w_attn_in", "w_conv_in", "conv_w", "w_mlp_up", "w_ple_proj")
ROW_SHARDED = ("w_attn_out", "w_conv_out", "w_mlp_down", "w_ple_gate")


def _unshard(name, gathered, shard_shape):
    a = gathered.reshape((N_CHIPS,) + tuple(shard_shape))
    if name in COL_SHARDED:
        a = jnp.moveaxis(a, 0, -2)
        return a.reshape(a.shape[:-2] + (N_CHIPS * shard_shape[-1],))
    a = jnp.moveaxis(a, 0, 1)
    return a.reshape((shard_shape[0], N_CHIPS * shard_shape[1], shard_shape[2]))


def _to_shard_major(name, full):
    if name == "b_forget":
        return jnp.broadcast_to(full.reshape(1, -1), (N_CHIPS, full.size))
    if name in COL_SHARDED:
        a = full.reshape(full.shape[:-1] + (N_CHIPS, full.shape[-1] // N_CHIPS))
        a = jnp.moveaxis(a, -2, 0)
    else:
        a = full.reshape((full.shape[0], N_CHIPS, full.shape[1] // N_CHIPS, full.shape[2]))
        a = jnp.moveaxis(a, 1, 0)
    return a.reshape(N_CHIPS, -1)


def _reduce_scatter(tensors, c_idx):
    swapped = _sibling_swap_halves(tensors)
    pairs = [_pair_add(g, r, c_idx) for g, r in zip(tensors, swapped)]
    return _sibling_join([_chip_sum(r, c_idx) for r in _chip_scatter(pairs)])


def kernel(x, p, norm_g, w_attn_in, b_forget, w_attn_out, w_conv_in, conv_w, w_conv_out, w_mlp_up, w_mlp_down, w_ple_proj, w_ple_gate, loss_target, m_norm_g, m_w_attn_in, m_b_forget, m_w_attn_out, m_w_conv_in, m_conv_w, m_w_conv_out, m_w_mlp_up, m_w_mlp_down, m_w_ple_proj, m_w_ple_gate, v_norm_g, v_w_attn_in, v_b_forget, v_w_attn_out, v_w_conv_in, v_conv_w, v_w_conv_out, v_w_mlp_up, v_w_mlp_down, v_w_ple_proj, v_w_ple_gate):
    w_local = dict(norm_g=norm_g, w_attn_in=w_attn_in, b_forget=b_forget, w_attn_out=w_attn_out,
                   w_conv_in=w_conv_in, conv_w=conv_w, w_conv_out=w_conv_out, w_mlp_up=w_mlp_up,
                   w_mlp_down=w_mlp_down, w_ple_proj=w_ple_proj, w_ple_gate=w_ple_gate)
    m_local = dict(norm_g=m_norm_g, w_attn_in=m_w_attn_in, b_forget=m_b_forget, w_attn_out=m_w_attn_out,
                   w_conv_in=m_w_conv_in, conv_w=m_conv_w, w_conv_out=m_w_conv_out, w_mlp_up=m_w_mlp_up,
                   w_mlp_down=m_w_mlp_down, w_ple_proj=m_w_ple_proj, w_ple_gate=m_w_ple_gate)
    v_local = dict(norm_g=v_norm_g, w_attn_in=v_w_attn_in, b_forget=v_b_forget, w_attn_out=v_w_attn_out,
                   w_conv_in=v_w_conv_in, conv_w=v_conv_w, w_conv_out=v_w_conv_out, w_mlp_up=v_w_mlp_up,
                   w_mlp_down=v_w_mlp_down, w_ple_proj=v_w_ple_proj, w_ple_gate=v_w_ple_gate)
    shard_shapes = {k: tuple(a.shape) for k, a in w_local.items()}

    xs = x[0]
    target = loss_target[0]
    T, D = xs.shape
    depth = p.shape[0]
    H = b_forget.shape[1]
    qscale = float(D // H) ** -0.5
    head_sel = (jnp.arange(D)[None, :] // (D // H) == jnp.arange(H)[:, None]).astype(F32)

    big = [n for n in WEIGHT_NAMES if n not in ("norm_g", "conv_w", "b_forget")]
    small = jnp.concatenate([norm_g.reshape(-1), conv_w.reshape(-1)])
    n_small = small.shape[0]
    small_rows = -(-n_small // (LANES * 16)) * 16
    small = jnp.pad(small, (0, small_rows * LANES - n_small)).reshape(small_rows, LANES)
    shards = [w_local[n].astype(BF16).reshape(-1, shard_shapes[n][-1]) for n in big] + [small]
    gathered = _all_gather_shards(shards)
    full = {n: _unshard(n, g, shard_shapes[n]) for n, g in zip(big, gathered[:-1])}
    gs = gathered[-1].reshape(N_CHIPS, -1)
    full["norm_g"] = _unshard("norm_g", gs[:, :norm_g.size], shard_shapes["norm_g"])
    full["conv_w"] = _unshard("conv_w", gs[:, norm_g.size:n_small], shard_shapes["conv_w"])
    gains = full["norm_g"]

    def gain(i, k):
        return gains[i, k].reshape(1, D)

    def taps(j):
        return jnp.pad(full["conv_w"][j], ((0, 5), (0, 0)))

    saved = []
    h = xs
    for i in range(depth):
        j = i // 2
        s = {"x0": h}
        if i % 2 == 0:
            w_in = full["w_attn_in"][j]
            wqkv = w_in[:, :3 * D]
            wf = jnp.pad(w_in[:, 3 * D:], ((0, 0), (0, LANES - H)))
            bf = jnp.pad(b_forget[j].reshape(1, H), ((0, 0), (0, LANES - H)))
            q, k, v, fl, lf = _attn_in_fwd(h, gain(i, 0), wqkv, wf, bf, qscale)
            qaug, kaug = _aug_operands(_gate_cumsum(lf[:, :H].T), H, D)
            o, lse_a, lse_b = _flash_fwd(q, k, v, qaug, kaug, H)
            s.update(q=q, k=k, v=v, fl=fl, qaug=qaug, kaug=kaug, o=o, lse_a=lse_a, lse_b=lse_b, wqkv=wqkv, wf=wf)
            mix_in, w_out = o, full["w_attn_out"][j]
        else:
            b, c, u, zc, y = _conv_in_fwd(h, gain(i, 0), full["w_conv_in"][j], taps(j))
            s.update(b=b, c=c, u=u, zc=zc, y=y)
            mix_in, w_out = y, full["w_conv_out"][j]
        m1, x1 = _out_proj_fwd("mixer_out_fwd", mix_in, w_out, gain(i, 1), h)
        up, a = _mlp_up_fwd(x1, gain(i, 2), full["w_mlp_up"][i])
        m3, x2 = _out_proj_fwd("mlp_down_fwd", a, full["w_mlp_down"][i], gain(i, 3), x1)
        gl, pe, x3 = _ple_fwd(x2, p[i, 0], gain(i, 4), gain(i, 5), full["w_ple_gate"][i], full["w_ple_proj"][i])
        s.update(m1=m1, x1=x1, up=up, a=a, m3=m3, x2=x2, gl=gl, pe=pe, w_out=w_out)
        saved.append(s)
        h = x3

    dh, loss_blk = _loss_fwd_bwd(h, target)
    loss = lax.psum(loss_blk[0, 0], ("x", "y", "c"))

    g_gain = [[None] * 6 for _ in range(depth)]
    grads = {n: [None] * w_local[n].shape[0] for n in WEIGHT_NAMES if n != "norm_g"}
    for i in reversed(range(depth)):
        j = i // 2
        s = saved[i]
        dx2, dwp, dwg, g_gain[i][4], g_gain[i][5] = _ple_bwd(
            dh, s["x2"], p[i, 0], s["gl"], s["pe"], gain(i, 4), gain(i, 5), full["w_ple_gate"][i].T)
        grads["w_ple_proj"][i], grads["w_ple_gate"][i] = dwp, dwg
        dup, dwd, g_gain[i][3] = _out_proj_bwd(
            "mlp_down_bwd", dx2, s["m3"], gain(i, 3), [s["a"], s["up"]], full["w_mlp_down"][i].T, "relu2")
        grads["w_mlp_down"][i] = dwd
        dx1, dwu, g_gain[i][2] = _in_proj_bwd(
            "mlp_up_bwd", [dup], [full["w_mlp_up"][i].T], s["x1"], gain(i, 2), dx2)
        grads["w_mlp_up"][i] = dwu
        if i % 2 == 0:
            do, dwo, g_gain[i][1] = _out_proj_bwd(
                "attn_out_bwd", dx1, s["m1"], gain(i, 1), [s["o"]], s["w_out"].T, "plain")
            grads["w_attn_out"][j] = dwo
            delta = _attn_delta(do, s["o"], head_sel).reshape(H, 1, T)
            dqt, dkt, dvt, dca, dcb, dra, drb = _flash_bwd(s["q"], s["k"], s["v"], s["qaug"], s["kaug"], do,
                                                           s["lse_a"], s["lse_b"], delta, qscale)
            dcol = jnp.concatenate([dca, dcb], axis=1).reshape(H, T)
            drow = jnp.concatenate([dra, drb], axis=1).reshape(H, T)
            dfl_t, dbf = _gate_bwd(drow, dcol, s["fl"][:, :H].T)
            grads["b_forget"][j] = dbf.reshape(H)
            dfl = jnp.pad(dfl_t.T, ((0, 0), (0, LANES - H))).astype(BF16)
            wqkv_t = s["wqkv"].T
            dh, dwq, dwk, dwv, dwf, g_gain[i][0] = _in_proj_bwd(
                "attn_in_bwd", [dqt.T, dkt.T, dvt.T, dfl], [wqkv_t[:D], wqkv_t[D:2 * D], wqkv_t[2 * D:], s["wf"].T],
                s["x0"], gain(i, 0), dx1)
            grads["w_attn_in"][j] = jnp.concatenate([dwq, dwk, dwv, dwf[:, :H]], axis=1)
        else:
            dy, dwo, g_gain[i][1] = _out_proj_bwd(
                "conv_out_bwd", dx1, s["m1"], gain(i, 1), [s["y"]], s["w_out"].T, "plain_f32")
            grads["w_conv_out"][j] = dwo
            db, dc, du, dcw = _conv_bwd(dy, s["b"], s["c"], s["u"], s["zc"], taps(j))
            grads["conv_w"][j] = dcw[:conv_w.shape[1]]
            w_t = full["w_conv_in"][j].T
            dh, dwb, dwc, dwu2, g_gain[i][0] = _in_proj_bwd(
                "conv_in_bwd", [db, dc, du], [w_t[:D], w_t[D:2 * D], w_t[2 * D:]], s["x0"], gain(i, 0), dx1)
            grads["w_conv_in"][j] = jnp.concatenate([dwb, dwc, dwu2], axis=1)
    grad_x = dh.reshape(x.shape)

    grad_full = {n: jnp.stack(grads[n]) for n in grads}
    grad_full["norm_g"] = jnp.stack([jnp.concatenate(row, axis=0) for row in g_gain])

    c_idx = lax.axis_index("c").astype(jnp.int32).reshape(1)
    tiny = ("norm_g", "conv_w", "b_forget")
    tiny_flat = jnp.concatenate([_to_shard_major(n, grad_full[n]) for n in tiny], axis=1)
    n_tiny = tiny_flat.shape[1]
    tiny_rows = -(-n_tiny // (LANES * 32)) * 32
    tiny_t = jnp.pad(tiny_flat, ((0, 0), (0, tiny_rows * LANES - n_tiny))).reshape(N_CHIPS, tiny_rows, LANES)
    tensors = [_to_shard_major(n, grad_full[n]).reshape(N_CHIPS, -1, shard_shapes[n][-1]) for n in big]
    reduced = _reduce_scatter(tensors + [tiny_t], c_idx)
    g_out = {n: r.reshape(shard_shapes[n]) for n, r in zip(big, reduced[:-1])}
    tiny_red, off = reduced[-1].reshape(-1), 0
    for n in tiny:
        size = int(np.prod(shard_shapes[n]))
        g_out[n] = tiny_red[off:off + size].reshape(shard_shapes[n])
        off += size
    d_out, m_out, v_out = {}, {}, {}
    for n in WEIGHT_NAMES:
        shp = shard_shapes[n]
        two_d = (-1, shp[-1])
        d, mn, vn = _adamw(w_local[n].reshape(two_d), g_out[n].reshape(two_d),
                           m_local[n].reshape(two_d), v_local[n].reshape(two_d))
        d_out[n], m_out[n], v_out[n] = d.reshape(shp), mn.reshape(shp), vn.reshape(shp)
    return (loss, grad_x, *[g_out[n] for n in WEIGHT_NAMES], *[d_out[n] for n in WEIGHT_NAMES],
            *[m_out[n] for n in WEIGHT_NAMES], *[v_out[n] for n in WEIGHT_NAMES])
```

```python
import numpy as np
import jax
import jax.numpy as jnp
from jax import lax
from jax.experimental import pallas as pl
from jax.experimental.pallas import tpu as pltpu

F32 = jnp.float32
BF16 = jnp.bfloat16
MESH = pl.DeviceIdType.MESH

RMS_EPS = 1e-6
NEG_INF = -1e30
ADAM_LR = 0.001
ADAM_B1 = 0.9
ADAM_B2 = 0.999
ADAM_EPS = 1e-08
ADAM_WD = 0.01
ADAM_STEP = 10

N_CHIPS = 4
LANES = 128
ROW_TILE_FWD = 512
ROW_TILE_BWD = 512
ROW_TILE_BWD_FF = 256
ATTN_Q_TILE_FWD = 4096
ATTN_Q_TILE_BWD = 2048
ATTN_K_TILE = 512
ATTN_KEY_STRIP = 128
ATTN_QUERY_STRIP = 256
ATTN_ROTATE = 4
COL_CHUNK = 512
PACK_ROW_TILE = 256
AUG_ROWS = 16
N_AUG = 3
VMEM_LIMIT = 56 * 1024 * 1024


def _tile(n, pref):
    return pref if n % pref == 0 else n


def _dot(a, b):
    return jnp.dot(a, b, preferred_element_type=F32)


def _dot_tn(a, b):
    return lax.dot_general(a, b, (((0,), (0,)), ((), ())), preferred_element_type=F32)


def _dot_nt(a, b):
    return lax.dot_general(a, b, (((1,), (1,)), ((), ())), preferred_element_type=F32)


def _rms_fwd(x, g):
    r = lax.rsqrt(jnp.mean(x * x, axis=-1, keepdims=True) + RMS_EPS)
    return (x * r) * g


def _rms_bwd(x, g, dy):
    r = lax.rsqrt(jnp.mean(x * x, axis=-1, keepdims=True) + RMS_EPS)
    xh = x * r
    dyg = dy * g
    dx = r * (dyg - xh * jnp.mean(dyg * xh, axis=-1, keepdims=True))
    return dx, jnp.sum(dy * xh, axis=0, keepdims=True)


def _params(n_axes=1):
    return pltpu.CompilerParams(dimension_semantics=("arbitrary",) * n_axes, vmem_limit_bytes=VMEM_LIMIT)


def _row_call(name, body, n_rows, tm, row_ins, const_ins, row_outs, acc_outs=(), scratch=(), reverse=False):
    nb = n_rows // tm
    rmap = (lambda i: (nb - 1 - i, 0)) if reverse else (lambda i: (i, 0))

    def whole(shape):
        nd = len(shape)
        return pl.BlockSpec(tuple(shape), lambda i: (0,) * nd)

    in_specs = [pl.BlockSpec((tm, a.shape[1]), rmap) for a in row_ins] + [whole(a.shape) for a in const_ins]
    out_shape = [jax.ShapeDtypeStruct((n_rows, w), dt) for (w, dt) in row_outs]
    out_shape += [jax.ShapeDtypeStruct(tuple(s), dt) for (s, dt) in acc_outs]
    out_specs = [pl.BlockSpec((tm, w), rmap) for (w, _) in row_outs] + [whole(s) for (s, _) in acc_outs]
    return pl.pallas_call(
        body, name=name, grid=(nb,), in_specs=in_specs, out_specs=out_specs, out_shape=out_shape,
        scratch_shapes=list(scratch), compiler_params=_params(),
    )(*row_ins, *const_ins)


def _attn_in_fwd(x, g, wqkv, wf, bf, qscale):
    T, D = x.shape
    tm, ch = _tile(T, ROW_TILE_FWD), _tile(D, COL_CHUNK)

    def body(x_ref, g_ref, w_ref, wf_ref, bf_ref, q_ref, k_ref, v_ref, fl_ref, lf_ref):
        h = _rms_fwd(x_ref[...], g_ref[...]).astype(BF16)
        for part, o_ref in enumerate((q_ref, k_ref, v_ref)):
            for n0 in range(0, D, ch):
                r = _dot(h, w_ref[:, part * D + n0:part * D + n0 + ch])
                if part == 0:
                    r = r * qscale
                o_ref[:, n0:n0 + ch] = r.astype(BF16)
        fl = _dot(h, wf_ref[...]) + bf_ref[...]
        fl_ref[...] = fl
        lf_ref[...] = jnp.minimum(fl, 0.0) - jnp.log1p(jnp.exp(-jnp.abs(fl)))

    return _row_call("attn_in_fwd", body, T, tm, [x], [g, wqkv, wf, bf],
                     [(D, BF16), (D, BF16), (D, BF16), (LANES, F32), (LANES, F32)])


def _conv_in_fwd(x, g, w, cw):
    T, D = x.shape
    tm, ch = _tile(T, ROW_TILE_FWD), _tile(D, COL_CHUNK)

    def body(x_ref, g_ref, w_ref, cw_ref, b_ref, c_ref, u_ref, zc_ref, y_ref, tail_ref):
        i = pl.program_id(0)

        @pl.when(i == 0)
        def _():
            tail_ref[...] = jnp.zeros_like(tail_ref)

        h = _rms_fwd(x_ref[...], g_ref[...]).astype(BF16)
        for part, o_ref in enumerate((b_ref, c_ref, u_ref)):
            for n0 in range(0, D, ch):
                o_ref[:, n0:n0 + ch] = _dot(h, w_ref[:, part * D + n0:part * D + n0 + ch])
        z = c_ref[...] * u_ref[...]
        row = lax.broadcasted_iota(jnp.int32, (tm, 1), 0)
        t6, t7 = tail_ref[6:7, :], tail_ref[7:8, :]
        z1 = jnp.where(row == 0, t7, pltpu.roll(z, 1, axis=0))
        z2 = jnp.where(row == 0, t6, jnp.where(row == 1, t7, pltpu.roll(z, 2, axis=0)))
        zc = cw_ref[0:1, :] * z2 + cw_ref[1:2, :] * z1 + cw_ref[2:3, :] * z
        zc_ref[...] = zc
        y_ref[...] = (b_ref[...] * zc).astype(BF16)
        tail_ref[...] = z[tm - 8:tm, :]

    return _row_call("conv_in_fwd", body, T, tm, [x], [g, w, cw],
                     [(D, F32), (D, F32), (D, F32), (D, F32), (D, BF16)], scratch=[pltpu.VMEM((8, D), F32)])


def _mlp_up_fwd(x, g, w):
    T, D = x.shape
    F = w.shape[1]
    tm, ch = _tile(T, ROW_TILE_FWD), _tile(F, COL_CHUNK)

    def body(x_ref, g_ref, w_ref, up_ref, a_ref):
        h = _rms_fwd(x_ref[...], g_ref[...]).astype(BF16)
        for n0 in range(0, F, ch):
            r = _dot(h, w_ref[:, n0:n0 + ch])
            up_ref[:, n0:n0 + ch] = r.astype(BF16)
            rl = jnp.maximum(r, 0.0)
            a_ref[:, n0:n0 + ch] = (rl * rl).astype(BF16)

    return _row_call("mlp_up_fwd", body, T, tm, [x], [g, w], [(F, BF16), (F, BF16)])


def _out_proj_fwd(name, a, w, g, x):
    T, D = x.shape
    tm = _tile(T, ROW_TILE_FWD)

    def body(a_ref, x_ref, w_ref, g_ref, m_ref, xn_ref):
        m = _dot(a_ref[...], w_ref[...])
        m_ref[...] = m
        xn_ref[...] = x_ref[...] + _rms_fwd(m, g_ref[...])

    return _row_call(name, body, T, tm, [a, x], [w, g], [(D, F32), (D, F32)])


def _ple_fwd(x, p, g4, g5, wg, wp):
    T, D = x.shape
    tm = _tile(T, ROW_TILE_FWD)

    def body(x_ref, p_ref, g4_ref, g5_ref, wg_ref, wp_ref, gl_ref, pe_ref, xn_ref):
        xv = x_ref[...]
        gl = _dot(_rms_fwd(xv, g4_ref[...]).astype(BF16), wg_ref[...])
        pe = _dot(p_ref[...].astype(BF16), wp_ref[...])
        gl_ref[...] = gl
        pe_ref[...] = pe
        e = pe * (1.0 / (1.0 + jnp.exp(-gl)))
        xn_ref[...] = xv + _rms_fwd(e, g5_ref[...])

    return _row_call("ple_fwd", body, T, tm, [x, p], [g4, g5, wg, wp], [(D, F32), (D, F32), (D, F32)])


def _loss_fwd_bwd(y, target):
    T, D = y.shape
    tm = _tile(T, ROW_TILE_FWD)

    def body(y_ref, t_ref, dy_ref, loss_ref):
        @pl.when(pl.program_id(0) == 0)
        def _():
            loss_ref[...] = jnp.zeros_like(loss_ref)

        err = y_ref[...] - t_ref[...]
        dy_ref[...] = err * (1.0 / D)
        part = 0.5 * jnp.sum(jnp.mean(err * err, axis=-1, keepdims=True), axis=0, keepdims=True)
        loss_ref[...] += jnp.broadcast_to(part, loss_ref.shape)

    return _row_call("loss", body, T, tm, [y, target], [], [(D, F32)], acc_outs=[((8, LANES), F32)])


def _out_proj_bwd(name, dres, m, g, a_ins, wt, mode):
    T, D = dres.shape
    Ka = wt.shape[1]
    tm, ch = _tile(T, ROW_TILE_BWD_FF if mode == "relu2" else ROW_TILE_BWD), _tile(Ka, COL_CHUNK)
    out_dt = F32 if mode == "plain_f32" else BF16

    def body(dres_ref, m_ref, *rest):
        a_ref = rest[0]
        up_ref = rest[1] if mode == "relu2" else None
        k = len(a_ins)
        g_ref, wt_ref, da_ref, dw_ref, dg_ref = rest[k:k + 5]

        @pl.when(pl.program_id(0) == 0)
        def _():
            dw_ref[...] = jnp.zeros_like(dw_ref)
            dg_ref[...] = jnp.zeros_like(dg_ref)

        dm, dgp = _rms_bwd(m_ref[...], g_ref[...], dres_ref[...])
        dg_ref[...] += dgp
        dmb = dm.astype(BF16)
        for n0 in range(0, Ka, ch):
            da = _dot(dmb, wt_ref[:, n0:n0 + ch])
            if mode == "relu2":
                da = da * (2.0 * jnp.maximum(up_ref[:, n0:n0 + ch].astype(F32), 0.0))
            da_ref[:, n0:n0 + ch] = da.astype(out_dt)
            dw_ref[n0:n0 + ch, :] += _dot_tn(a_ref[:, n0:n0 + ch], dmb)

    return _row_call(name, body, T, tm, [dres, m] + list(a_ins), [g, wt],
                     [(Ka, out_dt)], acc_outs=[((Ka, D), F32), ((1, D), F32)])


def _in_proj_bwd(name, pieces, wts, x, g, dres):
    T, D = x.shape
    tm = _tile(T, ROW_TILE_BWD)
    k = len(pieces)
    widths = [pc.shape[1] for pc in pieces]

    def body(*refs):
        pc_refs = refs[:k]
        x_ref, dres_ref, g_ref = refs[k:k + 3]
        wt_refs = refs[k + 3:2 * k + 3]
        dx_ref = refs[2 * k + 3]
        dw_refs = refs[2 * k + 4:3 * k + 4]
        dg_ref = refs[3 * k + 4]

        @pl.when(pl.program_id(0) == 0)
        def _():
            for r in dw_refs:
                r[...] = jnp.zeros_like(r)
            dg_ref[...] = jnp.zeros_like(dg_ref)

        xv, gv = x_ref[...], g_ref[...]
        hb = _rms_fwd(xv, gv).astype(BF16)
        dh = None
        for pc_ref, wt_ref, dw_ref, n in zip(pc_refs, wt_refs, dw_refs, widths):
            d = _dot(pc_ref[...], wt_ref[...])
            dh = d if dh is None else dh + d
            ch = _tile(n, COL_CHUNK)
            for n0 in range(0, n, ch):
                dw_ref[:, n0:n0 + ch] += _dot_tn(hb, pc_ref[:, n0:n0 + ch])
        dxn, dgp = _rms_bwd(xv, gv, dh)
        dg_ref[...] += dgp
        dx_ref[...] = dres_ref[...] + dxn

    return _row_call(name, body, T, tm, list(pieces) + [x, dres], [g] + list(wts), [(D, F32)],
                     acc_outs=[((D, n), F32) for n in widths] + [((1, D), F32)])


def _conv_bwd(dy, b, c, u, zc, cw):
    T, D = dy.shape
    tm = _tile(T, ROW_TILE_BWD)

    def body(dy_ref, b_ref, c_ref, u_ref, zc_ref, cw_ref, db_ref, dc_ref, du_ref, dcw_ref, head_ref):
        @pl.when(pl.program_id(0) == 0)
        def _():
            head_ref[...] = jnp.zeros_like(head_ref)
            dcw_ref[...] = jnp.zeros_like(dcw_ref)

        dyv, cv, uv = dy_ref[...], c_ref[...], u_ref[...]
        db_ref[...] = (dyv * zc_ref[...]).astype(BF16)
        dzc = dyv * b_ref[...]
        row = lax.broadcasted_iota(jnp.int32, (tm, 1), 0)
        h0, h1 = head_ref[0:1, :], head_ref[1:2, :]
        d1 = jnp.where(row == tm - 1, h0, pltpu.roll(dzc, tm - 1, axis=0))
        d2 = jnp.where(row == tm - 1, h1, jnp.where(row == tm - 2, h0, pltpu.roll(dzc, tm - 2, axis=0)))
        dz = cw_ref[2:3, :] * dzc + cw_ref[1:2, :] * d1 + cw_ref[0:1, :] * d2
        dc_ref[...] = (dz * uv).astype(BF16)
        du_ref[...] = (dz * cv).astype(BF16)
        z = cv * uv
        dcw_ref[0:1, :] += jnp.sum(d2 * z, axis=0, keepdims=True)
        dcw_ref[1:2, :] += jnp.sum(d1 * z, axis=0, keepdims=True)
        dcw_ref[2:3, :] += jnp.sum(dzc * z, axis=0, keepdims=True)
        head_ref[...] = dzc[0:8, :]

    return _row_call("conv_bwd", body, T, tm, [dy, b, c, u, zc], [cw], [(D, BF16), (D, BF16), (D, BF16)],
                     acc_outs=[((8, D), F32)], scratch=[pltpu.VMEM((8, D), F32)], reverse=True)


def _ple_bwd(dres, x, p, gl, pe, g4, g5, wgt):
    T, D = x.shape
    P = p.shape[1]
    tm = _tile(T, ROW_TILE_BWD)

    def body(dres_ref, x_ref, p_ref, gl_ref, pe_ref, g4_ref, g5_ref, wgt_ref,
             dx_ref, dwp_ref, dwg_ref, dg4_ref, dg5_ref):
        @pl.when(pl.program_id(0) == 0)
        def _():
            for r in (dwp_ref, dwg_ref, dg4_ref, dg5_ref):
                r[...] = jnp.zeros_like(r)

        dr, xv, pe_v = dres_ref[...], x_ref[...], pe_ref[...]
        gate = 1.0 / (1.0 + jnp.exp(-gl_ref[...]))
        de, dg5p = _rms_bwd(pe_v * gate, g5_ref[...], dr)
        dg5_ref[...] += dg5p
        dpe = (de * gate).astype(BF16)
        dgl = (de * pe_v * gate * (1.0 - gate)).astype(BF16)
        dwp_ref[...] += _dot_tn(p_ref[...].astype(BF16), dpe)
        g4v = g4_ref[...]
        dwg_ref[...] += _dot_tn(_rms_fwd(xv, g4v).astype(BF16), dgl)
        dxn, dg4p = _rms_bwd(xv, g4v, _dot(dgl, wgt_ref[...]))
        dg4_ref[...] += dg4p
        dx_ref[...] = dr + dxn

    return _row_call("ple_bwd", body, T, tm, [dres, x, p, gl, pe], [g4, g5, wgt], [(D, F32)],
                     acc_outs=[((P, D), F32), ((D, D), F32), ((1, D), F32), ((1, D), F32)])


def _scan_lanes(v, reverse):
    n = v.shape[1]
    lane = lax.broadcasted_iota(jnp.int32, v.shape, 1)
    s = 1
    while s < n:
        if reverse:
            v = v + jnp.where(lane < n - s, pltpu.roll(v, n - s, axis=1), 0.0)
        else:
            v = v + jnp.where(lane >= s, pltpu.roll(v, s, axis=1), 0.0)
        s *= 2
    return v


def _gate_cumsum(lf_t):
    def body(lf_ref, *piece_refs):
        rest = _scan_lanes(lf_ref[...], reverse=False)
        for r in piece_refs:
            piece = rest.astype(BF16)
            r[...] = piece
            rest = rest - piece.astype(F32)

    return pl.pallas_call(body, name="gate_cumsum", out_shape=[jax.ShapeDtypeStruct(lf_t.shape, BF16)] * N_AUG,
                          compiler_params=pltpu.CompilerParams(vmem_limit_bytes=VMEM_LIMIT))(lf_t)


def _gate_bwd(drow_t, dcol_t, fl_t):
    H = fl_t.shape[0]

    def body(dr_ref, dc_ref, fl_ref, dfl_ref, dbf_ref):
        dlf = _scan_lanes(dr_ref[...] - dc_ref[...], reverse=True)
        dfl = dlf * (1.0 / (1.0 + jnp.exp(fl_ref[...])))
        dfl_ref[...] = dfl
        dbf_ref[...] = jnp.sum(dfl, axis=1, keepdims=True)

    return pl.pallas_call(
        body, name="gate_bwd",
        out_shape=(jax.ShapeDtypeStruct(fl_t.shape, F32), jax.ShapeDtypeStruct((H, 1), F32)),
        compiler_params=pltpu.CompilerParams(vmem_limit_bytes=VMEM_LIMIT))(drow_t, dcol_t, fl_t)


def _aug_operands(pieces, H, D):
    T = pieces[0].shape[1]
    dh = D // H
    one = jnp.ones((H, T), BF16)
    qa = jnp.stack(list(pieces) + [one] * N_AUG, axis=-1)
    ka = jnp.stack([one] * N_AUG + [-pc for pc in pieces], axis=-1)

    def place(a):
        a = jnp.pad(a, ((0, 0), (0, 0), (0, dh - 2 * N_AUG))).reshape(H // 2, 2, T, dh)
        return jnp.transpose(a[:, ::-1], (2, 0, 1, 3)).reshape(T, D)

    return place(qa), place(ka)


def _strip_kind(koff, k0, ksz, q0, qsz):
    if koff is None or koff + k0 + ksz - 1 <= q0:
        return "full"
    return "skip" if koff + k0 > q0 + qsz - 1 else "partial"


def _score_strip(km, qm, koff, k0, ksz, q0, qsz, kind):
    st = _dot_nt(km[k0:k0 + ksz, :], qm[q0:q0 + qsz, :])
    if kind == "partial":
        kpos = koff + k0 + lax.broadcasted_iota(jnp.int32, (ksz, qsz), 0)
        st = jnp.where(kpos <= q0 + lax.broadcasted_iota(jnp.int32, (ksz, qsz), 1), st, NEG_INF)
    return st


def _fold8(v, op):
    rows, n = v.shape
    v3 = v.reshape(rows // 8, 8, n)
    out = v3[0]
    for r in range(1, rows // 8):
        out = op(out, v3[r])
    return out


def _causal_tables(nbq, r, q_outer):
    a, b = [], []
    for o in range(nbq if q_outer else nbq * r):
        inner = range((o + 1) * r) if q_outer else range(o // r, nbq)
        for n in inner:
            a.append(o)
            b.append(n)
    return jnp.asarray(np.array(a, np.int32)), jnp.asarray(np.array(b, np.int32))


def _attn_tiles(T, q_tile, k_tile=ATTN_K_TILE):
    tq = _tile(T, q_tile)
    tk = _tile(tq, k_tile)
    return tq, tk, _tile(tk, ATTN_KEY_STRIP), _tile(tq, ATTN_QUERY_STRIP)


def _flash_fwd(q, k, v, qaug, kaug, H):
    T, D = q.shape
    Hp = H // 2
    W = D // Hp
    dh = W // 2
    tq, tk, ksz, qsz = _attn_tiles(T, ATTN_Q_TILE_FWD)
    r = tq // tk
    ii, jj = _causal_tables(T // tq, r, q_outer=True)
    n_steps = int(ii.shape[0])

    def body(ii_ref, jj_ref, q_ref, k_ref, v_ref, qa_ref, ka_ref, o_ref, lsea_ref, lseb_ref,
             qm_s, m_s, l_s, acc_s, p_s):
        n = pl.program_id(1)
        i, j = ii_ref[n], jj_ref[n]
        d = j - i * r
        in_a = lax.broadcasted_iota(jnp.int32, (1, W), 1) < dh
        top = lax.broadcasted_iota(jnp.int32, (W, 1), 0) < dh

        @pl.when(j == 0)
        def _():
            qv, qa = q_ref[...], qa_ref[...]
            qm_s[0] = jnp.where(in_a, qv, qa)
            qm_s[1] = jnp.where(in_a, qa, qv)
            m_s[...] = jnp.full(m_s.shape, NEG_INF, F32)
            l_s[...] = jnp.zeros_like(l_s)
            acc_s[...] = jnp.zeros_like(acc_s)

        def step(koff):
            kv, ka, vv = k_ref[...], ka_ref[...], v_ref[...]
            kms = (jnp.where(in_a, kv, ka), jnp.where(in_a, ka, kv))
            vt = vv.T
            alphas = {}

            def softmax_stage(hh, q0):
                cols = slice(q0, q0 + qsz)
                kinds = [(k0, _strip_kind(koff, k0, ksz, q0, qsz)) for k0 in range(0, tk, ksz)]
                if all(kind == "skip" for _, kind in kinds):
                    return False
                strips, part = {}, None
                for k0, kind in kinds:
                    if kind != "skip":
                        strips[k0] = _score_strip(kms[hh], qm_s[hh], koff, k0, ksz, q0, qsz, kind)
                        p8 = _fold8(strips[k0], jnp.maximum)
                        part = p8 if part is None else jnp.maximum(part, p8)
                m_prev = m_s[hh, :, cols]
                m_new = jnp.maximum(m_prev, jnp.max(part, axis=0, keepdims=True))
                l8 = None
                for k0, kind in kinds:
                    if kind == "skip":
                        p_s[hh, k0:k0 + ksz, cols] = jnp.zeros((ksz, qsz), BF16)
                        continue
                    pt = jnp.exp(strips[k0] - m_new)
                    s8 = _fold8(pt, jnp.add)
                    l8 = s8 if l8 is None else l8 + s8
                    p_s[hh, k0:k0 + ksz, cols] = pt.astype(BF16)
                alpha = jnp.exp(m_prev - m_new)
                l_s[hh, :, cols] = alpha * l_s[hh, :, cols] + jnp.sum(l8, axis=0, keepdims=True)
                m_s[hh, :, cols] = m_new
                alphas[(hh, q0)] = alpha
                return True

            def value_stage(hh, q0):
                cols, rows = slice(q0, q0 + qsz), slice(hh * dh, (hh + 1) * dh)
                acc_s[rows, cols] = acc_s[rows, cols] * alphas[(hh, q0)] + _dot(vt[rows, :], p_s[hh, :, cols])

            units = [(hh, q0) for q0 in range(0, tq, qsz) for hh in range(2)]
            pending = None
            for unit in units:
                live = softmax_stage(*unit)
                if pending is not None:
                    value_stage(*pending)
                pending = unit if live else None
            if pending is not None:
                value_stage(*pending)

        @pl.when(d < 0)
        def _():
            step(None)

        for dd in range(r):
            @pl.when(d == dd)
            def _():
                step(dd * tk)

        @pl.when(d == r - 1)
        def _():
            inv = jnp.where(top, 1.0 / l_s[0], 1.0 / l_s[1])
            o_ref[...] = (acc_s[...] * inv).T.astype(BF16)
            lsea_ref[...] = m_s[0] + jnp.log(l_s[0])
            lseb_ref[...] = m_s[1] + jnp.log(l_s[1])

    qspec = pl.BlockSpec((tq, W), lambda hp, n, ii, jj: (ii[n], hp))
    kspec = pl.BlockSpec((tk, W), lambda hp, n, ii, jj: (jj[n], hp))
    rspec = pl.BlockSpec((None, 1, tq), lambda hp, n, ii, jj: (hp, 0, ii[n]))
    grid_spec = pltpu.PrefetchScalarGridSpec(
        num_scalar_prefetch=2, grid=(Hp, n_steps),
        in_specs=[qspec, kspec, kspec, qspec, kspec],
        out_specs=[qspec, rspec, rspec],
        scratch_shapes=[pltpu.VMEM((2, tq, W), BF16), pltpu.VMEM((2, 1, tq), F32), pltpu.VMEM((2, 1, tq), F32),
                        pltpu.VMEM((W, tq), F32), pltpu.VMEM((2, tk, tq), BF16)],
    )
    return pl.pallas_call(
        body, name="flash_fwd", grid_spec=grid_spec,
        out_shape=(jax.ShapeDtypeStruct((T, D), BF16), jax.ShapeDtypeStruct((Hp, 1, T), F32),
                   jax.ShapeDtypeStruct((Hp, 1, T), F32)),
        compiler_params=_params(2),
    )(ii, jj, q, k, v, qaug, kaug)


def _attn_delta(do, o, head_sel):
    T, D = do.shape
    H = head_sel.shape[0]
    tm = _tile(T, ROW_TILE_FWD)

    def body(do_ref, o_ref, sel_ref, out_ref):
        prod = do_ref[...].astype(F32) * o_ref[...].astype(F32)
        out_ref[...] = lax.dot_general(sel_ref[...], prod, (((1,), (1,)), ((), ())),
                                       precision=lax.Precision.HIGHEST, preferred_element_type=F32)

    return pl.pallas_call(
        body, name="attn_delta", grid=(T // tm,),
        in_specs=[pl.BlockSpec((tm, D), lambda i: (i, 0)), pl.BlockSpec((tm, D), lambda i: (i, 0)),
                  pl.BlockSpec((H, D), lambda i: (0, 0))],
        out_specs=pl.BlockSpec((H, tm), lambda i: (0, i)),
        out_shape=jax.ShapeDtypeStruct((H, T), F32), compiler_params=_params())(do, o, head_sel)


def _flash_bwd(q, k, v, qaug, kaug, do, lse_a, lse_b, delta, qscale):
    T, D = q.shape
    Hp = lse_a.shape[0]
    W = D // Hp
    dh = W // 2
    tq, tk, ksz, qsz = _attn_tiles(T, ATTN_Q_TILE_BWD, 4 * ATTN_K_TILE)
    r = tq // tk
    nbq = T // tq
    jj, ii = _causal_tables(nbq, r, q_outer=False)
    n_steps = int(ii.shape[0])
    row_row = dh
    col_row = dh + N_AUG

    def body(jj_ref, ii_ref, q_ref, k_ref, v_ref, qa_ref, ka_ref, do_ref, lsea_ref, lseb_ref, dla_ref, dlb_ref,
             dqt_ref, dkt_ref, dvt_ref, dcola_ref, dcolb_ref, drowa_ref, drowb_ref,
             dqt_acc, dk_acc, dv_acc, *rot):
        n = pl.program_id(1)
        i, j = ii_ref[n], jj_ref[n]
        d = j - i * r
        in_a = lax.broadcasted_iota(jnp.int32, (1, W), 1) < dh
        top = lax.broadcasted_iota(jnp.int32, (W, 1), 0) < dh

        @pl.when(n == 0)
        def _():
            dqt_acc[...] = jnp.zeros_like(dqt_acc)

        @pl.when(d >= 0)
        def _():
            dk_acc[...] = jnp.zeros_like(dk_acc)
            dv_acc[...] = jnp.zeros_like(dv_acc)

        def step(koff):
            qv, qa, kv, ka = q_ref[...], qa_ref[...], k_ref[...], ka_ref[...]
            vv, dov = v_ref[...], do_ref[...]
            zb = jnp.zeros_like(dov)
            kms = (jnp.where(in_a, kv, ka), jnp.where(in_a, ka, kv))
            qms = (jnp.where(in_a, qv, qa), jnp.where(in_a, qa, qv))
            doms = (jnp.where(in_a, dov, zb), jnp.where(in_a, zb, dov))
            def own_first(a, hh):
                return a[:dh + AUG_ROWS] if hh == 0 else jnp.concatenate([a[dh:], a[:AUG_ROWS]], axis=0)

            kxs = tuple(own_first(kms[hh].T, hh) for hh in range(2))
            qxs = tuple(own_first(qms[hh].T, hh) for hh in range(2))
            dot = dov.T
            dxs = (dot[:dh], dot[dh:])
            lses, dls = (lsea_ref[...], lseb_ref[...]), (dla_ref[...], dlb_ref[...])
            dvs, dks = ([], []), ([], [])

            def score_stage(u, hh, q0):
                p_s, ds_s = rot[2 * (u % ATTN_ROTATE)], rot[2 * (u % ATTN_ROTATE) + 1]
                cols = slice(q0, q0 + qsz)
                kinds = [(k0, _strip_kind(koff, k0, ksz, q0, qsz)) for k0 in range(0, tk, ksz)]
                if all(kind == "skip" for _, kind in kinds):
                    return False
                for k0, kind in kinds:
                    if kind == "skip":
                        p_s[k0:k0 + ksz, :] = jnp.zeros((ksz, qsz), BF16)
                        ds_s[k0:k0 + ksz, :] = jnp.zeros((ksz, qsz), BF16)
                        continue
                    st = _score_strip(kms[hh], qms[hh], koff, k0, ksz, q0, qsz, kind)
                    pt = jnp.exp(st - lses[hh][:, cols])
                    dst = pt * (_dot_nt(vv[k0:k0 + ksz, :], doms[hh][cols, :]) - dls[hh][:, cols])
                    p_s[k0:k0 + ksz, :] = pt.astype(BF16)
                    ds_s[k0:k0 + ksz, :] = dst.astype(BF16)
                return True

            def grad_stage(u, hh, q0):
                p_s, ds_s = rot[2 * (u % ATTN_ROTATE)], rot[2 * (u % ATTN_ROTATE) + 1]
                cols = slice(q0, q0 + qsz)
                ptb, dsb = p_s[...], ds_s[...]
                dvs[hh].append(_dot_nt(dxs[hh][:, cols], ptb))
                dks[hh].append(_dot_nt(qxs[hh][:, cols], dsb))
                dqt_acc[i, hh, :, cols] += _dot(kxs[hh], dsb)

            order = [(hh, q0) for q0 in range(0, tq, qsz) for hh in range(2)]
            units = [(u, hh, q0) for u, (hh, q0) in enumerate(order)]
            pending = None
            for unit in units:
                live = score_stage(*unit)
                if pending is not None:
                    grad_stage(*pending)
                pending = unit if live else None
            if pending is not None:
                grad_stage(*pending)
            for hh in range(2):
                dv_acc[hh] += sum(dvs[hh][1:], dvs[hh][0])
                dk_acc[hh] += sum(dks[hh][1:], dks[hh][0])

        @pl.when(d < 0)
        def _():
            step(None)

        for dd in range(r):
            @pl.when(d == dd)
            def _():
                step(dd * tk)

        @pl.when(i == nbq - 1)
        def _():
            dkt_ref[...] = jnp.concatenate([dk_acc[0, :dh], dk_acc[1, :dh]], axis=0).astype(BF16)
            dvt_ref[...] = jnp.concatenate([dv_acc[0], dv_acc[1]], axis=0).astype(BF16)
            dcola_ref[...] = dk_acc[0, col_row:col_row + 1, :]
            dcolb_ref[...] = dk_acc[1, col_row:col_row + 1, :]

        @pl.when(n == n_steps - 1)
        def _():
            for b in range(nbq):
                cols = slice(b * tq, (b + 1) * tq)
                both = jnp.concatenate([dqt_acc[b, 0, :dh], dqt_acc[b, 1, :dh]], axis=0)
                dqt_ref[:, cols] = (both * qscale).astype(BF16)
                drowa_ref[:, cols] = dqt_acc[b, 0, row_row:row_row + 1, :]
                drowb_ref[:, cols] = dqt_acc[b, 1, row_row:row_row + 1, :]

    qspec = pl.BlockSpec((tq, W), lambda hp, n, jj, ii: (ii[n], hp))
    kspec = pl.BlockSpec((tk, W), lambda hp, n, jj, ii: (jj[n], hp))
    ktspec = pl.BlockSpec((W, tk), lambda hp, n, jj, ii: (hp, jj[n]))
    pair_row = pl.BlockSpec((None, 1, tq), lambda hp, n, jj, ii: (hp, 0, ii[n]))
    key_row = pl.BlockSpec((None, 1, tk), lambda hp, n, jj, ii: (hp, 0, jj[n]))
    whole_row = pl.BlockSpec((None, 1, T), lambda hp, n, jj, ii: (hp, 0, 0))
    grid_spec = pltpu.PrefetchScalarGridSpec(
        num_scalar_prefetch=2, grid=(Hp, n_steps),
        in_specs=[
            qspec, kspec, kspec, qspec, kspec, qspec, pair_row, pair_row,
            pl.BlockSpec((None, 1, tq), lambda hp, n, jj, ii: (2 * hp, 0, ii[n])),
            pl.BlockSpec((None, 1, tq), lambda hp, n, jj, ii: (2 * hp + 1, 0, ii[n])),
        ],
        out_specs=[
            pl.BlockSpec((W, T), lambda hp, n, jj, ii: (hp, 0)),
            ktspec, ktspec, key_row, key_row, whole_row, whole_row,
        ],
        scratch_shapes=[pltpu.VMEM((nbq, 2, dh + AUG_ROWS, tq), F32), pltpu.VMEM((2, dh + AUG_ROWS, tk), F32),
                        pltpu.VMEM((2, dh, tk), F32)] + [pltpu.VMEM((tk, qsz), BF16)] * (2 * ATTN_ROTATE),
    )
    return pl.pallas_call(
        body, name="flash_bwd", grid_spec=grid_spec,
        out_shape=(jax.ShapeDtypeStruct((D, T), BF16), jax.ShapeDtypeStruct((D, T), BF16),
                   jax.ShapeDtypeStruct((D, T), BF16), jax.ShapeDtypeStruct((Hp, 1, T), F32),
                   jax.ShapeDtypeStruct((Hp, 1, T), F32),
                   jax.ShapeDtypeStruct((Hp, 1, T), F32), jax.ShapeDtypeStruct((Hp, 1, T), F32)),
        compiler_params=_params(2),
    )(jj, ii, q, k, v, qaug, kaug, do, lse_a, lse_b, delta, delta)


HBM_SPEC = pl.BlockSpec(memory_space=pltpu.HBM)


def _mesh_pos():
    return lax.axis_index("x"), lax.axis_index("y"), lax.axis_index("c")


def _all_gather_shards(shards):
    n = len(shards)

    def body(*refs):
        in_refs, out_refs = refs[:n], refs[n:2 * n]
        send1, recv1, send2, recv2 = refs[2 * n:]
        x, y, c = _mesh_pos()
        me = 2 * x + y
        chips = [(1 - x, y), (x, 1 - y), (1 - x, 1 - y)]

        def half(t, chip_idx, pc):
            hr = shards[t].shape[0] // 2
            return out_refs[t].at[chip_idx, pl.ds(pc * hr, hr), :]

        first = []
        for t in range(n):
            hr = shards[t].shape[0] // 2
            for kk, (cx, cy) in enumerate(chips):
                first.append(pltpu.make_async_remote_copy(
                    src_ref=in_refs[t].at[pl.ds(c * hr, hr), :], dst_ref=half(t, me, c),
                    send_sem=send1.at[3 * t + kk], recv_sem=recv1.at[3 * t + kk],
                    device_id=(cx, cy, c), device_id_type=MESH))
        for cp in first:
            cp.start()
        passed = []
        for t in range(n):
            for kk, (cx, cy) in enumerate(chips):
                src_chip = 2 * cx + cy
                landed = half(t, src_chip, c)
                pltpu.make_async_remote_copy(
                    src_ref=landed, dst_ref=landed, send_sem=send1.at[3 * t + kk], recv_sem=recv1.at[3 * t + kk],
                    device_id=(cx, cy, c), device_id_type=MESH).wait_recv()
                fwd = pltpu.make_async_remote_copy(
                    src_ref=landed, dst_ref=landed, send_sem=send2.at[3 * t + kk], recv_sem=recv2.at[3 * t + kk],
                    device_id=(x, y, 1 - c), device_id_type=MESH)
                fwd.start()
                passed.append(fwd)
        for t in range(n):
            for kk, (cx, cy) in enumerate(chips):
                other = half(t, 2 * cx + cy, 1 - c)
                pltpu.make_async_remote_copy(
                    src_ref=other, dst_ref=other, send_sem=send2.at[3 * t + kk], recv_sem=recv2.at[3 * t + kk],
                    device_id=(x, y, 1 - c), device_id_type=MESH).wait_recv()
        for cp in first + passed:
            cp.wait_send()

    gathered = pl.pallas_call(
        body, name="weights_all_gather",
        out_shape=[jax.ShapeDtypeStruct((N_CHIPS,) + s.shape, s.dtype) for s in shards],
        in_specs=[HBM_SPEC] * n, out_specs=[HBM_SPEC] * n,
        scratch_shapes=[pltpu.SemaphoreType.DMA((3 * n,)), pltpu.SemaphoreType.DMA((3 * n,)),
                        pltpu.SemaphoreType.DMA((3 * n,)), pltpu.SemaphoreType.DMA((3 * n,))],
    )(*shards)
    own = lax.broadcasted_iota(jnp.int32, (N_CHIPS, 1, 1), 0) == 2 * lax.axis_index("x") + lax.axis_index("y")
    return [jnp.where(own, s[None], g) for s, g in zip(shards, gathered)]


def _sibling_swap_halves(gs):
    n = len(gs)

    def body(*refs):
        g_refs, r_refs = refs[:n], refs[n:2 * n]
        ssem, rsem = refs[2 * n:]
        x, y, c = _mesh_pos()
        cps = []
        for t in range(n):
            hr = gs[t].shape[1] // 2
            cps += [pltpu.make_async_remote_copy(
                src_ref=g_refs[t].at[s, pl.ds((1 - c) * hr, hr), :], dst_ref=r_refs[t].at[s],
                send_sem=ssem.at[N_CHIPS * t + s], recv_sem=rsem.at[N_CHIPS * t + s],
                device_id=(x, y, 1 - c), device_id_type=MESH) for s in range(N_CHIPS)]
        for cp in cps:
            cp.start()
        for cp in cps:
            cp.wait()

    return pl.pallas_call(
        body, name="grads_sibling_swap",
        out_shape=[jax.ShapeDtypeStruct((N_CHIPS, g.shape[1] // 2, g.shape[2]), F32) for g in gs],
        in_specs=[HBM_SPEC] * n, out_specs=[HBM_SPEC] * n,
        scratch_shapes=[pltpu.SemaphoreType.DMA((N_CHIPS * n,)), pltpu.SemaphoreType.DMA((N_CHIPS * n,))],
    )(*gs)


def _pair_add(g, r, c_idx):
    _, M, C = g.shape
    hr = M // 2
    tr = _tile(hr, PACK_ROW_TILE)
    nbk = hr // tr

    def body(c_ref, g_ref, r_ref, o_ref):
        o_ref[...] = (g_ref[...] + r_ref[...]).astype(BF16)

    grid_spec = pltpu.PrefetchScalarGridSpec(
        num_scalar_prefetch=1, grid=(N_CHIPS, nbk),
        in_specs=[pl.BlockSpec((None, tr, C), lambda s, i, c: (s, c[0] * nbk + i, 0)),
                  pl.BlockSpec((None, tr, C), lambda s, i, c: (s, i, 0))],
        out_specs=pl.BlockSpec((None, tr, C), lambda s, i, c: (s, i, 0)),
    )
    return pl.pallas_call(body, name="grads_pair_add", grid_spec=grid_spec,
                          out_shape=jax.ShapeDtypeStruct((N_CHIPS, hr, C), BF16),
                          compiler_params=_params(2))(c_idx, g, r)


def _chip_scatter(pps):
    n = len(pps)

    def body(*refs):
        p_refs, r_refs = refs[:n], refs[n:2 * n]
        ssem, rsem, lsem = refs[2 * n:]
        x, y, c = _mesh_pos()
        me = 2 * x + y
        chips = [(1 - x, y), (x, 1 - y), (1 - x, 1 - y)]
        own = [pltpu.make_async_copy(p_refs[t].at[me], r_refs[t].at[me], lsem.at[t]) for t in range(n)]
        for cp in own:
            cp.start()
        cps = [pltpu.make_async_remote_copy(
            src_ref=p_refs[t].at[2 * cx + cy], dst_ref=r_refs[t].at[me],
            send_sem=ssem.at[3 * t + kk], recv_sem=rsem.at[3 * t + kk],
            device_id=(cx, cy, c), device_id_type=MESH) for t in range(n) for kk, (cx, cy) in enumerate(chips)]
        for cp in cps:
            cp.start()
        for t in range(n):
            for kk, (cx, cy) in enumerate(chips):
                got = r_refs[t].at[2 * cx + cy]
                pltpu.make_async_remote_copy(
                    src_ref=got, dst_ref=got, send_sem=ssem.at[3 * t + kk], recv_sem=rsem.at[3 * t + kk],
                    device_id=(cx, cy, c), device_id_type=MESH).wait_recv()
        for cp in cps:
            cp.wait_send()
        for cp in own:
            cp.wait()

    return pl.pallas_call(
        body, name="grads_chip_scatter", out_shape=[jax.ShapeDtypeStruct(pp.shape, pp.dtype) for pp in pps],
        in_specs=[HBM_SPEC] * n, out_specs=[HBM_SPEC] * n,
        scratch_shapes=[pltpu.SemaphoreType.DMA((3 * n,)), pltpu.SemaphoreType.DMA((3 * n,)),
                        pltpu.SemaphoreType.DMA((n,))],
    )(*pps)


def _chip_sum(r, c_idx):
    _, hr, C = r.shape
    tr = _tile(hr, PACK_ROW_TILE)
    nbk = hr // tr

    def body(c_ref, r_ref, o_ref):
        r0, r1, r2, r3 = (r_ref[s].astype(F32) for s in range(N_CHIPS))
        o_ref[...] = ((r0 + r1) + r2) + r3

    grid_spec = pltpu.PrefetchScalarGridSpec(
        num_scalar_prefetch=1, grid=(nbk,),
        in_specs=[pl.BlockSpec((N_CHIPS, tr, C), lambda i, c: (0, i, 0))],
        out_specs=pl.BlockSpec((tr, C), lambda i, c: (c[0] * nbk + i, 0)),
    )
    return pl.pallas_call(body, name="grads_chip_sum", grid_spec=grid_spec,
                          out_shape=jax.ShapeDtypeStruct((2 * hr, C), F32), compiler_params=_params())(c_idx, r)


def _sibling_join(bufs):
    n = len(bufs)

    def body(*refs):
        o_refs = refs[n:2 * n]
        ssem, rsem = refs[2 * n:]
        x, y, c = _mesh_pos()
        cps = []
        for t in range(n):
            hr = bufs[t].shape[0] // 2
            mine = o_refs[t].at[pl.ds(c * hr, hr), :]
            cps.append(pltpu.make_async_remote_copy(src_ref=mine, dst_ref=mine, send_sem=ssem.at[t], recv_sem=rsem.at[t],
                                                    device_id=(x, y, 1 - c), device_id_type=MESH))
        for cp in cps:
            cp.start()
        for t in range(n):
            hr = bufs[t].shape[0] // 2
            theirs = o_refs[t].at[pl.ds((1 - c) * hr, hr), :]
            pltpu.make_async_remote_copy(src_ref=theirs, dst_ref=theirs, send_sem=ssem.at[t], recv_sem=rsem.at[t],
                                         device_id=(x, y, 1 - c), device_id_type=MESH).wait_recv()
        for cp in cps:
            cp.wait_send()

    return pl.pallas_call(
        body, name="grads_sibling_join", out_shape=[jax.ShapeDtypeStruct(b.shape, F32) for b in bufs],
        in_specs=[HBM_SPEC] * n, out_specs=[HBM_SPEC] * n, input_output_aliases={t: t for t in range(n)},
        scratch_shapes=[pltpu.SemaphoreType.DMA((n,)), pltpu.SemaphoreType.DMA((n,))],
    )(*bufs)


def _adamw(w, g, m, v):
    M, C = w.shape
    tr = _tile(M, PACK_ROW_TILE)

    def body(w_ref, g_ref, m_ref, v_ref, d_ref, mo_ref, vo_ref):
        gv = g_ref[...]
        mn = ADAM_B1 * m_ref[...] + (1.0 - ADAM_B1) * gv
        vn = ADAM_B2 * v_ref[...] + (1.0 - ADAM_B2) * (gv * gv)
        m_hat = mn / (1.0 - ADAM_B1 ** ADAM_STEP)
        v_hat = vn / (1.0 - ADAM_B2 ** ADAM_STEP)
        d_ref[...] = -ADAM_LR * (m_hat / (jnp.sqrt(v_hat) + ADAM_EPS) + ADAM_WD * w_ref[...])
        mo_ref[...] = mn
        vo_ref[...] = vn

    spec = pl.BlockSpec((tr, C), lambda i: (i, 0))
    return pl.pallas_call(
        body, name="adamw", grid=(M // tr,), in_specs=[spec] * 4, out_specs=[spec] * 3,
        out_shape=[jax.ShapeDtypeStruct((M, C), F32)] * 3, compiler_params=_params())(w, g, m, v)


WEIGHT_NAMES = ("norm_g", "w_attn_in", "b_forget", "w_attn_out", "w_conv_in", "conv_w", "w_conv_out",
                "w_mlp_up", "w_mlp_down", "w_ple_proj", "w_ple_gate")
COL_SHARDED = ("norm_g", "w_attn_in", "w_conv_in", "conv_w", "w_mlp_up", "w_ple_proj")
ROW_SHARDED = ("w_attn_out", "w_conv_out", "w_mlp_down", "w_ple_gate")


def _unshard(name, gathered, shard_shape):
    a = gathered.reshape((N_CHIPS,) + tuple(shard_shape))
    if name in COL_SHARDED:
        a = jnp.moveaxis(a, 0, -2)
        return a.reshape(a.shape[:-2] + (N_CHIPS * shard_shape[-1],))
    a = jnp.moveaxis(a, 0, 1)
    return a.reshape((shard_shape[0], N_CHIPS * shard_shape[1], shard_shape[2]))


def _to_shard_major(name, full):
    if name == "b_forget":
        return jnp.broadcast_to(full.reshape(1, -1), (N_CHIPS, full.size))
    if name in COL_SHARDED:
        a = full.reshape(full.shape[:-1] + (N_CHIPS, full.shape[-1] // N_CHIPS))
        a = jnp.moveaxis(a, -2, 0)
    else:
        a = full.reshape((full.shape[0], N_CHIPS, full.shape[1] // N_CHIPS, full.shape[2]))
        a = jnp.moveaxis(a, 1, 0)
    return a.reshape(N_CHIPS, -1)


def _reduce_scatter(tensors, c_idx):
    swapped = _sibling_swap_halves(tensors)
    pairs = [_pair_add(g, r, c_idx) for g, r in zip(tensors, swapped)]
    return _sibling_join([_chip_sum(r, c_idx) for r in _chip_scatter(pairs)])


def kernel(x, p, norm_g, w_attn_in, b_forget, w_attn_out, w_conv_in, conv_w, w_conv_out, w_mlp_up, w_mlp_down, w_ple_proj, w_ple_gate, loss_target, m_norm_g, m_w_attn_in, m_b_forget, m_w_attn_out, m_w_conv_in, m_conv_w, m_w_conv_out, m_w_mlp_up, m_w_mlp_down, m_w_ple_proj, m_w_ple_gate, v_norm_g, v_w_attn_in, v_b_forget, v_w_attn_out, v_w_conv_in, v_conv_w, v_w_conv_out, v_w_mlp_up, v_w_mlp_down, v_w_ple_proj, v_w_ple_gate):
    w_local = dict(norm_g=norm_g, w_attn_in=w_attn_in, b_forget=b_forget, w_attn_out=w_attn_out,
                   w_conv_in=w_conv_in, conv_w=conv_w, w_conv_out=w_conv_out, w_mlp_up=w_mlp_up,
                   w_mlp_down=w_mlp_down, w_ple_proj=w_ple_proj, w_ple_gate=w_ple_gate)
    m_local = dict(norm_g=m_norm_g, w_attn_in=m_w_attn_in, b_forget=m_b_forget, w_attn_out=m_w_attn_out,
                   w_conv_in=m_w_conv_in, conv_w=m_conv_w, w_conv_out=m_w_conv_out, w_mlp_up=m_w_mlp_up,
                   w_mlp_down=m_w_mlp_down, w_ple_proj=m_w_ple_proj, w_ple_gate=m_w_ple_gate)
    v_local = dict(norm_g=v_norm_g, w_attn_in=v_w_attn_in, b_forget=v_b_forget, w_attn_out=v_w_attn_out,
                   w_conv_in=v_w_conv_in, conv_w=v_conv_w, w_conv_out=v_w_conv_out, w_mlp_up=v_w_mlp_up,
                   w_mlp_down=v_w_mlp_down, w_ple_proj=v_w_ple_proj, w_ple_gate=v_w_ple_gate)
    shard_shapes = {k: tuple(a.shape) for k, a in w_local.items()}

    xs = x[0]
    target = loss_target[0]
    T, D = xs.shape
    depth = p.shape[0]
    H = b_forget.shape[1]
    qscale = float(D // H) ** -0.5
    head_sel = (jnp.arange(D)[None, :] // (D // H) == jnp.arange(H)[:, None]).astype(F32)

    big = [n for n in WEIGHT_NAMES if n not in ("norm_g", "conv_w", "b_forget")]
    small = jnp.concatenate([norm_g.reshape(-1), conv_w.reshape(-1)])
    n_small = small.shape[0]
    small_rows = -(-n_small // (LANES * 16)) * 16
    small = jnp.pad(small, (0, small_rows * LANES - n_small)).reshape(small_rows, LANES)
    shards = [w_local[n].astype(BF16).reshape(-1, shard_shapes[n][-1]) for n in big] + [small]
    gathered = _all_gather_shards(shards)
    full = {n: _unshard(n, g, shard_shapes[n]) for n, g in zip(big, gathered[:-1])}
    gs = gathered[-1].reshape(N_CHIPS, -1)
    full["norm_g"] = _unshard("norm_g", gs[:, :norm_g.size], shard_shapes["norm_g"])
    full["conv_w"] = _unshard("conv_w", gs[:, norm_g.size:n_small], shard_shapes["conv_w"])
    gains = full["norm_g"]

    def gain(i, k):
        return gains[i, k].reshape(1, D)

    def taps(j):
        return jnp.pad(full["conv_w"][j], ((0, 5), (0, 0)))

    saved = []
    h = xs
    for i in range(depth):
        j = i // 2
        s = {"x0": h}
        if i % 2 == 0:
            w_in = full["w_attn_in"][j]
            wqkv = w_in[:, :3 * D]
            wf = jnp.pad(w_in[:, 3 * D:], ((0, 0), (0, LANES - H)))
            bf = jnp.pad(b_forget[j].reshape(1, H), ((0, 0), (0, LANES - H)))
            q, k, v, fl, lf = _attn_in_fwd(h, gain(i, 0), wqkv, wf, bf, qscale)
            qaug, kaug = _aug_operands(_gate_cumsum(lf[:, :H].T), H, D)
            o, lse_a, lse_b = _flash_fwd(q, k, v, qaug, kaug, H)
            s.update(q=q, k=k, v=v, fl=fl, qaug=qaug, kaug=kaug, o=o, lse_a=lse_a, lse_b=lse_b, wqkv=wqkv, wf=wf)
            mix_in, w_out = o, full["w_attn_out"][j]
        else:
            b, c, u, zc, y = _conv_in_fwd(h, gain(i, 0), full["w_conv_in"][j], taps(j))
            s.update(b=b, c=c, u=u, zc=zc, y=y)
            mix_in, w_out = y, full["w_conv_out"][j]
        m1, x1 = _out_proj_fwd("mixer_out_fwd", mix_in, w_out, gain(i, 1), h)
        up, a = _mlp_up_fwd(x1, gain(i, 2), full["w_mlp_up"][i])
        m3, x2 = _out_proj_fwd("mlp_down_fwd", a, full["w_mlp_down"][i], gain(i, 3), x1)
        gl, pe, x3 = _ple_fwd(x2, p[i, 0], gain(i, 4), gain(i, 5), full["w_ple_gate"][i], full["w_ple_proj"][i])
        s.update(m1=m1, x1=x1, up=up, a=a, m3=m3, x2=x2, gl=gl, pe=pe, w_out=w_out)
        saved.append(s)
        h = x3

    dh, loss_blk = _loss_fwd_bwd(h, target)
    loss = lax.psum(loss_blk[0, 0], ("x", "y", "c"))

    g_gain = [[None] * 6 for _ in range(depth)]
    grads = {n: [None] * w_local[n].shape[0] for n in WEIGHT_NAMES if n != "norm_g"}
    for i in reversed(range(depth)):
        j = i // 2
        s = saved[i]
        dx2, dwp, dwg, g_gain[i][4], g_gain[i][5] = _ple_bwd(
            dh, s["x2"], p[i, 0], s["gl"], s["pe"], gain(i, 4), gain(i, 5), full["w_ple_gate"][i].T)
        grads["w_ple_proj"][i], grads["w_ple_gate"][i] = dwp, dwg
        dup, dwd, g_gain[i][3] = _out_proj_bwd(
            "mlp_down_bwd", dx2, s["m3"], gain(i, 3), [s["a"], s["up"]], full["w_mlp_down"][i].T, "relu2")
        grads["w_mlp_down"][i] = dwd
        dx1, dwu, g_gain[i][2] = _in_proj_bwd(
            "mlp_up_bwd", [dup], [full["w_mlp_up"][i].T], s["x1"], gain(i, 2), dx2)
        grads["w_mlp_up"][i] = dwu
        if i % 2 == 0:
            do, dwo, g_gain[i][1] = _out_proj_bwd(
                "attn_out_bwd", dx1, s["m1"], gain(i, 1), [s["o"]], s["w_out"].T, "plain")
            grads["w_attn_out"][j] = dwo
            delta = _attn_delta(do, s["o"], head_sel).reshape(H, 1, T)
            dqt, dkt, dvt, dca, dcb, dra, drb = _flash_bwd(s["q"], s["k"], s["v"], s["qaug"], s["kaug"], do,
                                                           s["lse_a"], s["lse_b"], delta, qscale)
            dcol = jnp.concatenate([dca, dcb], axis=1).reshape(H, T)
            drow = jnp.concatenate([dra, drb], axis=1).reshape(H, T)
            dfl_t, dbf = _gate_bwd(drow, dcol, s["fl"][:, :H].T)
            grads["b_forget"][j] = dbf.reshape(H)
            dfl = jnp.pad(dfl_t.T, ((0, 0), (0, LANES - H))).astype(BF16)
            wqkv_t = s["wqkv"].T
            dh, dwq, dwk, dwv, dwf, g_gain[i][0] = _in_proj_bwd(
                "attn_in_bwd", [dqt.T, dkt.T, dvt.T, dfl], [wqkv_t[:D], wqkv_t[D:2 * D], wqkv_t[2 * D:], s["wf"].T],
                s["x0"], gain(i, 0), dx1)
            grads["w_attn_in"][j] = jnp.concatenate([dwq, dwk, dwv, dwf[:, :H]], axis=1)
        else:
            dy, dwo, g_gain[i][1] = _out_proj_bwd(
                "conv_out_bwd", dx1, s["m1"], gain(i, 1), [s["y"]], s["w_out"].T, "plain_f32")
            grads["w_conv_out"][j] = dwo
            db, dc, du, dcw = _conv_bwd(dy, s["b"], s["c"], s["u"], s["zc"], taps(j))
            grads["conv_w"][j] = dcw[:conv_w.shape[1]]
            w_t = full["w_conv_in"][j].T
            dh, dwb, dwc, dwu2, g_gain[i][0] = _in_proj_bwd(
                "conv_in_bwd", [db, dc, du], [w_t[:D], w_t[D:2 * D], w_t[2 * D:]], s["x0"], gain(i, 0), dx1)
            grads["w_conv_in"][j] = jnp.concatenate([dwb, dwc, dwu2], axis=1)
    grad_x = dh.reshape(x.shape)

    grad_full = {n: jnp.stack(grads[n]) for n in grads}
    grad_full["norm_g"] = jnp.stack([jnp.concatenate(row, axis=0) for row in g_gain])

    c_idx = lax.axis_index("c").astype(jnp.int32).reshape(1)
    tiny = ("norm_g", "conv_w", "b_forget")
    tiny_flat = jnp.concatenate([_to_shard_major(n, grad_full[n]) for n in tiny], axis=1)
    n_tiny = tiny_flat.shape[1]
    tiny_rows = -(-n_tiny // (LANES * 32)) * 32
    tiny_t = jnp.pad(tiny_flat, ((0, 0), (0, tiny_rows * LANES - n_tiny))).reshape(N_CHIPS, tiny_rows, LANES)
    tensors = [_to_shard_major(n, grad_full[n]).reshape(N_CHIPS, -1, shard_shapes[n][-1]) for n in big]
    reduced = _reduce_scatter(tensors + [tiny_t], c_idx)
    g_out = {n: r.reshape(shard_shapes[n]) for n, r in zip(big, reduced[:-1])}
    tiny_red, off = reduced[-1].reshape(-1), 0
    for n in tiny:
        size = int(np.prod(shard_shapes[n]))
        g_out[n] = tiny_red[off:off + size].reshape(shard_shapes[n])
        off += size
    d_out, m_out, v_out = {}, {}, {}
    for n in WEIGHT_NAMES:
        shp = shard_shapes[n]
        two_d = (-1, shp[-1])
        d, mn, vn = _adamw(w_local[n].reshape(two_d), g_out[n].reshape(two_d),
                           m_local[n].reshape(two_d), v_local[n].reshape(two_d))
        d_out[n], m_out[n], v_out[n] = d.reshape(shp), mn.reshape(shp), vn.reshape(shp)
    return (loss, grad_x, *[g_out[n] for n in WEIGHT_NAMES], *[d_out[n] for n in WEIGHT_NAMES],
            *[m_out[n] for n in WEIGHT_NAMES], *[v_out[n] for n in WEIGHT_NAMES])
```

```python
import numpy as np
import jax
import jax.numpy as jnp
from jax import lax
from jax.experimental import pallas as pl
from jax.experimental.pallas import tpu as pltpu

F32 = jnp.float32
BF16 = jnp.bfloat16
MESH = pl.DeviceIdType.MESH

RMS_EPS = 1e-6
NEG_INF = -1e30
ADAM_LR = 0.001
ADAM_B1 = 0.9
ADAM_B2 = 0.999
ADAM_EPS = 1e-08
ADAM_WD = 0.01
ADAM_STEP = 10

N_CHIPS = 4
LANES = 128
ROW_TILE_FWD = 512
ROW_TILE_BWD = 512
ROW_TILE_BWD_FF = 256
ATTN_Q_TILE_FWD = 4096
ATTN_Q_TILE_BWD = 2048
ATTN_K_TILE = 512
ATTN_KEY_STRIP = 128
ATTN_QUERY_STRIP = 256
ATTN_ROTATE = 4
COL_CHUNK = 512
PACK_ROW_TILE = 256
AUG_ROWS = 16
N_AUG = 3
VMEM_LIMIT = 56 * 1024 * 1024


def _tile(n, pref):
    return pref if n % pref == 0 else n


def _dot(a, b):
    return jnp.dot(a, b, preferred_element_type=F32)


def _dot_tn(a, b):
    return lax.dot_general(a, b, (((0,), (0,)), ((), ())), preferred_element_type=F32)


def _dot_nt(a, b):
    return lax.dot_general(a, b, (((1,), (1,)), ((), ())), preferred_element_type=F32)


def _rms_fwd(x, g):
    r = lax.rsqrt(jnp.mean(x * x, axis=-1, keepdims=True) + RMS_EPS)
    return (x * r) * g


def _rms_bwd(x, g, dy):
    r = lax.rsqrt(jnp.mean(x * x, axis=-1, keepdims=True) + RMS_EPS)
    xh = x * r
    dyg = dy * g
    dx = r * (dyg - xh * jnp.mean(dyg * xh, axis=-1, keepdims=True))
    return dx, jnp.sum(dy * xh, axis=0, keepdims=True)


def _params(n_axes=1):
    return pltpu.CompilerParams(dimension_semantics=("arbitrary",) * n_axes, vmem_limit_bytes=VMEM_LIMIT)


def _row_call(name, body, n_rows, tm, row_ins, const_ins, row_outs, acc_outs=(), scratch=(), reverse=False):
    nb = n_rows // tm
    rmap = (lambda i: (nb - 1 - i, 0)) if reverse else (lambda i: (i, 0))

    def whole(shape):
        nd = len(shape)
        return pl.BlockSpec(tuple(shape), lambda i: (0,) * nd)

    in_specs = [pl.BlockSpec((tm, a.shape[1]), rmap) for a in row_ins] + [whole(a.shape) for a in const_ins]
    out_shape = [jax.ShapeDtypeStruct((n_rows, w), dt) for (w, dt) in row_outs]
    out_shape += [jax.ShapeDtypeStruct(tuple(s), dt) for (s, dt) in acc_outs]
    out_specs = [pl.BlockSpec((tm, w), rmap) for (w, _) in row_outs] + [whole(s) for (s, _) in acc_outs]
    return pl.pallas_call(
        body, name=name, grid=(nb,), in_specs=in_specs, out_specs=out_specs, out_shape=out_shape,
        scratch_shapes=list(scratch), compiler_params=_params(),
    )(*row_ins, *const_ins)


def _attn_in_fwd(x, g, wqkv, wf, bf, qscale):
    T, D = x.shape
    tm, ch = _tile(T, ROW_TILE_FWD), _tile(D, COL_CHUNK)

    def body(x_ref, g_ref, w_ref, wf_ref, bf_ref, q_ref, k_ref, v_ref, fl_ref, lf_ref):
        h = _rms_fwd(x_ref[...], g_ref[...]).astype(BF16)
        for part, o_ref in enumerate((q_ref, k_ref, v_ref)):
            for n0 in range(0, D, ch):
                r = _dot(h, w_ref[:, part * D + n0:part * D + n0 + ch])
                if part == 0:
                    r = r * qscale
                o_ref[:, n0:n0 + ch] = r.astype(BF16)
        fl = _dot(h, wf_ref[...]) + bf_ref[...]
        fl_ref[...] = fl
        lf_ref[...] = jnp.minimum(fl, 0.0) - jnp.log1p(jnp.exp(-jnp.abs(fl)))

    return _row_call("attn_in_fwd", body, T, tm, [x], [g, wqkv, wf, bf],
                     [(D, BF16), (D, BF16), (D, BF16), (LANES, F32), (LANES, F32)])


def _conv_in_fwd(x, g, w, cw):
    T, D = x.shape
    tm, ch = _tile(T, ROW_TILE_FWD), _tile(D, COL_CHUNK)

    def body(x_ref, g_ref, w_ref, cw_ref, b_ref, c_ref, u_ref, zc_ref, y_ref, tail_ref):
        i = pl.program_id(0)

        @pl.when(i == 0)
        def _():
            tail_ref[...] = jnp.zeros_like(tail_ref)

        h = _rms_fwd(x_ref[...], g_ref[...]).astype(BF16)
        for part, o_ref in enumerate((b_ref, c_ref, u_ref)):
            for n0 in range(0, D, ch):
                o_ref[:, n0:n0 + ch] = _dot(h, w_ref[:, part * D + n0:part * D + n0 + ch])
        z = c_ref[...] * u_ref[...]
        row = lax.broadcasted_iota(jnp.int32, (tm, 1), 0)
        t6, t7 = tail_ref[6:7, :], tail_ref[7:8, :]
        z1 = jnp.where(row == 0, t7, pltpu.roll(z, 1, axis=0))
        z2 = jnp.where(row == 0, t6, jnp.where(row == 1, t7, pltpu.roll(z, 2, axis=0)))
        zc = cw_ref[0:1, :] * z2 + cw_ref[1:2, :] * z1 + cw_ref[2:3, :] * z
        zc_ref[...] = zc
        y_ref[...] = (b_ref[...] * zc).astype(BF16)
        tail_ref[...] = z[tm - 8:tm, :]

    return _row_call("conv_in_fwd", body, T, tm, [x], [g, w, cw],
                     [(D, F32), (D, F32), (D, F32), (D, F32), (D, BF16)], scratch=[pltpu.VMEM((8, D), F32)])


def _mlp_up_fwd(x, g, w):
    T, D = x.shape
    F = w.shape[1]
    tm, ch = _tile(T, ROW_TILE_FWD), _tile(F, COL_CHUNK)

    def body(x_ref, g_ref, w_ref, up_ref, a_ref):
        h = _rms_fwd(x_ref[...], g_ref[...]).astype(BF16)
        for n0 in range(0, F, ch):
            r = _dot(h, w_ref[:, n0:n0 + ch])
            up_ref[:, n0:n0 + ch] = r.astype(BF16)
            rl = jnp.maximum(r, 0.0)
            a_ref[:, n0:n0 + ch] = (rl * rl).astype(BF16)

    return _row_call("mlp_up_fwd", body, T, tm, [x], [g, w], [(F, BF16), (F, BF16)])


def _out_proj_fwd(name, a, w, g, x):
    T, D = x.shape
    tm = _tile(T, ROW_TILE_FWD)

    def body(a_ref, x_ref, w_ref, g_ref, m_ref, xn_ref):
        m = _dot(a_ref[...], w_ref[...])
        m_ref[...] = m
        xn_ref[...] = x_ref[...] + _rms_fwd(m, g_ref[...])

    return _row_call(name, body, T, tm, [a, x], [w, g], [(D, F32), (D, F32)])


def _ple_fwd(x, p, g4, g5, wg, wp):
    T, D = x.shape
    tm = _tile(T, ROW_TILE_FWD)

    def body(x_ref, p_ref, g4_ref, g5_ref, wg_ref, wp_ref, gl_ref, pe_ref, xn_ref):
        xv = x_ref[...]
        gl = _dot(_rms_fwd(xv, g4_ref[...]).astype(BF16), wg_ref[...])
        pe = _dot(p_ref[...].astype(BF16), wp_ref[...])
        gl_ref[...] = gl
        pe_ref[...] = pe
        e = pe * (1.0 / (1.0 + jnp.exp(-gl)))
        xn_ref[...] = xv + _rms_fwd(e, g5_ref[...])

    return _row_call("ple_fwd", body, T, tm, [x, p], [g4, g5, wg, wp], [(D, F32), (D, F32), (D, F32)])


def _loss_fwd_bwd(y, target):
    T, D = y.shape
    tm = _tile(T, ROW_TILE_FWD)

    def body(y_ref, t_ref, dy_ref, loss_ref):
        @pl.when(pl.program_id(0) == 0)
        def _():
            loss_ref[...] = jnp.zeros_like(loss_ref)

        err = y_ref[...] - t_ref[...]
        dy_ref[...] = err * (1.0 / D)
        part = 0.5 * jnp.sum(jnp.mean(err * err, axis=-1, keepdims=True), axis=0, keepdims=True)
        loss_ref[...] += jnp.broadcast_to(part, loss_ref.shape)

    return _row_call("loss", body, T, tm, [y, target], [], [(D, F32)], acc_outs=[((8, LANES), F32)])


def _out_proj_bwd(name, dres, m, g, a_ins, wt, mode):
    T, D = dres.shape
    Ka = wt.shape[1]
    tm, ch = _tile(T, ROW_TILE_BWD_FF if mode == "relu2" else ROW_TILE_BWD), _tile(Ka, COL_CHUNK)
    out_dt = F32 if mode == "plain_f32" else BF16

    def body(dres_ref, m_ref, *rest):
        a_ref = rest[0]
        up_ref = rest[1] if mode == "relu2" else None
        k = len(a_ins)
        g_ref, wt_ref, da_ref, dw_ref, dg_ref = rest[k:k + 5]

        @pl.when(pl.program_id(0) == 0)
        def _():
            dw_ref[...] = jnp.zeros_like(dw_ref)
            dg_ref[...] = jnp.zeros_like(dg_ref)

        dm, dgp = _rms_bwd(m_ref[...], g_ref[...], dres_ref[...])
        dg_ref[...] += dgp
        dmb = dm.astype(BF16)
        for n0 in range(0, Ka, ch):
            da = _dot(dmb, wt_ref[:, n0:n0 + ch])
            if mode == "relu2":
                da = da * (2.0 * jnp.maximum(up_ref[:, n0:n0 + ch].astype(F32), 0.0))
            da_ref[:, n0:n0 + ch] = da.astype(out_dt)
            dw_ref[n0:n0 + ch, :] += _dot_tn(a_ref[:, n0:n0 + ch], dmb)

    return _row_call(name, body, T, tm, [dres, m] + list(a_ins), [g, wt],
                     [(Ka, out_dt)], acc_outs=[((Ka, D), F32), ((1, D), F32)])


def _in_proj_bwd(name, pieces, wts, x, g, dres):
    T, D = x.shape
    tm = _tile(T, ROW_TILE_BWD)
    k = len(pieces)
    widths = [pc.shape[1] for pc in pieces]

    def body(*refs):
        pc_refs = refs[:k]
        x_ref, dres_ref, g_ref = refs[k:k + 3]
        wt_refs = refs[k + 3:2 * k + 3]
        dx_ref = refs[2 * k + 3]
        dw_refs = refs[2 * k + 4:3 * k + 4]
        dg_ref = refs[3 * k + 4]

        @pl.when(pl.program_id(0) == 0)
        def _():
            for r in dw_refs:
                r[...] = jnp.zeros_like(r)
            dg_ref[...] = jnp.zeros_like(dg_ref)

        xv, gv = x_ref[...], g_ref[...]
        hb = _rms_fwd(xv, gv).astype(BF16)
        dh = None
        for pc_ref, wt_ref, dw_ref, n in zip(pc_refs, wt_refs, dw_refs, widths):
            d = _dot(pc_ref[...], wt_ref[...])
            dh = d if dh is None else dh + d
            ch = _tile(n, COL_CHUNK)
            for n0 in range(0, n, ch):
                dw_ref[:, n0:n0 + ch] += _dot_tn(hb, pc_ref[:, n0:n0 + ch])
        dxn, dgp = _rms_bwd(xv, gv, dh)
        dg_ref[...] += dgp
        dx_ref[...] = dres_ref[...] + dxn

    return _row_call(name, body, T, tm, list(pieces) + [x, dres], [g] + list(wts), [(D, F32)],
                     acc_outs=[((D, n), F32) for n in widths] + [((1, D), F32)])


def _conv_bwd(dy, b, c, u, zc, cw):
    T, D = dy.shape
    tm = _tile(T, ROW_TILE_BWD)

    def body(dy_ref, b_ref, c_ref, u_ref, zc_ref, cw_ref, db_ref, dc_ref, du_ref, dcw_ref, head_ref):
        @pl.when(pl.program_id(0) == 0)
        def _():
            head_ref[...] = jnp.zeros_like(head_ref)
            dcw_ref[...] = jnp.zeros_like(dcw_ref)

        dyv, cv, uv = dy_ref[...], c_ref[...], u_ref[...]
        db_ref[...] = (dyv * zc_ref[...]).astype(BF16)
        dzc = dyv * b_ref[...]
        row = lax.broadcasted_iota(jnp.int32, (tm, 1), 0)
        h0, h1 = head_ref[0:1, :], head_ref[1:2, :]
        d1 = jnp.where(row == tm - 1, h0, pltpu.roll(dzc, tm - 1, axis=0))
        d2 = jnp.where(row == tm - 1, h1, jnp.where(row == tm - 2, h0, pltpu.roll(dzc, tm - 2, axis=0)))
        dz = cw_ref[2:3, :] * dzc + cw_ref[1:2, :] * d1 + cw_ref[0:1, :] * d2
        dc_ref[...] = (dz * uv).astype(BF16)
        du_ref[...] = (dz * cv).astype(BF16)
        z = cv * uv
        dcw_ref[0:1, :] += jnp.sum(d2 * z, axis=0, keepdims=True)
        dcw_ref[1:2, :] += jnp.sum(d1 * z, axis=0, keepdims=True)
        dcw_ref[2:3, :] += jnp.sum(dzc * z, axis=0, keepdims=True)
        head_ref[...] = dzc[0:8, :]

    return _row_call("conv_bwd", body, T, tm, [dy, b, c, u, zc], [cw], [(D, BF16), (D, BF16), (D, BF16)],
                     acc_outs=[((8, D), F32)], scratch=[pltpu.VMEM((8, D), F32)], reverse=True)


def _ple_bwd(dres, x, p, gl, pe, g4, g5, wgt):
    T, D = x.shape
    P = p.shape[1]
    tm = _tile(T, ROW_TILE_BWD)

    def body(dres_ref, x_ref, p_ref, gl_ref, pe_ref, g4_ref, g5_ref, wgt_ref,
             dx_ref, dwp_ref, dwg_ref, dg4_ref, dg5_ref):
        @pl.when(pl.program_id(0) == 0)
        def _():
            for r in (dwp_ref, dwg_ref, dg4_ref, dg5_ref):
                r[...] = jnp.zeros_like(r)

        dr, xv, pe_v = dres_ref[...], x_ref[...], pe_ref[...]
        gate = 1.0 / (1.0 + jnp.exp(-gl_ref[...]))
        de, dg5p = _rms_bwd(pe_v * gate, g5_ref[...], dr)
        dg5_ref[...] += dg5p
        dpe = (de * gate).astype(BF16)
        dgl = (de * pe_v * gate * (1.0 - gate)).astype(BF16)
        dwp_ref[...] += _dot_tn(p_ref[...].astype(BF16), dpe)
        g4v = g4_ref[...]
        dwg_ref[...] += _dot_tn(_rms_fwd(xv, g4v).astype(BF16), dgl)
        dxn, dg4p = _rms_bwd(xv, g4v, _dot(dgl, wgt_ref[...]))
        dg4_ref[...] += dg4p
        dx_ref[...] = dr + dxn

    return _row_call("ple_bwd", body, T, tm, [dres, x, p, gl, pe], [g4, g5, wgt], [(D, F32)],
                     acc_outs=[((P, D), F32), ((D, D), F32), ((1, D), F32), ((1, D), F32)])


def _scan_lanes(v, reverse):
    n = v.shape[1]
    lane = lax.broadcasted_iota(jnp.int32, v.shape, 1)
    s = 1
    while s < n:
        if reverse:
            v = v + jnp.where(lane < n - s, pltpu.roll(v, n - s, axis=1), 0.0)
        else:
            v = v + jnp.where(lane >= s, pltpu.roll(v, s, axis=1), 0.0)
        s *= 2
    return v


def _gate_cumsum(lf_t):
    def body(lf_ref, *piece_refs):
        rest = _scan_lanes(lf_ref[...], reverse=False)
        for r in piece_refs:
            piece = rest.astype(BF16)
            r[...] = piece
            rest = rest - piece.astype(F32)

    return pl.pallas_call(body, name="gate_cumsum", out_shape=[jax.ShapeDtypeStruct(lf_t.shape, BF16)] * N_AUG,
                          compiler_params=pltpu.CompilerParams(vmem_limit_bytes=VMEM_LIMIT))(lf_t)


def _gate_bwd(drow_t, dcol_t, fl_t):
    H = fl_t.shape[0]

    def body(dr_ref, dc_ref, fl_ref, dfl_ref, dbf_ref):
        dlf = _scan_lanes(dr_ref[...] - dc_ref[...], reverse=True)
        dfl = dlf * (1.0 / (1.0 + jnp.exp(fl_ref[...])))
        dfl_ref[...] = dfl
        dbf_ref[...] = jnp.sum(dfl, axis=1, keepdims=True)

    return pl.pallas_call(
        body, name="gate_bwd",
        out_shape=(jax.ShapeDtypeStruct(fl_t.shape, F32), jax.ShapeDtypeStruct((H, 1), F32)),
        compiler_params=pltpu.CompilerParams(vmem_limit_bytes=VMEM_LIMIT))(drow_t, dcol_t, fl_t)


def _aug_operands(pieces, H, D):
    T = pieces[0].shape[1]
    dh = D // H
    one = jnp.ones((H, T), BF16)
    qa = jnp.stack(list(pieces) + [one] * N_AUG, axis=-1)
    ka = jnp.stack([one] * N_AUG + [-pc for pc in pieces], axis=-1)

    def place(a):
        a = jnp.pad(a, ((0, 0), (0, 0), (0, dh - 2 * N_AUG))).reshape(H // 2, 2, T, dh)
        return jnp.transpose(a[:, ::-1], (2, 0, 1, 3)).reshape(T, D)

    return place(qa), place(ka)


def _strip_kind(koff, k0, ksz, q0, qsz):
    if koff is None or koff + k0 + ksz - 1 <= q0:
        return "full"
    return "skip" if koff + k0 > q0 + qsz - 1 else "partial"


def _score_strip(km, qm, koff, k0, ksz, q0, qsz, kind):
    st = _dot_nt(km[k0:k0 + ksz, :], qm[q0:q0 + qsz, :])
    if kind == "partial":
        kpos = koff + k0 + lax.broadcasted_iota(jnp.int32, (ksz, qsz), 0)
        st = jnp.where(kpos <= q0 + lax.broadcasted_iota(jnp.int32, (ksz, qsz), 1), st, NEG_INF)
    return st


def _fold8(v, op):
    rows, n = v.shape
    v3 = v.reshape(rows // 8, 8, n)
    out = v3[0]
    for r in range(1, rows // 8):
        out = op(out, v3[r])
    return out


def _causal_tables(nbq, r, q_outer):
    a, b = [], []
    for o in range(nbq if q_outer else nbq * r):
        inner = range((o + 1) * r) if q_outer else range(o // r, nbq)
        for n in inner:
            a.append(o)
            b.append(n)
    return jnp.asarray(np.array(a, np.int32)), jnp.asarray(np.array(b, np.int32))


def _attn_tiles(T, q_tile, k_tile=ATTN_K_TILE):
    tq = _tile(T, q_tile)
    tk = _tile(tq, k_tile)
    return tq, tk, _tile(tk, ATTN_KEY_STRIP), _tile(tq, ATTN_QUERY_STRIP)


def _flash_fwd(q, k, v, qaug, kaug, H):
    T, D = q.shape
    Hp = H // 2
    W = D // Hp
    dh = W // 2
    tq, tk, ksz, qsz = _attn_tiles(T, ATTN_Q_TILE_FWD, 2 * ATTN_K_TILE)
    r = tq // tk
    ii, jj = _causal_tables(T // tq, r, q_outer=True)
    n_steps = int(ii.shape[0])

    def body(ii_ref, jj_ref, q_ref, k_ref, v_ref, qa_ref, ka_ref, o_ref, lsea_ref, lseb_ref,
             qm_s, m_s, l_s, acc_s, p_s):
        n = pl.program_id(1)
        i, j = ii_ref[n], jj_ref[n]
        d = j - i * r
        in_a = lax.broadcasted_iota(jnp.int32, (1, W), 1) < dh
        top = lax.broadcasted_iota(jnp.int32, (W, 1), 0) < dh

        @pl.when(j == 0)
        def _():
            qv, qa = q_ref[...], qa_ref[...]
            qm_s[0] = jnp.where(in_a, qv, qa)
            qm_s[1] = jnp.where(in_a, qa, qv)
            m_s[...] = jnp.full(m_s.shape, NEG_INF, F32)
            l_s[...] = jnp.zeros_like(l_s)
            acc_s[...] = jnp.zeros_like(acc_s)

        def step(koff):
            kv, ka, vv = k_ref[...], ka_ref[...], v_ref[...]
            kms = (jnp.where(in_a, kv, ka), jnp.where(in_a, ka, kv))
            vt = vv.T
            alphas = {}

            def softmax_stage(hh, q0):
                cols = slice(q0, q0 + qsz)
                kinds = [(k0, _strip_kind(koff, k0, ksz, q0, qsz)) for k0 in range(0, tk, ksz)]
                if all(kind == "skip" for _, kind in kinds):
                    return False
                strips, part = {}, None
                for k0, kind in kinds:
                    if kind != "skip":
                        strips[k0] = _score_strip(kms[hh], qm_s[hh], koff, k0, ksz, q0, qsz, kind)
                        p8 = _fold8(strips[k0], jnp.maximum)
                        part = p8 if part is None else jnp.maximum(part, p8)
                m_prev = m_s[hh, :, cols]
                m_new = jnp.maximum(m_prev, jnp.max(part, axis=0, keepdims=True))
                l8 = None
                for k0, kind in kinds:
                    if kind == "skip":
                        p_s[hh, k0:k0 + ksz, cols] = jnp.zeros((ksz, qsz), BF16)
                        continue
                    pt = jnp.exp(strips[k0] - m_new)
                    s8 = _fold8(pt, jnp.add)
                    l8 = s8 if l8 is None else l8 + s8
                    p_s[hh, k0:k0 + ksz, cols] = pt.astype(BF16)
                alpha = jnp.exp(m_prev - m_new)
                l_s[hh, :, cols] = alpha * l_s[hh, :, cols] + jnp.sum(l8, axis=0, keepdims=True)
                m_s[hh, :, cols] = m_new
                alphas[(hh, q0)] = alpha
                return True

            def value_stage(hh, q0):
                cols, rows = slice(q0, q0 + qsz), slice(hh * dh, (hh + 1) * dh)
                acc_s[rows, cols] = acc_s[rows, cols] * alphas[(hh, q0)] + _dot(vt[rows, :], p_s[hh, :, cols])

            units = [(hh, q0) for q0 in range(0, tq, qsz) for hh in range(2)]
            pending = None
            for unit in units:
                live = softmax_stage(*unit)
                if pending is not None:
                    value_stage(*pending)
                pending = unit if live else None
            if pending is not None:
                value_stage(*pending)

        @pl.when(d < 0)
        def _():
            step(None)

        for dd in range(r):
            @pl.when(d == dd)
            def _():
                step(dd * tk)

        @pl.when(d == r - 1)
        def _():
            inv = jnp.where(top, 1.0 / l_s[0], 1.0 / l_s[1])
            o_ref[...] = (acc_s[...] * inv).T.astype(BF16)
            lsea_ref[...] = m_s[0] + jnp.log(l_s[0])
            lseb_ref[...] = m_s[1] + jnp.log(l_s[1])

    qspec = pl.BlockSpec((tq, W), lambda hp, n, ii, jj: (ii[n], hp))
    kspec = pl.BlockSpec((tk, W), lambda hp, n, ii, jj: (jj[n], hp))
    rspec = pl.BlockSpec((None, 1, tq), lambda hp, n, ii, jj: (hp, 0, ii[n]))
    grid_spec = pltpu.PrefetchScalarGridSpec(
        num_scalar_prefetch=2, grid=(Hp, n_steps),
        in_specs=[qspec, kspec, kspec, qspec, kspec],
        out_specs=[qspec, rspec, rspec],
        scratch_shapes=[pltpu.VMEM((2, tq, W), BF16), pltpu.VMEM((2, 1, tq), F32), pltpu.VMEM((2, 1, tq), F32),
                        pltpu.VMEM((W, tq), F32), pltpu.VMEM((2, tk, tq), BF16)],
    )
    return pl.pallas_call(
        body, name="flash_fwd", grid_spec=grid_spec,
        out_shape=(jax.ShapeDtypeStruct((T, D), BF16), jax.ShapeDtypeStruct((Hp, 1, T), F32),
                   jax.ShapeDtypeStruct((Hp, 1, T), F32)),
        compiler_params=_params(2),
    )(ii, jj, q, k, v, qaug, kaug)


def _attn_delta(do, o, head_sel):
    T, D = do.shape
    H = head_sel.shape[0]
    tm = _tile(T, ROW_TILE_FWD)

    def body(do_ref, o_ref, sel_ref, out_ref):
        prod = do_ref[...].astype(F32) * o_ref[...].astype(F32)
        out_ref[...] = lax.dot_general(sel_ref[...], prod, (((1,), (1,)), ((), ())),
                                       precision=lax.Precision.HIGHEST, preferred_element_type=F32)

    return pl.pallas_call(
        body, name="attn_delta", grid=(T // tm,),
        in_specs=[pl.BlockSpec((tm, D), lambda i: (i, 0)), pl.BlockSpec((tm, D), lambda i: (i, 0)),
                  pl.BlockSpec((H, D), lambda i: (0, 0))],
        out_specs=pl.BlockSpec((H, tm), lambda i: (0, i)),
        out_shape=jax.ShapeDtypeStruct((H, T), F32), compiler_params=_params())(do, o, head_sel)


def _flash_bwd(q, k, v, qaug, kaug, do, lse_a, lse_b, delta, qscale):
    T, D = q.shape
    Hp = lse_a.shape[0]
    W = D // Hp
    dh = W // 2
    tq, tk, ksz, qsz = _attn_tiles(T, ATTN_Q_TILE_BWD, 4 * ATTN_K_TILE)
    r = tq // tk
    nbq = T // tq
    jj, ii = _causal_tables(nbq, r, q_outer=False)
    n_steps = int(ii.shape[0])
    row_row = dh
    col_row = dh + N_AUG

    def body(jj_ref, ii_ref, q_ref, k_ref, v_ref, qa_ref, ka_ref, do_ref, lsea_ref, lseb_ref, dla_ref, dlb_ref,
             dqt_ref, dkt_ref, dvt_ref, dcola_ref, dcolb_ref, drowa_ref, drowb_ref,
             dqt_acc, dk_acc, dv_acc, *rot):
        n = pl.program_id(1)
        i, j = ii_ref[n], jj_ref[n]
        d = j - i * r
        in_a = lax.broadcasted_iota(jnp.int32, (1, W), 1) < dh
        top = lax.broadcasted_iota(jnp.int32, (W, 1), 0) < dh

        @pl.when(n == 0)
        def _():
            dqt_acc[...] = jnp.zeros_like(dqt_acc)

        @pl.when(d >= 0)
        def _():
            dk_acc[...] = jnp.zeros_like(dk_acc)
            dv_acc[...] = jnp.zeros_like(dv_acc)

        def step(koff):
            qv, qa, kv, ka = q_ref[...], qa_ref[...], k_ref[...], ka_ref[...]
            vv, dov = v_ref[...], do_ref[...]
            zb = jnp.zeros_like(dov)
            kms = (jnp.where(in_a, kv, ka), jnp.where(in_a, ka, kv))
            qms = (jnp.where(in_a, qv, qa), jnp.where(in_a, qa, qv))
            doms = (jnp.where(in_a, dov, zb), jnp.where(in_a, zb, dov))
            def own_first(a, hh):
                return a[:dh + AUG_ROWS] if hh == 0 else jnp.concatenate([a[dh:], a[:AUG_ROWS]], axis=0)

            kxs = tuple(own_first(kms[hh].T, hh) for hh in range(2))
            qxs = tuple(own_first(qms[hh].T, hh) for hh in range(2))
            dot = dov.T
            dxs = (dot[:dh], dot[dh:])
            lses, dls = (lsea_ref[...], lseb_ref[...]), (dla_ref[...], dlb_ref[...])
            dvs, dks = ([], []), ([], [])

            def score_stage(u, hh, q0):
                p_s, ds_s = rot[2 * (u % ATTN_ROTATE)], rot[2 * (u % ATTN_ROTATE) + 1]
                cols = slice(q0, q0 + qsz)
                kinds = [(k0, _strip_kind(koff, k0, ksz, q0, qsz)) for k0 in range(0, tk, ksz)]
                if all(kind == "skip" for _, kind in kinds):
                    return False
                for k0, kind in kinds:
                    if kind == "skip":
                        p_s[k0:k0 + ksz, :] = jnp.zeros((ksz, qsz), BF16)
                        ds_s[k0:k0 + ksz, :] = jnp.zeros((ksz, qsz), BF16)
                        continue
                    st = _score_strip(kms[hh], qms[hh], koff, k0, ksz, q0, qsz, kind)
                    pt = jnp.exp(st - lses[hh][:, cols])
                    dst = pt * (_dot_nt(vv[k0:k0 + ksz, :], doms[hh][cols, :]) - dls[hh][:, cols])
                    p_s[k0:k0 + ksz, :] = pt.astype(BF16)
                    ds_s[k0:k0 + ksz, :] = dst.astype(BF16)
                return True

            def grad_stage(u, hh, q0):
                p_s, ds_s = rot[2 * (u % ATTN_ROTATE)], rot[2 * (u % ATTN_ROTATE) + 1]
                cols = slice(q0, q0 + qsz)
                ptb, dsb = p_s[...], ds_s[...]
                dvs[hh].append(_dot_nt(dxs[hh][:, cols], ptb))
                dks[hh].append(_dot_nt(qxs[hh][:, cols], dsb))
                dqt_acc[i, hh, :, cols] += _dot(kxs[hh], dsb)

            order = [(hh, q0) for q0 in range(0, tq, qsz) for hh in range(2)]
            units = [(u, hh, q0) for u, (hh, q0) in enumerate(order)]
            pending = None
            for unit in units:
                live = score_stage(*unit)
                if pending is not None:
                    grad_stage(*pending)
                pending = unit if live else None
            if pending is not None:
                grad_stage(*pending)
            for hh in range(2):
                dv_acc[hh] += sum(dvs[hh][1:], dvs[hh][0])
                dk_acc[hh] += sum(dks[hh][1:], dks[hh][0])

        @pl.when(d < 0)
        def _():
            step(None)

        for dd in range(r):
            @pl.when(d == dd)
            def _():
                step(dd * tk)

        @pl.when(i == nbq - 1)
        def _():
            dkt_ref[...] = jnp.concatenate([dk_acc[0, :dh], dk_acc[1, :dh]], axis=0).astype(BF16)
            dvt_ref[...] = jnp.concatenate([dv_acc[0], dv_acc[1]], axis=0).astype(BF16)
            dcola_ref[...] = dk_acc[0, col_row:col_row + 1, :]
            dcolb_ref[...] = dk_acc[1, col_row:col_row + 1, :]

        @pl.when(n == n_steps - 1)
        def _():
            for b in range(nbq):
                cols = slice(b * tq, (b + 1) * tq)
                both = jnp.concatenate([dqt_acc[b, 0, :dh], dqt_acc[b, 1, :dh]], axis=0)
                dqt_ref[:, cols] = (both * qscale).astype(BF16)
                drowa_ref[:, cols] = dqt_acc[b, 0, row_row:row_row + 1, :]
                drowb_ref[:, cols] = dqt_acc[b, 1, row_row:row_row + 1, :]

    qspec = pl.BlockSpec((tq, W), lambda hp, n, jj, ii: (ii[n], hp))
    kspec = pl.BlockSpec((tk, W), lambda hp, n, jj, ii: (jj[n], hp))
    ktspec = pl.BlockSpec((W, tk), lambda hp, n, jj, ii: (hp, jj[n]))
    pair_row = pl.BlockSpec((None, 1, tq), lambda hp, n, jj, ii: (hp, 0, ii[n]))
    key_row = pl.BlockSpec((None, 1, tk), lambda hp, n, jj, ii: (hp, 0, jj[n]))
    whole_row = pl.BlockSpec((None, 1, T), lambda hp, n, jj, ii: (hp, 0, 0))
    grid_spec = pltpu.PrefetchScalarGridSpec(
        num_scalar_prefetch=2, grid=(Hp, n_steps),
        in_specs=[
            qspec, kspec, kspec, qspec, kspec, qspec, pair_row, pair_row,
            pl.BlockSpec((None, 1, tq), lambda hp, n, jj, ii: (2 * hp, 0, ii[n])),
            pl.BlockSpec((None, 1, tq), lambda hp, n, jj, ii: (2 * hp + 1, 0, ii[n])),
        ],
        out_specs=[
            pl.BlockSpec((W, T), lambda hp, n, jj, ii: (hp, 0)),
            ktspec, ktspec, key_row, key_row, whole_row, whole_row,
        ],
        scratch_shapes=[pltpu.VMEM((nbq, 2, dh + AUG_ROWS, tq), F32), pltpu.VMEM((2, dh + AUG_ROWS, tk), F32),
                        pltpu.VMEM((2, dh, tk), F32)] + [pltpu.VMEM((tk, qsz), BF16)] * (2 * ATTN_ROTATE),
    )
    return pl.pallas_call(
        body, name="flash_bwd", grid_spec=grid_spec,
        out_shape=(jax.ShapeDtypeStruct((D, T), BF16), jax.ShapeDtypeStruct((D, T), BF16),
                   jax.ShapeDtypeStruct((D, T), BF16), jax.ShapeDtypeStruct((Hp, 1, T), F32),
                   jax.ShapeDtypeStruct((Hp, 1, T), F32),
                   jax.ShapeDtypeStruct((Hp, 1, T), F32), jax.ShapeDtypeStruct((Hp, 1, T), F32)),
        compiler_params=_params(2),
    )(jj, ii, q, k, v, qaug, kaug, do, lse_a, lse_b, delta, delta)


HBM_SPEC = pl.BlockSpec(memory_space=pltpu.HBM)


def _mesh_pos():
    return lax.axis_index("x"), lax.axis_index("y"), lax.axis_index("c")


def _all_gather_shards(shards):
    n = len(shards)

    def body(*refs):
        in_refs, out_refs = refs[:n], refs[n:2 * n]
        send1, recv1, send2, recv2 = refs[2 * n:]
        x, y, c = _mesh_pos()
        me = 2 * x + y
        chips = [(1 - x, y), (x, 1 - y), (1 - x, 1 - y)]

        def half(t, chip_idx, pc):
            hr = shards[t].shape[0] // 2
            return out_refs[t].at[chip_idx, pl.ds(pc * hr, hr), :]

        first = []
        for t in range(n):
            hr = shards[t].shape[0] // 2
            for kk, (cx, cy) in enumerate(chips):
                first.append(pltpu.make_async_remote_copy(
                    src_ref=in_refs[t].at[pl.ds(c * hr, hr), :], dst_ref=half(t, me, c),
                    send_sem=send1.at[3 * t + kk], recv_sem=recv1.at[3 * t + kk],
                    device_id=(cx, cy, c), device_id_type=MESH))
        for cp in first:
            cp.start()
        passed = []
        for t in range(n):
            for kk, (cx, cy) in enumerate(chips):
                src_chip = 2 * cx + cy
                landed = half(t, src_chip, c)
                pltpu.make_async_remote_copy(
                    src_ref=landed, dst_ref=landed, send_sem=send1.at[3 * t + kk], recv_sem=recv1.at[3 * t + kk],
                    device_id=(cx, cy, c), device_id_type=MESH).wait_recv()
                fwd = pltpu.make_async_remote_copy(
                    src_ref=landed, dst_ref=landed, send_sem=send2.at[3 * t + kk], recv_sem=recv2.at[3 * t + kk],
                    device_id=(x, y, 1 - c), device_id_type=MESH)
                fwd.start()
                passed.append(fwd)
        for t in range(n):
            for kk, (cx, cy) in enumerate(chips):
                other = half(t, 2 * cx + cy, 1 - c)
                pltpu.make_async_remote_copy(
                    src_ref=other, dst_ref=other, send_sem=send2.at[3 * t + kk], recv_sem=recv2.at[3 * t + kk],
                    device_id=(x, y, 1 - c), device_id_type=MESH).wait_recv()
        for cp in first + passed:
            cp.wait_send()

    gathered = pl.pallas_call(
        body, name="weights_all_gather",
        out_shape=[jax.ShapeDtypeStruct((N_CHIPS,) + s.shape, s.dtype) for s in shards],
        in_specs=[HBM_SPEC] * n, out_specs=[HBM_SPEC] * n,
        scratch_shapes=[pltpu.SemaphoreType.DMA((3 * n,)), pltpu.SemaphoreType.DMA((3 * n,)),
                        pltpu.SemaphoreType.DMA((3 * n,)), pltpu.SemaphoreType.DMA((3 * n,))],
    )(*shards)
    own = lax.broadcasted_iota(jnp.int32, (N_CHIPS, 1, 1), 0) == 2 * lax.axis_index("x") + lax.axis_index("y")
    return [jnp.where(own, s[None], g) for s, g in zip(shards, gathered)]


def _sibling_swap_halves(gs):
    n = len(gs)

    def body(*refs):
        g_refs, r_refs = refs[:n], refs[n:2 * n]
        ssem, rsem = refs[2 * n:]
        x, y, c = _mesh_pos()
        cps = []
        for t in range(n):
            hr = gs[t].shape[1] // 2
            cps += [pltpu.make_async_remote_copy(
                src_ref=g_refs[t].at[s, pl.ds((1 - c) * hr, hr), :], dst_ref=r_refs[t].at[s],
                send_sem=ssem.at[N_CHIPS * t + s], recv_sem=rsem.at[N_CHIPS * t + s],
                device_id=(x, y, 1 - c), device_id_type=MESH) for s in range(N_CHIPS)]
        for cp in cps:
            cp.start()
        for cp in cps:
            cp.wait()

    return pl.pallas_call(
        body, name="grads_sibling_swap",
        out_shape=[jax.ShapeDtypeStruct((N_CHIPS, g.shape[1] // 2, g.shape[2]), F32) for g in gs],
        in_specs=[HBM_SPEC] * n, out_specs=[HBM_SPEC] * n,
        scratch_shapes=[pltpu.SemaphoreType.DMA((N_CHIPS * n,)), pltpu.SemaphoreType.DMA((N_CHIPS * n,))],
    )(*gs)


def _pair_add(g, r, c_idx):
    _, M, C = g.shape
    hr = M // 2
    tr = _tile(hr, PACK_ROW_TILE)
    nbk = hr // tr

    def body(c_ref, g_ref, r_ref, o_ref):
        o_ref[...] = (g_ref[...] + r_ref[...]).astype(BF16)

    grid_spec = pltpu.PrefetchScalarGridSpec(
        num_scalar_prefetch=1, grid=(N_CHIPS, nbk),
        in_specs=[pl.BlockSpec((None, tr, C), lambda s, i, c: (s, c[0] * nbk + i, 0)),
                  pl.BlockSpec((None, tr, C), lambda s, i, c: (s, i, 0))],
        out_specs=pl.BlockSpec((None, tr, C), lambda s, i, c: (s, i, 0)),
    )
    return pl.pallas_call(body, name="grads_pair_add", grid_spec=grid_spec,
                          out_shape=jax.ShapeDtypeStruct((N_CHIPS, hr, C), BF16),
                          compiler_params=_params(2))(c_idx, g, r)


def _chip_scatter(pps):
    n = len(pps)

    def body(*refs):
        p_refs, r_refs = refs[:n], refs[n:2 * n]
        ssem, rsem, lsem = refs[2 * n:]
        x, y, c = _mesh_pos()
        me = 2 * x + y
        chips = [(1 - x, y), (x, 1 - y), (1 - x, 1 - y)]
        own = [pltpu.make_async_copy(p_refs[t].at[me], r_refs[t].at[me], lsem.at[t]) for t in range(n)]
        for cp in own:
            cp.start()
        cps = [pltpu.make_async_remote_copy(
            src_ref=p_refs[t].at[2 * cx + cy], dst_ref=r_refs[t].at[me],
            send_sem=ssem.at[3 * t + kk], recv_sem=rsem.at[3 * t + kk],
            device_id=(cx, cy, c), device_id_type=MESH) for t in range(n) for kk, (cx, cy) in enumerate(chips)]
        for cp in cps:
            cp.start()
        for t in range(n):
            for kk, (cx, cy) in enumerate(chips):
                got = r_refs[t].at[2 * cx + cy]
                pltpu.make_async_remote_copy(
                    src_ref=got, dst_ref=got, send_sem=ssem.at[3 * t + kk], recv_sem=rsem.at[3 * t + kk],
                    device_id=(cx, cy, c), device_id_type=MESH).wait_recv()
        for cp in cps:
            cp.wait_send()
        for cp in own:
            cp.wait()

    return pl.pallas_call(
        body, name="grads_chip_scatter", out_shape=[jax.ShapeDtypeStruct(pp.shape, pp.dtype) for pp in pps],
        in_specs=[HBM_SPEC] * n, out_specs=[HBM_SPEC] * n,
        scratch_shapes=[pltpu.SemaphoreType.DMA((3 * n,)), pltpu.SemaphoreType.DMA((3 * n,)),
                        pltpu.SemaphoreType.DMA((n,))],
    )(*pps)


def _chip_sum(r, c_idx):
    _, hr, C = r.shape
    tr = _tile(hr, PACK_ROW_TILE)
    nbk = hr // tr

    def body(c_ref, r_ref, o_ref):
        r0, r1, r2, r3 = (r_ref[s].astype(F32) for s in range(N_CHIPS))
        o_ref[...] = ((r0 + r1) + r2) + r3

    grid_spec = pltpu.PrefetchScalarGridSpec(
        num_scalar_prefetch=1, grid=(nbk,),
        in_specs=[pl.BlockSpec((N_CHIPS, tr, C), lambda i, c: (0, i, 0))],
        out_specs=pl.BlockSpec((tr, C), lambda i, c: (c[0] * nbk + i, 0)),
    )
    return pl.pallas_call(body, name="grads_chip_sum", grid_spec=grid_spec,
                          out_shape=jax.ShapeDtypeStruct((2 * hr, C), F32), compiler_params=_params())(c_idx, r)


def _sibling_join(bufs):
    n = len(bufs)

    def body(*refs):
        o_refs = refs[n:2 * n]
        ssem, rsem = refs[2 * n:]
        x, y, c = _mesh_pos()
        cps = []
        for t in range(n):
            hr = bufs[t].shape[0] // 2
            mine = o_refs[t].at[pl.ds(c * hr, hr), :]
            cps.append(pltpu.make_async_remote_copy(src_ref=mine, dst_ref=mine, send_sem=ssem.at[t], recv_sem=rsem.at[t],
                                                    device_id=(x, y, 1 - c), device_id_type=MESH))
        for cp in cps:
            cp.start()
        for t in range(n):
            hr = bufs[t].shape[0] // 2
            theirs = o_refs[t].at[pl.ds((1 - c) * hr, hr), :]
            pltpu.make_async_remote_copy(src_ref=theirs, dst_ref=theirs, send_sem=ssem.at[t], recv_sem=rsem.at[t],
                                         device_id=(x, y, 1 - c), device_id_type=MESH).wait_recv()
        for cp in cps:
            cp.wait_send()

    return pl.pallas_call(
        body, name="grads_sibling_join", out_shape=[jax.ShapeDtypeStruct(b.shape, F32) for b in bufs],
        in_specs=[HBM_SPEC] * n, out_specs=[HBM_SPEC] * n, input_output_aliases={t: t for t in range(n)},
        scratch_shapes=[pltpu.SemaphoreType.DMA((n,)), pltpu.SemaphoreType.DMA((n,))],
    )(*bufs)


def _adamw(w, g, m, v):
    M, C = w.shape
    tr = _tile(M, PACK_ROW_TILE)

    def body(w_ref, g_ref, m_ref, v_ref, d_ref, mo_ref, vo_ref):
        gv = g_ref[...]
        mn = ADAM_B1 * m_ref[...] + (1.0 - ADAM_B1) * gv
        vn = ADAM_B2 * v_ref[...] + (1.0 - ADAM_B2) * (gv * gv)
        m_hat = mn / (1.0 - ADAM_B1 ** ADAM_STEP)
        v_hat = vn / (1.0 - ADAM_B2 ** ADAM_STEP)
        d_ref[...] = -ADAM_LR * (m_hat / (jnp.sqrt(v_hat) + ADAM_EPS) + ADAM_WD * w_ref[...])
        mo_ref[...] = mn
        vo_ref[...] = vn

    spec = pl.BlockSpec((tr, C), lambda i: (i, 0))
    return pl.pallas_call(
        body, name="adamw", grid=(M // tr,), in_specs=[spec] * 4, out_specs=[spec] * 3,
        out_shape=[jax.ShapeDtypeStruct((M, C), F32)] * 3, compiler_params=_params())(w, g, m, v)


WEIGHT_NAMES = ("norm_g", "w_attn_in", "b_forget", "w_attn_out", "w_conv_in", "conv_w", "w_conv_out",
                "w_mlp_up", "w_mlp_down", "w_ple_proj", "w_ple_gate")
COL_SHARDED = ("norm_g", "w_attn_in", "w_conv_in", "conv_w", "w_mlp_up", "w_ple_proj")
ROW_SHARDED = ("w_attn_out", "w_conv_out", "w_mlp_down", "w_ple_gate")


def _unshard(name, gathered, shard_shape):
    a = gathered.reshape((N_CHIPS,) + tuple(shard_shape))
    if name in COL_SHARDED:
        a = jnp.moveaxis(a, 0, -2)
        return a.reshape(a.shape[:-2] + (N_CHIPS * shard_shape[-1],))
    a = jnp.moveaxis(a, 0, 1)
    return a.reshape((shard_shape[0], N_CHIPS * shard_shape[1], shard_shape[2]))


def _to_shard_major(name, full):
    if name == "b_forget":
        return jnp.broadcast_to(full.reshape(1, -1), (N_CHIPS, full.size))
    if name in COL_SHARDED:
        a = full.reshape(full.shape[:-1] + (N_CHIPS, full.shape[-1] // N_CHIPS))
        a = jnp.moveaxis(a, -2, 0)
    else:
        a = full.reshape((full.shape[0], N_CHIPS, full.shape[1] // N_CHIPS, full.shape[2]))
        a = jnp.moveaxis(a, 1, 0)
    return a.reshape(N_CHIPS, -1)


def _reduce_scatter(tensors, c_idx):
    swapped = _sibling_swap_halves(tensors)
    pairs = [_pair_add(g, r, c_idx) for g, r in zip(tensors, swapped)]
    return _sibling_join([_chip_sum(r, c_idx) for r in _chip_scatter(pairs)])


def kernel(x, p, norm_g, w_attn_in, b_forget, w_attn_out, w_conv_in, conv_w, w_conv_out, w_mlp_up, w_mlp_down, w_ple_proj, w_ple_gate, loss_target, m_norm_g, m_w_attn_in, m_b_forget, m_w_attn_out, m_w_conv_in, m_conv_w, m_w_conv_out, m_w_mlp_up, m_w_mlp_down, m_w_ple_proj, m_w_ple_gate, v_norm_g, v_w_attn_in, v_b_forget, v_w_attn_out, v_w_conv_in, v_conv_w, v_w_conv_out, v_w_mlp_up, v_w_mlp_down, v_w_ple_proj, v_w_ple_gate):
    w_local = dict(norm_g=norm_g, w_attn_in=w_attn_in, b_forget=b_forget, w_attn_out=w_attn_out,
                   w_conv_in=w_conv_in, conv_w=conv_w, w_conv_out=w_conv_out, w_mlp_up=w_mlp_up,
                   w_mlp_down=w_mlp_down, w_ple_proj=w_ple_proj, w_ple_gate=w_ple_gate)
    m_local = dict(norm_g=m_norm_g, w_attn_in=m_w_attn_in, b_forget=m_b_forget, w_attn_out=m_w_attn_out,
                   w_conv_in=m_w_conv_in, conv_w=m_conv_w, w_conv_out=m_w_conv_out, w_mlp_up=m_w_mlp_up,
                   w_mlp_down=m_w_mlp_down, w_ple_proj=m_w_ple_proj, w_ple_gate=m_w_ple_gate)
    v_local = dict(norm_g=v_norm_g, w_attn_in=v_w_attn_in, b_forget=v_b_forget, w_attn_out=v_w_attn_out,
                   w_conv_in=v_w_conv_in, conv_w=v_conv_w, w_conv_out=v_w_conv_out, w_mlp_up=v_w_mlp_up,
                   w_mlp_down=v_w_mlp_down, w_ple_proj=v_w_ple_proj, w_ple_gate=v_w_ple_gate)
    shard_shapes = {k: tuple(a.shape) for k, a in w_local.items()}

    xs = x[0]
    target = loss_target[0]
    T, D = xs.shape
    depth = p.shape[0]
    H = b_forget.shape[1]
    qscale = float(D // H) ** -0.5
    head_sel = (jnp.arange(D)[None, :] // (D // H) == jnp.arange(H)[:, None]).astype(F32)

    big = [n for n in WEIGHT_NAMES if n not in ("norm_g", "conv_w", "b_forget")]
    small = jnp.concatenate([norm_g.reshape(-1), conv_w.reshape(-1)])
    n_small = small.shape[0]
    small_rows = -(-n_small // (LANES * 16)) * 16
    small = jnp.pad(small, (0, small_rows * LANES - n_small)).reshape(small_rows, LANES)
    shards = [w_local[n].astype(BF16).reshape(-1, shard_shapes[n][-1]) for n in big] + [small]
    gathered = _all_gather_shards(shards)
    full = {n: _unshard(n, g, shard_shapes[n]) for n, g in zip(big, gathered[:-1])}
    gs = gathered[-1].reshape(N_CHIPS, -1)
    full["norm_g"] = _unshard("norm_g", gs[:, :norm_g.size], shard_shapes["norm_g"])
    full["conv_w"] = _unshard("conv_w", gs[:, norm_g.size:n_small], shard_shapes["conv_w"])
    gains = full["norm_g"]

    def gain(i, k):
        return gains[i, k].reshape(1, D)

    def taps(j):
        return jnp.pad(full["conv_w"][j], ((0, 5), (0, 0)))

    saved = []
    h = xs
    for i in range(depth):
        j = i // 2
        s = {"x0": h}
        if i % 2 == 0:
            w_in = full["w_attn_in"][j]
            wqkv = w_in[:, :3 * D]
            wf = jnp.pad(w_in[:, 3 * D:], ((0, 0), (0, LANES - H)))
            bf = jnp.pad(b_forget[j].reshape(1, H), ((0, 0), (0, LANES - H)))
            q, k, v, fl, lf = _attn_in_fwd(h, gain(i, 0), wqkv, wf, bf, qscale)
            qaug, kaug = _aug_operands(_gate_cumsum(lf[:, :H].T), H, D)
            o, lse_a, lse_b = _flash_fwd(q, k, v, qaug, kaug, H)
            s.update(q=q, k=k, v=v, fl=fl, qaug=qaug, kaug=kaug, o=o, lse_a=lse_a, lse_b=lse_b, wqkv=wqkv, wf=wf)
            mix_in, w_out = o, full["w_attn_out"][j]
        else:
            b, c, u, zc, y = _conv_in_fwd(h, gain(i, 0), full["w_conv_in"][j], taps(j))
            s.update(b=b, c=c, u=u, zc=zc, y=y)
            mix_in, w_out = y, full["w_conv_out"][j]
        m1, x1 = _out_proj_fwd("mixer_out_fwd", mix_in, w_out, gain(i, 1), h)
        up, a = _mlp_up_fwd(x1, gain(i, 2), full["w_mlp_up"][i])
        m3, x2 = _out_proj_fwd("mlp_down_fwd", a, full["w_mlp_down"][i], gain(i, 3), x1)
        gl, pe, x3 = _ple_fwd(x2, p[i, 0], gain(i, 4), gain(i, 5), full["w_ple_gate"][i], full["w_ple_proj"][i])
        s.update(m1=m1, x1=x1, up=up, a=a, m3=m3, x2=x2, gl=gl, pe=pe, w_out=w_out)
        saved.append(s)
        h = x3

    dh, loss_blk = _loss_fwd_bwd(h, target)
    loss = lax.psum(loss_blk[0, 0], ("x", "y", "c"))

    g_gain = [[None] * 6 for _ in range(depth)]
    grads = {n: [None] * w_local[n].shape[0] for n in WEIGHT_NAMES if n != "norm_g"}
    for i in reversed(range(depth)):
        j = i // 2
        s = saved[i]
        dx2, dwp, dwg, g_gain[i][4], g_gain[i][5] = _ple_bwd(
            dh, s["x2"], p[i, 0], s["gl"], s["pe"], gain(i, 4), gain(i, 5), full["w_ple_gate"][i].T)
        grads["w_ple_proj"][i], grads["w_ple_gate"][i] = dwp, dwg
        dup, dwd, g_gain[i][3] = _out_proj_bwd(
            "mlp_down_bwd", dx2, s["m3"], gain(i, 3), [s["a"], s["up"]], full["w_mlp_down"][i].T, "relu2")
        grads["w_mlp_down"][i] = dwd
        dx1, dwu, g_gain[i][2] = _in_proj_bwd(
            "mlp_up_bwd", [dup], [full["w_mlp_up"][i].T], s["x1"], gain(i, 2), dx2)
        grads["w_mlp_up"][i] = dwu
        if i % 2 == 0:
            do, dwo, g_gain[i][1] = _out_proj_bwd(
                "attn_out_bwd", dx1, s["m1"], gain(i, 1), [s["o"]], s["w_out"].T, "plain")
            grads["w_attn_out"][j] = dwo
            delta = _attn_delta(do, s["o"], head_sel).reshape(H, 1, T)
            dqt, dkt, dvt, dca, dcb, dra, drb = _flash_bwd(s["q"], s["k"], s["v"], s["qaug"], s["kaug"], do,
                                                           s["lse_a"], s["lse_b"], delta, qscale)
            dcol = jnp.concatenate([dca, dcb], axis=1).reshape(H, T)
            drow = jnp.concatenate([dra, drb], axis=1).reshape(H, T)
            dfl_t, dbf = _gate_bwd(drow, dcol, s["fl"][:, :H].T)
            grads["b_forget"][j] = dbf.reshape(H)
            dfl = jnp.pad(dfl_t.T, ((0, 0), (0, LANES - H))).astype(BF16)
            wqkv_t = s["wqkv"].T
            dh, dwq, dwk, dwv, dwf, g_gain[i][0] = _in_proj_bwd(
                "attn_in_bwd", [dqt.T, dkt.T, dvt.T, dfl], [wqkv_t[:D], wqkv_t[D:2 * D], wqkv_t[2 * D:], s["wf"].T],
                s["x0"], gain(i, 0), dx1)
            grads["w_attn_in"][j] = jnp.concatenate([dwq, dwk, dwv, dwf[:, :H]], axis=1)
        else:
            dy, dwo, g_gain[i][1] = _out_proj_bwd(
                "conv_out_bwd", dx1, s["m1"], gain(i, 1), [s["y"]], s["w_out"].T, "plain_f32")
            grads["w_conv_out"][j] = dwo
            db, dc, du, dcw = _conv_bwd(dy, s["b"], s["c"], s["u"], s["zc"], taps(j))
            grads["conv_w"][j] = dcw[:conv_w.shape[1]]
            w_t = full["w_conv_in"][j].T
            dh, dwb, dwc, dwu2, g_gain[i][0] = _in_proj_bwd(
                "conv_in_bwd", [db, dc, du], [w_t[:D], w_t[D:2 * D], w_t[2 * D:]], s["x0"], gain(i, 0), dx1)
            grads["w_conv_in"][j] = jnp.concatenate([dwb, dwc, dwu2], axis=1)
    grad_x = dh.reshape(x.shape)

    grad_full = {n: jnp.stack(grads[n]) for n in grads}
    grad_full["norm_g"] = jnp.stack([jnp.concatenate(row, axis=0) for row in g_gain])

    c_idx = lax.axis_index("c").astype(jnp.int32).reshape(1)
    tiny = ("norm_g", "conv_w", "b_forget")
    tiny_flat = jnp.concatenate([_to_shard_major(n, grad_full[n]) for n in tiny], axis=1)
    n_tiny = tiny_flat.shape[1]
    tiny_rows = -(-n_tiny // (LANES * 32)) * 32
    tiny_t = jnp.pad(tiny_flat, ((0, 0), (0, tiny_rows * LANES - n_tiny))).reshape(N_CHIPS, tiny_rows, LANES)
    tensors = [_to_shard_major(n, grad_full[n]).reshape(N_CHIPS, -1, shard_shapes[n][-1]) for n in big]
    reduced = _reduce_scatter(tensors + [tiny_t], c_idx)
    g_out = {n: r.reshape(shard_shapes[n]) for n, r in zip(big, reduced[:-1])}
    tiny_red, off = reduced[-1].reshape(-1), 0
    for n in tiny:
        size = int(np.prod(shard_shapes[n]))
        g_out[n] = tiny_red[off:off + size].reshape(shard_shapes[n])
        off += size
    d_out, m_out, v_out = {}, {}, {}
    for n in WEIGHT_NAMES:
        shp = shard_shapes[n]
        two_d = (-1, shp[-1])
        d, mn, vn = _adamw(w_local[n].reshape(two_d), g_out[n].reshape(two_d),
                           m_local[n].reshape(two_d), v_local[n].reshape(two_d))
        d_out[n], m_out[n], v_out[n] = d.reshape(shp), mn.reshape(shp), vn.reshape(shp)
    return (loss, grad_x, *[g_out[n] for n in WEIGHT_NAMES], *[d_out[n] for n in WEIGHT_NAMES],
            *[m_out[n] for n in WEIGHT_NAMES], *[v_out[n] for n in WEIGHT_NAMES])
```
